```python
import math
import jax, jax.numpy as jnp
from jax import lax
import numpy as np

D_MODEL = 1024
BATCH = 16
SEQ = 4096
DEPTH = 1

CHUNK = 64
D_MIX = D_MODEL
LRU_WIDTH = D_MIX // 2
LRU_BLOCKS = 8
LRU_BLOCK = LRU_WIDTH // LRU_BLOCKS
CONV_WIDTH = 4
LRU_C = 8.0
SB_WIDTH = D_MIX - LRU_WIDTH
SB_HEADS = 8
SB_HEAD_DIM = SB_WIDTH // SB_HEADS
Q_BLOCK = 128
D_FF = 4 * D_MODEL
IN_COLS = 2 * LRU_WIDTH + 3 * SB_WIDTH
EPS = 1e-6

kernel_name = "hymba_rglru_stickbreaking_block"


def _rmsnorm(x, g):
    xf = x.astype(jnp.float32)
    y = xf * lax.rsqrt(jnp.mean(xf * xf, axis=-1, keepdims=True) + EPS)
    return (y * g.astype(jnp.float32)).astype(x.dtype)


def _causal_depthwise_conv(u, w, b):
    rhs = w[:, None, :].astype(u.dtype)
    y = lax.conv_general_dilated(
        u, rhs, window_strides=(1,), padding=[(CONV_WIDTH - 1, 0)],
        dimension_numbers=("NWC", "WIO", "NWC"), feature_group_count=u.shape[-1])
    return y + b.astype(u.dtype)


def _rg_lru(u, w_a, b_a, w_x, b_x, lam):
    B, S, _ = u.shape
    uf = u.astype(jnp.float32)
    ub = uf.reshape(B, S, LRU_BLOCKS, LRU_BLOCK)
    r = jax.nn.sigmoid(jnp.einsum('bsgi,gij->bsgj', ub, w_a.astype(jnp.float32))
                       + b_a.astype(jnp.float32)).reshape(B, S, LRU_WIDTH)
    i = jax.nn.sigmoid(jnp.einsum('bsgi,gij->bsgj', ub, w_x.astype(jnp.float32))
                       + b_x.astype(jnp.float32)).reshape(B, S, LRU_WIDTH)
    log_a = -LRU_C * r * jax.nn.softplus(-lam.astype(jnp.float32))
    a = jnp.exp(log_a)
    bterm = jnp.sqrt(-jnp.expm1(2.0 * log_a)) * (i * uf)

    def combine(left, right):
        a1, b1 = left
        a2, b2 = right
        return a1 * a2, a2 * b1 + b2

    _, h = lax.associative_scan(combine, (a, bterm), axis=1)
    return h.astype(u.dtype)


def _stick_breaking(q, k, v):
    B, S, H, Dh = q.shape
    scale = 1.0 / math.sqrt(Dh)
    qh = jnp.transpose(q, (0, 2, 1, 3)).astype(jnp.float32)
    kh = jnp.transpose(k, (0, 2, 1, 3)).astype(jnp.float32)
    vh = jnp.transpose(v, (0, 2, 1, 3)).astype(jnp.float32)
    outs = []
    for blk in range(S // Q_BLOCK):
        start = blk * Q_BLOCK
        end = start + Q_BLOCK
        qb = qh[:, :, start:end]
        kb = kh[:, :, :end]
        vb = vh[:, :, :end]
        z = jnp.einsum('bhqd,bhkd->bhqk', qb, kb) * scale
        t_idx = jnp.arange(start, end)[:, None]
        s_idx = jnp.arange(end)[None, :]
        mask = s_idx < t_idx
        log_1m_beta = jnp.where(mask, jax.nn.log_sigmoid(-z), 0.0)
        surv = lax.cumsum(log_1m_beta, axis=3, reverse=True) - log_1m_beta
        att = jnp.where(mask, jnp.exp(jax.nn.log_sigmoid(z) + surv), 0.0)
        outs.append(jnp.einsum('bhqk,bhkd->bhqd', att, vb))
    o = jnp.concatenate(outs, axis=2)
    return jnp.transpose(o, (0, 2, 1, 3)).reshape(B, S, H * Dh).astype(q.dtype)


def _fwd_setup_inputs(seed: int = 0) -> dict:
    key = jax.random.key(seed)
    ks = jax.random.split(key, 20)
    f32 = jnp.float32
    x = jax.random.normal(ks[0], (BATCH, SEQ, D_MODEL), f32)
    norm1_g = 1.0 + 0.01 * jax.random.normal(ks[1], (DEPTH, D_MODEL), f32)
    w_in = jax.random.normal(ks[2], (DEPTH, D_MODEL, IN_COLS), f32) * D_MODEL ** -0.5
    conv_w = jax.random.normal(ks[3], (DEPTH, CONV_WIDTH, LRU_WIDTH), f32) * CONV_WIDTH ** -0.5
    conv_b = 0.01 * jax.random.normal(ks[4], (DEPTH, LRU_WIDTH), f32)
    lru_w_a = jax.random.normal(ks[5], (DEPTH, LRU_BLOCKS, LRU_BLOCK, LRU_BLOCK), f32) * LRU_BLOCK ** -0.5
    lru_b_a = 0.01 * jax.random.normal(ks[6], (DEPTH, LRU_BLOCKS, LRU_BLOCK), f32)
    lru_w_x = jax.random.normal(ks[7], (DEPTH, LRU_BLOCKS, LRU_BLOCK, LRU_BLOCK), f32) * LRU_BLOCK ** -0.5
    lru_b_x = 0.01 * jax.random.normal(ks[8], (DEPTH, LRU_BLOCKS, LRU_BLOCK), f32)
    u = jax.random.uniform(ks[9], (DEPTH, LRU_WIDTH), f32, 0.9, 0.999)
    a0 = u ** (1.0 / LRU_C)
    lru_lambda = jnp.log(a0) - jnp.log1p(-a0)
    lru_out_g = 1.0 + 0.01 * jax.random.normal(ks[10], (DEPTH, LRU_WIDTH), f32)
    sb_out_g = 1.0 + 0.01 * jax.random.normal(ks[11], (DEPTH, SB_WIDTH), f32)
    w_out = jax.random.normal(ks[12], (DEPTH, D_MIX, D_MODEL), f32) * D_MIX ** -0.5
    norm2_g = 1.0 + 0.01 * jax.random.normal(ks[13], (DEPTH, D_MODEL), f32)
    w_up = jax.random.normal(ks[14], (DEPTH, D_MODEL, D_FF), f32) * D_MODEL ** -0.5
    w_down = jax.random.normal(ks[15], (DEPTH, D_FF, D_MODEL), f32) * D_FF ** -0.5
    final_g = 1.0 + 0.01 * jax.random.normal(ks[16], (D_MODEL,), f32)
    return {"x": x, "norm1_g": norm1_g, "w_in": w_in, "conv_w": conv_w, "conv_b": conv_b,
            "lru_w_a": lru_w_a, "lru_b_a": lru_b_a, "lru_w_x": lru_w_x, "lru_b_x": lru_b_x,
            "lru_lambda": lru_lambda, "lru_out_g": lru_out_g, "sb_out_g": sb_out_g,
            "w_out": w_out, "norm2_g": norm2_g, "w_up": w_up, "w_down": w_down,
            "final_g": final_g}


def _fwd_reference(x, norm1_g, w_in, conv_w, conv_b, lru_w_a, lru_b_a, lru_w_x, lru_b_x,
              lru_lambda, lru_out_g, sb_out_g, w_out, norm2_g, w_up, w_down, final_g):
    B, S, _ = x.shape
    split_pts = [LRU_WIDTH, 2 * LRU_WIDTH, 2 * LRU_WIDTH + SB_WIDTH, 2 * LRU_WIDTH + 2 * SB_WIDTH]
    h = x
    for layer in range(DEPTH):
        xn = _rmsnorm(h, norm1_g[layer])
        proj = xn @ w_in[layer].astype(xn.dtype)
        x_lru, g_lru, q, k, v = jnp.split(proj, split_pts, axis=-1)
        c = _causal_depthwise_conv(x_lru, conv_w[layer], conv_b[layer])
        y_lru = _rg_lru(c, lru_w_a[layer], lru_b_a[layer], lru_w_x[layer], lru_b_x[layer],
                        lru_lambda[layer]) * jax.nn.gelu(g_lru, approximate=True)
        y_sb = _stick_breaking(q.reshape(B, S, SB_HEADS, SB_HEAD_DIM),
                               k.reshape(B, S, SB_HEADS, SB_HEAD_DIM),
                               v.reshape(B, S, SB_HEADS, SB_HEAD_DIM))
        mix = jnp.concatenate([_rmsnorm(y_lru, lru_out_g[layer]),
                               _rmsnorm(y_sb, sb_out_g[layer])], axis=-1)
        h = h + mix @ w_out[layer].astype(mix.dtype)
        hn = _rmsnorm(h, norm2_g[layer])
        up = jax.nn.relu(hn @ w_up[layer].astype(hn.dtype))
        h = h + (up * up) @ w_down[layer].astype(up.dtype)
    return _rmsnorm(h, final_g)


import jax as _jax
import jax.numpy as _jnp

TWIN_FORMAT = 'train_step'
FWD_PARAMS = ['x', 'norm1_g', 'w_in', 'conv_w', 'conv_b', 'lru_w_a', 'lru_b_a', 'lru_w_x', 'lru_b_x', 'lru_lambda', 'lru_out_g', 'sb_out_g', 'w_out', 'norm2_g', 'w_up', 'w_down', 'final_g']
TWIN_WEIGHTS = ['norm1_g', 'w_in', 'conv_w', 'conv_b', 'lru_w_a', 'lru_b_a', 'lru_w_x', 'lru_b_x', 'lru_lambda', 'lru_out_g', 'sb_out_g', 'w_out', 'norm2_g', 'w_up', 'w_down', 'final_g']
TWIN_DIFF_INPUT = 'x'
TWIN_INPUTS = ['x', 'norm1_g', 'w_in', 'conv_w', 'conv_b', 'lru_w_a', 'lru_b_a', 'lru_w_x', 'lru_b_x', 'lru_lambda', 'lru_out_g', 'sb_out_g', 'w_out', 'norm2_g', 'w_up', 'w_down', 'final_g', 'loss_target', 'm_norm1_g', 'm_w_in', 'm_conv_w', 'm_conv_b', 'm_lru_w_a', 'm_lru_b_a', 'm_lru_w_x', 'm_lru_b_x', 'm_lru_lambda', 'm_lru_out_g', 'm_sb_out_g', 'm_w_out', 'm_norm2_g', 'm_w_up', 'm_w_down', 'm_final_g', 'v_norm1_g', 'v_w_in', 'v_conv_w', 'v_conv_b', 'v_lru_w_a', 'v_lru_b_a', 'v_lru_w_x', 'v_lru_b_x', 'v_lru_lambda', 'v_lru_out_g', 'v_sb_out_g', 'v_w_out', 'v_norm2_g', 'v_w_up', 'v_w_down', 'v_final_g']
TWIN_OUTPUTS = ['loss', 'grad_x', 'grad_norm1_g', 'grad_w_in', 'grad_conv_w', 'grad_conv_b', 'grad_lru_w_a', 'grad_lru_b_a', 'grad_lru_w_x', 'grad_lru_b_x', 'grad_lru_lambda', 'grad_lru_out_g', 'grad_sb_out_g', 'grad_w_out', 'grad_norm2_g', 'grad_w_up', 'grad_w_down', 'grad_final_g', 'delta_norm1_g', 'delta_w_in', 'delta_conv_w', 'delta_conv_b', 'delta_lru_w_a', 'delta_lru_b_a', 'delta_lru_w_x', 'delta_lru_b_x', 'delta_lru_lambda', 'delta_lru_out_g', 'delta_sb_out_g', 'delta_w_out', 'delta_norm2_g', 'delta_w_up', 'delta_w_down', 'delta_final_g', 'new_m_norm1_g', 'new_m_w_in', 'new_m_conv_w', 'new_m_conv_b', 'new_m_lru_w_a', 'new_m_lru_b_a', 'new_m_lru_w_x', 'new_m_lru_b_x', 'new_m_lru_lambda', 'new_m_lru_out_g', 'new_m_sb_out_g', 'new_m_w_out', 'new_m_norm2_g', 'new_m_w_up', 'new_m_w_down', 'new_m_final_g', 'new_v_norm1_g', 'new_v_w_in', 'new_v_conv_w', 'new_v_conv_b', 'new_v_lru_w_a', 'new_v_lru_b_a', 'new_v_lru_w_x', 'new_v_lru_b_x', 'new_v_lru_lambda', 'new_v_lru_out_g', 'new_v_sb_out_g', 'new_v_w_out', 'new_v_norm2_g', 'new_v_w_up', 'new_v_w_down', 'new_v_final_g']
TWIN_LEAF_KINDS = {'loss': 'loss', 'grad_x': 'grad_x', 'grad_norm1_g': 'grad_w', 'grad_w_in': 'grad_w', 'grad_conv_w': 'grad_w', 'grad_conv_b': 'grad_w', 'grad_lru_w_a': 'grad_w', 'grad_lru_b_a': 'grad_w', 'grad_lru_w_x': 'grad_w', 'grad_lru_b_x': 'grad_w', 'grad_lru_lambda': 'grad_w', 'grad_lru_out_g': 'grad_w', 'grad_sb_out_g': 'grad_w', 'grad_w_out': 'grad_w', 'grad_norm2_g': 'grad_w', 'grad_w_up': 'grad_w', 'grad_w_down': 'grad_w', 'grad_final_g': 'grad_w', 'delta_norm1_g': 'delta_w', 'delta_w_in': 'delta_w', 'delta_conv_w': 'delta_w', 'delta_conv_b': 'delta_w', 'delta_lru_w_a': 'delta_w', 'delta_lru_b_a': 'delta_w', 'delta_lru_w_x': 'delta_w', 'delta_lru_b_x': 'delta_w', 'delta_lru_lambda': 'delta_w', 'delta_lru_out_g': 'delta_w', 'delta_sb_out_g': 'delta_w', 'delta_w_out': 'delta_w', 'delta_norm2_g': 'delta_w', 'delta_w_up': 'delta_w', 'delta_w_down': 'delta_w', 'delta_final_g': 'delta_w', 'new_m_norm1_g': 'new_m', 'new_m_w_in': 'new_m', 'new_m_conv_w': 'new_m', 'new_m_conv_b': 'new_m', 'new_m_lru_w_a': 'new_m', 'new_m_lru_b_a': 'new_m', 'new_m_lru_w_x': 'new_m', 'new_m_lru_b_x': 'new_m', 'new_m_lru_lambda': 'new_m', 'new_m_lru_out_g': 'new_m', 'new_m_sb_out_g': 'new_m', 'new_m_w_out': 'new_m', 'new_m_norm2_g': 'new_m', 'new_m_w_up': 'new_m', 'new_m_w_down': 'new_m', 'new_m_final_g': 'new_m', 'new_v_norm1_g': 'new_v', 'new_v_w_in': 'new_v', 'new_v_conv_w': 'new_v', 'new_v_conv_b': 'new_v', 'new_v_lru_w_a': 'new_v', 'new_v_lru_b_a': 'new_v', 'new_v_lru_w_x': 'new_v', 'new_v_lru_b_x': 'new_v', 'new_v_lru_lambda': 'new_v', 'new_v_lru_out_g': 'new_v', 'new_v_sb_out_g': 'new_v', 'new_v_w_out': 'new_v', 'new_v_norm2_g': 'new_v', 'new_v_w_up': 'new_v', 'new_v_w_down': 'new_v', 'new_v_final_g': 'new_v'}


def _forward(args):
    return _fwd_reference(*[args[k] for k in FWD_PARAMS])


def _output_shape():
    out = _jax.eval_shape(lambda: _forward(_fwd_setup_inputs(0)))
    return out.shape, out.dtype

N_MICROBATCH = 1
ADAM_LR = 0.001
ADAM_B1 = 0.9
ADAM_B2 = 0.999
ADAM_EPS = 1e-08
ADAM_WD = 0.01
ADAM_STEP = 10
PER_EXAMPLE_BATCH_AXIS = {'x': 0, 'loss_target': 0}
SHARED_INPUTS = []
_WEIGHT_DTYPES = {'norm1_g': _jnp.float32, 'w_in': _jnp.float32, 'conv_w': _jnp.float32, 'conv_b': _jnp.float32, 'lru_w_a': _jnp.float32, 'lru_b_a': _jnp.float32, 'lru_w_x': _jnp.float32, 'lru_b_x': _jnp.float32, 'lru_lambda': _jnp.float32, 'lru_out_g': _jnp.float32, 'sb_out_g': _jnp.float32, 'w_out': _jnp.float32, 'norm2_g': _jnp.float32, 'w_up': _jnp.float32, 'w_down': _jnp.float32, 'final_g': _jnp.float32}
MOMENT_SCALE = {'norm1_g': 2.536153e-01, 'w_in': 1.585342e-01, 'conv_w': 2.082862e-01, 'conv_b': 2.560967e+00, 'lru_w_a': 7.214561e-02, 'lru_b_a': 6.923199e-02, 'lru_w_x': 1.316532e-01, 'lru_b_x': 7.237876e-02, 'lru_lambda': 1.025711e-01, 'lru_out_g': 1.913923e-01, 'sb_out_g': 2.047469e-01, 'w_out': 1.898430e-01, 'norm2_g': 1.888694e-01, 'w_up': 9.497285e-02, 'w_down': 1.668075e-01, 'final_g': 6.454777e+01}


def _to_microbatches(a, axis):
    t = _jnp.moveaxis(a, axis, 0)
    t = t.reshape((N_MICROBATCH, t.shape[0] // N_MICROBATCH) + t.shape[1:])
    return _jnp.moveaxis(t, 1, axis + 1)


def setup_inputs(seed: int = 0) -> dict:
    inp = _fwd_setup_inputs(seed)
    key = _jax.random.fold_in(_jax.random.key(seed), 7919)
    shape, _ = _output_shape()
    out = dict(inp)
    out["loss_target"] = _jax.random.normal(_jax.random.fold_in(key, 0), shape, _jnp.float32)
    for i, name in enumerate(TWIN_WEIGHTS):
        w = inp[name].astype(_jnp.float32)
        if MOMENT_SCALE is None:
            s = _jnp.sqrt(_jnp.mean(_jnp.square(w)) + 1e-30)
        else:
            s = MOMENT_SCALE[name]
        km, kv = _jax.random.split(_jax.random.fold_in(key, i + 1))
        out[name] = w
        out["m_" + name] = s * _jax.random.normal(km, w.shape, _jnp.float32)
        out["v_" + name] = (s * s) * _jax.random.uniform(kv, w.shape, _jnp.float32, 0.5, 1.5)
    if N_MICROBATCH > 1:
        for name, axis in PER_EXAMPLE_BATCH_AXIS.items():
            out[name] = _to_microbatches(out[name], axis)
    return {'x': out['x'], 'norm1_g': out['norm1_g'], 'w_in': out['w_in'], 'conv_w': out['conv_w'], 'conv_b': out['conv_b'], 'lru_w_a': out['lru_w_a'], 'lru_b_a': out['lru_b_a'], 'lru_w_x': out['lru_w_x'], 'lru_b_x': out['lru_b_x'], 'lru_lambda': out['lru_lambda'], 'lru_out_g': out['lru_out_g'], 'sb_out_g': out['sb_out_g'], 'w_out': out['w_out'], 'norm2_g': out['norm2_g'], 'w_up': out['w_up'], 'w_down': out['w_down'], 'final_g': out['final_g'], 'loss_target': out['loss_target'], 'm_norm1_g': out['m_norm1_g'], 'm_w_in': out['m_w_in'], 'm_conv_w': out['m_conv_w'], 'm_conv_b': out['m_conv_b'], 'm_lru_w_a': out['m_lru_w_a'], 'm_lru_b_a': out['m_lru_b_a'], 'm_lru_w_x': out['m_lru_w_x'], 'm_lru_b_x': out['m_lru_b_x'], 'm_lru_lambda': out['m_lru_lambda'], 'm_lru_out_g': out['m_lru_out_g'], 'm_sb_out_g': out['m_sb_out_g'], 'm_w_out': out['m_w_out'], 'm_norm2_g': out['m_norm2_g'], 'm_w_up': out['m_w_up'], 'm_w_down': out['m_w_down'], 'm_final_g': out['m_final_g'], 'v_norm1_g': out['v_norm1_g'], 'v_w_in': out['v_w_in'], 'v_conv_w': out['v_conv_w'], 'v_conv_b': out['v_conv_b'], 'v_lru_w_a': out['v_lru_w_a'], 'v_lru_b_a': out['v_lru_b_a'], 'v_lru_w_x': out['v_lru_w_x'], 'v_lru_b_x': out['v_lru_b_x'], 'v_lru_lambda': out['v_lru_lambda'], 'v_lru_out_g': out['v_lru_out_g'], 'v_sb_out_g': out['v_sb_out_g'], 'v_w_out': out['v_w_out'], 'v_norm2_g': out['v_norm2_g'], 'v_w_up': out['v_w_up'], 'v_w_down': out['v_w_down'], 'v_final_g': out['v_final_g']}


def _loss(weights, diff, rest, loss_target):
    with _jax.named_scope("forward"):
        args = {**rest, TWIN_DIFF_INPUT: diff, **{k: w.astype(_WEIGHT_DTYPES[k]) for k, w in weights.items()}}
        y = _forward(args)
    with _jax.named_scope("loss_head"):
        err = _jnp.square(y.astype(_jnp.float32) - loss_target)
        return 0.5 * _jnp.sum(_jnp.mean(err, axis=-1)) if err.ndim else 0.5 * err


def _adamw(w, g, m, v):
    m = ADAM_B1 * m + (1.0 - ADAM_B1) * g
    v = ADAM_B2 * v + (1.0 - ADAM_B2) * _jnp.square(g)
    m_hat = m / (1.0 - ADAM_B1 ** ADAM_STEP)
    v_hat = v / (1.0 - ADAM_B2 ** ADAM_STEP)
    delta = -ADAM_LR * (m_hat / (_jnp.sqrt(v_hat) + ADAM_EPS) + ADAM_WD * w)
    return delta, m, v


def reference(x, norm1_g, w_in, conv_w, conv_b, lru_w_a, lru_b_a, lru_w_x, lru_b_x, lru_lambda, lru_out_g, sb_out_g, w_out, norm2_g, w_up, w_down, final_g, loss_target, m_norm1_g, m_w_in, m_conv_w, m_conv_b, m_lru_w_a, m_lru_b_a, m_lru_w_x, m_lru_b_x, m_lru_lambda, m_lru_out_g, m_sb_out_g, m_w_out, m_norm2_g, m_w_up, m_w_down, m_final_g, v_norm1_g, v_w_in, v_conv_w, v_conv_b, v_lru_w_a, v_lru_b_a, v_lru_w_x, v_lru_b_x, v_lru_lambda, v_lru_out_g, v_sb_out_g, v_w_out, v_norm2_g, v_w_up, v_w_down, v_final_g):
    given = dict(x=x, norm1_g=norm1_g, w_in=w_in, conv_w=conv_w, conv_b=conv_b, lru_w_a=lru_w_a, lru_b_a=lru_b_a, lru_w_x=lru_w_x, lru_b_x=lru_b_x, lru_lambda=lru_lambda, lru_out_g=lru_out_g, sb_out_g=sb_out_g, w_out=w_out, norm2_g=norm2_g, w_up=w_up, w_down=w_down, final_g=final_g, loss_target=loss_target, m_norm1_g=m_norm1_g, m_w_in=m_w_in, m_conv_w=m_conv_w, m_conv_b=m_conv_b, m_lru_w_a=m_lru_w_a, m_lru_b_a=m_lru_b_a, m_lru_w_x=m_lru_w_x, m_lru_b_x=m_lru_b_x, m_lru_lambda=m_lru_lambda, m_lru_out_g=m_lru_out_g, m_sb_out_g=m_sb_out_g, m_w_out=m_w_out, m_norm2_g=m_norm2_g, m_w_up=m_w_up, m_w_down=m_w_down, m_final_g=m_final_g, v_norm1_g=v_norm1_g, v_w_in=v_w_in, v_conv_w=v_conv_w, v_conv_b=v_conv_b, v_lru_w_a=v_lru_w_a, v_lru_b_a=v_lru_b_a, v_lru_w_x=v_lru_w_x, v_lru_b_x=v_lru_b_x, v_lru_lambda=v_lru_lambda, v_lru_out_g=v_lru_out_g, v_sb_out_g=v_sb_out_g, v_w_out=v_w_out, v_norm2_g=v_norm2_g, v_w_up=v_w_up, v_w_down=v_w_down, v_final_g=v_final_g)
    weights = {n: given[n] for n in TWIN_WEIGHTS}
    shared = {n: given[n] for n in SHARED_INPUTS}
    per_example = {n: given[n] for n in ['x']}
    grad_fn = _jax.value_and_grad(_loss, argnums=(0, 1))

    def one_microbatch(ex, loss_target):
        ex = dict(ex)
        diff = ex.pop(TWIN_DIFF_INPUT)
        return grad_fn(weights, diff, {**shared, **ex}, loss_target)

    if N_MICROBATCH == 1:
        loss, (grad_w, grad_x) = one_microbatch(per_example, given["loss_target"])
    else:
        def body(carry, xs):
            loss_sum, grad_sum = carry
            l_k, (gw_k, gx_k) = one_microbatch(xs[0], xs[1])
            with _jax.named_scope("update"):
                return (loss_sum + l_k, _jax.tree.map(_jnp.add, grad_sum, gw_k)), gx_k

        init = (_jnp.zeros((), _jnp.float32), _jax.tree.map(_jnp.zeros_like, weights))
        (loss, grad_w), grad_x = _jax.lax.scan(body, init, (per_example, given["loss_target"]))
    with _jax.named_scope("update"):
        delta_w, new_m, new_v = {}, {}, {}
        for n in TWIN_WEIGHTS:
            delta_w[n], new_m[n], new_v[n] = _adamw(weights[n], grad_w[n], given["m_" + n], given["v_" + n])
    return (loss, grad_x, *[grad_w[n] for n in TWIN_WEIGHTS], *[delta_w[n] for n in TWIN_WEIGHTS],
            *[new_m[n] for n in TWIN_WEIGHTS], *[new_v[n] for n in TWIN_WEIGHTS])
```

```python
import jax
import jax.numpy as jnp
from jax import lax
from jax.experimental import pallas as pl
from jax.experimental.pallas import tpu as pltpu

F32 = jnp.float32
BF16 = jnp.bfloat16

D_MODEL = 1024
LRU_W = 512
SB_W = 512
HEAD_D = 64
D_FF = 4096
IN_COLS = 2 * LRU_W + 3 * SB_W
CONV_K = 4
LRU_BLOCKS = 8
LRU_C = 8.0
EPS = 1e-6
N_DEV = 8
LANES = 128
SUBLANES = 8
FF_CHUNK = 512
Q_BLK = 128

ADAM_LR = 0.001
ADAM_B1 = 0.9
ADAM_B2 = 0.999
ADAM_EPS = 1e-08
ADAM_WD = 0.01
ADAM_STEP = 10

SMALL = (("norm1_g", 1024), ("conv_b", 512), ("lru_w_a", 32768), ("lru_b_a", 512), ("lru_w_x", 32768),
         ("lru_b_x", 512), ("lru_lambda", 512), ("lru_out_g", 512), ("sb_out_g", 512), ("norm2_g", 1024),
         ("final_g", 1024))
SMALL_ROWS = sum(n for _, n in SMALL) // LANES
EXCH_ROWS = SMALL_ROWS + (CONV_K * LRU_W) // LANES + 8
ADAM_ROWS = SMALL_ROWS + 8


def _params(sem=None, vmem_mb=None):
    kw = {}
    if sem is not None:
        kw["dimension_semantics"] = sem
    if vmem_mb is not None:
        kw["vmem_limit_bytes"] = vmem_mb << 20
    return pltpu.CompilerParams(**kw)


def _dot(a, b):
    return jnp.dot(a, b, preferred_element_type=F32)


def _dot_nt(a, b):
    return lax.dot_general(a, b, (((1,), (1,)), ((), ())), preferred_element_type=F32)


def _dot_tn(a, b):
    return lax.dot_general(a, b, (((0,), (0,)), ((), ())), preferred_element_type=F32)


def _rms_fwd(x, g):
    rstd = lax.rsqrt(jnp.mean(x * x, axis=-1, keepdims=True) + EPS)
    xhat = x * rstd
    return xhat * g, xhat, rstd


def _rms_bwd(dy, xhat, rstd, g):
    dxhat = dy * g
    return rstd * (dxhat - xhat * jnp.mean(dxhat * xhat, axis=-1, keepdims=True))


def _sigmoid(x):
    return 1.0 / (1.0 + jnp.exp(-x))


def _log1p_pos(e):
    series = e * (1.0 - e * (0.5 - e * (1.0 / 3.0 - e * 0.25)))
    return jnp.where(e < 1e-2, series, jnp.log(1.0 + e))


def _neg_expm1(x):
    series = -x * (1.0 + x * (0.5 + x * (1.0 / 6.0 + x * (1.0 / 24.0))))
    return jnp.where(x > -1e-2, series, 1.0 - jnp.exp(x))


def _gelu_parts(g):
    k0 = 0.7978845608028654
    k1 = 0.044715
    t = jnp.tanh(k0 * (g + k1 * g * g * g))
    val = 0.5 * g * (1.0 + t)
    grad = 0.5 * (1.0 + t) + 0.5 * g * (1.0 - t * t) * k0 * (1.0 + 3.0 * k1 * g * g)
    return val, grad


def _scan(a, b, reverse):
    n = a.shape[0]
    row = lax.broadcasted_iota(jnp.int32, a.shape, 0)
    s = 1
    while s < n:
        if reverse:
            keep = row < n - s
            shift = n - s
        else:
            keep = row >= s
            shift = s
        bs = jnp.where(keep, pltpu.roll(b, shift, 0), 0.0)
        a_s = jnp.where(keep, pltpu.roll(a, shift, 0), 1.0)
        b = a * bs + b
        a = a * a_s
        s *= 2
    return b, a


def _adamw(w, g, m, v):
    m = ADAM_B1 * m + (1.0 - ADAM_B1) * g
    v = ADAM_B2 * v + (1.0 - ADAM_B2) * (g * g)
    m_hat = m / (1.0 - ADAM_B1 ** ADAM_STEP)
    v_hat = v / (1.0 - ADAM_B2 ** ADAM_STEP)
    delta = -ADAM_LR * (m_hat / (jnp.sqrt(v_hat) + ADAM_EPS) + ADAM_WD * w)
    return delta, m, v


def _my_index():
    return 4 * lax.axis_index("x") + 2 * lax.axis_index("y") + lax.axis_index("c")


def _peer(k):
    x, y, c = lax.axis_index("x"), lax.axis_index("y"), lax.axis_index("c")
    px = 1 - x if (k >> 2) & 1 else x
    py = 1 - y if (k >> 1) & 1 else y
    pc = 1 - c if k & 1 else c
    return (px, py, pc), 4 * px + 2 * py + pc


def _exchange(name, srcs, sliced):
    n = len(srcs)
    out_shape = [jax.ShapeDtypeStruct(s.shape if sl else (N_DEV,) + s.shape, s.dtype) for s, sl in zip(srcs, sliced)]

    def body(*refs):
        ins, outs = refs[:n], refs[n:2 * n]
        send_sems, recv_sems, local_sems = refs[2 * n:]
        me = _my_index()

        def part(a, p):
            return ins[a].at[p] if sliced[a] else ins[a]

        local = [pltpu.make_async_copy(part(a, me), outs[a].at[me], local_sems.at[a]) for a in range(n)]
        for cp in local:
            cp.start()
        for k in range(1, N_DEV):
            dev, idx = _peer(k)
            for a in range(n):
                pltpu.make_async_remote_copy(
                    src_ref=part(a, idx), dst_ref=outs[a].at[me], send_sem=send_sems.at[a, k - 1],
                    recv_sem=recv_sems.at[a, k - 1], device_id=dev, device_id_type=pl.DeviceIdType.MESH).start()
        for k in range(1, N_DEV):
            dev, idx = _peer(k)
            for a in range(n):
                pltpu.make_async_remote_copy(
                    src_ref=part(a, idx), dst_ref=outs[a].at[idx], send_sem=send_sems.at[a, k - 1],
                    recv_sem=recv_sems.at[a, k - 1], device_id=dev, device_id_type=pl.DeviceIdType.MESH).wait()
        for cp in local:
            cp.wait()

    return pl.pallas_call(
        body, name=name, out_shape=out_shape,
        in_specs=[pl.BlockSpec(memory_space=pl.ANY)] * n,
        out_specs=[pl.BlockSpec(memory_space=pl.ANY)] * n,
        scratch_shapes=[pltpu.SemaphoreType.DMA((n, N_DEV - 1)), pltpu.SemaphoreType.DMA((n, N_DEV - 1)),
                        pltpu.SemaphoreType.DMA((n,))],
    )(*srcs)


def _cast_shards(ws):
    def body(*refs):
        for i in range(len(ws)):
            refs[len(ws) + i][...] = refs[i][...].astype(BF16)

    return pl.pallas_call(
        body, name="cast_shards", out_shape=[jax.ShapeDtypeStruct(w.shape, BF16) for w in ws],
        compiler_params=_params(vmem_mb=32),
    )(*ws)


def _fwd_in(x2, g1, w_in16, tm):
    T = x2.shape[0]

    def body(x_ref, g_ref, w_ref, lru_ref, qkv_ref):
        xn, _, _ = _rms_fwd(x_ref[...], g_ref[...])
        xn = xn.astype(BF16)
        lru_ref[...] = _dot(xn, w_ref[:, 0:2 * LRU_W])
        qkv_ref[...] = _dot(xn, w_ref[:, 2 * LRU_W:IN_COLS]).astype(BF16)

    return pl.pallas_call(
        body, name="fwd_in", grid=(T // tm,),
        in_specs=[pl.BlockSpec((tm, D_MODEL), lambda i: (i, 0)),
                  pl.BlockSpec((1, D_MODEL), lambda i: (0, 0)),
                  pl.BlockSpec((D_MODEL, IN_COLS), lambda i: (0, 0))],
        out_specs=[pl.BlockSpec((tm, 2 * LRU_W), lambda i: (i, 0)),
                   pl.BlockSpec((tm, 3 * SB_W), lambda i: (i, 0))],
        out_shape=[jax.ShapeDtypeStruct((T, 2 * LRU_W), F32), jax.ShapeDtypeStruct((T, 3 * SB_W), BF16)],
        compiler_params=_params(("arbitrary",), 48),
    )(x2, g1, w_in16)


def _lru_gates(c, wa_ref, wx_ref, ba_ref, bx_ref, lam_ref):
    c16 = c.astype(BF16)
    r = _sigmoid(_dot(c16, wa_ref[0]) + ba_ref[...])
    i = _sigmoid(_dot(c16, wx_ref[0]) + bx_ref[...])
    lam = lam_ref[...]
    e = jnp.exp(-jnp.abs(lam))
    sp = jnp.maximum(-lam, 0.0) + _log1p_pos(e)
    dsp_dlam = -jnp.where(lam >= 0.0, e, 1.0) / (1.0 + e)
    log_a = (-LRU_C) * r * sp
    a = jnp.exp(log_a)
    s = jnp.sqrt(_neg_expm1(2.0 * log_a))
    return r, i, sp, dsp_dlam, a, s


def _conv_taps(x, halo, lc):
    xe = jnp.concatenate([halo, x], axis=0)
    return [x] + [pltpu.roll(xe, k, 0)[SUBLANES:SUBLANES + lc] for k in range(1, CONV_K)]


def _lru_fwd(proj_lru, conv_w, conv_b, wa_bd, wx_bd, b_a, b_x, lam, B, S, lc):
    T = B * S
    nc = S // lc
    ncb = LRU_W // LANES

    def body(x_ref, g_ref, cw_ref, cb_ref, wa_ref, wx_ref, ba_ref, bx_ref, lam_ref, y_ref, h_ref, tail, carry):
        ci = pl.program_id(2)

        @pl.when(ci == 0)
        def _():
            tail[...] = jnp.zeros_like(tail)
            carry[...] = jnp.zeros_like(carry)

        x = x_ref[...]
        taps = _conv_taps(x, tail[...], lc)
        c = cb_ref[...] + sum(cw_ref[pl.ds(CONV_K - 1 - k, 1), :] * taps[k] for k in range(CONV_K))
        tail[...] = x_ref[pl.ds(lc - SUBLANES, SUBLANES), :]
        r, i, sp, _, a, s = _lru_gates(c, wa_ref, wx_ref, ba_ref, bx_ref, lam_ref)
        h_loc, a_run = _scan(a, s * (i * c), reverse=False)
        h_ref[...] = h_loc + a_run * carry[...]
        carry[...] = h_ref[pl.ds(lc - 1, 1), :]
        gelu, _ = _gelu_parts(g_ref[...])
        y_ref[...] = h_ref[...] * gelu

    chan = lambda b, cb, ci: (0, cb)
    return pl.pallas_call(
        body, name="lru_fwd", grid=(B, ncb, nc),
        in_specs=[pl.BlockSpec((lc, LANES), lambda b, cb, ci: (b * nc + ci, cb)),
                  pl.BlockSpec((lc, LANES), lambda b, cb, ci: (b * nc + ci, ncb + cb)),
                  pl.BlockSpec((CONV_K, LANES), chan), pl.BlockSpec((1, LANES), chan),
                  pl.BlockSpec((1, LANES, LANES), lambda b, cb, ci: (cb, 0, 0)),
                  pl.BlockSpec((1, LANES, LANES), lambda b, cb, ci: (cb, 0, 0)),
                  pl.BlockSpec((1, LANES), chan), pl.BlockSpec((1, LANES), chan), pl.BlockSpec((1, LANES), chan)],
        out_specs=[pl.BlockSpec((lc, LANES), lambda b, cb, ci: (b * nc + ci, cb))] * 2,
        out_shape=[jax.ShapeDtypeStruct((T, LRU_W), F32)] * 2,
        scratch_shapes=[pltpu.VMEM((SUBLANES, LANES), F32), pltpu.VMEM((1, LANES), F32)],
        compiler_params=_params(("arbitrary", "arbitrary", "arbitrary"), 32),
    )(proj_lru, proj_lru, conv_w, conv_b, wa_bd, wx_bd, b_a, b_x, lam)


def _cumsum_mm(v, tri):
    hi = v.astype(BF16)
    lo = (v - hi.astype(F32)).astype(BF16)
    return _dot(hi, tri) + _dot(lo, tri)


def _attn_consts():
    r = lax.broadcasted_iota(jnp.int32, (Q_BLK, Q_BLK), 0)
    c = lax.broadcasted_iota(jnp.int32, (Q_BLK, Q_BLK), 1)
    causal = c < r
    tri_suffix = (r >= c).astype(BF16)
    tri_prefix = (r <= c).astype(BF16)
    lane = lax.broadcasted_iota(jnp.int32, (1, LANES), 1)
    return causal, tri_suffix, tri_prefix, lane, (lane < HEAD_D, lane >= HEAD_D)


def _attn_scores(qh, ks, causal, tri_suffix, run, masked):
    z = _dot_nt(qh, ks)
    e = jnp.exp(-jnp.abs(z))
    lg = -(jnp.maximum(z, 0.0) + jnp.log(1.0 + e))
    if masked:
        lg = jnp.where(causal, lg, 0.0)
    suffix = _cumsum_mm(lg, tri_suffix)
    att = jnp.exp(z + suffix + run)
    if masked:
        att = jnp.where(causal, att, 0.0)
    return z, lg, suffix, att


def _attn_fwd(qkv, B, S):
    T = B * S
    nq = S // Q_BLK
    nhp = SB_W // LANES
    scale = HEAD_D ** -0.5
    assert nq <= LANES

    def body(q_ref, k_ref, v_ref, o_ref, run_ref):
        qi = pl.program_id(2)
        causal, tri_suffix, _, lane, halves = _attn_consts()
        q = q_ref[...]
        out = jnp.zeros((Q_BLK, LANES), F32)
        for hd, hm in enumerate(halves):
            qh = jnp.where(hm, q, jnp.zeros_like(q)) * jnp.asarray(scale, BF16)

            def tile(kb, run, acc, table, masked):
                rows = pl.ds(pl.multiple_of(kb * Q_BLK, Q_BLK), Q_BLK)
                vs = v_ref[rows, :]
                _, _, suffix, att = _attn_scores(qh, k_ref[rows, :], causal, tri_suffix, run, masked)
                acc = acc + _dot(att.astype(BF16), jnp.where(hm, vs, jnp.zeros_like(vs)))
                return run + suffix[:, 0:1], acc, jnp.where(lane == kb, run, table)

            zero = jnp.zeros((Q_BLK, LANES), F32)
            carry = tile(qi, jnp.zeros((Q_BLK, 1), F32), zero, zero, True)
            carry = lax.fori_loop(0, qi, lambda it, cr: tile(qi - 1 - it, cr[0], cr[1], cr[2], False), carry)
            out = out + carry[1]
            run_ref[:, hd * LANES:(hd + 1) * LANES] = carry[2]
        o_ref[...] = out

    return pl.pallas_call(
        body, name="attn_fwd", grid=(B, nhp, nq),
        in_specs=[pl.BlockSpec((Q_BLK, LANES), lambda b, hp, qi: (b * nq + qi, hp)),
                  pl.BlockSpec((S, LANES), lambda b, hp, qi: (b, nhp + hp)),
                  pl.BlockSpec((S, LANES), lambda b, hp, qi: (b, 2 * nhp + hp))],
        out_specs=[pl.BlockSpec((Q_BLK, LANES), lambda b, hp, qi: (b * nq + qi, hp)),
                   pl.BlockSpec((Q_BLK, 2 * LANES), lambda b, hp, qi: (b * nq + qi, hp))],
        out_shape=[jax.ShapeDtypeStruct((T, SB_W), F32), jax.ShapeDtypeStruct((T, 2 * SB_W), F32)],
        compiler_params=_params(("arbitrary", "arbitrary", "arbitrary"), 32),
    )(qkv, qkv, qkv)


def _fwd_mix(y_lru, y_sb, ga, gb, w_out16, x2, tm):
    T = x2.shape[0]

    def body(yl_ref, ys_ref, ga_ref, gb_ref, w_ref, x_ref, h1_ref, mix_ref):
        na, _, _ = _rms_fwd(yl_ref[...], ga_ref[...])
        nb, _, _ = _rms_fwd(ys_ref[...], gb_ref[...])
        na = na.astype(BF16)
        nb = nb.astype(BF16)
        mix_ref[:, 0:LRU_W] = na
        mix_ref[:, LRU_W:D_MODEL] = nb
        h1_ref[...] = x_ref[...] + _dot(na, w_ref[0:LRU_W, :]) + _dot(nb, w_ref[LRU_W:D_MODEL, :])

    row = lambda i: (i, 0)
    fix = lambda i: (0, 0)
    return pl.pallas_call(
        body, name="fwd_mix", grid=(T // tm,),
        in_specs=[pl.BlockSpec((tm, LRU_W), row), pl.BlockSpec((tm, SB_W), row),
                  pl.BlockSpec((1, LRU_W), fix), pl.BlockSpec((1, SB_W), fix),
                  pl.BlockSpec((D_MODEL, D_MODEL), fix), pl.BlockSpec((tm, D_MODEL), row)],
        out_specs=[pl.BlockSpec((tm, D_MODEL), row), pl.BlockSpec((tm, D_MODEL), row)],
        out_shape=[jax.ShapeDtypeStruct((T, D_MODEL), F32), jax.ShapeDtypeStruct((T, D_MODEL), BF16)],
        compiler_params=_params(("arbitrary",), 48),
    )(y_lru, y_sb, ga, gb, w_out16, x2)


def _fwd_mlp(h1, g2, w_up16, w_down16, gf, tgt, tm):
    T = h1.shape[0]
    nf = D_FF // FF_CHUNK

    def body(h1_ref, g2_ref, wu_ref, wd_ref, gf_ref, t_ref, up_ref, dh2_ref, dgf_ref, loss_ref, hn_s, acc):
        i, j = pl.program_id(0), pl.program_id(1)

        @pl.when(j == 0)
        def _():
            h1v = h1_ref[...]
            hn, _, _ = _rms_fwd(h1v, g2_ref[...])
            hn_s[...] = hn.astype(BF16)
            acc[...] = h1v

        up = jnp.maximum(_dot(hn_s[...], wu_ref[...]), 0.0)
        up_ref[...] = up.astype(BF16)
        acc[...] += _dot((up * up).astype(BF16), wd_ref[...])

        @pl.when((i == 0) & (j == 0))
        def _():
            dgf_ref[...] = jnp.zeros_like(dgf_ref)
            loss_ref[...] = jnp.zeros_like(loss_ref)

        @pl.when(j == nf - 1)
        def _():
            gfv = gf_ref[...]
            y, xhat, rstd = _rms_fwd(acc[...], gfv)
            err = y - t_ref[...]
            loss_ref[...] += jnp.sum(0.5 * jnp.sum(err * err, axis=-1, keepdims=True) * (1.0 / D_MODEL))
            dy = err * (1.0 / D_MODEL)
            dgf_ref[...] += jnp.sum(dy * xhat, axis=0, keepdims=True)
            dh2_ref[...] = _rms_bwd(dy, xhat, rstd, gfv)

    row = lambda i, j: (i, 0)
    fix = lambda i, j: (0, 0)
    return pl.pallas_call(
        body, name="fwd_mlp", grid=(T // tm, nf),
        in_specs=[pl.BlockSpec((tm, D_MODEL), row), pl.BlockSpec((1, D_MODEL), fix),
                  pl.BlockSpec((None, D_MODEL, FF_CHUNK), lambda i, j: (j, 0, 0)),
                  pl.BlockSpec((FF_CHUNK, D_MODEL), lambda i, j: (j, 0)),
                  pl.BlockSpec((1, D_MODEL), fix), pl.BlockSpec((tm, D_MODEL), row)],
        out_specs=[pl.BlockSpec((tm, FF_CHUNK), lambda i, j: (i, j)), pl.BlockSpec((tm, D_MODEL), row),
                   pl.BlockSpec((1, D_MODEL), fix), pl.BlockSpec((1, LANES), fix)],
        out_shape=[jax.ShapeDtypeStruct((T, D_FF), BF16), jax.ShapeDtypeStruct((T, D_MODEL), F32),
                   jax.ShapeDtypeStruct((1, D_MODEL), F32), jax.ShapeDtypeStruct((1, LANES), F32)],
        scratch_shapes=[pltpu.VMEM((tm, D_MODEL), BF16), pltpu.VMEM((tm, D_MODEL), F32)],
        compiler_params=_params(("arbitrary", "arbitrary"), 48),
    )(h1, g2, w_up16, w_down16, gf, tgt)


def _bwd_mlp(dh2, up16, h1, g2, w_up16, w_down16, tm):
    T = h1.shape[0]
    nf = D_FF // FF_CHUNK

    def body(dh2_ref, up_ref, h1_ref, g2_ref, wu_ref, wd_ref, dup_ref, dh1_ref, hn_ref, dh2b_ref, dg2_ref, acc):
        i, j = pl.program_id(0), pl.program_id(1)

        @pl.when(j == 0)
        def _():
            hn, _, _ = _rms_fwd(h1_ref[...], g2_ref[...])
            hn_ref[...] = hn.astype(BF16)
            dh2b_ref[...] = dh2_ref[...].astype(BF16)
            acc[...] = jnp.zeros_like(acc)

        u = up_ref[...].astype(F32)
        dup = (2.0 * u * _dot_nt(dh2b_ref[...], wd_ref[...])).astype(BF16)
        dup_ref[...] = dup
        acc[...] += _dot_nt(dup, wu_ref[...])

        @pl.when((i == 0) & (j == 0))
        def _():
            dg2_ref[...] = jnp.zeros_like(dg2_ref)

        @pl.when(j == nf - 1)
        def _():
            g2v = g2_ref[...]
            _, xhat, rstd = _rms_fwd(h1_ref[...], g2v)
            dhn = acc[...]
            dg2_ref[...] += jnp.sum(dhn * xhat, axis=0, keepdims=True)
            dh1_ref[...] = dh2_ref[...] + _rms_bwd(dhn, xhat, rstd, g2v)

    row = lambda i, j: (i, 0)
    fix = lambda i, j: (0, 0)
    return pl.pallas_call(
        body, name="bwd_mlp", grid=(T // tm, nf),
        in_specs=[pl.BlockSpec((tm, D_MODEL), row), pl.BlockSpec((tm, FF_CHUNK), lambda i, j: (i, j)),
                  pl.BlockSpec((tm, D_MODEL), row), pl.BlockSpec((1, D_MODEL), fix),
                  pl.BlockSpec((None, D_MODEL, FF_CHUNK), lambda i, j: (j, 0, 0)),
                  pl.BlockSpec((FF_CHUNK, D_MODEL), lambda i, j: (j, 0))],
        out_specs=[pl.BlockSpec((tm, FF_CHUNK), lambda i, j: (i, j)), pl.BlockSpec((tm, D_MODEL), row),
                   pl.BlockSpec((tm, D_MODEL), row), pl.BlockSpec((tm, D_MODEL), row),
                   pl.BlockSpec((1, D_MODEL), fix)],
        out_shape=[jax.ShapeDtypeStruct((T, D_FF), BF16), jax.ShapeDtypeStruct((T, D_MODEL), F32),
                   jax.ShapeDtypeStruct((T, D_MODEL), BF16), jax.ShapeDtypeStruct((T, D_MODEL), BF16),
                   jax.ShapeDtypeStruct((1, D_MODEL), F32)],
        scratch_shapes=[pltpu.VMEM((tm, D_MODEL), F32)],
        compiler_params=_params(("arbitrary", "arbitrary"), 48),
    )(dh2, up16, h1, g2, w_up16, w_down16)


def _matmul_tn(name, a, b, bm, bn, tk, out_shape, out_block, out_index, a_prep=None, b_prep=None):
    T, M = a.shape
    N = b.shape[1]

    def body(a_ref, b_ref, o_ref):
        k = pl.program_id(2)
        av = a_ref[...] if a_prep is None else a_prep(a_ref[...])
        bv = b_ref[...] if b_prep is None else b_prep(b_ref[...])
        p = _dot_tn(av, bv)

        @pl.when(k == 0)
        def _():
            o_ref[...] = p

        @pl.when(k > 0)
        def _():
            o_ref[...] += p

    return pl.pallas_call(
        body, name=name, grid=(M // bm, N // bn, T // tk),
        in_specs=[pl.BlockSpec((tk, bm), lambda m, n, k: (k, m)), pl.BlockSpec((tk, bn), lambda m, n, k: (k, n))],
        out_specs=pl.BlockSpec(out_block, out_index),
        out_shape=jax.ShapeDtypeStruct(out_shape, F32),
        compiler_params=_params(("arbitrary", "arbitrary", "arbitrary"), 48),
    )(a, b)


def _bwd_mix(dh1, w_out16, y_lru, y_sb, ga, gb, tm):
    T = dh1.shape[0]

    def body(d_ref, w_ref, yl_ref, ys_ref, ga_ref, gb_ref, dyl_ref, dys_ref, dga_ref, dgb_ref):
        @pl.when(pl.program_id(0) == 0)
        def _():
            dga_ref[...] = jnp.zeros_like(dga_ref)
            dgb_ref[...] = jnp.zeros_like(dgb_ref)

        d16 = d_ref[...].astype(BF16)
        for y_ref, g_ref, lo, dy_ref, dg_ref in ((yl_ref, ga_ref, 0, dyl_ref, dga_ref),
                                                 (ys_ref, gb_ref, LRU_W, dys_ref, dgb_ref)):
            gv = g_ref[...]
            dn = _dot_nt(d16, w_ref[lo:lo + LRU_W, :])
            _, xhat, rstd = _rms_fwd(y_ref[...], gv)
            dg_ref[...] += jnp.sum(dn * xhat, axis=0, keepdims=True)
            dy_ref[...] = _rms_bwd(dn, xhat, rstd, gv)

    row = lambda i: (i, 0)
    fix = lambda i: (0, 0)
    return pl.pallas_call(
        body, name="bwd_mix", grid=(T // tm,),
        in_specs=[pl.BlockSpec((tm, D_MODEL), row), pl.BlockSpec((D_MODEL, D_MODEL), fix),
                  pl.BlockSpec((tm, LRU_W), row), pl.BlockSpec((tm, SB_W), row),
                  pl.BlockSpec((1, LRU_W), fix), pl.BlockSpec((1, SB_W), fix)],
        out_specs=[pl.BlockSpec((tm, LRU_W), row), pl.BlockSpec((tm, SB_W), row),
                   pl.BlockSpec((1, LRU_W), fix), pl.BlockSpec((1, SB_W), fix)],
        out_shape=[jax.ShapeDtypeStruct((T, LRU_W), F32), jax.ShapeDtypeStruct((T, SB_W), F32),
                   jax.ShapeDtypeStruct((1, LRU_W), F32), jax.ShapeDtypeStruct((1, SB_W), F32)],
        compiler_params=_params(("arbitrary",), 48),
    )(dh1, w_out16, y_lru, y_sb, ga, gb)


def _attn_bwd(qkv, run_tab, dy_sb, B, S):
    T = B * S
    nq = S // Q_BLK
    nhp = SB_W // LANES
    scale = HEAD_D ** -0.5

    def body(q_ref, k_ref, v_ref, run_ref, do_ref, dq_ref, dk_ref, dv_ref):
        qi = pl.program_id(2)

        @pl.when(qi == 0)
        def _():
            dk_ref[...] = jnp.zeros_like(dk_ref)
            dv_ref[...] = jnp.zeros_like(dv_ref)

        causal, tri_suffix, tri_prefix, lane, halves = _attn_consts()
        q = q_ref[...]
        do = do_ref[...]
        dq_out = jnp.zeros((Q_BLK, LANES), F32)
        for hd, hm in enumerate(halves):
            qh = jnp.where(hm, q, jnp.zeros_like(q)) * jnp.asarray(scale, BF16)
            doh16 = jnp.where(hm, do, 0.0).astype(BF16)
            table = run_ref[:, hd * LANES:(hd + 1) * LANES]

            def tile(kb, prefix, dq, masked):
                rows = pl.ds(pl.multiple_of(kb * Q_BLK, Q_BLK), Q_BLK)
                ks = k_ref[rows, :]
                run = jnp.sum(jnp.where(lane == kb, table, 0.0), axis=1, keepdims=True)
                z, lg, _, att = _attn_scores(qh, ks, causal, tri_suffix, run, masked)
                g = att * _dot_nt(doh16, v_ref[rows, :])
                gpre = _cumsum_mm(g, tri_prefix)
                dz = g - jnp.exp(z + lg) * (prefix + gpre)
                if masked:
                    dz = jnp.where(causal, dz, 0.0)
                dz16 = dz.astype(BF16)
                dq = dq + _dot(dz16, jnp.where(hm, ks, jnp.zeros_like(ks)))
                dk_ref[rows, :] += _dot_tn(dz16, qh)
                dv_ref[rows, :] += _dot_tn(att.astype(BF16), doh16)
                return prefix + gpre[:, Q_BLK - 1:Q_BLK], dq

            carry = (jnp.zeros((Q_BLK, 1), F32), jnp.zeros((Q_BLK, LANES), F32))
            carry = lax.fori_loop(0, qi, lambda kb, cr: tile(kb, cr[0], cr[1], False), carry)
            dq_out = dq_out + tile(qi, carry[0], carry[1], True)[1]
        dq_ref[...] = dq_out * scale

    qblk = pl.BlockSpec((Q_BLK, LANES), lambda b, hp, qi: (b * nq + qi, hp))
    seq = pl.BlockSpec((S, LANES), lambda b, hp, qi: (b, hp))
    return pl.pallas_call(
        body, name="attn_bwd", grid=(B, nhp, nq),
        in_specs=[qblk, pl.BlockSpec((S, LANES), lambda b, hp, qi: (b, nhp + hp)),
                  pl.BlockSpec((S, LANES), lambda b, hp, qi: (b, 2 * nhp + hp)),
                  pl.BlockSpec((Q_BLK, 2 * LANES), lambda b, hp, qi: (b * nq + qi, hp)), qblk],
        out_specs=[qblk, seq, seq],
        out_shape=[jax.ShapeDtypeStruct((T, SB_W), F32)] * 3,
        compiler_params=_params(("arbitrary", "arbitrary", "arbitrary"), 32),
    )(qkv, qkv, qkv, run_tab, dy_sb)


def _lru_bwd(proj_lru, h, dy_lru, conv_w, conv_b, wa_bd, wx_bd, b_a, b_x, lam, B, S, lc):
    T = B * S
    nc = S // lc
    ncb = LRU_W // LANES
    hpc = lc // SUBLANES

    def body(x_ref, xh_ref, g_ref, h_ref, hh_ref, dy_ref, cw_ref, cb_ref, wa_ref, wx_ref, ba_ref, bx_ref, lam_ref,
             dx_ref, dg_ref, dcw_ref, dcb_ref, dwa_ref, dwx_ref, dba_ref, dbx_ref, dlam_ref,
             lam_s, dc_s, a_first, lam_first, dc_head):
        b, ci = pl.program_id(1), pl.program_id(2)
        first_chunk = ci == nc - 1

        @pl.when(ci == 0)
        def _():
            a_first[...] = jnp.zeros_like(a_first)
            lam_first[...] = jnp.zeros_like(lam_first)
            dc_head[...] = jnp.zeros_like(dc_head)

        @pl.when((b == 0) & (ci == 0))
        def _():
            for ref in (dcw_ref, dcb_ref, dwa_ref, dwx_ref, dba_ref, dbx_ref, dlam_ref):
                ref[...] = jnp.zeros_like(ref)

        x = x_ref[...]
        taps = _conv_taps(x, jnp.where(first_chunk, 0.0, xh_ref[...]), lc)
        c = cb_ref[...] + sum(cw_ref[pl.ds(CONV_K - 1 - k, 1), :] * taps[k] for k in range(CONV_K))
        r, i, sp, dsp_dlam, a, s = _lru_gates(c, wa_ref, wx_ref, ba_ref, bx_ref, lam_ref)
        hv = h_ref[...]
        he = jnp.concatenate([jnp.where(first_chunk, 0.0, hh_ref[...]), hv], axis=0)
        h_prev = pltpu.roll(he, 1, 0)[SUBLANES:SUBLANES + lc]
        dy = dy_ref[...]
        gelu, dgelu = _gelu_parts(g_ref[...])
        dg_ref[...] = dy * hv * dgelu

        row = lax.broadcasted_iota(jnp.int32, (lc, LANES), 0)
        a_next = jnp.where(row < lc - 1, pltpu.roll(a, lc - 1, 0), a_first[...])
        lam_loc, a_run = _scan(a_next, dy * gelu, reverse=True)
        lam_s[...] = lam_loc + a_run * lam_first[...]
        lam_t = lam_s[...]
        lam_first[...] = lam_s[pl.ds(0, 1), :]
        lam_s[...] = a
        a_first[...] = lam_s[pl.ds(0, 1), :]

        ic = i * c
        dlog_a = lam_t * h_prev * a - (lam_t * ic) * (a * a) / s
        dpre_r = (dlog_a * ((-LRU_C) * sp)) * r * (1.0 - r)
        dpre_i = (lam_t * s * c) * i * (1.0 - i)
        dlam_ref[...] += jnp.sum(dlog_a * r, axis=0, keepdims=True) * ((-LRU_C) * dsp_dlam)
        dr16 = dpre_r.astype(BF16)
        di16 = dpre_i.astype(BF16)
        c16 = c.astype(BF16)
        dwa_ref[0] += _dot_tn(c16, dr16)
        dwx_ref[0] += _dot_tn(c16, di16)
        dba_ref[...] += jnp.sum(dpre_r, axis=0, keepdims=True)
        dbx_ref[...] += jnp.sum(dpre_i, axis=0, keepdims=True)
        dc = lam_t * s * i + _dot_nt(dr16, wa_ref[0]) + _dot_nt(di16, wx_ref[0])
        dcb_ref[...] += jnp.sum(dc, axis=0, keepdims=True)
        for k in range(CONV_K):
            dcw_ref[pl.ds(CONV_K - 1 - k, 1), :] += jnp.sum(dc * taps[k], axis=0, keepdims=True)
        dce = jnp.concatenate([dc, dc_head[...]], axis=0)
        dx = cw_ref[pl.ds(CONV_K - 1, 1), :] * dc
        for k in range(1, CONV_K):
            dx = dx + cw_ref[pl.ds(CONV_K - 1 - k, 1), :] * pltpu.roll(dce, lc + SUBLANES - k, 0)[0:lc]
        dx_ref[...] = dx
        dc_s[...] = dc
        dc_head[...] = dc_s[pl.ds(0, SUBLANES), :]

    def chunk(col):
        return pl.BlockSpec((lc, LANES), lambda cb, b, ci: (b * nc + nc - 1 - ci, col(cb)))

    def halo(col):
        return pl.BlockSpec((SUBLANES, LANES),
                            lambda cb, b, ci: (jnp.maximum((b * nc + nc - 1 - ci) * hpc - 1, 0), col(cb)))

    chan = lambda cb, b, ci: (0, cb)
    blk = lambda cb, b, ci: (cb, 0, 0)
    vec = pl.BlockSpec((1, LANES), chan)
    mat = pl.BlockSpec((1, LANES, LANES), blk)
    return pl.pallas_call(
        body, name="lru_bwd", grid=(ncb, B, nc),
        in_specs=[chunk(lambda cb: cb), halo(lambda cb: cb), chunk(lambda cb: ncb + cb),
                  chunk(lambda cb: cb), halo(lambda cb: cb), chunk(lambda cb: cb),
                  pl.BlockSpec((CONV_K, LANES), chan), vec, mat, mat, vec, vec, vec],
        out_specs=[chunk(lambda cb: cb), chunk(lambda cb: cb), pl.BlockSpec((CONV_K, LANES), chan), vec,
                   mat, mat, vec, vec, vec],
        out_shape=[jax.ShapeDtypeStruct((T, LRU_W), F32), jax.ShapeDtypeStruct((T, LRU_W), F32),
                   jax.ShapeDtypeStruct((CONV_K, LRU_W), F32), jax.ShapeDtypeStruct((1, LRU_W), F32),
                   jax.ShapeDtypeStruct((ncb, LANES, LANES), F32), jax.ShapeDtypeStruct((ncb, LANES, LANES), F32),
                   jax.ShapeDtypeStruct((1, LRU_W), F32), jax.ShapeDtypeStruct((1, LRU_W), F32),
                   jax.ShapeDtypeStruct((1, LRU_W), F32)],
        scratch_shapes=[pltpu.VMEM((lc, LANES), F32), pltpu.VMEM((lc, LANES), F32), pltpu.VMEM((1, LANES), F32),
                        pltpu.VMEM((1, LANES), F32), pltpu.VMEM((SUBLANES, LANES), F32)],
        compiler_params=_params(("arbitrary", "arbitrary", "arbitrary"), 32),
    )(proj_lru, proj_lru, proj_lru, h, h, dy_lru, conv_w, conv_b, wa_bd, wx_bd, b_a, b_x, lam)


def _bwd_in(pieces, w_in16, x2, g1, dh1, tm):
    T = x2.shape[0]
    npc = len(pieces)

    def body(*refs):
        p_refs = refs[:npc]
        w_ref, x_ref, g_ref, d_ref, dx_ref, dproj_ref, xn_ref, dg1_ref = refs[npc:]

        @pl.when(pl.program_id(0) == 0)
        def _():
            dg1_ref[...] = jnp.zeros_like(dg1_ref)

        dxn = jnp.zeros((tm, D_MODEL), F32)
        for n, p_ref in enumerate(p_refs):
            cols = slice(n * LRU_W, (n + 1) * LRU_W)
            p16 = p_ref[...].astype(BF16)
            dproj_ref[:, cols] = p16
            dxn = dxn + _dot_nt(p16, w_ref[:, cols])
        gv = g_ref[...]
        xn, xhat, rstd = _rms_fwd(x_ref[...], gv)
        xn_ref[...] = xn.astype(BF16)
        dg1_ref[...] += jnp.sum(dxn * xhat, axis=0, keepdims=True)
        dx_ref[...] = d_ref[...] + _rms_bwd(dxn, xhat, rstd, gv)

    row = lambda i: (i, 0)
    fix = lambda i: (0, 0)
    return pl.pallas_call(
        body, name="bwd_in", grid=(T // tm,),
        in_specs=[pl.BlockSpec((tm, LRU_W), row)] * npc + [
            pl.BlockSpec((D_MODEL, IN_COLS), fix), pl.BlockSpec((tm, D_MODEL), row),
            pl.BlockSpec((1, D_MODEL), fix), pl.BlockSpec((tm, D_MODEL), row)],
        out_specs=[pl.BlockSpec((tm, D_MODEL), row), pl.BlockSpec((tm, IN_COLS), row),
                   pl.BlockSpec((tm, D_MODEL), row), pl.BlockSpec((1, D_MODEL), fix)],
        out_shape=[jax.ShapeDtypeStruct((T, D_MODEL), F32), jax.ShapeDtypeStruct((T, IN_COLS), BF16),
                   jax.ShapeDtypeStruct((T, D_MODEL), BF16), jax.ShapeDtypeStruct((1, D_MODEL), F32)],
        compiler_params=_params(("arbitrary",), 56),
    )(*pieces, w_in16, x2, g1, dh1)


def _adam_shard(name, parts, w, m, v, tr):
    R, C = w.shape

    def body(p_ref, w_ref, m_ref, v_ref, g_ref, d_ref, m2_ref, v2_ref):
        g = p_ref[0]
        for p in range(1, N_DEV):
            g = g + p_ref[p]
        g_ref[...] = g
        d_ref[...], m2_ref[...], v2_ref[...] = _adamw(w_ref[...], g, m_ref[...], v_ref[...])

    blk = pl.BlockSpec((tr, C), lambda i: (i, 0))
    return pl.pallas_call(
        body, name=name, grid=(R // tr,),
        in_specs=[pl.BlockSpec((N_DEV, tr, C), lambda i: (0, i, 0)), blk, blk, blk],
        out_specs=[blk] * 4, out_shape=[jax.ShapeDtypeStruct((R, C), F32)] * 4,
        compiler_params=_params(("arbitrary",), 48),
    )(parts, w, m, v)


def _sum_parts(name, parts):
    def body(p_ref, g_ref):
        g = p_ref[0]
        for p in range(1, N_DEV):
            g = g + p_ref[p]
        g_ref[...] = g

    return pl.pallas_call(body, name=name, out_shape=jax.ShapeDtypeStruct(parts.shape[1:], F32))(parts)


def _adam_packed(w, g, m, v):
    def body(w_ref, g_ref, m_ref, v_ref, d_ref, m2_ref, v2_ref):
        d_ref[...], m2_ref[...], v2_ref[...] = _adamw(w_ref[...], g_ref[...], m_ref[...], v_ref[...])

    return pl.pallas_call(body, name="adam_small", out_shape=[jax.ShapeDtypeStruct(w.shape, F32)] * 3)(w, g, m, v)


def _pack(vals, rows):
    flat = jnp.concatenate([v.reshape(-1).astype(F32) for v in vals])
    return jnp.pad(flat, (0, rows * LANES - flat.shape[0])).reshape(rows, LANES)


def _unpack(packed, sizes):
    flat = packed.reshape(-1)
    out, off = [], 0
    for n in sizes:
        out.append(flat[off:off + n])
        off += n
    return out


def _block_diag_pairs(w):
    w = w.reshape(LRU_BLOCKS // 2, 2, HEAD_D, HEAD_D)
    out = jnp.zeros((LRU_BLOCKS // 2, LANES, LANES), w.dtype)
    out = out.at[:, :HEAD_D, :HEAD_D].set(w[:, 0])
    return out.at[:, HEAD_D:, HEAD_D:].set(w[:, 1])


def _diag_blocks(w):
    return jnp.stack([w[:, :HEAD_D, :HEAD_D], w[:, HEAD_D:, HEAD_D:]], axis=1).reshape(LRU_BLOCKS, HEAD_D, HEAD_D)


def kernel(x, norm1_g, w_in, conv_w, conv_b, lru_w_a, lru_b_a, lru_w_x, lru_b_x, lru_lambda, lru_out_g, sb_out_g, w_out, norm2_g, w_up, w_down, final_g, loss_target, m_norm1_g, m_w_in, m_conv_w, m_conv_b, m_lru_w_a, m_lru_b_a, m_lru_w_x, m_lru_b_x, m_lru_lambda, m_lru_out_g, m_sb_out_g, m_w_out, m_norm2_g, m_w_up, m_w_down, m_final_g, v_norm1_g, v_w_in, v_conv_w, v_conv_b, v_lru_w_a, v_lru_b_a, v_lru_w_x, v_lru_b_x, v_lru_lambda, v_lru_out_g, v_sb_out_g, v_w_out, v_norm2_g, v_w_up, v_w_down, v_final_g):
    B, S, _ = x.shape
    T = B * S
    tm = min(512, T)
    lc = min(512, S)
    me = _my_index()
    x2 = x.reshape(T, D_MODEL)
    tgt = loss_target.reshape(T, D_MODEL)
    cw_cols = CONV_K * LRU_W // N_DEV // CONV_K

    shards16 = _cast_shards([w_in[0], w_out[0], w_up[0], w_down[0]])
    cw_pad = jnp.zeros((SUBLANES, LANES), F32).at[:CONV_K, :cw_cols].set(conv_w[0])
    g_in, g_out, g_up, g_down, g_cw = _exchange("gather_weights", list(shards16) + [cw_pad], [False] * 5)
    w_in16 = g_in.transpose(1, 0, 2).reshape(D_MODEL, IN_COLS)
    w_out16 = g_out.reshape(D_MODEL, D_MODEL)
    w_down16 = g_down.reshape(D_FF, D_MODEL)
    conv_w_full = g_cw[:, :CONV_K, :cw_cols].transpose(1, 0, 2).reshape(CONV_K, LRU_W)
    wa_bd = _block_diag_pairs(lru_w_a[0]).astype(BF16)
    wx_bd = _block_diag_pairs(lru_w_x[0]).astype(BF16)
    b_a = lru_b_a.reshape(1, LRU_W)
    b_x = lru_b_x.reshape(1, LRU_W)
    gf = final_g.reshape(1, D_MODEL)

    proj_lru, qkv = _fwd_in(x2, norm1_g, w_in16, tm)
    y_lru, h = _lru_fwd(proj_lru, conv_w_full, conv_b, wa_bd, wx_bd, b_a, b_x, lru_lambda, B, S, lc)
    y_sb, run_tab = _attn_fwd(qkv, B, S)
    h1, mix16 = _fwd_mix(y_lru, y_sb, lru_out_g, sb_out_g, w_out16, x2, tm)
    up16, dh2, d_final_g, loss_part = _fwd_mlp(h1, norm2_g, g_up, w_down16, gf, tgt, tm)

    dup16, dh1, hn16, dh2b, d_norm2_g = _bwd_mlp(dh2, up16, h1, norm2_g, g_up, w_down16, tm)
    sq = lambda u: (u.astype(F32) * u.astype(F32)).astype(BF16)
    gw_up = _matmul_tn("grad_w_up", hn16, dup16, D_MODEL, FF_CHUNK, tm, (N_DEV, D_MODEL, FF_CHUNK),
                       (None, D_MODEL, FF_CHUNK), lambda m, n, k: (n, 0, 0))
    gw_down = _matmul_tn("grad_w_down", up16, dh2b, FF_CHUNK, D_MODEL, tm, (N_DEV, FF_CHUNK, D_MODEL),
                         (None, FF_CHUNK, D_MODEL), lambda m, n, k: (m, 0, 0), a_prep=sq)
    dy_lru, dy_sb, d_lru_out_g, d_sb_out_g = _bwd_mix(dh1, w_out16, y_lru, y_sb, lru_out_g, sb_out_g, tm)
    gw_out = _matmul_tn("grad_w_out", mix16, dh1, D_MODEL, FF_CHUNK, tm, (D_MODEL, D_MODEL),
                        (D_MODEL, FF_CHUNK), lambda m, n, k: (0, n), b_prep=lambda u: u.astype(BF16))
    dq, dk, dv = _attn_bwd(qkv, run_tab, dy_sb, B, S)
    (dx_lru, dg_lru, d_conv_w, d_conv_b, d_wa, d_wx, d_b_a, d_b_x, d_lambda) = _lru_bwd(
        proj_lru, h, dy_lru, conv_w_full, conv_b, wa_bd, wx_bd, b_a, b_x, lru_lambda, B, S, lc)
    dx, dproj16, xn16, d_norm1_g = _bwd_in([dx_lru, dg_lru, dq, dk, dv], w_in16, x2, norm1_g, dh1, tm)
    gw_in = _matmul_tn("grad_w_in", xn16, dproj16, D_MODEL, FF_CHUNK, tm, (D_MODEL, IN_COLS),
                       (D_MODEL, FF_CHUNK), lambda m, n, k: (0, n))

    small_grads = {"norm1_g": d_norm1_g, "conv_b": d_conv_b, "lru_w_a": _diag_blocks(d_wa), "lru_b_a": d_b_a,
                   "lru_w_x": _diag_blocks(d_wx), "lru_b_x": d_b_x, "lru_lambda": d_lambda,
                   "lru_out_g": d_lru_out_g, "sb_out_g": d_sb_out_g, "norm2_g": d_norm2_g, "final_g": d_final_g}
    packed = _pack([small_grads[n] for n, _ in SMALL] + [d_conv_w, loss_part], EXCH_ROWS)
    parts_in = gw_in.reshape(D_MODEL, N_DEV, IN_COLS // N_DEV).transpose(1, 0, 2)
    parts_out = gw_out.reshape(N_DEV, D_MODEL // N_DEV, D_MODEL)
    r_in, r_out, r_up, r_down, r_small = _exchange(
        "exchange_grads", [parts_in, parts_out, gw_up, gw_down, packed], [True, True, True, True, False])

    g_w_in, d_w_in, nm_w_in, nv_w_in = _adam_shard("adam_w_in", r_in, w_in[0], m_w_in[0], v_w_in[0], 256)
    g_w_out, d_w_out, nm_w_out, nv_w_out = _adam_shard("adam_w_out", r_out, w_out[0], m_w_out[0], v_w_out[0], 64)
    g_w_up, d_w_up, nm_w_up, nv_w_up = _adam_shard("adam_w_up", r_up, w_up[0], m_w_up[0], v_w_up[0], 256)
    g_w_down, d_w_down, nm_w_down, nv_w_down = _adam_shard("adam_w_down", r_down, w_down[0], m_w_down[0], v_w_down[0], 128)

    total = _sum_parts("sum_small", r_small)
    sizes = [n for _, n in SMALL]
    small_g = _unpack(total, sizes + [CONV_K * LRU_W, 1])
    loss = small_g[-1][0]
    g_conv_w = lax.dynamic_slice_in_dim(small_g[-2].reshape(CONV_K, LRU_W), me * cw_cols, cw_cols, axis=1)
    given = dict(norm1_g=(norm1_g, m_norm1_g, v_norm1_g), conv_b=(conv_b, m_conv_b, v_conv_b),
                 lru_w_a=(lru_w_a, m_lru_w_a, v_lru_w_a), lru_b_a=(lru_b_a, m_lru_b_a, v_lru_b_a),
                 lru_w_x=(lru_w_x, m_lru_w_x, v_lru_w_x), lru_b_x=(lru_b_x, m_lru_b_x, v_lru_b_x),
                 lru_lambda=(lru_lambda, m_lru_lambda, v_lru_lambda), lru_out_g=(lru_out_g, m_lru_out_g, v_lru_out_g),
                 sb_out_g=(sb_out_g, m_sb_out_g, v_sb_out_g), norm2_g=(norm2_g, m_norm2_g, v_norm2_g),
                 final_g=(final_g, m_final_g, v_final_g))
    names = [n for n, _ in SMALL]
    pw = _pack([given[n][0] for n in names] + [conv_w], ADAM_ROWS)
    pm = _pack([given[n][1] for n in names] + [m_conv_w], ADAM_ROWS)
    pv = _pack([given[n][2] for n in names] + [v_conv_w], ADAM_ROWS)
    pg = _pack(small_g[:len(names)] + [g_conv_w], ADAM_ROWS)
    pd, pm2, pv2 = _adam_packed(pw, pg, pm, pv)
    asz = sizes + [conv_w.size]
    shapes = {n: given[n][0].shape for n in names}
    shapes["conv_w"] = conv_w.shape
    order = names + ["conv_w"]
    g_small = dict(zip(order, [a.reshape(shapes[n]) for n, a in zip(order, _unpack(pg, asz))]))
    d_small = dict(zip(order, [a.reshape(shapes[n]) for n, a in zip(order, _unpack(pd, asz))]))
    m_small = dict(zip(order, [a.reshape(shapes[n]) for n, a in zip(order, _unpack(pm2, asz))]))
    v_small = dict(zip(order, [a.reshape(shapes[n]) for n, a in zip(order, _unpack(pv2, asz))]))

    big = {"w_in": (g_w_in, d_w_in, nm_w_in, nv_w_in), "w_out": (g_w_out, d_w_out, nm_w_out, nv_w_out),
           "w_up": (g_w_up, d_w_up, nm_w_up, nv_w_up), "w_down": (g_w_down, d_w_down, nm_w_down, nv_w_down)}
    weights = ["norm1_g", "w_in", "conv_w", "conv_b", "lru_w_a", "lru_b_a", "lru_w_x", "lru_b_x", "lru_lambda",
               "lru_out_g", "sb_out_g", "w_out", "norm2_g", "w_up", "w_down", "final_g"]

    def leaf(n, kind):
        if n in big:
            return big[n][kind][None]
        return (g_small, d_small, m_small, v_small)[kind][n]

    return (loss, dx.reshape(B, S, D_MODEL), *[leaf(n, 0) for n in weights], *[leaf(n, 1) for n in weights],
            *[leaf(n, 2) for n in weights], *[leaf(n, 3) for n in weights])
```

```python
import jax
import jax.numpy as jnp
from jax import lax
from jax.experimental import pallas as pl
from jax.experimental.pallas import tpu as pltpu

F32 = jnp.float32
BF16 = jnp.bfloat16

D_MODEL = 1024
LRU_W = 512
SB_W = 512
HEAD_D = 64
D_FF = 4096
IN_COLS = 2 * LRU_W + 3 * SB_W
CONV_K = 4
LRU_BLOCKS = 8
LRU_C = 8.0
EPS = 1e-6
N_DEV = 8
LANES = 128
SUBLANES = 8
FF_CHUNK = 512
Q_BLK = 128
K_BLK = 256
Q_PER_K = K_BLK // Q_BLK

ADAM_LR = 0.001
ADAM_B1 = 0.9
ADAM_B2 = 0.999
ADAM_EPS = 1e-08
ADAM_WD = 0.01
ADAM_STEP = 10

SMALL = (("norm1_g", 1024), ("conv_b", 512), ("lru_w_a", 32768), ("lru_b_a", 512), ("lru_w_x", 32768),
         ("lru_b_x", 512), ("lru_lambda", 512), ("lru_out_g", 512), ("sb_out_g", 512), ("norm2_g", 1024),
         ("final_g", 1024))
SMALL_ROWS = sum(n for _, n in SMALL) // LANES
EXCH_ROWS = SMALL_ROWS + (CONV_K * LRU_W) // LANES + 8
ADAM_ROWS = SMALL_ROWS + 8


def _params(sem=None, vmem_mb=None):
    kw = {}
    if sem is not None:
        kw["dimension_semantics"] = sem
    if vmem_mb is not None:
        kw["vmem_limit_bytes"] = vmem_mb << 20
    return pltpu.CompilerParams(**kw)


def _dot(a, b):
    return jnp.dot(a, b, preferred_element_type=F32)


def _dot_nt(a, b):
    return lax.dot_general(a, b, (((1,), (1,)), ((), ())), preferred_element_type=F32)


def _dot_tn(a, b):
    return lax.dot_general(a, b, (((0,), (0,)), ((), ())), preferred_element_type=F32)


def _rms_fwd(x, g):
    rstd = lax.rsqrt(jnp.mean(x * x, axis=-1, keepdims=True) + EPS)
    xhat = x * rstd
    return xhat * g, xhat, rstd


def _rms_bwd(dy, xhat, rstd, g):
    dxhat = dy * g
    return rstd * (dxhat - xhat * jnp.mean(dxhat * xhat, axis=-1, keepdims=True))


def _sigmoid(x):
    return 1.0 / (1.0 + jnp.exp(-x))


def _log1p_pos(e):
    series = e * (1.0 - e * (0.5 - e * (1.0 / 3.0 - e * 0.25)))
    return jnp.where(e < 1e-2, series, jnp.log(1.0 + e))


def _neg_expm1(x):
    series = -x * (1.0 + x * (0.5 + x * (1.0 / 6.0 + x * (1.0 / 24.0))))
    return jnp.where(x > -1e-2, series, 1.0 - jnp.exp(x))


def _gelu_parts(g):
    k0 = 0.7978845608028654
    k1 = 0.044715
    t = jnp.tanh(k0 * (g + k1 * g * g * g))
    val = 0.5 * g * (1.0 + t)
    grad = 0.5 * (1.0 + t) + 0.5 * g * (1.0 - t * t) * k0 * (1.0 + 3.0 * k1 * g * g)
    return val, grad


def _scan(a, b, reverse):
    n = a.shape[0]
    row = lax.broadcasted_iota(jnp.int32, a.shape, 0)
    s = 1
    while s < n:
        if reverse:
            keep = row < n - s
            shift = n - s
        else:
            keep = row >= s
            shift = s
        bs = jnp.where(keep, pltpu.roll(b, shift, 0), 0.0)
        a_s = jnp.where(keep, pltpu.roll(a, shift, 0), 1.0)
        b = a * bs + b
        a = a * a_s
        s *= 2
    return b, a


def _adamw(w, g, m, v):
    m = ADAM_B1 * m + (1.0 - ADAM_B1) * g
    v = ADAM_B2 * v + (1.0 - ADAM_B2) * (g * g)
    m_hat = m / (1.0 - ADAM_B1 ** ADAM_STEP)
    v_hat = v / (1.0 - ADAM_B2 ** ADAM_STEP)
    delta = -ADAM_LR * (m_hat / (jnp.sqrt(v_hat) + ADAM_EPS) + ADAM_WD * w)
    return delta, m, v


def _my_index():
    return 4 * lax.axis_index("x") + 2 * lax.axis_index("y") + lax.axis_index("c")


def _peer(k):
    x, y, c = lax.axis_index("x"), lax.axis_index("y"), lax.axis_index("c")
    px = 1 - x if (k >> 2) & 1 else x
    py = 1 - y if (k >> 1) & 1 else y
    pc = 1 - c if k & 1 else c
    return (px, py, pc), 4 * px + 2 * py + pc


def _exchange(name, srcs, sliced):
    n = len(srcs)
    out_shape = [jax.ShapeDtypeStruct(s.shape if sl else (N_DEV,) + s.shape, s.dtype) for s, sl in zip(srcs, sliced)]

    def body(*refs):
        ins, outs = refs[:n], refs[n:2 * n]
        send_sems, recv_sems, local_sems = refs[2 * n:]
        me = _my_index()

        def part(a, p):
            return ins[a].at[p] if sliced[a] else ins[a]

        local = [pltpu.make_async_copy(part(a, me), outs[a].at[me], local_sems.at[a]) for a in range(n)]
        for cp in local:
            cp.start()
        for k in range(1, N_DEV):
            dev, idx = _peer(k)
            for a in range(n):
                pltpu.make_async_remote_copy(
                    src_ref=part(a, idx), dst_ref=outs[a].at[me], send_sem=send_sems.at[a, k - 1],
                    recv_sem=recv_sems.at[a, k - 1], device_id=dev, device_id_type=pl.DeviceIdType.MESH).start()
        for k in range(1, N_DEV):
            dev, idx = _peer(k)
            for a in range(n):
                pltpu.make_async_remote_copy(
                    src_ref=part(a, idx), dst_ref=outs[a].at[idx], send_sem=send_sems.at[a, k - 1],
                    recv_sem=recv_sems.at[a, k - 1], device_id=dev, device_id_type=pl.DeviceIdType.MESH).wait()
        for cp in local:
            cp.wait()

    return pl.pallas_call(
        body, name=name, out_shape=out_shape,
        in_specs=[pl.BlockSpec(memory_space=pl.ANY)] * n,
        out_specs=[pl.BlockSpec(memory_space=pl.ANY)] * n,
        scratch_shapes=[pltpu.SemaphoreType.DMA((n, N_DEV - 1)), pltpu.SemaphoreType.DMA((n, N_DEV - 1)),
                        pltpu.SemaphoreType.DMA((n,))],
    )(*srcs)


def _cast_shards(ws):
    def body(*refs):
        for i in range(len(ws)):
            refs[len(ws) + i][...] = refs[i][...].astype(BF16)

    return pl.pallas_call(
        body, name="cast_shards", out_shape=[jax.ShapeDtypeStruct(w.shape, BF16) for w in ws],
        compiler_params=_params(vmem_mb=32),
    )(*ws)


def _fwd_in(x2, g1, w_in16, tm):
    T = x2.shape[0]

    def body(x_ref, g_ref, w_ref, lru_ref, qkv_ref):
        xn, _, _ = _rms_fwd(x_ref[...], g_ref[...])
        xn = xn.astype(BF16)
        lru_ref[...] = _dot(xn, w_ref[:, 0:2 * LRU_W])
        qkv_ref[...] = _dot(xn, w_ref[:, 2 * LRU_W:IN_COLS]).astype(BF16)

    return pl.pallas_call(
        body, name="fwd_in", grid=(T // tm,),
        in_specs=[pl.BlockSpec((tm, D_MODEL), lambda i: (i, 0)),
                  pl.BlockSpec((1, D_MODEL), lambda i: (0, 0)),
                  pl.BlockSpec((D_MODEL, IN_COLS), lambda i: (0, 0))],
        out_specs=[pl.BlockSpec((tm, 2 * LRU_W), lambda i: (i, 0)),
                   pl.BlockSpec((tm, 3 * SB_W), lambda i: (i, 0))],
        out_shape=[jax.ShapeDtypeStruct((T, 2 * LRU_W), F32), jax.ShapeDtypeStruct((T, 3 * SB_W), BF16)],
        compiler_params=_params(("arbitrary",), 48),
    )(x2, g1, w_in16)


def _lru_gates(c, wa_ref, wx_ref, ba_ref, bx_ref, lam_ref):
    c16 = c.astype(BF16)
    r = _sigmoid(_dot(c16, wa_ref[0]) + ba_ref[...])
    i = _sigmoid(_dot(c16, wx_ref[0]) + bx_ref[...])
    lam = lam_ref[...]
    e = jnp.exp(-jnp.abs(lam))
    sp = jnp.maximum(-lam, 0.0) + _log1p_pos(e)
    dsp_dlam = -jnp.where(lam >= 0.0, e, 1.0) / (1.0 + e)
    log_a = (-LRU_C) * r * sp
    a = jnp.exp(log_a)
    s = jnp.sqrt(_neg_expm1(2.0 * log_a))
    return r, i, sp, dsp_dlam, a, s


def _conv_taps(x, halo, lc):
    xe = jnp.concatenate([halo, x], axis=0)
    return [x] + [pltpu.roll(xe, k, 0)[SUBLANES:SUBLANES + lc] for k in range(1, CONV_K)]


def _lru_fwd(proj_lru, conv_w, conv_b, wa_bd, wx_bd, b_a, b_x, lam, B, S, lc):
    T = B * S
    nc = S // lc
    ncb = LRU_W // LANES

    def body(x_ref, g_ref, cw_ref, cb_ref, wa_ref, wx_ref, ba_ref, bx_ref, lam_ref, y_ref, h_ref, tail, carry):
        ci = pl.program_id(2)

        @pl.when(ci == 0)
        def _():
            tail[...] = jnp.zeros_like(tail)
            carry[...] = jnp.zeros_like(carry)

        x = x_ref[...]
        taps = _conv_taps(x, tail[...], lc)
        c = cb_ref[...] + sum(cw_ref[pl.ds(CONV_K - 1 - k, 1), :] * taps[k] for k in range(CONV_K))
        tail[...] = x_ref[pl.ds(lc - SUBLANES, SUBLANES), :]
        r, i, sp, _, a, s = _lru_gates(c, wa_ref, wx_ref, ba_ref, bx_ref, lam_ref)
        h_loc, a_run = _scan(a, s * (i * c), reverse=False)
        h_ref[...] = h_loc + a_run * carry[...]
        carry[...] = h_ref[pl.ds(lc - 1, 1), :]
        gelu, _ = _gelu_parts(g_ref[...])
        y_ref[...] = h_ref[...] * gelu

    chan = lambda b, cb, ci: (0, cb)
    return pl.pallas_call(
        body, name="lru_fwd", grid=(B, ncb, nc),
        in_specs=[pl.BlockSpec((lc, LANES), lambda b, cb, ci: (b * nc + ci, cb)),
                  pl.BlockSpec((lc, LANES), lambda b, cb, ci: (b * nc + ci, ncb + cb)),
                  pl.BlockSpec((CONV_K, LANES), chan), pl.BlockSpec((1, LANES), chan),
                  pl.BlockSpec((1, LANES, LANES), lambda b, cb, ci: (cb, 0, 0)),
                  pl.BlockSpec((1, LANES, LANES), lambda b, cb, ci: (cb, 0, 0)),
                  pl.BlockSpec((1, LANES), chan), pl.BlockSpec((1, LANES), chan), pl.BlockSpec((1, LANES), chan)],
        out_specs=[pl.BlockSpec((lc, LANES), lambda b, cb, ci: (b * nc + ci, cb))] * 2,
        out_shape=[jax.ShapeDtypeStruct((T, LRU_W), F32)] * 2,
        scratch_shapes=[pltpu.VMEM((SUBLANES, LANES), F32), pltpu.VMEM((1, LANES), F32)],
        compiler_params=_params(("arbitrary", "arbitrary", "arbitrary"), 32),
    )(proj_lru, proj_lru, conv_w, conv_b, wa_bd, wx_bd, b_a, b_x, lam)


def _cumsum_mm(v, tri):
    hi = v.astype(BF16)
    lo = (v - hi.astype(F32)).astype(BF16)
    return _dot(hi, tri) + _dot(lo, tri)


def _tri(prefix):
    r = lax.broadcasted_iota(jnp.int32, (K_BLK, K_BLK), 0)
    c = lax.broadcasted_iota(jnp.int32, (K_BLK, K_BLK), 1)
    return ((r <= c) if prefix else (r >= c)).astype(BF16)


def _attn_consts(qi):
    r = lax.broadcasted_iota(jnp.int32, (Q_BLK, K_BLK), 0)
    c = lax.broadcasted_iota(jnp.int32, (Q_BLK, K_BLK), 1)
    causal = c + ((qi // Q_PER_K) * K_BLK - qi * Q_BLK) < r
    lane = lax.broadcasted_iota(jnp.int32, (1, LANES), 1)
    return causal, c, lane, (lane < HEAD_D, lane >= HEAD_D)


def _log1m_beta(z, mask):
    lg = -(jnp.maximum(z, 0.0) + jnp.log(1.0 + jnp.exp(-jnp.abs(z))))
    return lg if mask is None else jnp.where(mask, lg, 0.0)


def _key_rows(j):
    return pl.ds(pl.multiple_of(j * K_BLK, K_BLK), K_BLK)


def _attn_fwd(qkv, tri_suffix, B, S):
    T = B * S
    nq = S // Q_BLK
    nhp = SB_W // LANES
    scale = HEAD_D ** -0.5
    assert S // K_BLK <= LANES

    def body(q_ref, k_ref, v_ref, tri_ref, o_ref, run_ref):
        qi = pl.program_id(2)
        jd = qi // Q_PER_K
        causal, col, lane, halves = _attn_consts(qi)
        q = q_ref[...]
        qh = [jnp.where(hm, q, jnp.zeros_like(q)) * jnp.asarray(scale, BF16) for hm in halves]

        def group(blocks, carry):
            runs, tables, acc = carry
            runs, tables = list(runs), list(tables)
            ks = [k_ref[_key_rows(jl), :] for jl, _, _ in blocks]
            vs = [v_ref[_key_rows(jl), :] for jl, _, _ in blocks]
            chains = [(b, h) for b in range(len(blocks)) for h in range(2)]
            z = {c: _dot_nt(qh[c[1]], ks[c[0]]) for c in chains}
            lg = {c: _log1m_beta(z[c], blocks[c[0]][2]) for c in chains}
            suf = {c: _cumsum_mm(lg[c], tri_ref[...]) for c in chains}
            att = {}
            for b, h in chains:
                _, jlane, mask = blocks[b]
                a = jnp.exp(z[b, h] + suf[b, h] + runs[h])
                att[b, h] = (a if mask is None else jnp.where(mask, a, 0.0)).astype(BF16)
                tables[h] = jnp.where(lane == jlane, runs[h], tables[h])
                runs[h] = runs[h] + suf[b, h][:, 0:1]
            for b, h in chains:
                acc = acc + _dot(att[b, h], jnp.where(halves[h], vs[b], jnp.zeros_like(vs[b])))
            return tuple(runs), tuple(tables), acc

        col0 = jnp.zeros((Q_BLK, 1), F32)
        zero = jnp.zeros((Q_BLK, LANES), F32)
        two = jd % 2
        before = (jnp.maximum(jd - 1, 0), jnp.where(two == 1, jd - 1, -1), col < two * K_BLK)
        carry = group([(jd, jd, causal), before], ((col0, col0), (zero, zero), zero))
        top = jd - 1 - two

        def step(it, cr):
            ja = top - 2 * it
            return group([(ja, ja, None), (ja - 1, ja - 1, None)], cr)

        _, tables, acc = lax.fori_loop(0, (top + 1) // 2, step, carry)
        o_ref[...] = acc
        run_ref[:, 0:LANES] = tables[0]
        run_ref[:, LANES:2 * LANES] = tables[1]

    return pl.pallas_call(
        body, name="attn_fwd", grid=(B, nhp, nq),
        in_specs=[pl.BlockSpec((Q_BLK, LANES), lambda b, hp, qi: (b * nq + qi, hp)),
                  pl.BlockSpec((S, LANES), lambda b, hp, qi: (b, nhp + hp)),
                  pl.BlockSpec((S, LANES), lambda b, hp, qi: (b, 2 * nhp + hp)),
                  pl.BlockSpec((K_BLK, K_BLK), lambda b, hp, qi: (0, 0))],
        out_specs=[pl.BlockSpec((Q_BLK, LANES), lambda b, hp, qi: (b * nq + qi, hp)),
                   pl.BlockSpec((Q_BLK, 2 * LANES), lambda b, hp, qi: (b * nq + qi, hp))],
        out_shape=[jax.ShapeDtypeStruct((T, SB_W), F32), jax.ShapeDtypeStruct((T, 2 * SB_W), F32)],
        compiler_params=_params(("arbitrary", "arbitrary", "arbitrary"), 48),
    )(qkv, qkv, qkv, tri_suffix)


def _fwd_mix(y_lru, y_sb, ga, gb, w_out16, x2, tm):
    T = x2.shape[0]

    def body(yl_ref, ys_ref, ga_ref, gb_ref, w_ref, x_ref, h1_ref, mix_ref):
        na, _, _ = _rms_fwd(yl_ref[...], ga_ref[...])
        nb, _, _ = _rms_fwd(ys_ref[...], gb_ref[...])
        na = na.astype(BF16)
        nb = nb.astype(BF16)
        mix_ref[:, 0:LRU_W] = na
        mix_ref[:, LRU_W:D_MODEL] = nb
        h1_ref[...] = x_ref[...] + _dot(na, w_ref[0:LRU_W, :]) + _dot(nb, w_ref[LRU_W:D_MODEL, :])

    row = lambda i: (i, 0)
    fix = lambda i: (0, 0)
    return pl.pallas_call(
        body, name="fwd_mix", grid=(T // tm,),
        in_specs=[pl.BlockSpec((tm, LRU_W), row), pl.BlockSpec((tm, SB_W), row),
                  pl.BlockSpec((1, LRU_W), fix), pl.BlockSpec((1, SB_W), fix),
                  pl.BlockSpec((D_MODEL, D_MODEL), fix), pl.BlockSpec((tm, D_MODEL), row)],
        out_specs=[pl.BlockSpec((tm, D_MODEL), row), pl.BlockSpec((tm, D_MODEL), row)],
        out_shape=[jax.ShapeDtypeStruct((T, D_MODEL), F32), jax.ShapeDtypeStruct((T, D_MODEL), BF16)],
        compiler_params=_params(("arbitrary",), 48),
    )(y_lru, y_sb, ga, gb, w_out16, x2)


def _fwd_mlp(h1, g2, w_up16, w_down16, gf, tgt, tm):
    T = h1.shape[0]
    nf = D_FF // FF_CHUNK

    def body(h1_ref, g2_ref, wu_ref, wd_ref, gf_ref, t_ref, up_ref, dh2_ref, dgf_ref, loss_ref, hn_s, acc):
        i, j = pl.program_id(0), pl.program_id(1)

        @pl.when(j == 0)
        def _():
            h1v = h1_ref[...]
            hn, _, _ = _rms_fwd(h1v, g2_ref[...])
            hn_s[...] = hn.astype(BF16)
            acc[...] = h1v

        up = jnp.maximum(_dot(hn_s[...], wu_ref[...]), 0.0)
        up_ref[...] = up.astype(BF16)
        acc[...] += _dot((up * up).astype(BF16), wd_ref[...])

        @pl.when((i == 0) & (j == 0))
        def _():
            dgf_ref[...] = jnp.zeros_like(dgf_ref)
            loss_ref[...] = jnp.zeros_like(loss_ref)

        @pl.when(j == nf - 1)
        def _():
            gfv = gf_ref[...]
            y, xhat, rstd = _rms_fwd(acc[...], gfv)
            err = y - t_ref[...]
            loss_ref[...] += jnp.sum(0.5 * jnp.sum(err * err, axis=-1, keepdims=True) * (1.0 / D_MODEL))
            dy = err * (1.0 / D_MODEL)
            dgf_ref[...] += jnp.sum(dy * xhat, axis=0, keepdims=True)
            dh2_ref[...] = _rms_bwd(dy, xhat, rstd, gfv)

    row = lambda i, j: (i, 0)
    fix = lambda i, j: (0, 0)
    return pl.pallas_call(
        body, name="fwd_mlp", grid=(T // tm, nf),
        in_specs=[pl.BlockSpec((tm, D_MODEL), row), pl.BlockSpec((1, D_MODEL), fix),
                  pl.BlockSpec((None, D_MODEL, FF_CHUNK), lambda i, j: (j, 0, 0)),
                  pl.BlockSpec((FF_CHUNK, D_MODEL), lambda i, j: (j, 0)),
                  pl.BlockSpec((1, D_MODEL), fix), pl.BlockSpec((tm, D_MODEL), row)],
        out_specs=[pl.BlockSpec((tm, FF_CHUNK), lambda i, j: (i, j)), pl.BlockSpec((tm, D_MODEL), row),
                   pl.BlockSpec((1, D_MODEL), fix), pl.BlockSpec((1, LANES), fix)],
        out_shape=[jax.ShapeDtypeStruct((T, D_FF), BF16), jax.ShapeDtypeStruct((T, D_MODEL), F32),
                   jax.ShapeDtypeStruct((1, D_MODEL), F32), jax.ShapeDtypeStruct((1, LANES), F32)],
        scratch_shapes=[pltpu.VMEM((tm, D_MODEL), BF16), pltpu.VMEM((tm, D_MODEL), F32)],
        compiler_params=_params(("arbitrary", "arbitrary"), 48),
    )(h1, g2, w_up16, w_down16, gf, tgt)


def _bwd_mlp(dh2, up16, h1, g2, w_up16, w_down16, tm):
    T = h1.shape[0]
    nf = D_FF // FF_CHUNK

    def body(dh2_ref, up_ref, h1_ref, g2_ref, wu_ref, wd_ref, dup_ref, dh1_ref, hn_ref, dh2b_ref, dg2_ref, acc):
        i, j = pl.program_id(0), pl.program_id(1)

        @pl.when(j == 0)
        def _():
            hn, _, _ = _rms_fwd(h1_ref[...], g2_ref[...])
            hn_ref[...] = hn.astype(BF16)
            dh2b_ref[...] = dh2_ref[...].astype(BF16)
            acc[...] = jnp.zeros_like(acc)

        u = up_ref[...].astype(F32)
        dup = (2.0 * u * _dot_nt(dh2b_ref[...], wd_ref[...])).astype(BF16)
        dup_ref[...] = dup
        acc[...] += _dot_nt(dup, wu_ref[...])

        @pl.when((i == 0) & (j == 0))
        def _():
            dg2_ref[...] = jnp.zeros_like(dg2_ref)

        @pl.when(j == nf - 1)
        def _():
            g2v = g2_ref[...]
            _, xhat, rstd = _rms_fwd(h1_ref[...], g2v)
            dhn = acc[...]
            dg2_ref[...] += jnp.sum(dhn * xhat, axis=0, keepdims=True)
            dh1_ref[...] = dh2_ref[...] + _rms_bwd(dhn, xhat, rstd, g2v)

    row = lambda i, j: (i, 0)
    fix = lambda i, j: (0, 0)
    return pl.pallas_call(
        body, name="bwd_mlp", grid=(T // tm, nf),
        in_specs=[pl.BlockSpec((tm, D_MODEL), row), pl.BlockSpec((tm, FF_CHUNK), lambda i, j: (i, j)),
                  pl.BlockSpec((tm, D_MODEL), row), pl.BlockSpec((1, D_MODEL), fix),
                  pl.BlockSpec((None, D_MODEL, FF_CHUNK), lambda i, j: (j, 0, 0)),
                  pl.BlockSpec((FF_CHUNK, D_MODEL), lambda i, j: (j, 0))],
        out_specs=[pl.BlockSpec((tm, FF_CHUNK), lambda i, j: (i, j)), pl.BlockSpec((tm, D_MODEL), row),
                   pl.BlockSpec((tm, D_MODEL), row), pl.BlockSpec((tm, D_MODEL), row),
                   pl.BlockSpec((1, D_MODEL), fix)],
        out_shape=[jax.ShapeDtypeStruct((T, D_FF), BF16), jax.ShapeDtypeStruct((T, D_MODEL), F32),
                   jax.ShapeDtypeStruct((T, D_MODEL), BF16), jax.ShapeDtypeStruct((T, D_MODEL), BF16),
                   jax.ShapeDtypeStruct((1, D_MODEL), F32)],
        scratch_shapes=[pltpu.VMEM((tm, D_MODEL), F32)],
        compiler_params=_params(("arbitrary", "arbitrary"), 48),
    )(dh2, up16, h1, g2, w_up16, w_down16)


def _matmul_tn(name, a, b, bm, bn, tk, out_shape, out_block, out_index, a_prep=None, b_prep=None):
    T, M = a.shape
    N = b.shape[1]

    def body(a_ref, b_ref, o_ref):
        k = pl.program_id(2)
        av = a_ref[...] if a_prep is None else a_prep(a_ref[...])
        bv = b_ref[...] if b_prep is None else b_prep(b_ref[...])
        p = _dot_tn(av, bv)

        @pl.when(k == 0)
        def _():
            o_ref[...] = p

        @pl.when(k > 0)
        def _():
            o_ref[...] += p

    return pl.pallas_call(
        body, name=name, grid=(M // bm, N // bn, T // tk),
        in_specs=[pl.BlockSpec((tk, bm), lambda m, n, k: (k, m)), pl.BlockSpec((tk, bn), lambda m, n, k: (k, n))],
        out_specs=pl.BlockSpec(out_block, out_index),
        out_shape=jax.ShapeDtypeStruct(out_shape, F32),
        compiler_params=_params(("arbitrary", "arbitrary", "arbitrary"), 48),
    )(a, b)


def _bwd_mix(dh1, w_out16, y_lru, y_sb, ga, gb, tm):
    T = dh1.shape[0]

    def body(d_ref, w_ref, yl_ref, ys_ref, ga_ref, gb_ref, dyl_ref, dys_ref, dga_ref, dgb_ref):
        @pl.when(pl.program_id(0) == 0)
        def _():
            dga_ref[...] = jnp.zeros_like(dga_ref)
            dgb_ref[...] = jnp.zeros_like(dgb_ref)

        d16 = d_ref[...].astype(BF16)
        for y_ref, g_ref, lo, dy_ref, dg_ref in ((yl_ref, ga_ref, 0, dyl_ref, dga_ref),
                                                 (ys_ref, gb_ref, LRU_W, dys_ref, dgb_ref)):
            gv = g_ref[...]
            dn = _dot_nt(d16, w_ref[lo:lo + LRU_W, :])
            _, xhat, rstd = _rms_fwd(y_ref[...], gv)
            dg_ref[...] += jnp.sum(dn * xhat, axis=0, keepdims=True)
            dy_ref[...] = _rms_bwd(dn, xhat, rstd, gv)

    row = lambda i: (i, 0)
    fix = lambda i: (0, 0)
    return pl.pallas_call(
        body, name="bwd_mix", grid=(T // tm,),
        in_specs=[pl.BlockSpec((tm, D_MODEL), row), pl.BlockSpec((D_MODEL, D_MODEL), fix),
                  pl.BlockSpec((tm, LRU_W), row), pl.BlockSpec((tm, SB_W), row),
                  pl.BlockSpec((1, LRU_W), fix), pl.BlockSpec((1, SB_W), fix)],
        out_specs=[pl.BlockSpec((tm, LRU_W), row), pl.BlockSpec((tm, SB_W), row),
                   pl.BlockSpec((1, LRU_W), fix), pl.BlockSpec((1, SB_W), fix)],
        out_shape=[jax.ShapeDtypeStruct((T, LRU_W), F32), jax.ShapeDtypeStruct((T, SB_W), F32),
                   jax.ShapeDtypeStruct((1, LRU_W), F32), jax.ShapeDtypeStruct((1, SB_W), F32)],
        compiler_params=_params(("arbitrary",), 48),
    )(dh1, w_out16, y_lru, y_sb, ga, gb)


def _attn_bwd(qkv, run_tab, dy_sb, tri_suffix, tri_prefix, B, S):
    T = B * S
    nq = S // Q_BLK
    nkb = S // K_BLK
    nhp = SB_W // LANES
    scale = HEAD_D ** -0.5

    def body(q_ref, k_ref, v_ref, run_ref, do_ref, ts_ref, tp_ref, dq_ref, dkt_ref, dvt_ref):
        qi = pl.program_id(2)
        jd = qi // Q_PER_K

        @pl.when(qi == 0)
        def _():
            dkt_ref[...] = jnp.zeros_like(dkt_ref)
            dvt_ref[...] = jnp.zeros_like(dvt_ref)

        causal, col, lane, halves = _attn_consts(qi)
        q = q_ref[...]
        do = do_ref[...]
        qh = [jnp.where(hm, q, jnp.zeros_like(q)) * jnp.asarray(scale, BF16) for hm in halves]
        doh = [jnp.where(hm, do, 0.0).astype(BF16) for hm in halves]
        qt = jnp.concatenate([h.astype(F32).T.astype(BF16) for h in qh], axis=1)
        dot_ = jnp.concatenate([h.astype(F32).T.astype(BF16) for h in doh], axis=1)
        tables = [run_ref[:, 0:LANES], run_ref[:, LANES:2 * LANES]]

        def group(blocks, carry):
            prefixes, dq = carry
            prefixes = list(prefixes)
            ks = [k_ref[_key_rows(jl), :] for jl, _, _ in blocks]
            vs = [v_ref[_key_rows(jl), :] for jl, _, _ in blocks]
            chains = [(b, h) for b in range(len(blocks)) for h in range(2)]
            z = {c: _dot_nt(qh[c[1]], ks[c[0]]) for c in chains}
            da = {c: _dot_nt(doh[c[1]], vs[c[0]]) for c in chains}
            lg = {c: _log1m_beta(z[c], blocks[c[0]][2]) for c in chains}
            suf = {c: _cumsum_mm(lg[c], ts_ref[...]) for c in chains}
            att, g = {}, {}
            for b, h in chains:
                _, jlane, mask = blocks[b]
                run = jnp.sum(jnp.where(lane == jlane, tables[h], 0.0), axis=1, keepdims=True)
                a = jnp.exp(z[b, h] + suf[b, h] + run)
                a = a if mask is None else jnp.where(mask, a, 0.0)
                g[b, h] = a * da[b, h]
                att[b, h] = a.astype(BF16)
            gpre = {c: _cumsum_mm(g[c], tp_ref[...]) for c in chains}
            dz = {}
            for b, h in chains:
                mask = blocks[b][2]
                d = g[b, h] - jnp.exp(z[b, h] + lg[b, h]) * (prefixes[h] + gpre[b, h])
                dz[b, h] = (d if mask is None else jnp.where(mask, d, 0.0)).astype(BF16)
                prefixes[h] = prefixes[h] + gpre[b, h][:, K_BLK - 1:K_BLK]
            for b, h in chains:
                dq = dq + _dot(dz[b, h], jnp.where(halves[h], ks[b], jnp.zeros_like(ks[b])))
            for b, (jl, _, _) in enumerate(blocks):
                dkt_ref[jl] += _dot(qt, jnp.concatenate([dz[b, 0], dz[b, 1]], axis=0))
                dvt_ref[jl] += _dot(dot_, jnp.concatenate([att[b, 0], att[b, 1]], axis=0))
            return tuple(prefixes), dq

        def step(it, cr):
            return group([(2 * it, 2 * it, None), (2 * it + 1, 2 * it + 1, None)], cr)

        col0 = jnp.zeros((Q_BLK, 1), F32)
        carry = lax.fori_loop(0, jd // 2, step, ((col0, col0), jnp.zeros((Q_BLK, LANES), F32)))
        two = jd % 2
        before = (jnp.maximum(jd - 1, 0), jnp.where(two == 1, jd - 1, -1), col < two * K_BLK)
        dq_ref[...] = group([before, (jd, jd, causal)], carry)[1] * scale

    qblk = pl.BlockSpec((Q_BLK, LANES), lambda b, hp, qi: (b * nq + qi, hp))
    tri = pl.BlockSpec((K_BLK, K_BLK), lambda b, hp, qi: (0, 0))
    acc = pl.BlockSpec((None, None, nkb, LANES, K_BLK), lambda b, hp, qi: (b, hp, 0, 0, 0))
    return pl.pallas_call(
        body, name="attn_bwd", grid=(B, nhp, nq),
        in_specs=[qblk, pl.BlockSpec((S, LANES), lambda b, hp, qi: (b, nhp + hp)),
                  pl.BlockSpec((S, LANES), lambda b, hp, qi: (b, 2 * nhp + hp)),
                  pl.BlockSpec((Q_BLK, 2 * LANES), lambda b, hp, qi: (b * nq + qi, hp)), qblk, tri, tri],
        out_specs=[qblk, acc, acc],
        out_shape=[jax.ShapeDtypeStruct((T, SB_W), F32)] + [jax.ShapeDtypeStruct((B, nhp, nkb, LANES, K_BLK), F32)] * 2,
        compiler_params=_params(("arbitrary", "arbitrary", "arbitrary"), 48),
    )(qkv, qkv, qkv, run_tab, dy_sb, tri_suffix, tri_prefix)


def _untranspose(t, B, S):
    return t.transpose(0, 2, 4, 1, 3).reshape(B * S, SB_W)


def _lru_bwd(proj_lru, h, dy_lru, conv_w, conv_b, wa_bd, wx_bd, b_a, b_x, lam, B, S, lc):
    T = B * S
    nc = S // lc
    ncb = LRU_W // LANES
    hpc = lc // SUBLANES

    def body(x_ref, xh_ref, g_ref, h_ref, hh_ref, dy_ref, cw_ref, cb_ref, wa_ref, wx_ref, ba_ref, bx_ref, lam_ref,
             dx_ref, dg_ref, dcw_ref, dcb_ref, dwa_ref, dwx_ref, dba_ref, dbx_ref, dlam_ref,
             lam_s, dc_s, a_first, lam_first, dc_head):
        b, ci = pl.program_id(1), pl.program_id(2)
        first_chunk = ci == nc - 1

        @pl.when(ci == 0)
        def _():
            a_first[...] = jnp.zeros_like(a_first)
            lam_first[...] = jnp.zeros_like(lam_first)
            dc_head[...] = jnp.zeros_like(dc_head)

        @pl.when((b == 0) & (ci == 0))
        def _():
            for ref in (dcw_ref, dcb_ref, dwa_ref, dwx_ref, dba_ref, dbx_ref, dlam_ref):
                ref[...] = jnp.zeros_like(ref)

        x = x_ref[...]
        taps = _conv_taps(x, jnp.where(first_chunk, 0.0, xh_ref[...]), lc)
        c = cb_ref[...] + sum(cw_ref[pl.ds(CONV_K - 1 - k, 1), :] * taps[k] for k in range(CONV_K))
        r, i, sp, dsp_dlam, a, s = _lru_gates(c, wa_ref, wx_ref, ba_ref, bx_ref, lam_ref)
        hv = h_ref[...]
        he = jnp.concatenate([jnp.where(first_chunk, 0.0, hh_ref[...]), hv], axis=0)
        h_prev = pltpu.roll(he, 1, 0)[SUBLANES:SUBLANES + lc]
        dy = dy_ref[...]
        gelu, dgelu = _gelu_parts(g_ref[...])
        dg_ref[...] = dy * hv * dgelu

        row = lax.broadcasted_iota(jnp.int32, (lc, LANES), 0)
        a_next = jnp.where(row < lc - 1, pltpu.roll(a, lc - 1, 0), a_first[...])
        lam_loc, a_run = _scan(a_next, dy * gelu, reverse=True)
        lam_s[...] = lam_loc + a_run * lam_first[...]
        lam_t = lam_s[...]
        lam_first[...] = lam_s[pl.ds(0, 1), :]
        lam_s[...] = a
        a_first[...] = lam_s[pl.ds(0, 1), :]

        ic = i * c
        dlog_a = lam_t * h_prev * a - (lam_t * ic) * (a * a) / s
        dpre_r = (dlog_a * ((-LRU_C) * sp)) * r * (1.0 - r)
        dpre_i = (lam_t * s * c) * i * (1.0 - i)
        dlam_ref[...] += jnp.sum(dlog_a * r, axis=0, keepdims=True) * ((-LRU_C) * dsp_dlam)
        dr16 = dpre_r.astype(BF16)
        di16 = dpre_i.astype(BF16)
        c16 = c.astype(BF16)
        dwa_ref[0] += _dot_tn(c16, dr16)
        dwx_ref[0] += _dot_tn(c16, di16)
        dba_ref[...] += jnp.sum(dpre_r, axis=0, keepdims=True)
        dbx_ref[...] += jnp.sum(dpre_i, axis=0, keepdims=True)
        dc = lam_t * s * i + _dot_nt(dr16, wa_ref[0]) + _dot_nt(di16, wx_ref[0])
        dcb_ref[...] += jnp.sum(dc, axis=0, keepdims=True)
        for k in range(CONV_K):
            dcw_ref[pl.ds(CONV_K - 1 - k, 1), :] += jnp.sum(dc * taps[k], axis=0, keepdims=True)
        dce = jnp.concatenate([dc, dc_head[...]], axis=0)
        dx = cw_ref[pl.ds(CONV_K - 1, 1), :] * dc
        for k in range(1, CONV_K):
            dx = dx + cw_ref[pl.ds(CONV_K - 1 - k, 1), :] * pltpu.roll(dce, lc + SUBLANES - k, 0)[0:lc]
        dx_ref[...] = dx
        dc_s[...] = dc
        dc_head[...] = dc_s[pl.ds(0, SUBLANES), :]

    def chunk(col):
        return pl.BlockSpec((lc, LANES), lambda cb, b, ci: (b * nc + nc - 1 - ci, col(cb)))

    def halo(col):
        return pl.BlockSpec((SUBLANES, LANES),
                            lambda cb, b, ci: (jnp.maximum((b * nc + nc - 1 - ci) * hpc - 1, 0), col(cb)))

    chan = lambda cb, b, ci: (0, cb)
    blk = lambda cb, b, ci: (cb, 0, 0)
    vec = pl.BlockSpec((1, LANES), chan)
    mat = pl.BlockSpec((1, LANES, LANES), blk)
    return pl.pallas_call(
        body, name="lru_bwd", grid=(ncb, B, nc),
        in_specs=[chunk(lambda cb: cb), halo(lambda cb: cb), chunk(lambda cb: ncb + cb),
                  chunk(lambda cb: cb), halo(lambda cb: cb), chunk(lambda cb: cb),
                  pl.BlockSpec((CONV_K, LANES), chan), vec, mat, mat, vec, vec, vec],
        out_specs=[chunk(lambda cb: cb), chunk(lambda cb: cb), pl.BlockSpec((CONV_K, LANES), chan), vec,
                   mat, mat, vec, vec, vec],
        out_shape=[jax.ShapeDtypeStruct((T, LRU_W), F32), jax.ShapeDtypeStruct((T, LRU_W), F32),
                   jax.ShapeDtypeStruct((CONV_K, LRU_W), F32), jax.ShapeDtypeStruct((1, LRU_W), F32),
                   jax.ShapeDtypeStruct((ncb, LANES, LANES), F32), jax.ShapeDtypeStruct((ncb, LANES, LANES), F32),
                   jax.ShapeDtypeStruct((1, LRU_W), F32), jax.ShapeDtypeStruct((1, LRU_W), F32),
                   jax.ShapeDtypeStruct((1, LRU_W), F32)],
        scratch_shapes=[pltpu.VMEM((lc, LANES), F32), pltpu.VMEM((lc, LANES), F32), pltpu.VMEM((1, LANES), F32),
                        pltpu.VMEM((1, LANES), F32), pltpu.VMEM((SUBLANES, LANES), F32)],
        compiler_params=_params(("arbitrary", "arbitrary", "arbitrary"), 32),
    )(proj_lru, proj_lru, proj_lru, h, h, dy_lru, conv_w, conv_b, wa_bd, wx_bd, b_a, b_x, lam)


def _bwd_in(pieces, w_in16, x2, g1, dh1, tm):
    T = x2.shape[0]
    npc = len(pieces)

    def body(*refs):
        p_refs = refs[:npc]
        w_ref, x_ref, g_ref, d_ref, dx_ref, dproj_ref, xn_ref, dg1_ref = refs[npc:]

        @pl.when(pl.program_id(0) == 0)
        def _():
            dg1_ref[...] = jnp.zeros_like(dg1_ref)

        dxn = jnp.zeros((tm, D_MODEL), F32)
        for n, p_ref in enumerate(p_refs):
            cols = slice(n * LRU_W, (n + 1) * LRU_W)
            p16 = p_ref[...].astype(BF16)
            dproj_ref[:, cols] = p16
            dxn = dxn + _dot_nt(p16, w_ref[:, cols])
        gv = g_ref[...]
        xn, xhat, rstd = _rms_fwd(x_ref[...], gv)
        xn_ref[...] = xn.astype(BF16)
        dg1_ref[...] += jnp.sum(dxn * xhat, axis=0, keepdims=True)
        dx_ref[...] = d_ref[...] + _rms_bwd(dxn, xhat, rstd, gv)

    row = lambda i: (i, 0)
    fix = lambda i: (0, 0)
    return pl.pallas_call(
        body, name="bwd_in", grid=(T // tm,),
        in_specs=[pl.BlockSpec((tm, LRU_W), row)] * npc + [
            pl.BlockSpec((D_MODEL, IN_COLS), fix), pl.BlockSpec((tm, D_MODEL), row),
            pl.BlockSpec((1, D_MODEL), fix), pl.BlockSpec((tm, D_MODEL), row)],
        out_specs=[pl.BlockSpec((tm, D_MODEL), row), pl.BlockSpec((tm, IN_COLS), row),
                   pl.BlockSpec((tm, D_MODEL), row), pl.BlockSpec((1, D_MODEL), fix)],
        out_shape=[jax.ShapeDtypeStruct((T, D_MODEL), F32), jax.ShapeDtypeStruct((T, IN_COLS), BF16),
                   jax.ShapeDtypeStruct((T, D_MODEL), BF16), jax.ShapeDtypeStruct((1, D_MODEL), F32)],
        compiler_params=_params(("arbitrary",), 56),
    )(*pieces, w_in16, x2, g1, dh1)


def _adam_shard(name, parts, w, m, v, tr):
    R, C = w.shape

    def body(p_ref, w_ref, m_ref, v_ref, g_ref, d_ref, m2_ref, v2_ref):
        g = p_ref[0]
        for p in range(1, N_DEV):
            g = g + p_ref[p]
        g_ref[...] = g
        d_ref[...], m2_ref[...], v2_ref[...] = _adamw(w_ref[...], g, m_ref[...], v_ref[...])

    blk = pl.BlockSpec((tr, C), lambda i: (i, 0))
    return pl.pallas_call(
        body, name=name, grid=(R // tr,),
        in_specs=[pl.BlockSpec((N_DEV, tr, C), lambda i: (0, i, 0)), blk, blk, blk],
        out_specs=[blk] * 4, out_shape=[jax.ShapeDtypeStruct((R, C), F32)] * 4,
        compiler_params=_params(("arbitrary",), 48),
    )(parts, w, m, v)


def _sum_parts(name, parts):
    def body(p_ref, g_ref):
        g = p_ref[0]
        for p in range(1, N_DEV):
            g = g + p_ref[p]
        g_ref[...] = g

    return pl.pallas_call(body, name=name, out_shape=jax.ShapeDtypeStruct(parts.shape[1:], F32))(parts)


def _adam_packed(w, g, m, v):
    def body(w_ref, g_ref, m_ref, v_ref, d_ref, m2_ref, v2_ref):
        d_ref[...], m2_ref[...], v2_ref[...] = _adamw(w_ref[...], g_ref[...], m_ref[...], v_ref[...])

    return pl.pallas_call(body, name="adam_small", out_shape=[jax.ShapeDtypeStruct(w.shape, F32)] * 3)(w, g, m, v)


def _pack(vals, rows):
    flat = jnp.concatenate([v.reshape(-1).astype(F32) for v in vals])
    return jnp.pad(flat, (0, rows * LANES - flat.shape[0])).reshape(rows, LANES)


def _unpack(packed, sizes):
    flat = packed.reshape(-1)
    out, off = [], 0
    for n in sizes:
        out.append(flat[off:off + n])
        off += n
    return out


def _block_diag_pairs(w):
    w = w.reshape(LRU_BLOCKS // 2, 2, HEAD_D, HEAD_D)
    out = jnp.zeros((LRU_BLOCKS // 2, LANES, LANES), w.dtype)
    out = out.at[:, :HEAD_D, :HEAD_D].set(w[:, 0])
    return out.at[:, HEAD_D:, HEAD_D:].set(w[:, 1])


def _diag_blocks(w):
    return jnp.stack([w[:, :HEAD_D, :HEAD_D], w[:, HEAD_D:, HEAD_D:]], axis=1).reshape(LRU_BLOCKS, HEAD_D, HEAD_D)


def kernel(x, norm1_g, w_in, conv_w, conv_b, lru_w_a, lru_b_a, lru_w_x, lru_b_x, lru_lambda, lru_out_g, sb_out_g, w_out, norm2_g, w_up, w_down, final_g, loss_target, m_norm1_g, m_w_in, m_conv_w, m_conv_b, m_lru_w_a, m_lru_b_a, m_lru_w_x, m_lru_b_x, m_lru_lambda, m_lru_out_g, m_sb_out_g, m_w_out, m_norm2_g, m_w_up, m_w_down, m_final_g, v_norm1_g, v_w_in, v_conv_w, v_conv_b, v_lru_w_a, v_lru_b_a, v_lru_w_x, v_lru_b_x, v_lru_lambda, v_lru_out_g, v_sb_out_g, v_w_out, v_norm2_g, v_w_up, v_w_down, v_final_g):
    B, S, _ = x.shape
    T = B * S
    tm = min(512, T)
    lc = min(512, S)
    me = _my_index()
    x2 = x.reshape(T, D_MODEL)
    tgt = loss_target.reshape(T, D_MODEL)
    cw_cols = CONV_K * LRU_W // N_DEV // CONV_K

    shards16 = _cast_shards([w_in[0], w_out[0], w_up[0], w_down[0]])
    cw_pad = jnp.zeros((SUBLANES, LANES), F32).at[:CONV_K, :cw_cols].set(conv_w[0])
    g_in, g_out, g_up, g_down, g_cw = _exchange("gather_weights", list(shards16) + [cw_pad], [False] * 5)
    w_in16 = g_in.transpose(1, 0, 2).reshape(D_MODEL, IN_COLS)
    w_out16 = g_out.reshape(D_MODEL, D_MODEL)
    w_down16 = g_down.reshape(D_FF, D_MODEL)
    conv_w_full = g_cw[:, :CONV_K, :cw_cols].transpose(1, 0, 2).reshape(CONV_K, LRU_W)
    wa_bd = _block_diag_pairs(lru_w_a[0]).astype(BF16)
    wx_bd = _block_diag_pairs(lru_w_x[0]).astype(BF16)
    b_a = lru_b_a.reshape(1, LRU_W)
    b_x = lru_b_x.reshape(1, LRU_W)
    gf = final_g.reshape(1, D_MODEL)

    proj_lru, qkv = _fwd_in(x2, norm1_g, w_in16, tm)
    y_lru, h = _lru_fwd(proj_lru, conv_w_full, conv_b, wa_bd, wx_bd, b_a, b_x, lru_lambda, B, S, lc)
    tri_suffix, tri_prefix = _tri(False), _tri(True)
    y_sb, run_tab = _attn_fwd(qkv, tri_suffix, B, S)
    h1, mix16 = _fwd_mix(y_lru, y_sb, lru_out_g, sb_out_g, w_out16, x2, tm)
    up16, dh2, d_final_g, loss_part = _fwd_mlp(h1, norm2_g, g_up, w_down16, gf, tgt, tm)

    dup16, dh1, hn16, dh2b, d_norm2_g = _bwd_mlp(dh2, up16, h1, norm2_g, g_up, w_down16, tm)
    sq = lambda u: (u.astype(F32) * u.astype(F32)).astype(BF16)
    gw_up = _matmul_tn("grad_w_up", hn16, dup16, D_MODEL, FF_CHUNK, tm, (N_DEV, D_MODEL, FF_CHUNK),
                       (None, D_MODEL, FF_CHUNK), lambda m, n, k: (n, 0, 0))
    gw_down = _matmul_tn("grad_w_down", up16, dh2b, FF_CHUNK, D_MODEL, tm, (N_DEV, FF_CHUNK, D_MODEL),
                         (None, FF_CHUNK, D_MODEL), lambda m, n, k: (m, 0, 0), a_prep=sq)
    dy_lru, dy_sb, d_lru_out_g, d_sb_out_g = _bwd_mix(dh1, w_out16, y_lru, y_sb, lru_out_g, sb_out_g, tm)
    gw_out = _matmul_tn("grad_w_out", mix16, dh1, D_MODEL, FF_CHUNK, tm, (D_MODEL, D_MODEL),
                        (D_MODEL, FF_CHUNK), lambda m, n, k: (0, n), b_prep=lambda u: u.astype(BF16))
    dq, dkt, dvt = _attn_bwd(qkv, run_tab, dy_sb, tri_suffix, tri_prefix, B, S)
    dk, dv = _untranspose(dkt, B, S), _untranspose(dvt, B, S)
    (dx_lru, dg_lru, d_conv_w, d_conv_b, d_wa, d_wx, d_b_a, d_b_x, d_lambda) = _lru_bwd(
        proj_lru, h, dy_lru, conv_w_full, conv_b, wa_bd, wx_bd, b_a, b_x, lru_lambda, B, S, lc)
    dx, dproj16, xn16, d_norm1_g = _bwd_in([dx_lru, dg_lru, dq, dk, dv], w_in16, x2, norm1_g, dh1, tm)
    gw_in = _matmul_tn("grad_w_in", xn16, dproj16, D_MODEL, FF_CHUNK, tm, (D_MODEL, IN_COLS),
                       (D_MODEL, FF_CHUNK), lambda m, n, k: (0, n))

    small_grads = {"norm1_g": d_norm1_g, "conv_b": d_conv_b, "lru_w_a": _diag_blocks(d_wa), "lru_b_a": d_b_a,
                   "lru_w_x": _diag_blocks(d_wx), "lru_b_x": d_b_x, "lru_lambda": d_lambda,
                   "lru_out_g": d_lru_out_g, "sb_out_g": d_sb_out_g, "norm2_g": d_norm2_g, "final_g": d_final_g}
    packed = _pack([small_grads[n] for n, _ in SMALL] + [d_conv_w, loss_part], EXCH_ROWS)
    parts_in = gw_in.reshape(D_MODEL, N_DEV, IN_COLS // N_DEV).transpose(1, 0, 2)
    parts_out = gw_out.reshape(N_DEV, D_MODEL // N_DEV, D_MODEL)
    r_in, r_out, r_up, r_down, r_small = _exchange(
        "exchange_grads", [parts_in, parts_out, gw_up, gw_down, packed], [True, True, True, True, False])

    g_w_in, d_w_in, nm_w_in, nv_w_in = _adam_shard("adam_w_in", r_in, w_in[0], m_w_in[0], v_w_in[0], 256)
    g_w_out, d_w_out, nm_w_out, nv_w_out = _adam_shard("adam_w_out", r_out, w_out[0], m_w_out[0], v_w_out[0], 64)
    g_w_up, d_w_up, nm_w_up, nv_w_up = _adam_shard("adam_w_up", r_up, w_up[0], m_w_up[0], v_w_up[0], 256)
    g_w_down, d_w_down, nm_w_down, nv_w_down = _adam_shard("adam_w_down", r_down, w_down[0], m_w_down[0], v_w_down[0], 128)

    total = _sum_parts("sum_small", r_small)
    sizes = [n for _, n in SMALL]
    small_g = _unpack(total, sizes + [CONV_K * LRU_W, 1])
    loss = small_g[-1][0]
    g_conv_w = lax.dynamic_slice_in_dim(small_g[-2].reshape(CONV_K, LRU_W), me * cw_cols, cw_cols, axis=1)
    given = dict(norm1_g=(norm1_g, m_norm1_g, v_norm1_g), conv_b=(conv_b, m_conv_b, v_conv_b),
                 lru_w_a=(lru_w_a, m_lru_w_a, v_lru_w_a), lru_b_a=(lru_b_a, m_lru_b_a, v_lru_b_a),
                 lru_w_x=(lru_w_x, m_lru_w_x, v_lru_w_x), lru_b_x=(lru_b_x, m_lru_b_x, v_lru_b_x),
                 lru_lambda=(lru_lambda, m_lru_lambda, v_lru_lambda), lru_out_g=(lru_out_g, m_lru_out_g, v_lru_out_g),
                 sb_out_g=(sb_out_g, m_sb_out_g, v_sb_out_g), norm2_g=(norm2_g, m_norm2_g, v_norm2_g),
                 final_g=(final_g, m_final_g, v_final_g))
    names = [n for n, _ in SMALL]
    pw = _pack([given[n][0] for n in names] + [conv_w], ADAM_ROWS)
    pm = _pack([given[n][1] for n in names] + [m_conv_w], ADAM_ROWS)
    pv = _pack([given[n][2] for n in names] + [v_conv_w], ADAM_ROWS)
    pg = _pack(small_g[:len(names)] + [g_conv_w], ADAM_ROWS)
    pd, pm2, pv2 = _adam_packed(pw, pg, pm, pv)
    asz = sizes + [conv_w.size]
    shapes = {n: given[n][0].shape for n in names}
    shapes["conv_w"] = conv_w.shape
    order = names + ["conv_w"]
    g_small = dict(zip(order, [a.reshape(shapes[n]) for n, a in zip(order, _unpack(pg, asz))]))
    d_small = dict(zip(order, [a.reshape(shapes[n]) for n, a in zip(order, _unpack(pd, asz))]))
    m_small = dict(zip(order, [a.reshape(shapes[n]) for n, a in zip(order, _unpack(pm2, asz))]))
    v_small = dict(zip(order, [a.reshape(shapes[n]) for n, a in zip(order, _unpack(pv2, asz))]))

    big = {"w_in": (g_w_in, d_w_in, nm_w_in, nv_w_in), "w_out": (g_w_out, d_w_out, nm_w_out, nv_w_out),
           "w_up": (g_w_up, d_w_up, nm_w_up, nv_w_up), "w_down": (g_w_down, d_w_down, nm_w_down, nv_w_down)}
    weights = ["norm1_g", "w_in", "conv_w", "conv_b", "lru_w_a", "lru_b_a", "lru_w_x", "lru_b_x", "lru_lambda",
               "lru_out_g", "sb_out_g", "w_out", "norm2_g", "w_up", "w_down", "final_g"]

    def leaf(n, kind):
        if n in big:
            return big[n][kind][None]
        return (g_small, d_small, m_small, v_small)[kind][n]

    return (loss, dx.reshape(B, S, D_MODEL), *[leaf(n, 0) for n in weights], *[leaf(n, 1) for n in weights],
            *[leaf(n, 2) for n in weights], *[leaf(n, 3) for n in weights])
```

```python
import jax
import jax.numpy as jnp
from jax import lax
from jax.experimental import pallas as pl
from jax.experimental.pallas import tpu as pltpu

F32 = jnp.float32
BF16 = jnp.bfloat16

D_MODEL = 1024
LRU_W = 512
SB_W = 512
HEAD_D = 64
D_FF = 4096
IN_COLS = 2 * LRU_W + 3 * SB_W
CONV_K = 4
LRU_BLOCKS = 8
LRU_C = 8.0
EPS = 1e-6
N_DEV = 8
LANES = 128
SUBLANES = 8
FF_CHUNK = 512
Q_BLK = 128
K_BLK = 256
Q_PER_K = K_BLK // Q_BLK

ADAM_LR = 0.001
ADAM_B1 = 0.9
ADAM_B2 = 0.999
ADAM_EPS = 1e-08
ADAM_WD = 0.01
ADAM_STEP = 10

SMALL = (("norm1_g", 1024), ("conv_b", 512), ("lru_w_a", 32768), ("lru_b_a", 512), ("lru_w_x", 32768),
         ("lru_b_x", 512), ("lru_lambda", 512), ("lru_out_g", 512), ("sb_out_g", 512), ("norm2_g", 1024),
         ("final_g", 1024))
SMALL_ROWS = sum(n for _, n in SMALL) // LANES
EXCH_ROWS = SMALL_ROWS + (CONV_K * LRU_W) // LANES + 8
ADAM_ROWS = SMALL_ROWS + 8


def _params(sem=None, vmem_mb=None):
    kw = {}
    if sem is not None:
        kw["dimension_semantics"] = sem
    if vmem_mb is not None:
        kw["vmem_limit_bytes"] = vmem_mb << 20
    return pltpu.CompilerParams(**kw)


def _dot(a, b):
    return jnp.dot(a, b, preferred_element_type=F32)


def _dot_nt(a, b):
    return lax.dot_general(a, b, (((1,), (1,)), ((), ())), preferred_element_type=F32)


def _dot_tn(a, b):
    return lax.dot_general(a, b, (((0,), (0,)), ((), ())), preferred_element_type=F32)


def _rms_fwd(x, g):
    rstd = lax.rsqrt(jnp.mean(x * x, axis=-1, keepdims=True) + EPS)
    xhat = x * rstd
    return xhat * g, xhat, rstd


def _rms_bwd(dy, xhat, rstd, g):
    dxhat = dy * g
    return rstd * (dxhat - xhat * jnp.mean(dxhat * xhat, axis=-1, keepdims=True))


def _sigmoid(x):
    return 1.0 / (1.0 + jnp.exp(-x))


def _log1p_pos(e):
    series = e * (1.0 - e * (0.5 - e * (1.0 / 3.0 - e * 0.25)))
    return jnp.where(e < 1e-2, series, jnp.log(1.0 + e))


def _neg_expm1(x):
    series = -x * (1.0 + x * (0.5 + x * (1.0 / 6.0 + x * (1.0 / 24.0))))
    return jnp.where(x > -1e-2, series, 1.0 - jnp.exp(x))


def _gelu_parts(g):
    k0 = 0.7978845608028654
    k1 = 0.044715
    t = jnp.tanh(k0 * (g + k1 * g * g * g))
    val = 0.5 * g * (1.0 + t)
    grad = 0.5 * (1.0 + t) + 0.5 * g * (1.0 - t * t) * k0 * (1.0 + 3.0 * k1 * g * g)
    return val, grad


def _scan(a, b, reverse):
    n = a.shape[0]
    row = lax.broadcasted_iota(jnp.int32, a.shape, 0)
    s = 1
    while s < n:
        if reverse:
            keep = row < n - s
            shift = n - s
        else:
            keep = row >= s
            shift = s
        bs = jnp.where(keep, pltpu.roll(b, shift, 0), 0.0)
        a_s = jnp.where(keep, pltpu.roll(a, shift, 0), 1.0)
        b = a * bs + b
        a = a * a_s
        s *= 2
    return b, a


def _adamw(w, g, m, v):
    m = ADAM_B1 * m + (1.0 - ADAM_B1) * g
    v = ADAM_B2 * v + (1.0 - ADAM_B2) * (g * g)
    m_hat = m / (1.0 - ADAM_B1 ** ADAM_STEP)
    v_hat = v / (1.0 - ADAM_B2 ** ADAM_STEP)
    delta = -ADAM_LR * (m_hat / (jnp.sqrt(v_hat) + ADAM_EPS) + ADAM_WD * w)
    return delta, m, v


def _my_index():
    return 4 * lax.axis_index("x") + 2 * lax.axis_index("y") + lax.axis_index("c")


def _peer(k):
    x, y, c = lax.axis_index("x"), lax.axis_index("y"), lax.axis_index("c")
    px = 1 - x if (k >> 2) & 1 else x
    py = 1 - y if (k >> 1) & 1 else y
    pc = 1 - c if k & 1 else c
    return (px, py, pc), 4 * px + 2 * py + pc


def _exchange_shapes(srcs, sliced):
    n = len(srcs)
    out_shape = [jax.ShapeDtypeStruct(s.shape if sl else (N_DEV,) + s.shape, s.dtype) for s, sl in zip(srcs, sliced)]
    specs = [pl.BlockSpec(memory_space=pl.ANY)] * n
    sems = [pltpu.SemaphoreType.DMA((n, N_DEV - 1)), pltpu.SemaphoreType.DMA((n, N_DEV - 1)),
            pltpu.SemaphoreType.DMA((n,))]
    return out_shape, specs, sems


def _exchange_copies(ins, outs, sliced, send_sems, recv_sems, local_sems):
    n = len(ins)

    def part(a, p):
        return ins[a].at[p] if sliced[a] else ins[a]

    def copies(receiving):
        me = _my_index()
        local = [pltpu.make_async_copy(part(a, me), outs[a].at[me], local_sems.at[a]) for a in range(n)]
        remote = []
        for k in range(1, N_DEV):
            dev, idx = _peer(k)
            for a in range(n):
                remote.append(pltpu.make_async_remote_copy(
                    src_ref=part(a, idx), dst_ref=outs[a].at[idx if receiving else me],
                    send_sem=send_sems.at[a, k - 1], recv_sem=recv_sems.at[a, k - 1],
                    device_id=dev, device_id_type=pl.DeviceIdType.MESH))
        return local, remote

    def start():
        local, remote = copies(receiving=False)
        for cp in local + remote:
            cp.start()

    def wait():
        local, remote = copies(receiving=True)
        for cp in remote + local:
            cp.wait()

    return start, wait


def _exchange(name, srcs, sliced):
    n = len(srcs)
    out_shape, specs, sems = _exchange_shapes(srcs, sliced)

    def body(*refs):
        start, wait = _exchange_copies(refs[:n], refs[n:2 * n], sliced, *refs[2 * n:])
        start()
        wait()

    return pl.pallas_call(body, name=name, out_shape=out_shape, in_specs=specs, out_specs=specs,
                          scratch_shapes=sems)(*srcs)


def _cast_shards(ws):
    def body(*refs):
        for i in range(len(ws)):
            refs[len(ws) + i][...] = refs[i][...].astype(BF16)

    return pl.pallas_call(
        body, name="cast_shards", out_shape=[jax.ShapeDtypeStruct(w.shape, BF16) for w in ws],
        compiler_params=_params(vmem_mb=32),
    )(*ws)


def _fwd_in(x2, g1, w_in16, tm):
    T = x2.shape[0]

    def body(x_ref, g_ref, w_ref, lru_ref, qkv_ref):
        xn, _, _ = _rms_fwd(x_ref[...], g_ref[...])
        xn = xn.astype(BF16)
        lru_ref[...] = _dot(xn, w_ref[:, 0:2 * LRU_W])
        qkv_ref[...] = _dot(xn, w_ref[:, 2 * LRU_W:IN_COLS]).astype(BF16)

    return pl.pallas_call(
        body, name="fwd_in", grid=(T // tm,),
        in_specs=[pl.BlockSpec((tm, D_MODEL), lambda i: (i, 0)),
                  pl.BlockSpec((1, D_MODEL), lambda i: (0, 0)),
                  pl.BlockSpec((D_MODEL, IN_COLS), lambda i: (0, 0))],
        out_specs=[pl.BlockSpec((tm, 2 * LRU_W), lambda i: (i, 0)),
                   pl.BlockSpec((tm, 3 * SB_W), lambda i: (i, 0))],
        out_shape=[jax.ShapeDtypeStruct((T, 2 * LRU_W), F32), jax.ShapeDtypeStruct((T, 3 * SB_W), BF16)],
        compiler_params=_params(("arbitrary",), 48),
    )(x2, g1, w_in16)


def _lru_gates(c, wa_ref, wx_ref, ba_ref, bx_ref, lam_ref):
    c16 = c.astype(BF16)
    r = _sigmoid(_dot(c16, wa_ref[0]) + ba_ref[...])
    i = _sigmoid(_dot(c16, wx_ref[0]) + bx_ref[...])
    lam = lam_ref[...]
    e = jnp.exp(-jnp.abs(lam))
    sp = jnp.maximum(-lam, 0.0) + _log1p_pos(e)
    dsp_dlam = -jnp.where(lam >= 0.0, e, 1.0) / (1.0 + e)
    log_a = (-LRU_C) * r * sp
    a = jnp.exp(log_a)
    s = jnp.sqrt(_neg_expm1(2.0 * log_a))
    return r, i, sp, dsp_dlam, a, s


def _conv_taps(x, halo, lc):
    xe = jnp.concatenate([halo, x], axis=0)
    return [x] + [pltpu.roll(xe, k, 0)[SUBLANES:SUBLANES + lc] for k in range(1, CONV_K)]


def _lru_fwd(proj_lru, conv_w, conv_b, wa_bd, wx_bd, b_a, b_x, lam, B, S, lc):
    T = B * S
    nc = S // lc
    ncb = LRU_W // LANES

    def body(x_ref, g_ref, cw_ref, cb_ref, wa_ref, wx_ref, ba_ref, bx_ref, lam_ref, y_ref, h_ref, tail, carry):
        ci = pl.program_id(2)

        @pl.when(ci == 0)
        def _():
            tail[...] = jnp.zeros_like(tail)
            carry[...] = jnp.zeros_like(carry)

        x = x_ref[...]
        taps = _conv_taps(x, tail[...], lc)
        c = cb_ref[...] + sum(cw_ref[pl.ds(CONV_K - 1 - k, 1), :] * taps[k] for k in range(CONV_K))
        tail[...] = x_ref[pl.ds(lc - SUBLANES, SUBLANES), :]
        r, i, sp, _, a, s = _lru_gates(c, wa_ref, wx_ref, ba_ref, bx_ref, lam_ref)
        h_loc, a_run = _scan(a, s * (i * c), reverse=False)
        h_ref[...] = h_loc + a_run * carry[...]
        carry[...] = h_ref[pl.ds(lc - 1, 1), :]
        gelu, _ = _gelu_parts(g_ref[...])
        y_ref[...] = h_ref[...] * gelu

    chan = lambda b, cb, ci: (0, cb)
    return pl.pallas_call(
        body, name="lru_fwd", grid=(B, ncb, nc),
        in_specs=[pl.BlockSpec((lc, LANES), lambda b, cb, ci: (b * nc + ci, cb)),
                  pl.BlockSpec((lc, LANES), lambda b, cb, ci: (b * nc + ci, ncb + cb)),
                  pl.BlockSpec((CONV_K, LANES), chan), pl.BlockSpec((1, LANES), chan),
                  pl.BlockSpec((1, LANES, LANES), lambda b, cb, ci: (cb, 0, 0)),
                  pl.BlockSpec((1, LANES, LANES), lambda b, cb, ci: (cb, 0, 0)),
                  pl.BlockSpec((1, LANES), chan), pl.BlockSpec((1, LANES), chan), pl.BlockSpec((1, LANES), chan)],
        out_specs=[pl.BlockSpec((lc, LANES), lambda b, cb, ci: (b * nc + ci, cb))] * 2,
        out_shape=[jax.ShapeDtypeStruct((T, LRU_W), F32)] * 2,
        scratch_shapes=[pltpu.VMEM((SUBLANES, LANES), F32), pltpu.VMEM((1, LANES), F32)],
        compiler_params=_params(("arbitrary", "arbitrary", "arbitrary"), 32),
    )(proj_lru, proj_lru, conv_w, conv_b, wa_bd, wx_bd, b_a, b_x, lam)


def _cumsum_mm(v, tri):
    hi = v.astype(BF16)
    lo = (v - hi.astype(F32)).astype(BF16)
    return _dot(hi, tri) + _dot(lo, tri)


def _tri(prefix):
    r = lax.broadcasted_iota(jnp.int32, (K_BLK, K_BLK), 0)
    c = lax.broadcasted_iota(jnp.int32, (K_BLK, K_BLK), 1)
    return ((r <= c) if prefix else (r >= c)).astype(BF16)


def _attn_consts(qi):
    r = lax.broadcasted_iota(jnp.int32, (Q_BLK, K_BLK), 0)
    c = lax.broadcasted_iota(jnp.int32, (Q_BLK, K_BLK), 1)
    causal = c + ((qi // Q_PER_K) * K_BLK - qi * Q_BLK) < r
    lane = lax.broadcasted_iota(jnp.int32, (1, LANES), 1)
    return causal, c, lane, (lane < HEAD_D, lane >= HEAD_D)


def _log1m_beta(z, mask):
    lg = -(jnp.maximum(z, 0.0) + jnp.log(1.0 + jnp.exp(-jnp.abs(z))))
    return lg if mask is None else jnp.where(mask, lg, 0.0)


def _key_rows(j):
    return pl.ds(pl.multiple_of(j * K_BLK, K_BLK), K_BLK)


def _attn_fwd(qkv, tri_suffix, ride, ride_sliced, B, S):
    T = B * S
    nq = S // Q_BLK
    nhp = SB_W // LANES
    scale = HEAD_D ** -0.5
    assert S // K_BLK <= LANES
    nr = len(ride)
    ride_shape, ride_specs, ride_sems = _exchange_shapes(ride, ride_sliced)

    def body(q_ref, k_ref, v_ref, tri_ref, *rest):
        o_ref, run_ref = rest[nr:nr + 2]
        start_ride, wait_ride = _exchange_copies(rest[:nr], rest[nr + 2:2 * nr + 2], ride_sliced, *rest[2 * nr + 2:])
        qi = pl.program_id(2)
        step_no = (pl.program_id(0) * nhp + pl.program_id(1)) * nq + qi
        pl.when(step_no == 0)(start_ride)
        jd = qi // Q_PER_K
        causal, col, lane, halves = _attn_consts(qi)
        q = q_ref[...]
        qh = [jnp.where(hm, q, jnp.zeros_like(q)) * jnp.asarray(scale, BF16) for hm in halves]

        def group(blocks, carry):
            runs, tables, acc = carry
            runs, tables = list(runs), list(tables)
            ks = [k_ref[_key_rows(jl), :] for jl, _, _ in blocks]
            vs = [v_ref[_key_rows(jl), :] for jl, _, _ in blocks]
            chains = [(b, h) for b in range(len(blocks)) for h in range(2)]
            z = {c: _dot_nt(qh[c[1]], ks[c[0]]) for c in chains}
            lg = {c: _log1m_beta(z[c], blocks[c[0]][2]) for c in chains}
            suf = {c: _cumsum_mm(lg[c], tri_ref[...]) for c in chains}
            att = {}
            for b, h in chains:
                _, jlane, mask = blocks[b]
                a = jnp.exp(z[b, h] + suf[b, h] + runs[h])
                att[b, h] = (a if mask is None else jnp.where(mask, a, 0.0)).astype(BF16)
                tables[h] = jnp.where(lane == jlane, runs[h], tables[h])
                runs[h] = runs[h] + suf[b, h][:, 0:1]
            for b, h in chains:
                acc = acc + _dot(att[b, h], jnp.where(halves[h], vs[b], jnp.zeros_like(vs[b])))
            return tuple(runs), tuple(tables), acc

        col0 = jnp.zeros((Q_BLK, 1), F32)
        zero = jnp.zeros((Q_BLK, LANES), F32)
        two = jd % 2
        before = (jnp.maximum(jd - 1, 0), jnp.where(two == 1, jd - 1, -1), col < two * K_BLK)
        carry = group([(jd, jd, causal), before], ((col0, col0), (zero, zero), zero))
        top = jd - 1 - two

        def step(it, cr):
            ja = top - 2 * it
            return group([(ja, ja, None), (ja - 1, ja - 1, None)], cr)

        _, tables, acc = lax.fori_loop(0, (top + 1) // 2, step, carry)
        o_ref[...] = acc
        run_ref[:, 0:LANES] = tables[0]
        run_ref[:, LANES:2 * LANES] = tables[1]
        pl.when(step_no == B * nhp * nq - 1)(wait_ride)

    return pl.pallas_call(
        body, name="attn_fwd", grid=(B, nhp, nq),
        in_specs=[pl.BlockSpec((Q_BLK, LANES), lambda b, hp, qi: (b * nq + qi, hp)),
                  pl.BlockSpec((S, LANES), lambda b, hp, qi: (b, nhp + hp)),
                  pl.BlockSpec((S, LANES), lambda b, hp, qi: (b, 2 * nhp + hp)),
                  pl.BlockSpec((K_BLK, K_BLK), lambda b, hp, qi: (0, 0))] + ride_specs,
        out_specs=[pl.BlockSpec((Q_BLK, LANES), lambda b, hp, qi: (b * nq + qi, hp)),
                   pl.BlockSpec((Q_BLK, 2 * LANES), lambda b, hp, qi: (b * nq + qi, hp))] + ride_specs,
        out_shape=[jax.ShapeDtypeStruct((T, SB_W), F32), jax.ShapeDtypeStruct((T, 2 * SB_W), F32)] + ride_shape,
        scratch_shapes=ride_sems,
        compiler_params=_params(("arbitrary", "arbitrary", "arbitrary"), 48),
    )(qkv, qkv, qkv, tri_suffix, *ride)


def _fwd_mix(y_lru, y_sb, ga, gb, w_out16, x2, tm):
    T = x2.shape[0]

    def body(yl_ref, ys_ref, ga_ref, gb_ref, w_ref, x_ref, h1_ref, mix_ref):
        na, _, _ = _rms_fwd(yl_ref[...], ga_ref[...])
        nb, _, _ = _rms_fwd(ys_ref[...], gb_ref[...])
        na = na.astype(BF16)
        nb = nb.astype(BF16)
        mix_ref[:, 0:LRU_W] = na
        mix_ref[:, LRU_W:D_MODEL] = nb
        h1_ref[...] = x_ref[...] + _dot(na, w_ref[0:LRU_W, :]) + _dot(nb, w_ref[LRU_W:D_MODEL, :])

    row = lambda i: (i, 0)
    fix = lambda i: (0, 0)
    return pl.pallas_call(
        body, name="fwd_mix", grid=(T // tm,),
        in_specs=[pl.BlockSpec((tm, LRU_W), row), pl.BlockSpec((tm, SB_W), row),
                  pl.BlockSpec((1, LRU_W), fix), pl.BlockSpec((1, SB_W), fix),
                  pl.BlockSpec((D_MODEL, D_MODEL), fix), pl.BlockSpec((tm, D_MODEL), row)],
        out_specs=[pl.BlockSpec((tm, D_MODEL), row), pl.BlockSpec((tm, D_MODEL), row)],
        out_shape=[jax.ShapeDtypeStruct((T, D_MODEL), F32), jax.ShapeDtypeStruct((T, D_MODEL), BF16)],
        compiler_params=_params(("arbitrary",), 48),
    )(y_lru, y_sb, ga, gb, w_out16, x2)


def _fwd_mlp(h1, g2, w_up16, w_down16, gf, tgt, tm):
    T = h1.shape[0]
    nf = D_FF // FF_CHUNK

    def body(h1_ref, g2_ref, wu_ref, wd_ref, gf_ref, t_ref, up_ref, dh2_ref, dgf_ref, loss_ref, hn_s, acc):
        i, j = pl.program_id(0), pl.program_id(1)

        @pl.when(j == 0)
        def _():
            h1v = h1_ref[...]
            hn, _, _ = _rms_fwd(h1v, g2_ref[...])
            hn_s[...] = hn.astype(BF16)
            acc[...] = h1v

        up = jnp.maximum(_dot(hn_s[...], wu_ref[...]), 0.0)
        up_ref[...] = up.astype(BF16)
        acc[...] += _dot((up * up).astype(BF16), wd_ref[...])

        @pl.when((i == 0) & (j == 0))
        def _():
            dgf_ref[...] = jnp.zeros_like(dgf_ref)
            loss_ref[...] = jnp.zeros_like(loss_ref)

        @pl.when(j == nf - 1)
        def _():
            gfv = gf_ref[...]
            y, xhat, rstd = _rms_fwd(acc[...], gfv)
            err = y - t_ref[...]
            loss_ref[...] += jnp.sum(0.5 * jnp.sum(err * err, axis=-1, keepdims=True) * (1.0 / D_MODEL))
            dy = err * (1.0 / D_MODEL)
            dgf_ref[...] += jnp.sum(dy * xhat, axis=0, keepdims=True)
            dh2_ref[...] = _rms_bwd(dy, xhat, rstd, gfv)

    row = lambda i, j: (i, 0)
    fix = lambda i, j: (0, 0)
    return pl.pallas_call(
        body, name="fwd_mlp", grid=(T // tm, nf),
        in_specs=[pl.BlockSpec((tm, D_MODEL), row), pl.BlockSpec((1, D_MODEL), fix),
                  pl.BlockSpec((None, D_MODEL, FF_CHUNK), lambda i, j: (j, 0, 0)),
                  pl.BlockSpec((FF_CHUNK, D_MODEL), lambda i, j: (j, 0)),
                  pl.BlockSpec((1, D_MODEL), fix), pl.BlockSpec((tm, D_MODEL), row)],
        out_specs=[pl.BlockSpec((tm, FF_CHUNK), lambda i, j: (i, j)), pl.BlockSpec((tm, D_MODEL), row),
                   pl.BlockSpec((1, D_MODEL), fix), pl.BlockSpec((1, LANES), fix)],
        out_shape=[jax.ShapeDtypeStruct((T, D_FF), BF16), jax.ShapeDtypeStruct((T, D_MODEL), F32),
                   jax.ShapeDtypeStruct((1, D_MODEL), F32), jax.ShapeDtypeStruct((1, LANES), F32)],
        scratch_shapes=[pltpu.VMEM((tm, D_MODEL), BF16), pltpu.VMEM((tm, D_MODEL), F32)],
        compiler_params=_params(("arbitrary", "arbitrary"), 48),
    )(h1, g2, w_up16, w_down16, gf, tgt)


def _bwd_mlp(dh2, up16, h1, g2, w_up16, w_down16, tm):
    T = h1.shape[0]
    nf = D_FF // FF_CHUNK

    def body(dh2_ref, up_ref, h1_ref, g2_ref, wu_ref, wd_ref, dup_ref, dh1_ref, hn_ref, dh2b_ref, dg2_ref, acc):
        i, j = pl.program_id(0), pl.program_id(1)

        @pl.when(j == 0)
        def _():
            hn, _, _ = _rms_fwd(h1_ref[...], g2_ref[...])
            hn_ref[...] = hn.astype(BF16)
            dh2b_ref[...] = dh2_ref[...].astype(BF16)
            acc[...] = jnp.zeros_like(acc)

        u = up_ref[...].astype(F32)
        dup = (2.0 * u * _dot_nt(dh2b_ref[...], wd_ref[...])).astype(BF16)
        dup_ref[...] = dup
        acc[...] += _dot_nt(dup, wu_ref[...])

        @pl.when((i == 0) & (j == 0))
        def _():
            dg2_ref[...] = jnp.zeros_like(dg2_ref)

        @pl.when(j == nf - 1)
        def _():
            g2v = g2_ref[...]
            _, xhat, rstd = _rms_fwd(h1_ref[...], g2v)
            dhn = acc[...]
            dg2_ref[...] += jnp.sum(dhn * xhat, axis=0, keepdims=True)
            dh1_ref[...] = dh2_ref[...] + _rms_bwd(dhn, xhat, rstd, g2v)

    row = lambda i, j: (i, 0)
    fix = lambda i, j: (0, 0)
    return pl.pallas_call(
        body, name="bwd_mlp", grid=(T // tm, nf),
        in_specs=[pl.BlockSpec((tm, D_MODEL), row), pl.BlockSpec((tm, FF_CHUNK), lambda i, j: (i, j)),
                  pl.BlockSpec((tm, D_MODEL), row), pl.BlockSpec((1, D_MODEL), fix),
                  pl.BlockSpec((None, D_MODEL, FF_CHUNK), lambda i, j: (j, 0, 0)),
                  pl.BlockSpec((FF_CHUNK, D_MODEL), lambda i, j: (j, 0))],
        out_specs=[pl.BlockSpec((tm, FF_CHUNK), lambda i, j: (i, j)), pl.BlockSpec((tm, D_MODEL), row),
                   pl.BlockSpec((tm, D_MODEL), row), pl.BlockSpec((tm, D_MODEL), row),
                   pl.BlockSpec((1, D_MODEL), fix)],
        out_shape=[jax.ShapeDtypeStruct((T, D_FF), BF16), jax.ShapeDtypeStruct((T, D_MODEL), F32),
                   jax.ShapeDtypeStruct((T, D_MODEL), BF16), jax.ShapeDtypeStruct((T, D_MODEL), BF16),
                   jax.ShapeDtypeStruct((1, D_MODEL), F32)],
        scratch_shapes=[pltpu.VMEM((tm, D_MODEL), F32)],
        compiler_params=_params(("arbitrary", "arbitrary"), 48),
    )(dh2, up16, h1, g2, w_up16, w_down16)


def _matmul_tn(name, a, b, bm, bn, tk, out_shape, out_block, out_index, a_prep=None, b_prep=None):
    T, M = a.shape
    N = b.shape[1]

    def body(a_ref, b_ref, o_ref):
        k = pl.program_id(2)
        av = a_ref[...] if a_prep is None else a_prep(a_ref[...])
        bv = b_ref[...] if b_prep is None else b_prep(b_ref[...])
        p = _dot_tn(av, bv)

        @pl.when(k == 0)
        def _():
            o_ref[...] = p

        @pl.when(k > 0)
        def _():
            o_ref[...] += p

    return pl.pallas_call(
        body, name=name, grid=(M // bm, N // bn, T // tk),
        in_specs=[pl.BlockSpec((tk, bm), lambda m, n, k: (k, m)), pl.BlockSpec((tk, bn), lambda m, n, k: (k, n))],
        out_specs=pl.BlockSpec(out_block, out_index),
        out_shape=jax.ShapeDtypeStruct(out_shape, F32),
        compiler_params=_params(("arbitrary", "arbitrary", "arbitrary"), 48),
    )(a, b)


def _bwd_mix(dh1, w_out16, y_lru, y_sb, ga, gb, tm):
    T = dh1.shape[0]

    def body(d_ref, w_ref, yl_ref, ys_ref, ga_ref, gb_ref, dyl_ref, dys_ref, dga_ref, dgb_ref):
        @pl.when(pl.program_id(0) == 0)
        def _():
            dga_ref[...] = jnp.zeros_like(dga_ref)
            dgb_ref[...] = jnp.zeros_like(dgb_ref)

        d16 = d_ref[...].astype(BF16)
        for y_ref, g_ref, lo, dy_ref, dg_ref in ((yl_ref, ga_ref, 0, dyl_ref, dga_ref),
                                                 (ys_ref, gb_ref, LRU_W, dys_ref, dgb_ref)):
            gv = g_ref[...]
            dn = _dot_nt(d16, w_ref[lo:lo + LRU_W, :])
            _, xhat, rstd = _rms_fwd(y_ref[...], gv)
            dg_ref[...] += jnp.sum(dn * xhat, axis=0, keepdims=True)
            dy_ref[...] = _rms_bwd(dn, xhat, rstd, gv)

    row = lambda i: (i, 0)
    fix = lambda i: (0, 0)
    return pl.pallas_call(
        body, name="bwd_mix", grid=(T // tm,),
        in_specs=[pl.BlockSpec((tm, D_MODEL), row), pl.BlockSpec((D_MODEL, D_MODEL), fix),
                  pl.BlockSpec((tm, LRU_W), row), pl.BlockSpec((tm, SB_W), row),
                  pl.BlockSpec((1, LRU_W), fix), pl.BlockSpec((1, SB_W), fix)],
        out_specs=[pl.BlockSpec((tm, LRU_W), row), pl.BlockSpec((tm, SB_W), row),
                   pl.BlockSpec((1, LRU_W), fix), pl.BlockSpec((1, SB_W), fix)],
        out_shape=[jax.ShapeDtypeStruct((T, LRU_W), F32), jax.ShapeDtypeStruct((T, SB_W), F32),
                   jax.ShapeDtypeStruct((1, LRU_W), F32), jax.ShapeDtypeStruct((1, SB_W), F32)],
        compiler_params=_params(("arbitrary",), 48),
    )(dh1, w_out16, y_lru, y_sb, ga, gb)


def _attn_bwd(qkv, run_tab, dy_sb, tri_suffix, tri_prefix, ride, ride_sliced, B, S):
    T = B * S
    nq = S // Q_BLK
    nkb = S // K_BLK
    nhp = SB_W // LANES
    scale = HEAD_D ** -0.5

    nr = len(ride)
    ride_shape, ride_specs, ride_sems = _exchange_shapes(ride, ride_sliced)

    def body(q_ref, k_ref, v_ref, run_ref, do_ref, ts_ref, tp_ref, *rest):
        dq_ref, dkt_ref, dvt_ref = rest[nr:nr + 3]
        start_ride, wait_ride = _exchange_copies(rest[:nr], rest[nr + 3:2 * nr + 3], ride_sliced, *rest[2 * nr + 3:])
        qi = pl.program_id(2)
        step_no = (pl.program_id(0) * nhp + pl.program_id(1)) * nq + qi
        pl.when(step_no == 0)(start_ride)
        jd = qi // Q_PER_K

        @pl.when(qi == 0)
        def _():
            dkt_ref[...] = jnp.zeros_like(dkt_ref)
            dvt_ref[...] = jnp.zeros_like(dvt_ref)

        causal, col, lane, halves = _attn_consts(qi)
        q = q_ref[...]
        do = do_ref[...]
        qh = [jnp.where(hm, q, jnp.zeros_like(q)) * jnp.asarray(scale, BF16) for hm in halves]
        doh = [jnp.where(hm, do, 0.0).astype(BF16) for hm in halves]
        qt = jnp.concatenate([h.astype(F32).T.astype(BF16) for h in qh], axis=1)
        dot_ = jnp.concatenate([h.astype(F32).T.astype(BF16) for h in doh], axis=1)
        tables = [run_ref[:, 0:LANES], run_ref[:, LANES:2 * LANES]]

        def group(blocks, carry):
            prefixes, dq = carry
            prefixes = list(prefixes)
            ks = [k_ref[_key_rows(jl), :] for jl, _, _ in blocks]
            vs = [v_ref[_key_rows(jl), :] for jl, _, _ in blocks]
            chains = [(b, h) for b in range(len(blocks)) for h in range(2)]
            z = {c: _dot_nt(qh[c[1]], ks[c[0]]) for c in chains}
            da = {c: _dot_nt(doh[c[1]], vs[c[0]]) for c in chains}
            lg = {c: _log1m_beta(z[c], blocks[c[0]][2]) for c in chains}
            suf = {c: _cumsum_mm(lg[c], ts_ref[...]) for c in chains}
            att, g = {}, {}
            for b, h in chains:
                _, jlane, mask = blocks[b]
                run = jnp.sum(jnp.where(lane == jlane, tables[h], 0.0), axis=1, keepdims=True)
                a = jnp.exp(z[b, h] + suf[b, h] + run)
                a = a if mask is None else jnp.where(mask, a, 0.0)
                g[b, h] = a * da[b, h]
                att[b, h] = a.astype(BF16)
            gpre = {c: _cumsum_mm(g[c], tp_ref[...]) for c in chains}
            dz = {}
            for b, h in chains:
                mask = blocks[b][2]
                d = g[b, h] - jnp.exp(z[b, h] + lg[b, h]) * (prefixes[h] + gpre[b, h])
                dz[b, h] = (d if mask is None else jnp.where(mask, d, 0.0)).astype(BF16)
                prefixes[h] = prefixes[h] + gpre[b, h][:, K_BLK - 1:K_BLK]
            for b, h in chains:
                dq = dq + _dot(dz[b, h], jnp.where(halves[h], ks[b], jnp.zeros_like(ks[b])))
            for b, (jl, _, _) in enumerate(blocks):
                dkt_ref[jl] += _dot(qt, jnp.concatenate([dz[b, 0], dz[b, 1]], axis=0))
                dvt_ref[jl] += _dot(dot_, jnp.concatenate([att[b, 0], att[b, 1]], axis=0))
            return tuple(prefixes), dq

        def step(it, cr):
            return group([(2 * it, 2 * it, None), (2 * it + 1, 2 * it + 1, None)], cr)

        col0 = jnp.zeros((Q_BLK, 1), F32)
        carry = lax.fori_loop(0, jd // 2, step, ((col0, col0), jnp.zeros((Q_BLK, LANES), F32)))
        two = jd % 2
        before = (jnp.maximum(jd - 1, 0), jnp.where(two == 1, jd - 1, -1), col < two * K_BLK)
        dq_ref[...] = group([before, (jd, jd, causal)], carry)[1] * scale
        pl.when(step_no == B * nhp * nq - 1)(wait_ride)

    qblk = pl.BlockSpec((Q_BLK, LANES), lambda b, hp, qi: (b * nq + qi, hp))
    tri = pl.BlockSpec((K_BLK, K_BLK), lambda b, hp, qi: (0, 0))
    acc = pl.BlockSpec((None, None, nkb, LANES, K_BLK), lambda b, hp, qi: (b, hp, 0, 0, 0))
    return pl.pallas_call(
        body, name="attn_bwd", grid=(B, nhp, nq),
        in_specs=[qblk, pl.BlockSpec((S, LANES), lambda b, hp, qi: (b, nhp + hp)),
                  pl.BlockSpec((S, LANES), lambda b, hp, qi: (b, 2 * nhp + hp)),
                  pl.BlockSpec((Q_BLK, 2 * LANES), lambda b, hp, qi: (b * nq + qi, hp)), qblk, tri, tri] + ride_specs,
        out_specs=[qblk, acc, acc] + ride_specs,
        out_shape=[jax.ShapeDtypeStruct((T, SB_W), F32)] + [jax.ShapeDtypeStruct((B, nhp, nkb, LANES, K_BLK), F32)] * 2
        + ride_shape,
        scratch_shapes=ride_sems,
        compiler_params=_params(("arbitrary", "arbitrary", "arbitrary"), 48),
    )(qkv, qkv, qkv, run_tab, dy_sb, tri_suffix, tri_prefix, *ride)


def _untranspose(t, B, S):
    return t.transpose(0, 2, 4, 1, 3).reshape(B * S, SB_W)


def _lru_bwd(proj_lru, h, dy_lru, conv_w, conv_b, wa_bd, wx_bd, b_a, b_x, lam, B, S, lc):
    T = B * S
    nc = S // lc
    ncb = LRU_W // LANES
    hpc = lc // SUBLANES

    def body(x_ref, xh_ref, g_ref, h_ref, hh_ref, dy_ref, cw_ref, cb_ref, wa_ref, wx_ref, ba_ref, bx_ref, lam_ref,
             dx_ref, dg_ref, dcw_ref, dcb_ref, dwa_ref, dwx_ref, dba_ref, dbx_ref, dlam_ref,
             lam_s, dc_s, a_first, lam_first, dc_head):
        b, ci = pl.program_id(1), pl.program_id(2)
        first_chunk = ci == nc - 1

        @pl.when(ci == 0)
        def _():
            a_first[...] = jnp.zeros_like(a_first)
            lam_first[...] = jnp.zeros_like(lam_first)
            dc_head[...] = jnp.zeros_like(dc_head)

        @pl.when((b == 0) & (ci == 0))
        def _():
            for ref in (dcw_ref, dcb_ref, dwa_ref, dwx_ref, dba_ref, dbx_ref, dlam_ref):
                ref[...] = jnp.zeros_like(ref)

        x = x_ref[...]
        taps = _conv_taps(x, jnp.where(first_chunk, 0.0, xh_ref[...]), lc)
        c = cb_ref[...] + sum(cw_ref[pl.ds(CONV_K - 1 - k, 1), :] * taps[k] for k in range(CONV_K))
        r, i, sp, dsp_dlam, a, s = _lru_gates(c, wa_ref, wx_ref, ba_ref, bx_ref, lam_ref)
        hv = h_ref[...]
        he = jnp.concatenate([jnp.where(first_chunk, 0.0, hh_ref[...]), hv], axis=0)
        h_prev = pltpu.roll(he, 1, 0)[SUBLANES:SUBLANES + lc]
        dy = dy_ref[...]
        gelu, dgelu = _gelu_parts(g_ref[...])
        dg_ref[...] = dy * hv * dgelu

        row = lax.broadcasted_iota(jnp.int32, (lc, LANES), 0)
        a_next = jnp.where(row < lc - 1, pltpu.roll(a, lc - 1, 0), a_first[...])
        lam_loc, a_run = _scan(a_next, dy * gelu, reverse=True)
        lam_s[...] = lam_loc + a_run * lam_first[...]
        lam_t = lam_s[...]
        lam_first[...] = lam_s[pl.ds(0, 1), :]
        lam_s[...] = a
        a_first[...] = lam_s[pl.ds(0, 1), :]

        ic = i * c
        dlog_a = lam_t * h_prev * a - (lam_t * ic) * (a * a) / s
        dpre_r = (dlog_a * ((-LRU_C) * sp)) * r * (1.0 - r)
        dpre_i = (lam_t * s * c) * i * (1.0 - i)
        dlam_ref[...] += jnp.sum(dlog_a * r, axis=0, keepdims=True) * ((-LRU_C) * dsp_dlam)
        dr16 = dpre_r.astype(BF16)
        di16 = dpre_i.astype(BF16)
        c16 = c.astype(BF16)
        dwa_ref[0] += _dot_tn(c16, dr16)
        dwx_ref[0] += _dot_tn(c16, di16)
        dba_ref[...] += jnp.sum(dpre_r, axis=0, keepdims=True)
        dbx_ref[...] += jnp.sum(dpre_i, axis=0, keepdims=True)
        dc = lam_t * s * i + _dot_nt(dr16, wa_ref[0]) + _dot_nt(di16, wx_ref[0])
        dcb_ref[...] += jnp.sum(dc, axis=0, keepdims=True)
        for k in range(CONV_K):
            dcw_ref[pl.ds(CONV_K - 1 - k, 1), :] += jnp.sum(dc * taps[k], axis=0, keepdims=True)
        dce = jnp.concatenate([dc, dc_head[...]], axis=0)
        dx = cw_ref[pl.ds(CONV_K - 1, 1), :] * dc
        for k in range(1, CONV_K):
            dx = dx + cw_ref[pl.ds(CONV_K - 1 - k, 1), :] * pltpu.roll(dce, lc + SUBLANES - k, 0)[0:lc]
        dx_ref[...] = dx
        dc_s[...] = dc
        dc_head[...] = dc_s[pl.ds(0, SUBLANES), :]

    def chunk(col):
        return pl.BlockSpec((lc, LANES), lambda cb, b, ci: (b * nc + nc - 1 - ci, col(cb)))

    def halo(col):
        return pl.BlockSpec((SUBLANES, LANES),
                            lambda cb, b, ci: (jnp.maximum((b * nc + nc - 1 - ci) * hpc - 1, 0), col(cb)))

    chan = lambda cb, b, ci: (0, cb)
    blk = lambda cb, b, ci: (cb, 0, 0)
    vec = pl.BlockSpec((1, LANES), chan)
    mat = pl.BlockSpec((1, LANES, LANES), blk)
    return pl.pallas_call(
        body, name="lru_bwd", grid=(ncb, B, nc),
        in_specs=[chunk(lambda cb: cb), halo(lambda cb: cb), chunk(lambda cb: ncb + cb),
                  chunk(lambda cb: cb), halo(lambda cb: cb), chunk(lambda cb: cb),
                  pl.BlockSpec((CONV_K, LANES), chan), vec, mat, mat, vec, vec, vec],
        out_specs=[chunk(lambda cb: cb), chunk(lambda cb: cb), pl.BlockSpec((CONV_K, LANES), chan), vec,
                   mat, mat, vec, vec, vec],
        out_shape=[jax.ShapeDtypeStruct((T, LRU_W), F32), jax.ShapeDtypeStruct((T, LRU_W), F32),
                   jax.ShapeDtypeStruct((CONV_K, LRU_W), F32), jax.ShapeDtypeStruct((1, LRU_W), F32),
                   jax.ShapeDtypeStruct((ncb, LANES, LANES), F32), jax.ShapeDtypeStruct((ncb, LANES, LANES), F32),
                   jax.ShapeDtypeStruct((1, LRU_W), F32), jax.ShapeDtypeStruct((1, LRU_W), F32),
                   jax.ShapeDtypeStruct((1, LRU_W), F32)],
        scratch_shapes=[pltpu.VMEM((lc, LANES), F32), pltpu.VMEM((lc, LANES), F32), pltpu.VMEM((1, LANES), F32),
                        pltpu.VMEM((1, LANES), F32), pltpu.VMEM((SUBLANES, LANES), F32)],
        compiler_params=_params(("arbitrary", "arbitrary", "arbitrary"), 32),
    )(proj_lru, proj_lru, proj_lru, h, h, dy_lru, conv_w, conv_b, wa_bd, wx_bd, b_a, b_x, lam)


def _bwd_in(pieces, w_in16, x2, g1, dh1, tm):
    T = x2.shape[0]
    npc = len(pieces)

    def body(*refs):
        p_refs = refs[:npc]
        w_ref, x_ref, g_ref, d_ref, dx_ref, dproj_ref, xn_ref, dg1_ref = refs[npc:]

        @pl.when(pl.program_id(0) == 0)
        def _():
            dg1_ref[...] = jnp.zeros_like(dg1_ref)

        dxn = jnp.zeros((tm, D_MODEL), F32)
        for n, p_ref in enumerate(p_refs):
            cols = slice(n * LRU_W, (n + 1) * LRU_W)
            p16 = p_ref[...].astype(BF16)
            dproj_ref[:, cols] = p16
            dxn = dxn + _dot_nt(p16, w_ref[:, cols])
        gv = g_ref[...]
        xn, xhat, rstd = _rms_fwd(x_ref[...], gv)
        xn_ref[...] = xn.astype(BF16)
        dg1_ref[...] += jnp.sum(dxn * xhat, axis=0, keepdims=True)
        dx_ref[...] = d_ref[...] + _rms_bwd(dxn, xhat, rstd, gv)

    row = lambda i: (i, 0)
    fix = lambda i: (0, 0)
    return pl.pallas_call(
        body, name="bwd_in", grid=(T // tm,),
        in_specs=[pl.BlockSpec((tm, LRU_W), row)] * npc + [
            pl.BlockSpec((D_MODEL, IN_COLS), fix), pl.BlockSpec((tm, D_MODEL), row),
            pl.BlockSpec((1, D_MODEL), fix), pl.BlockSpec((tm, D_MODEL), row)],
        out_specs=[pl.BlockSpec((tm, D_MODEL), row), pl.BlockSpec((tm, IN_COLS), row),
                   pl.BlockSpec((tm, D_MODEL), row), pl.BlockSpec((1, D_MODEL), fix)],
        out_shape=[jax.ShapeDtypeStruct((T, D_MODEL), F32), jax.ShapeDtypeStruct((T, IN_COLS), BF16),
                   jax.ShapeDtypeStruct((T, D_MODEL), BF16), jax.ShapeDtypeStruct((1, D_MODEL), F32)],
        compiler_params=_params(("arbitrary",), 56),
    )(*pieces, w_in16, x2, g1, dh1)


def _adam_shard(name, parts, w, m, v, tr):
    R, C = w.shape

    def body(p_ref, w_ref, m_ref, v_ref, g_ref, d_ref, m2_ref, v2_ref):
        g = p_ref[0]
        for p in range(1, N_DEV):
            g = g + p_ref[p]
        g_ref[...] = g
        d_ref[...], m2_ref[...], v2_ref[...] = _adamw(w_ref[...], g, m_ref[...], v_ref[...])

    blk = pl.BlockSpec((tr, C), lambda i: (i, 0))
    return pl.pallas_call(
        body, name=name, grid=(R // tr,),
        in_specs=[pl.BlockSpec((N_DEV, tr, C), lambda i: (0, i, 0)), blk, blk, blk],
        out_specs=[blk] * 4, out_shape=[jax.ShapeDtypeStruct((R, C), F32)] * 4,
        compiler_params=_params(("arbitrary",), 48),
    )(parts, w, m, v)


def _sum_parts(name, parts):
    def body(p_ref, g_ref):
        g = p_ref[0]
        for p in range(1, N_DEV):
            g = g + p_ref[p]
        g_ref[...] = g

    return pl.pallas_call(body, name=name, out_shape=jax.ShapeDtypeStruct(parts.shape[1:], F32))(parts)


def _adam_packed(w, g, m, v):
    def body(w_ref, g_ref, m_ref, v_ref, d_ref, m2_ref, v2_ref):
        d_ref[...], m2_ref[...], v2_ref[...] = _adamw(w_ref[...], g_ref[...], m_ref[...], v_ref[...])

    return pl.pallas_call(body, name="adam_small", out_shape=[jax.ShapeDtypeStruct(w.shape, F32)] * 3)(w, g, m, v)


def _pack(vals, rows):
    flat = jnp.concatenate([v.reshape(-1).astype(F32) for v in vals])
    return jnp.pad(flat, (0, rows * LANES - flat.shape[0])).reshape(rows, LANES)


def _unpack(packed, sizes):
    flat = packed.reshape(-1)
    out, off = [], 0
    for n in sizes:
        out.append(flat[off:off + n])
        off += n
    return out


def _block_diag_pairs(w):
    w = w.reshape(LRU_BLOCKS // 2, 2, HEAD_D, HEAD_D)
    out = jnp.zeros((LRU_BLOCKS // 2, LANES, LANES), w.dtype)
    out = out.at[:, :HEAD_D, :HEAD_D].set(w[:, 0])
    return out.at[:, HEAD_D:, HEAD_D:].set(w[:, 1])


def _diag_blocks(w):
    return jnp.stack([w[:, :HEAD_D, :HEAD_D], w[:, HEAD_D:, HEAD_D:]], axis=1).reshape(LRU_BLOCKS, HEAD_D, HEAD_D)


def kernel(x, norm1_g, w_in, conv_w, conv_b, lru_w_a, lru_b_a, lru_w_x, lru_b_x, lru_lambda, lru_out_g, sb_out_g, w_out, norm2_g, w_up, w_down, final_g, loss_target, m_norm1_g, m_w_in, m_conv_w, m_conv_b, m_lru_w_a, m_lru_b_a, m_lru_w_x, m_lru_b_x, m_lru_lambda, m_lru_out_g, m_sb_out_g, m_w_out, m_norm2_g, m_w_up, m_w_down, m_final_g, v_norm1_g, v_w_in, v_conv_w, v_conv_b, v_lru_w_a, v_lru_b_a, v_lru_w_x, v_lru_b_x, v_lru_lambda, v_lru_out_g, v_sb_out_g, v_w_out, v_norm2_g, v_w_up, v_w_down, v_final_g):
    B, S, _ = x.shape
    T = B * S
    tm = min(512, T)
    lc = min(512, S)
    me = _my_index()
    x2 = x.reshape(T, D_MODEL)
    tgt = loss_target.reshape(T, D_MODEL)
    cw_cols = CONV_K * LRU_W // N_DEV // CONV_K

    shards16 = _cast_shards([w_in[0], w_out[0], w_up[0], w_down[0]])
    cw_pad = jnp.zeros((SUBLANES, LANES), F32).at[:CONV_K, :cw_cols].set(conv_w[0])
    g_in, g_cw = _exchange("gather_w_in", [shards16[0], cw_pad], [False, False])
    w_in16 = g_in.transpose(1, 0, 2).reshape(D_MODEL, IN_COLS)
    conv_w_full = g_cw[:, :CONV_K, :cw_cols].transpose(1, 0, 2).reshape(CONV_K, LRU_W)
    wa_bd = _block_diag_pairs(lru_w_a[0]).astype(BF16)
    wx_bd = _block_diag_pairs(lru_w_x[0]).astype(BF16)
    b_a = lru_b_a.reshape(1, LRU_W)
    b_x = lru_b_x.reshape(1, LRU_W)
    gf = final_g.reshape(1, D_MODEL)

    proj_lru, qkv = _fwd_in(x2, norm1_g, w_in16, tm)
    y_lru, h = _lru_fwd(proj_lru, conv_w_full, conv_b, wa_bd, wx_bd, b_a, b_x, lru_lambda, B, S, lc)
    tri_suffix, tri_prefix = _tri(False), _tri(True)
    y_sb, run_tab, g_out, g_up, g_down = _attn_fwd(qkv, tri_suffix, list(shards16[1:]), [False] * 3, B, S)
    w_out16 = g_out.reshape(D_MODEL, D_MODEL)
    w_down16 = g_down.reshape(D_FF, D_MODEL)
    h1, mix16 = _fwd_mix(y_lru, y_sb, lru_out_g, sb_out_g, w_out16, x2, tm)
    up16, dh2, d_final_g, loss_part = _fwd_mlp(h1, norm2_g, g_up, w_down16, gf, tgt, tm)

    dup16, dh1, hn16, dh2b, d_norm2_g = _bwd_mlp(dh2, up16, h1, norm2_g, g_up, w_down16, tm)
    sq = lambda u: (u.astype(F32) * u.astype(F32)).astype(BF16)
    gw_up = _matmul_tn("grad_w_up", hn16, dup16, D_MODEL, FF_CHUNK, tm, (N_DEV, D_MODEL, FF_CHUNK),
                       (None, D_MODEL, FF_CHUNK), lambda m, n, k: (n, 0, 0))
    gw_down = _matmul_tn("grad_w_down", up16, dh2b, FF_CHUNK, D_MODEL, tm, (N_DEV, FF_CHUNK, D_MODEL),
                         (None, FF_CHUNK, D_MODEL), lambda m, n, k: (m, 0, 0), a_prep=sq)
    dy_lru, dy_sb, d_lru_out_g, d_sb_out_g = _bwd_mix(dh1, w_out16, y_lru, y_sb, lru_out_g, sb_out_g, tm)
    gw_out = _matmul_tn("grad_w_out", mix16, dh1, D_MODEL, FF_CHUNK, tm, (D_MODEL, D_MODEL),
                        (D_MODEL, FF_CHUNK), lambda m, n, k: (0, n), b_prep=lambda u: u.astype(BF16))
    parts_out = gw_out.reshape(N_DEV, D_MODEL // N_DEV, D_MODEL)
    dq, dkt, dvt, r_out, r_up, r_down = _attn_bwd(qkv, run_tab, dy_sb, tri_suffix, tri_prefix,
                                                  [parts_out, gw_up, gw_down], [True] * 3, B, S)
    dk, dv = _untranspose(dkt, B, S), _untranspose(dvt, B, S)
    (dx_lru, dg_lru, d_conv_w, d_conv_b, d_wa, d_wx, d_b_a, d_b_x, d_lambda) = _lru_bwd(
        proj_lru, h, dy_lru, conv_w_full, conv_b, wa_bd, wx_bd, b_a, b_x, lru_lambda, B, S, lc)
    dx, dproj16, xn16, d_norm1_g = _bwd_in([dx_lru, dg_lru, dq, dk, dv], w_in16, x2, norm1_g, dh1, tm)
    gw_in = _matmul_tn("grad_w_in", xn16, dproj16, D_MODEL, FF_CHUNK, tm, (D_MODEL, IN_COLS),
                       (D_MODEL, FF_CHUNK), lambda m, n, k: (0, n))

    small_grads = {"norm1_g": d_norm1_g, "conv_b": d_conv_b, "lru_w_a": _diag_blocks(d_wa), "lru_b_a": d_b_a,
                   "lru_w_x": _diag_blocks(d_wx), "lru_b_x": d_b_x, "lru_lambda": d_lambda,
                   "lru_out_g": d_lru_out_g, "sb_out_g": d_sb_out_g, "norm2_g": d_norm2_g, "final_g": d_final_g}
    packed = _pack([small_grads[n] for n, _ in SMALL] + [d_conv_w, loss_part], EXCH_ROWS)
    parts_in = gw_in.reshape(D_MODEL, N_DEV, IN_COLS // N_DEV).transpose(1, 0, 2)
    r_in, r_small = _exchange("exchange_grads", [parts_in, packed], [True, False])

    g_w_in, d_w_in, nm_w_in, nv_w_in = _adam_shard("adam_w_in", r_in, w_in[0], m_w_in[0], v_w_in[0], 256)
    g_w_out, d_w_out, nm_w_out, nv_w_out = _adam_shard("adam_w_out", r_out, w_out[0], m_w_out[0], v_w_out[0], 64)
    g_w_up, d_w_up, nm_w_up, nv_w_up = _adam_shard("adam_w_up", r_up, w_up[0], m_w_up[0], v_w_up[0], 256)
    g_w_down, d_w_down, nm_w_down, nv_w_down = _adam_shard("adam_w_down", r_down, w_down[0], m_w_down[0], v_w_down[0], 128)

    total = _sum_parts("sum_small", r_small)
    sizes = [n for _, n in SMALL]
    small_g = _unpack(total, sizes + [CONV_K * LRU_W, 1])
    loss = small_g[-1][0]
    g_conv_w = lax.dynamic_slice_in_dim(small_g[-2].reshape(CONV_K, LRU_W), me * cw_cols, cw_cols, axis=1)
    given = dict(norm1_g=(norm1_g, m_norm1_g, v_norm1_g), conv_b=(conv_b, m_conv_b, v_conv_b),
                 lru_w_a=(lru_w_a, m_lru_w_a, v_lru_w_a), lru_b_a=(lru_b_a, m_lru_b_a, v_lru_b_a),
                 lru_w_x=(lru_w_x, m_lru_w_x, v_lru_w_x), lru_b_x=(lru_b_x, m_lru_b_x, v_lru_b_x),
                 lru_lambda=(lru_lambda, m_lru_lambda, v_lru_lambda), lru_out_g=(lru_out_g, m_lru_out_g, v_lru_out_g),
                 sb_out_g=(sb_out_g, m_sb_out_g, v_sb_out_g), norm2_g=(norm2_g, m_norm2_g, v_norm2_g),
                 final_g=(final_g, m_final_g, v_final_g))
    names = [n for n, _ in SMALL]
    pw = _pack([given[n][0] for n in names] + [conv_w], ADAM_ROWS)
    pm = _pack([given[n][1] for n in names] + [m_conv_w], ADAM_ROWS)
    pv = _pack([given[n][2] for n in names] + [v_conv_w], ADAM_ROWS)
    pg = _pack(small_g[:len(names)] + [g_conv_w], ADAM_ROWS)
    pd, pm2, pv2 = _adam_packed(pw, pg, pm, pv)
    asz = sizes + [conv_w.size]
    shapes = {n: given[n][0].shape for n in names}
    shapes["conv_w"] = conv_w.shape
    order = names + ["conv_w"]
    g_small = dict(zip(order, [a.reshape(shapes[n]) for n, a in zip(order, _unpack(pg, asz))]))
    d_small = dict(zip(order, [a.reshape(shapes[n]) for n, a in zip(order, _unpack(pd, asz))]))
    m_small = dict(zip(order, [a.reshape(shapes[n]) for n, a in zip(order, _unpack(pm2, asz))]))
    v_small = dict(zip(order, [a.reshape(shapes[n]) for n, a in zip(order, _unpack(pv2, asz))]))

    big = {"w_in": (g_w_in, d_w_in, nm_w_in, nv_w_in), "w_out": (g_w_out, d_w_out, nm_w_out, nv_w_out),
           "w_up": (g_w_up, d_w_up, nm_w_up, nv_w_up), "w_down": (g_w_down, d_w_down, nm_w_down, nv_w_down)}
    weights = ["norm1_g", "w_in", "conv_w", "conv_b", "lru_w_a", "lru_b_a", "lru_w_x", "lru_b_x", "lru_lambda",
               "lru_out_g", "sb_out_g", "w_out", "norm2_g", "w_up", "w_down", "final_g"]

    def leaf(n, kind):
        if n in big:
            return big[n][kind][None]
        return (g_small, d_small, m_small, v_small)[kind][n]

    return (loss, dx.reshape(B, S, D_MODEL), *[leaf(n, 0) for n in weights], *[leaf(n, 1) for n in weights],
            *[leaf(n, 2) for n in weights], *[leaf(n, 3) for n in weights])
```

```python
import jax
import jax.numpy as jnp
from jax import lax
from jax.experimental import pallas as pl
from jax.experimental.pallas import tpu as pltpu

F32 = jnp.float32
BF16 = jnp.bfloat16

D_MODEL = 1024
LRU_W = 512
SB_W = 512
HEAD_D = 64
D_FF = 4096
IN_COLS = 2 * LRU_W + 3 * SB_W
CONV_K = 4
LRU_BLOCKS = 8
LRU_C = 8.0
EPS = 1e-6
N_DEV = 8
LANES = 128
SUBLANES = 8
FF_CHUNK = 512
Q_BLK = 256
K_BLK = 256
Q_PER_K = K_BLK // Q_BLK

ADAM_LR = 0.001
ADAM_B1 = 0.9
ADAM_B2 = 0.999
ADAM_EPS = 1e-08
ADAM_WD = 0.01
ADAM_STEP = 10

SMALL = (("norm1_g", 1024), ("conv_b", 512), ("lru_w_a", 32768), ("lru_b_a", 512), ("lru_w_x", 32768),
         ("lru_b_x", 512), ("lru_lambda", 512), ("lru_out_g", 512), ("sb_out_g", 512), ("norm2_g", 1024),
         ("final_g", 1024))
SMALL_ROWS = sum(n for _, n in SMALL) // LANES
EXCH_ROWS = SMALL_ROWS + (CONV_K * LRU_W) // LANES + 8
ADAM_ROWS = SMALL_ROWS + 8


def _params(sem=None, vmem_mb=None):
    kw = {}
    if sem is not None:
        kw["dimension_semantics"] = sem
    if vmem_mb is not None:
        kw["vmem_limit_bytes"] = vmem_mb << 20
    return pltpu.CompilerParams(**kw)


def _dot(a, b):
    return jnp.dot(a, b, preferred_element_type=F32)


def _dot_nt(a, b):
    return lax.dot_general(a, b, (((1,), (1,)), ((), ())), preferred_element_type=F32)


def _dot_tn(a, b):
    return lax.dot_general(a, b, (((0,), (0,)), ((), ())), preferred_element_type=F32)


def _rms_fwd(x, g):
    rstd = lax.rsqrt(jnp.mean(x * x, axis=-1, keepdims=True) + EPS)
    xhat = x * rstd
    return xhat * g, xhat, rstd


def _rms_bwd(dy, xhat, rstd, g):
    dxhat = dy * g
    return rstd * (dxhat - xhat * jnp.mean(dxhat * xhat, axis=-1, keepdims=True))


def _sigmoid(x):
    return 1.0 / (1.0 + jnp.exp(-x))


def _log1p_pos(e):
    series = e * (1.0 - e * (0.5 - e * (1.0 / 3.0 - e * 0.25)))
    return jnp.where(e < 1e-2, series, jnp.log(1.0 + e))


def _neg_expm1(x):
    series = -x * (1.0 + x * (0.5 + x * (1.0 / 6.0 + x * (1.0 / 24.0))))
    return jnp.where(x > -1e-2, series, 1.0 - jnp.exp(x))


def _gelu_parts(g):
    k0 = 0.7978845608028654
    k1 = 0.044715
    t = jnp.tanh(k0 * (g + k1 * g * g * g))
    val = 0.5 * g * (1.0 + t)
    grad = 0.5 * (1.0 + t) + 0.5 * g * (1.0 - t * t) * k0 * (1.0 + 3.0 * k1 * g * g)
    return val, grad


def _scan(a, b, reverse):
    n = a.shape[0]
    row = lax.broadcasted_iota(jnp.int32, a.shape, 0)
    s = 1
    while s < n:
        if reverse:
            keep = row < n - s
            shift = n - s
        else:
            keep = row >= s
            shift = s
        bs = jnp.where(keep, pltpu.roll(b, shift, 0), 0.0)
        a_s = jnp.where(keep, pltpu.roll(a, shift, 0), 1.0)
        b = a * bs + b
        a = a * a_s
        s *= 2
    return b, a


def _adamw(w, g, m, v):
    m = ADAM_B1 * m + (1.0 - ADAM_B1) * g
    v = ADAM_B2 * v + (1.0 - ADAM_B2) * (g * g)
    m_hat = m / (1.0 - ADAM_B1 ** ADAM_STEP)
    v_hat = v / (1.0 - ADAM_B2 ** ADAM_STEP)
    delta = -ADAM_LR * (m_hat / (jnp.sqrt(v_hat) + ADAM_EPS) + ADAM_WD * w)
    return delta, m, v


def _my_index():
    return 4 * lax.axis_index("x") + 2 * lax.axis_index("y") + lax.axis_index("c")


def _peer(k):
    x, y, c = lax.axis_index("x"), lax.axis_index("y"), lax.axis_index("c")
    px = 1 - x if (k >> 2) & 1 else x
    py = 1 - y if (k >> 1) & 1 else y
    pc = 1 - c if k & 1 else c
    return (px, py, pc), 4 * px + 2 * py + pc


def _exchange_shapes(srcs, sliced):
    n = len(srcs)
    out_shape = [jax.ShapeDtypeStruct(s.shape if sl else (N_DEV,) + s.shape, s.dtype) for s, sl in zip(srcs, sliced)]
    specs = [pl.BlockSpec(memory_space=pl.ANY)] * n
    sems = [pltpu.SemaphoreType.DMA((n, N_DEV - 1)), pltpu.SemaphoreType.DMA((n, N_DEV - 1)),
            pltpu.SemaphoreType.DMA((n,))]
    return out_shape, specs, sems


def _exchange_copies(ins, outs, sliced, send_sems, recv_sems, local_sems):
    n = len(ins)

    def part(a, p):
        return ins[a].at[p] if sliced[a] else ins[a]

    def copies(receiving):
        me = _my_index()
        local = [pltpu.make_async_copy(part(a, me), outs[a].at[me], local_sems.at[a]) for a in range(n)]
        remote = []
        for k in range(1, N_DEV):
            dev, idx = _peer(k)
            for a in range(n):
                remote.append(pltpu.make_async_remote_copy(
                    src_ref=part(a, idx), dst_ref=outs[a].at[idx if receiving else me],
                    send_sem=send_sems.at[a, k - 1], recv_sem=recv_sems.at[a, k - 1],
                    device_id=dev, device_id_type=pl.DeviceIdType.MESH))
        return local, remote

    def start():
        local, remote = copies(receiving=False)
        for cp in local + remote:
            cp.start()

    def wait():
        local, remote = copies(receiving=True)
        for cp in remote + local:
            cp.wait()

    return start, wait


def _exchange(name, srcs, sliced):
    n = len(srcs)
    out_shape, specs, sems = _exchange_shapes(srcs, sliced)

    def body(*refs):
        start, wait = _exchange_copies(refs[:n], refs[n:2 * n], sliced, *refs[2 * n:])
        start()
        wait()

    return pl.pallas_call(body, name=name, out_shape=out_shape, in_specs=specs, out_specs=specs,
                          scratch_shapes=sems)(*srcs)


def _cast_shards(ws):
    def body(*refs):
        for i in range(len(ws)):
            refs[len(ws) + i][...] = refs[i][...].astype(BF16)

    return pl.pallas_call(
        body, name="cast_shards", out_shape=[jax.ShapeDtypeStruct(w.shape, BF16) for w in ws],
        compiler_params=_params(vmem_mb=32),
    )(*ws)


def _fwd_in(x2, g1, w_in16, tm):
    T = x2.shape[0]

    def body(x_ref, g_ref, w_ref, lru_ref, qkv_ref):
        xn, _, _ = _rms_fwd(x_ref[...], g_ref[...])
        xn = xn.astype(BF16)
        lru_ref[...] = _dot(xn, w_ref[:, 0:2 * LRU_W])
        qkv_ref[...] = _dot(xn, w_ref[:, 2 * LRU_W:IN_COLS]).astype(BF16)

    return pl.pallas_call(
        body, name="fwd_in", grid=(T // tm,),
        in_specs=[pl.BlockSpec((tm, D_MODEL), lambda i: (i, 0)),
                  pl.BlockSpec((1, D_MODEL), lambda i: (0, 0)),
                  pl.BlockSpec((D_MODEL, IN_COLS), lambda i: (0, 0))],
        out_specs=[pl.BlockSpec((tm, 2 * LRU_W), lambda i: (i, 0)),
                   pl.BlockSpec((tm, 3 * SB_W), lambda i: (i, 0))],
        out_shape=[jax.ShapeDtypeStruct((T, 2 * LRU_W), F32), jax.ShapeDtypeStruct((T, 3 * SB_W), BF16)],
        compiler_params=_params(("arbitrary",), 48),
    )(x2, g1, w_in16)


def _lru_gates(c, wa_ref, wx_ref, ba_ref, bx_ref, lam_ref):
    c16 = c.astype(BF16)
    r = _sigmoid(_dot(c16, wa_ref[0]) + ba_ref[...])
    i = _sigmoid(_dot(c16, wx_ref[0]) + bx_ref[...])
    lam = lam_ref[...]
    e = jnp.exp(-jnp.abs(lam))
    sp = jnp.maximum(-lam, 0.0) + _log1p_pos(e)
    dsp_dlam = -jnp.where(lam >= 0.0, e, 1.0) / (1.0 + e)
    log_a = (-LRU_C) * r * sp
    a = jnp.exp(log_a)
    s = jnp.sqrt(_neg_expm1(2.0 * log_a))
    return r, i, sp, dsp_dlam, a, s


def _conv_taps(x, halo, lc):
    xe = jnp.concatenate([halo, x], axis=0)
    return [x] + [pltpu.roll(xe, k, 0)[SUBLANES:SUBLANES + lc] for k in range(1, CONV_K)]


def _lru_fwd(proj_lru, conv_w, conv_b, wa_bd, wx_bd, b_a, b_x, lam, B, S, lc):
    T = B * S
    nc = S // lc
    ncb = LRU_W // LANES

    def body(x_ref, g_ref, cw_ref, cb_ref, wa_ref, wx_ref, ba_ref, bx_ref, lam_ref, y_ref, h_ref, tail, carry):
        ci = pl.program_id(2)

        @pl.when(ci == 0)
        def _():
            tail[...] = jnp.zeros_like(tail)
            carry[...] = jnp.zeros_like(carry)

        x = x_ref[...]
        taps = _conv_taps(x, tail[...], lc)
        c = cb_ref[...] + sum(cw_ref[pl.ds(CONV_K - 1 - k, 1), :] * taps[k] for k in range(CONV_K))
        tail[...] = x_ref[pl.ds(lc - SUBLANES, SUBLANES), :]
        r, i, sp, _, a, s = _lru_gates(c, wa_ref, wx_ref, ba_ref, bx_ref, lam_ref)
        h_loc, a_run = _scan(a, s * (i * c), reverse=False)
        h_ref[...] = h_loc + a_run * carry[...]
        carry[...] = h_ref[pl.ds(lc - 1, 1), :]
        gelu, _ = _gelu_parts(g_ref[...])
        y_ref[...] = h_ref[...] * gelu

    chan = lambda b, cb, ci: (0, cb)
    return pl.pallas_call(
        body, name="lru_fwd", grid=(B, ncb, nc),
        in_specs=[pl.BlockSpec((lc, LANES), lambda b, cb, ci: (b * nc + ci, cb)),
                  pl.BlockSpec((lc, LANES), lambda b, cb, ci: (b * nc + ci, ncb + cb)),
                  pl.BlockSpec((CONV_K, LANES), chan), pl.BlockSpec((1, LANES), chan),
                  pl.BlockSpec((1, LANES, LANES), lambda b, cb, ci: (cb, 0, 0)),
                  pl.BlockSpec((1, LANES, LANES), lambda b, cb, ci: (cb, 0, 0)),
                  pl.BlockSpec((1, LANES), chan), pl.BlockSpec((1, LANES), chan), pl.BlockSpec((1, LANES), chan)],
        out_specs=[pl.BlockSpec((lc, LANES), lambda b, cb, ci: (b * nc + ci, cb))] * 2,
        out_shape=[jax.ShapeDtypeStruct((T, LRU_W), F32)] * 2,
        scratch_shapes=[pltpu.VMEM((SUBLANES, LANES), F32), pltpu.VMEM((1, LANES), F32)],
        compiler_params=_params(("arbitrary", "arbitrary", "arbitrary"), 32),
    )(proj_lru, proj_lru, conv_w, conv_b, wa_bd, wx_bd, b_a, b_x, lam)


def _cumsum_mm(v, tri):
    hi = v.astype(BF16)
    lo = (v - hi.astype(F32)).astype(BF16)
    return _dot(hi, tri) + _dot(lo, tri)


def _tri(prefix):
    r = lax.broadcasted_iota(jnp.int32, (K_BLK, K_BLK), 0)
    c = lax.broadcasted_iota(jnp.int32, (K_BLK, K_BLK), 1)
    return ((r <= c) if prefix else (r >= c)).astype(BF16)


def _attn_consts(qi):
    r = lax.broadcasted_iota(jnp.int32, (Q_BLK, K_BLK), 0)
    c = lax.broadcasted_iota(jnp.int32, (Q_BLK, K_BLK), 1)
    causal = c + ((qi // Q_PER_K) * K_BLK - qi * Q_BLK) < r
    lane = lax.broadcasted_iota(jnp.int32, (1, LANES), 1)
    return causal, c, lane, (lane < HEAD_D, lane >= HEAD_D)


def _log1m_beta(z, mask):
    lg = -(jnp.maximum(z, 0.0) + jnp.log(1.0 + jnp.exp(-jnp.abs(z))))
    return lg if mask is None else jnp.where(mask, lg, 0.0)


def _key_rows(j):
    return pl.ds(pl.multiple_of(j * K_BLK, K_BLK), K_BLK)


def _attn_fwd(qkv, tri_suffix, ride, ride_sliced, B, S):
    T = B * S
    nq = S // Q_BLK
    nhp = SB_W // LANES
    scale = HEAD_D ** -0.5
    assert S // K_BLK <= LANES
    nr = len(ride)
    ride_shape, ride_specs, ride_sems = _exchange_shapes(ride, ride_sliced)

    def body(q_ref, k_ref, v_ref, tri_ref, *rest):
        o_ref, run_ref = rest[nr:nr + 2]
        start_ride, wait_ride = _exchange_copies(rest[:nr], rest[nr + 2:2 * nr + 2], ride_sliced, *rest[2 * nr + 2:])
        qi = pl.program_id(2)
        step_no = (pl.program_id(0) * nhp + pl.program_id(1)) * nq + qi
        pl.when(step_no == 0)(start_ride)
        jd = qi // Q_PER_K
        causal, col, lane, halves = _attn_consts(qi)
        q = q_ref[...]
        qh = [jnp.where(hm, q, jnp.zeros_like(q)) * jnp.asarray(scale, BF16) for hm in halves]

        def group(blocks, carry):
            runs, tables, acc = carry
            runs, tables = list(runs), list(tables)
            ks = [k_ref[_key_rows(jl), :] for jl, _, _ in blocks]
            vs = [v_ref[_key_rows(jl), :] for jl, _, _ in blocks]
            chains = [(b, h) for b in range(len(blocks)) for h in range(2)]
            z = {c: _dot_nt(qh[c[1]], ks[c[0]]) for c in chains}
            lg = {c: _log1m_beta(z[c], blocks[c[0]][2]) for c in chains}
            suf = {c: _cumsum_mm(lg[c], tri_ref[...]) for c in chains}
            att = {}
            for b, h in chains:
                _, jlane, mask = blocks[b]
                a = jnp.exp(z[b, h] + suf[b, h] + runs[h])
                att[b, h] = (a if mask is None else jnp.where(mask, a, 0.0)).astype(BF16)
                tables[h] = jnp.where(lane == jlane, runs[h], tables[h])
                runs[h] = runs[h] + suf[b, h][:, 0:1]
            for b, h in chains:
                acc = acc + _dot(att[b, h], jnp.where(halves[h], vs[b], jnp.zeros_like(vs[b])))
            return tuple(runs), tuple(tables), acc

        col0 = jnp.zeros((Q_BLK, 1), F32)
        zero = jnp.zeros((Q_BLK, LANES), F32)
        two = jd % 2
        before = (jnp.maximum(jd - 1, 0), jnp.where(two == 1, jd - 1, -1), col < two * K_BLK)
        carry = group([(jd, jd, causal), before], ((col0, col0), (zero, zero), zero))
        top = jd - 1 - two

        def step(it, cr):
            ja = top - 2 * it
            return group([(ja, ja, None), (ja - 1, ja - 1, None)], cr)

        _, tables, acc = lax.fori_loop(0, (top + 1) // 2, step, carry)
        o_ref[...] = acc
        run_ref[:, 0:LANES] = tables[0]
        run_ref[:, LANES:2 * LANES] = tables[1]
        pl.when(step_no == B * nhp * nq - 1)(wait_ride)

    return pl.pallas_call(
        body, name="attn_fwd", grid=(B, nhp, nq),
        in_specs=[pl.BlockSpec((Q_BLK, LANES), lambda b, hp, qi: (b * nq + qi, hp)),
                  pl.BlockSpec((S, LANES), lambda b, hp, qi: (b, nhp + hp)),
                  pl.BlockSpec((S, LANES), lambda b, hp, qi: (b, 2 * nhp + hp)),
                  pl.BlockSpec((K_BLK, K_BLK), lambda b, hp, qi: (0, 0))] + ride_specs,
        out_specs=[pl.BlockSpec((Q_BLK, LANES), lambda b, hp, qi: (b * nq + qi, hp)),
                   pl.BlockSpec((Q_BLK, 2 * LANES), lambda b, hp, qi: (b * nq + qi, hp))] + ride_specs,
        out_shape=[jax.ShapeDtypeStruct((T, SB_W), F32), jax.ShapeDtypeStruct((T, 2 * SB_W), F32)] + ride_shape,
        scratch_shapes=ride_sems,
        compiler_params=_params(("arbitrary", "arbitrary", "arbitrary"), 48),
    )(qkv, qkv, qkv, tri_suffix, *ride)


def _fwd_mix(y_lru, y_sb, ga, gb, w_out16, x2, tm):
    T = x2.shape[0]

    def body(yl_ref, ys_ref, ga_ref, gb_ref, w_ref, x_ref, h1_ref, mix_ref):
        na, _, _ = _rms_fwd(yl_ref[...], ga_ref[...])
        nb, _, _ = _rms_fwd(ys_ref[...], gb_ref[...])
        na = na.astype(BF16)
        nb = nb.astype(BF16)
        mix_ref[:, 0:LRU_W] = na
        mix_ref[:, LRU_W:D_MODEL] = nb
        h1_ref[...] = x_ref[...] + _dot(na, w_ref[0:LRU_W, :]) + _dot(nb, w_ref[LRU_W:D_MODEL, :])

    row = lambda i: (i, 0)
    fix = lambda i: (0, 0)
    return pl.pallas_call(
        body, name="fwd_mix", grid=(T // tm,),
        in_specs=[pl.BlockSpec((tm, LRU_W), row), pl.BlockSpec((tm, SB_W), row),
                  pl.BlockSpec((1, LRU_W), fix), pl.BlockSpec((1, SB_W), fix),
                  pl.BlockSpec((D_MODEL, D_MODEL), fix), pl.BlockSpec((tm, D_MODEL), row)],
        out_specs=[pl.BlockSpec((tm, D_MODEL), row), pl.BlockSpec((tm, D_MODEL), row)],
        out_shape=[jax.ShapeDtypeStruct((T, D_MODEL), F32), jax.ShapeDtypeStruct((T, D_MODEL), BF16)],
        compiler_params=_params(("arbitrary",), 48),
    )(y_lru, y_sb, ga, gb, w_out16, x2)


def _fwd_mlp(h1, g2, w_up16, w_down16, gf, tgt, tm):
    T = h1.shape[0]
    nf = D_FF // FF_CHUNK

    def body(h1_ref, g2_ref, wu_ref, wd_ref, gf_ref, t_ref, up_ref, dh2_ref, dgf_ref, loss_ref, hn_s, acc):
        i, j = pl.program_id(0), pl.program_id(1)

        @pl.when(j == 0)
        def _():
            h1v = h1_ref[...]
            hn, _, _ = _rms_fwd(h1v, g2_ref[...])
            hn_s[...] = hn.astype(BF16)
            acc[...] = h1v

        up = jnp.maximum(_dot(hn_s[...], wu_ref[...]), 0.0)
        up_ref[...] = up.astype(BF16)
        acc[...] += _dot((up * up).astype(BF16), wd_ref[...])

        @pl.when((i == 0) & (j == 0))
        def _():
            dgf_ref[...] = jnp.zeros_like(dgf_ref)
            loss_ref[...] = jnp.zeros_like(loss_ref)

        @pl.when(j == nf - 1)
        def _():
            gfv = gf_ref[...]
            y, xhat, rstd = _rms_fwd(acc[...], gfv)
            err = y - t_ref[...]
            loss_ref[...] += jnp.sum(0.5 * jnp.sum(err * err, axis=-1, keepdims=True) * (1.0 / D_MODEL))
            dy = err * (1.0 / D_MODEL)
            dgf_ref[...] += jnp.sum(dy * xhat, axis=0, keepdims=True)
            dh2_ref[...] = _rms_bwd(dy, xhat, rstd, gfv)

    row = lambda i, j: (i, 0)
    fix = lambda i, j: (0, 0)
    return pl.pallas_call(
        body, name="fwd_mlp", grid=(T // tm, nf),
        in_specs=[pl.BlockSpec((tm, D_MODEL), row), pl.BlockSpec((1, D_MODEL), fix),
                  pl.BlockSpec((None, D_MODEL, FF_CHUNK), lambda i, j: (j, 0, 0)),
                  pl.BlockSpec((FF_CHUNK, D_MODEL), lambda i, j: (j, 0)),
                  pl.BlockSpec((1, D_MODEL), fix), pl.BlockSpec((tm, D_MODEL), row)],
        out_specs=[pl.BlockSpec((tm, FF_CHUNK), lambda i, j: (i, j)), pl.BlockSpec((tm, D_MODEL), row),
                   pl.BlockSpec((1, D_MODEL), fix), pl.BlockSpec((1, LANES), fix)],
        out_shape=[jax.ShapeDtypeStruct((T, D_FF), BF16), jax.ShapeDtypeStruct((T, D_MODEL), F32),
                   jax.ShapeDtypeStruct((1, D_MODEL), F32), jax.ShapeDtypeStruct((1, LANES), F32)],
        scratch_shapes=[pltpu.VMEM((tm, D_MODEL), BF16), pltpu.VMEM((tm, D_MODEL), F32)],
        compiler_params=_params(("arbitrary", "arbitrary"), 48),
    )(h1, g2, w_up16, w_down16, gf, tgt)


def _bwd_mlp(dh2, up16, h1, g2, w_up16, w_down16, tm):
    T = h1.shape[0]
    nf = D_FF // FF_CHUNK

    def body(dh2_ref, up_ref, h1_ref, g2_ref, wu_ref, wd_ref, dup_ref, dh1_ref, hn_ref, dh2b_ref, dg2_ref, acc):
        i, j = pl.program_id(0), pl.program_id(1)

        @pl.when(j == 0)
        def _():
            hn, _, _ = _rms_fwd(h1_ref[...], g2_ref[...])
            hn_ref[...] = hn.astype(BF16)
            dh2b_ref[...] = dh2_ref[...].astype(BF16)
            acc[...] = jnp.zeros_like(acc)

        u = up_ref[...].astype(F32)
        dup = (2.0 * u * _dot_nt(dh2b_ref[...], wd_ref[...])).astype(BF16)
        dup_ref[...] = dup
        acc[...] += _dot_nt(dup, wu_ref[...])

        @pl.when((i == 0) & (j == 0))
        def _():
            dg2_ref[...] = jnp.zeros_like(dg2_ref)

        @pl.when(j == nf - 1)
        def _():
            g2v = g2_ref[...]
            _, xhat, rstd = _rms_fwd(h1_ref[...], g2v)
            dhn = acc[...]
            dg2_ref[...] += jnp.sum(dhn * xhat, axis=0, keepdims=True)
            dh1_ref[...] = dh2_ref[...] + _rms_bwd(dhn, xhat, rstd, g2v)

    row = lambda i, j: (i, 0)
    fix = lambda i, j: (0, 0)
    return pl.pallas_call(
        body, name="bwd_mlp", grid=(T // tm, nf),
        in_specs=[pl.BlockSpec((tm, D_MODEL), row), pl.BlockSpec((tm, FF_CHUNK), lambda i, j: (i, j)),
                  pl.BlockSpec((tm, D_MODEL), row), pl.BlockSpec((1, D_MODEL), fix),
                  pl.BlockSpec((None, D_MODEL, FF_CHUNK), lambda i, j: (j, 0, 0)),
                  pl.BlockSpec((FF_CHUNK, D_MODEL), lambda i, j: (j, 0))],
        out_specs=[pl.BlockSpec((tm, FF_CHUNK), lambda i, j: (i, j)), pl.BlockSpec((tm, D_MODEL), row),
                   pl.BlockSpec((tm, D_MODEL), row), pl.BlockSpec((tm, D_MODEL), row),
                   pl.BlockSpec((1, D_MODEL), fix)],
        out_shape=[jax.ShapeDtypeStruct((T, D_FF), BF16), jax.ShapeDtypeStruct((T, D_MODEL), F32),
                   jax.ShapeDtypeStruct((T, D_MODEL), BF16), jax.ShapeDtypeStruct((T, D_MODEL), BF16),
                   jax.ShapeDtypeStruct((1, D_MODEL), F32)],
        scratch_shapes=[pltpu.VMEM((tm, D_MODEL), F32)],
        compiler_params=_params(("arbitrary", "arbitrary"), 48),
    )(dh2, up16, h1, g2, w_up16, w_down16)


def _matmul_tn(name, a, b, bm, bn, tk, out_shape, out_block, out_index, a_prep=None, b_prep=None):
    T, M = a.shape
    N = b.shape[1]

    def body(a_ref, b_ref, o_ref):
        k = pl.program_id(2)
        av = a_ref[...] if a_prep is None else a_prep(a_ref[...])
        bv = b_ref[...] if b_prep is None else b_prep(b_ref[...])
        p = _dot_tn(av, bv)

        @pl.when(k == 0)
        def _():
            o_ref[...] = p

        @pl.when(k > 0)
        def _():
            o_ref[...] += p

    return pl.pallas_call(
        body, name=name, grid=(M // bm, N // bn, T // tk),
        in_specs=[pl.BlockSpec((tk, bm), lambda m, n, k: (k, m)), pl.BlockSpec((tk, bn), lambda m, n, k: (k, n))],
        out_specs=pl.BlockSpec(out_block, out_index),
        out_shape=jax.ShapeDtypeStruct(out_shape, F32),
        compiler_params=_params(("arbitrary", "arbitrary", "arbitrary"), 48),
    )(a, b)


def _bwd_mix(dh1, w_out16, y_lru, y_sb, ga, gb, tm):
    T = dh1.shape[0]

    def body(d_ref, w_ref, yl_ref, ys_ref, ga_ref, gb_ref, dyl_ref, dys_ref, dga_ref, dgb_ref):
        @pl.when(pl.program_id(0) == 0)
        def _():
            dga_ref[...] = jnp.zeros_like(dga_ref)
            dgb_ref[...] = jnp.zeros_like(dgb_ref)

        d16 = d_ref[...].astype(BF16)
        for y_ref, g_ref, lo, dy_ref, dg_ref in ((yl_ref, ga_ref, 0, dyl_ref, dga_ref),
                                                 (ys_ref, gb_ref, LRU_W, dys_ref, dgb_ref)):
            gv = g_ref[...]
            dn = _dot_nt(d16, w_ref[lo:lo + LRU_W, :])
            _, xhat, rstd = _rms_fwd(y_ref[...], gv)
            dg_ref[...] += jnp.sum(dn * xhat, axis=0, keepdims=True)
            dy_ref[...] = _rms_bwd(dn, xhat, rstd, gv)

    row = lambda i: (i, 0)
    fix = lambda i: (0, 0)
    return pl.pallas_call(
        body, name="bwd_mix", grid=(T // tm,),
        in_specs=[pl.BlockSpec((tm, D_MODEL), row), pl.BlockSpec((D_MODEL, D_MODEL), fix),
                  pl.BlockSpec((tm, LRU_W), row), pl.BlockSpec((tm, SB_W), row),
                  pl.BlockSpec((1, LRU_W), fix), pl.BlockSpec((1, SB_W), fix)],
        out_specs=[pl.BlockSpec((tm, LRU_W), row), pl.BlockSpec((tm, SB_W), row),
                   pl.BlockSpec((1, LRU_W), fix), pl.BlockSpec((1, SB_W), fix)],
        out_shape=[jax.ShapeDtypeStruct((T, LRU_W), F32), jax.ShapeDtypeStruct((T, SB_W), F32),
                   jax.ShapeDtypeStruct((1, LRU_W), F32), jax.ShapeDtypeStruct((1, SB_W), F32)],
        compiler_params=_params(("arbitrary",), 48),
    )(dh1, w_out16, y_lru, y_sb, ga, gb)


def _attn_bwd(qkv, run_tab, dy_sb, tri_suffix, tri_prefix, ride, ride_sliced, B, S):
    T = B * S
    nq = S // Q_BLK
    nkb = S // K_BLK
    nhp = SB_W // LANES
    scale = HEAD_D ** -0.5

    nr = len(ride)
    ride_shape, ride_specs, ride_sems = _exchange_shapes(ride, ride_sliced)

    def body(q_ref, k_ref, v_ref, run_ref, do_ref, ts_ref, tp_ref, *rest):
        dq_ref, dkt_ref, dvt_ref = rest[nr:nr + 3]
        start_ride, wait_ride = _exchange_copies(rest[:nr], rest[nr + 3:2 * nr + 3], ride_sliced, *rest[2 * nr + 3:])
        qi = pl.program_id(2)
        step_no = (pl.program_id(0) * nhp + pl.program_id(1)) * nq + qi
        pl.when(step_no == 0)(start_ride)
        jd = qi // Q_PER_K

        @pl.when(qi == 0)
        def _():
            dkt_ref[...] = jnp.zeros_like(dkt_ref)
            dvt_ref[...] = jnp.zeros_like(dvt_ref)

        causal, col, lane, halves = _attn_consts(qi)
        q = q_ref[...]
        do = do_ref[...]
        qh = [jnp.where(hm, q, jnp.zeros_like(q)) * jnp.asarray(scale, BF16) for hm in halves]
        doh = [jnp.where(hm, do, 0.0).astype(BF16) for hm in halves]
        qt = jnp.concatenate([h.astype(F32).T.astype(BF16) for h in qh], axis=1)
        dot_ = jnp.concatenate([h.astype(F32).T.astype(BF16) for h in doh], axis=1)
        tables = [run_ref[:, 0:LANES], run_ref[:, LANES:2 * LANES]]

        def group(blocks, carry):
            prefixes, dq = carry
            prefixes = list(prefixes)
            ks = [k_ref[_key_rows(jl), :] for jl, _, _ in blocks]
            vs = [v_ref[_key_rows(jl), :] for jl, _, _ in blocks]
            chains = [(b, h) for b in range(len(blocks)) for h in range(2)]
            z = {c: _dot_nt(qh[c[1]], ks[c[0]]) for c in chains}
            da = {c: _dot_nt(doh[c[1]], vs[c[0]]) for c in chains}
            lg = {c: _log1m_beta(z[c], blocks[c[0]][2]) for c in chains}
            suf = {c: _cumsum_mm(lg[c], ts_ref[...]) for c in chains}
            att, g = {}, {}
            for b, h in chains:
                _, jlane, mask = blocks[b]
                run = jnp.sum(jnp.where(lane == jlane, tables[h], 0.0), axis=1, keepdims=True)
                a = jnp.exp(z[b, h] + suf[b, h] + run)
                a = a if mask is None else jnp.where(mask, a, 0.0)
                g[b, h] = a * da[b, h]
                att[b, h] = a.astype(BF16)
            gpre = {c: _cumsum_mm(g[c], tp_ref[...]) for c in chains}
            dz = {}
            for b, h in chains:
                mask = blocks[b][2]
                d = g[b, h] - jnp.exp(z[b, h] + lg[b, h]) * (prefixes[h] + gpre[b, h])
                dz[b, h] = (d if mask is None else jnp.where(mask, d, 0.0)).astype(BF16)
                prefixes[h] = prefixes[h] + gpre[b, h][:, K_BLK - 1:K_BLK]
            for b, h in chains:
                dq = dq + _dot(dz[b, h], jnp.where(halves[h], ks[b], jnp.zeros_like(ks[b])))
            for b, (jl, _, _) in enumerate(blocks):
                dkt_ref[jl] += _dot(qt, jnp.concatenate([dz[b, 0], dz[b, 1]], axis=0))
                dvt_ref[jl] += _dot(dot_, jnp.concatenate([att[b, 0], att[b, 1]], axis=0))
            return tuple(prefixes), dq

        def step(it, cr):
            return group([(2 * it, 2 * it, None), (2 * it + 1, 2 * it + 1, None)], cr)

        col0 = jnp.zeros((Q_BLK, 1), F32)
        carry = lax.fori_loop(0, jd // 2, step, ((col0, col0), jnp.zeros((Q_BLK, LANES), F32)))
        two = jd % 2
        before = (jnp.maximum(jd - 1, 0), jnp.where(two == 1, jd - 1, -1), col < two * K_BLK)
        dq_ref[...] = group([before, (jd, jd, causal)], carry)[1] * scale
        pl.when(step_no == B * nhp * nq - 1)(wait_ride)

    qblk = pl.BlockSpec((Q_BLK, LANES), lambda b, hp, qi: (b * nq + qi, hp))
    tri = pl.BlockSpec((K_BLK, K_BLK), lambda b, hp, qi: (0, 0))
    acc = pl.BlockSpec((None, None, nkb, LANES, K_BLK), lambda b, hp, qi: (b, hp, 0, 0, 0))
    return pl.pallas_call(
        body, name="attn_bwd", grid=(B, nhp, nq),
        in_specs=[qblk, pl.BlockSpec((S, LANES), lambda b, hp, qi: (b, nhp + hp)),
                  pl.BlockSpec((S, LANES), lambda b, hp, qi: (b, 2 * nhp + hp)),
                  pl.BlockSpec((Q_BLK, 2 * LANES), lambda b, hp, qi: (b * nq + qi, hp)), qblk, tri, tri] + ride_specs,
        out_specs=[qblk, acc, acc] + ride_specs,
        out_shape=[jax.ShapeDtypeStruct((T, SB_W), F32)] + [jax.ShapeDtypeStruct((B, nhp, nkb, LANES, K_BLK), F32)] * 2
        + ride_shape,
        scratch_shapes=ride_sems,
        compiler_params=_params(("arbitrary", "arbitrary", "arbitrary"), 48),
    )(qkv, qkv, qkv, run_tab, dy_sb, tri_suffix, tri_prefix, *ride)


def _untranspose(t, B, S):
    return t.transpose(0, 2, 4, 1, 3).reshape(B * S, SB_W)


def _lru_bwd(proj_lru, h, dy_lru, conv_w, conv_b, wa_bd, wx_bd, b_a, b_x, lam, B, S, lc):
    T = B * S
    nc = S // lc
    ncb = LRU_W // LANES
    hpc = lc // SUBLANES

    def body(x_ref, xh_ref, g_ref, h_ref, hh_ref, dy_ref, cw_ref, cb_ref, wa_ref, wx_ref, ba_ref, bx_ref, lam_ref,
             dx_ref, dg_ref, dcw_ref, dcb_ref, dwa_ref, dwx_ref, dba_ref, dbx_ref, dlam_ref,
             lam_s, dc_s, a_first, lam_first, dc_head):
        b, ci = pl.program_id(1), pl.program_id(2)
        first_chunk = ci == nc - 1

        @pl.when(ci == 0)
        def _():
            a_first[...] = jnp.zeros_like(a_first)
            lam_first[...] = jnp.zeros_like(lam_first)
            dc_head[...] = jnp.zeros_like(dc_head)

        @pl.when((b == 0) & (ci == 0))
        def _():
            for ref in (dcw_ref, dcb_ref, dwa_ref, dwx_ref, dba_ref, dbx_ref, dlam_ref):
                ref[...] = jnp.zeros_like(ref)

        x = x_ref[...]
        taps = _conv_taps(x, jnp.where(first_chunk, 0.0, xh_ref[...]), lc)
        c = cb_ref[...] + sum(cw_ref[pl.ds(CONV_K - 1 - k, 1), :] * taps[k] for k in range(CONV_K))
        r, i, sp, dsp_dlam, a, s = _lru_gates(c, wa_ref, wx_ref, ba_ref, bx_ref, lam_ref)
        hv = h_ref[...]
        he = jnp.concatenate([jnp.where(first_chunk, 0.0, hh_ref[...]), hv], axis=0)
        h_prev = pltpu.roll(he, 1, 0)[SUBLANES:SUBLANES + lc]
        dy = dy_ref[...]
        gelu, dgelu = _gelu_parts(g_ref[...])
        dg_ref[...] = dy * hv * dgelu

        row = lax.broadcasted_iota(jnp.int32, (lc, LANES), 0)
        a_next = jnp.where(row < lc - 1, pltpu.roll(a, lc - 1, 0), a_first[...])
        lam_loc, a_run = _scan(a_next, dy * gelu, reverse=True)
        lam_s[...] = lam_loc + a_run * lam_first[...]
        lam_t = lam_s[...]
        lam_first[...] = lam_s[pl.ds(0, 1), :]
        lam_s[...] = a
        a_first[...] = lam_s[pl.ds(0, 1), :]

        ic = i * c
        dlog_a = lam_t * h_prev * a - (lam_t * ic) * (a * a) / s
        dpre_r = (dlog_a * ((-LRU_C) * sp)) * r * (1.0 - r)
        dpre_i = (lam_t * s * c) * i * (1.0 - i)
        dlam_ref[...] += jnp.sum(dlog_a * r, axis=0, keepdims=True) * ((-LRU_C) * dsp_dlam)
        dr16 = dpre_r.astype(BF16)
        di16 = dpre_i.astype(BF16)
        c16 = c.astype(BF16)
        dwa_ref[0] += _dot_tn(c16, dr16)
        dwx_ref[0] += _dot_tn(c16, di16)
        dba_ref[...] += jnp.sum(dpre_r, axis=0, keepdims=True)
        dbx_ref[...] += jnp.sum(dpre_i, axis=0, keepdims=True)
        dc = lam_t * s * i + _dot_nt(dr16, wa_ref[0]) + _dot_nt(di16, wx_ref[0])
        dcb_ref[...] += jnp.sum(dc, axis=0, keepdims=True)
        for k in range(CONV_K):
            dcw_ref[pl.ds(CONV_K - 1 - k, 1), :] += jnp.sum(dc * taps[k], axis=0, keepdims=True)
        dce = jnp.concatenate([dc, dc_head[...]], axis=0)
        dx = cw_ref[pl.ds(CONV_K - 1, 1), :] * dc
        for k in range(1, CONV_K):
            dx = dx + cw_ref[pl.ds(CONV_K - 1 - k, 1), :] * pltpu.roll(dce, lc + SUBLANES - k, 0)[0:lc]
        dx_ref[...] = dx
        dc_s[...] = dc
        dc_head[...] = dc_s[pl.ds(0, SUBLANES), :]

    def chunk(col):
        return pl.BlockSpec((lc, LANES), lambda cb, b, ci: (b * nc + nc - 1 - ci, col(cb)))

    def halo(col):
        return pl.BlockSpec((SUBLANES, LANES),
                            lambda cb, b, ci: (jnp.maximum((b * nc + nc - 1 - ci) * hpc - 1, 0), col(cb)))

    chan = lambda cb, b, ci: (0, cb)
    blk = lambda cb, b, ci: (cb, 0, 0)
    vec = pl.BlockSpec((1, LANES), chan)
    mat = pl.BlockSpec((1, LANES, LANES), blk)
    return pl.pallas_call(
        body, name="lru_bwd", grid=(ncb, B, nc),
        in_specs=[chunk(lambda cb: cb), halo(lambda cb: cb), chunk(lambda cb: ncb + cb),
                  chunk(lambda cb: cb), halo(lambda cb: cb), chunk(lambda cb: cb),
                  pl.BlockSpec((CONV_K, LANES), chan), vec, mat, mat, vec, vec, vec],
        out_specs=[chunk(lambda cb: cb), chunk(lambda cb: cb), pl.BlockSpec((CONV_K, LANES), chan), vec,
                   mat, mat, vec, vec, vec],
        out_shape=[jax.ShapeDtypeStruct((T, LRU_W), F32), jax.ShapeDtypeStruct((T, LRU_W), F32),
                   jax.ShapeDtypeStruct((CONV_K, LRU_W), F32), jax.ShapeDtypeStruct((1, LRU_W), F32),
                   jax.ShapeDtypeStruct((ncb, LANES, LANES), F32), jax.ShapeDtypeStruct((ncb, LANES, LANES), F32),
                   jax.ShapeDtypeStruct((1, LRU_W), F32), jax.ShapeDtypeStruct((1, LRU_W), F32),
                   jax.ShapeDtypeStruct((1, LRU_W), F32)],
        scratch_shapes=[pltpu.VMEM((lc, LANES), F32), pltpu.VMEM((lc, LANES), F32), pltpu.VMEM((1, LANES), F32),
                        pltpu.VMEM((1, LANES), F32), pltpu.VMEM((SUBLANES, LANES), F32)],
        compiler_params=_params(("arbitrary", "arbitrary", "arbitrary"), 32),
    )(proj_lru, proj_lru, proj_lru, h, h, dy_lru, conv_w, conv_b, wa_bd, wx_bd, b_a, b_x, lam)


def _bwd_in(pieces, w_in16, x2, g1, dh1, tm):
    T = x2.shape[0]
    npc = len(pieces)

    def body(*refs):
        p_refs = refs[:npc]
        w_ref, x_ref, g_ref, d_ref, dx_ref, dproj_ref, xn_ref, dg1_ref = refs[npc:]

        @pl.when(pl.program_id(0) == 0)
        def _():
            dg1_ref[...] = jnp.zeros_like(dg1_ref)

        dxn = jnp.zeros((tm, D_MODEL), F32)
        for n, p_ref in enumerate(p_refs):
            cols = slice(n * LRU_W, (n + 1) * LRU_W)
            p16 = p_ref[...].astype(BF16)
            dproj_ref[:, cols] = p16
            dxn = dxn + _dot_nt(p16, w_ref[:, cols])
        gv = g_ref[...]
        xn, xhat, rstd = _rms_fwd(x_ref[...], gv)
        xn_ref[...] = xn.astype(BF16)
        dg1_ref[...] += jnp.sum(dxn * xhat, axis=0, keepdims=True)
        dx_ref[...] = d_ref[...] + _rms_bwd(dxn, xhat, rstd, gv)

    row = lambda i: (i, 0)
    fix = lambda i: (0, 0)
    return pl.pallas_call(
        body, name="bwd_in", grid=(T // tm,),
        in_specs=[pl.BlockSpec((tm, LRU_W), row)] * npc + [
            pl.BlockSpec((D_MODEL, IN_COLS), fix), pl.BlockSpec((tm, D_MODEL), row),
            pl.BlockSpec((1, D_MODEL), fix), pl.BlockSpec((tm, D_MODEL), row)],
        out_specs=[pl.BlockSpec((tm, D_MODEL), row), pl.BlockSpec((tm, IN_COLS), row),
                   pl.BlockSpec((tm, D_MODEL), row), pl.BlockSpec((1, D_MODEL), fix)],
        out_shape=[jax.ShapeDtypeStruct((T, D_MODEL), F32), jax.ShapeDtypeStruct((T, IN_COLS), BF16),
                   jax.ShapeDtypeStruct((T, D_MODEL), BF16), jax.ShapeDtypeStruct((1, D_MODEL), F32)],
        compiler_params=_params(("arbitrary",), 56),
    )(*pieces, w_in16, x2, g1, dh1)


def _adam_shard(name, parts, w, m, v, tr):
    R, C = w.shape

    def body(p_ref, w_ref, m_ref, v_ref, g_ref, d_ref, m2_ref, v2_ref):
        g = p_ref[0]
        for p in range(1, N_DEV):
            g = g + p_ref[p]
        g_ref[...] = g
        d_ref[...], m2_ref[...], v2_ref[...] = _adamw(w_ref[...], g, m_ref[...], v_ref[...])

    blk = pl.BlockSpec((tr, C), lambda i: (i, 0))
    return pl.pallas_call(
        body, name=name, grid=(R // tr,),
        in_specs=[pl.BlockSpec((N_DEV, tr, C), lambda i: (0, i, 0)), blk, blk, blk],
        out_specs=[blk] * 4, out_shape=[jax.ShapeDtypeStruct((R, C), F32)] * 4,
        compiler_params=_params(("arbitrary",), 48),
    )(parts, w, m, v)


def _sum_parts(name, parts):
    def body(p_ref, g_ref):
        g = p_ref[0]
        for p in range(1, N_DEV):
            g = g + p_ref[p]
        g_ref[...] = g

    return pl.pallas_call(body, name=name, out_shape=jax.ShapeDtypeStruct(parts.shape[1:], F32))(parts)


def _adam_packed(w, g, m, v):
    def body(w_ref, g_ref, m_ref, v_ref, d_ref, m2_ref, v2_ref):
        d_ref[...], m2_ref[...], v2_ref[...] = _adamw(w_ref[...], g_ref[...], m_ref[...], v_ref[...])

    return pl.pallas_call(body, name="adam_small", out_shape=[jax.ShapeDtypeStruct(w.shape, F32)] * 3)(w, g, m, v)


def _pack(vals, rows):
    flat = jnp.concatenate([v.reshape(-1).astype(F32) for v in vals])
    return jnp.pad(flat, (0, rows * LANES - flat.shape[0])).reshape(rows, LANES)


def _unpack(packed, sizes):
    flat = packed.reshape(-1)
    out, off = [], 0
    for n in sizes:
        out.append(flat[off:off + n])
        off += n
    return out


def _block_diag_pairs(w):
    w = w.reshape(LRU_BLOCKS // 2, 2, HEAD_D, HEAD_D)
    out = jnp.zeros((LRU_BLOCKS // 2, LANES, LANES), w.dtype)
    out = out.at[:, :HEAD_D, :HEAD_D].set(w[:, 0])
    return out.at[:, HEAD_D:, HEAD_D:].set(w[:, 1])


def _diag_blocks(w):
    return jnp.stack([w[:, :HEAD_D, :HEAD_D], w[:, HEAD_D:, HEAD_D:]], axis=1).reshape(LRU_BLOCKS, HEAD_D, HEAD_D)


def kernel(x, norm1_g, w_in, conv_w, conv_b, lru_w_a, lru_b_a, lru_w_x, lru_b_x, lru_lambda, lru_out_g, sb_out_g, w_out, norm2_g, w_up, w_down, final_g, loss_target, m_norm1_g, m_w_in, m_conv_w, m_conv_b, m_lru_w_a, m_lru_b_a, m_lru_w_x, m_lru_b_x, m_lru_lambda, m_lru_out_g, m_sb_out_g, m_w_out, m_norm2_g, m_w_up, m_w_down, m_final_g, v_norm1_g, v_w_in, v_conv_w, v_conv_b, v_lru_w_a, v_lru_b_a, v_lru_w_x, v_lru_b_x, v_lru_lambda, v_lru_out_g, v_sb_out_g, v_w_out, v_norm2_g, v_w_up, v_w_down, v_final_g):
    B, S, _ = x.shape
    T = B * S
    tm = min(512, T)
    lc = min(512, S)
    me = _my_index()
    x2 = x.reshape(T, D_MODEL)
    tgt = loss_target.reshape(T, D_MODEL)
    cw_cols = CONV_K * LRU_W // N_DEV // CONV_K

    shards16 = _cast_shards([w_in[0], w_out[0], w_up[0], w_down[0]])
    cw_pad = jnp.zeros((SUBLANES, LANES), F32).at[:CONV_K, :cw_cols].set(conv_w[0])
    g_in, g_cw = _exchange("gather_w_in", [shards16[0], cw_pad], [False, False])
    w_in16 = g_in.transpose(1, 0, 2).reshape(D_MODEL, IN_COLS)
    conv_w_full = g_cw[:, :CONV_K, :cw_cols].transpose(1, 0, 2).reshape(CONV_K, LRU_W)
    wa_bd = _block_diag_pairs(lru_w_a[0]).astype(BF16)
    wx_bd = _block_diag_pairs(lru_w_x[0]).astype(BF16)
    b_a = lru_b_a.reshape(1, LRU_W)
    b_x = lru_b_x.reshape(1, LRU_W)
    gf = final_g.reshape(1, D_MODEL)

    proj_lru, qkv = _fwd_in(x2, norm1_g, w_in16, tm)
    y_lru, h = _lru_fwd(proj_lru, conv_w_full, conv_b, wa_bd, wx_bd, b_a, b_x, lru_lambda, B, S, lc)
    tri_suffix, tri_prefix = _tri(False), _tri(True)
    y_sb, run_tab, g_out, g_up, g_down = _attn_fwd(qkv, tri_suffix, list(shards16[1:]), [False] * 3, B, S)
    w_out16 = g_out.reshape(D_MODEL, D_MODEL)
    w_down16 = g_down.reshape(D_FF, D_MODEL)
    h1, mix16 = _fwd_mix(y_lru, y_sb, lru_out_g, sb_out_g, w_out16, x2, tm)
    up16, dh2, d_final_g, loss_part = _fwd_mlp(h1, norm2_g, g_up, w_down16, gf, tgt, tm)

    dup16, dh1, hn16, dh2b, d_norm2_g = _bwd_mlp(dh2, up16, h1, norm2_g, g_up, w_down16, tm)
    sq = lambda u: (u.astype(F32) * u.astype(F32)).astype(BF16)
    gw_up = _matmul_tn("grad_w_up", hn16, dup16, D_MODEL, FF_CHUNK, tm, (N_DEV, D_MODEL, FF_CHUNK),
                       (None, D_MODEL, FF_CHUNK), lambda m, n, k: (n, 0, 0))
    gw_down = _matmul_tn("grad_w_down", up16, dh2b, FF_CHUNK, D_MODEL, tm, (N_DEV, FF_CHUNK, D_MODEL),
                         (None, FF_CHUNK, D_MODEL), lambda m, n, k: (m, 0, 0), a_prep=sq)
    dy_lru, dy_sb, d_lru_out_g, d_sb_out_g = _bwd_mix(dh1, w_out16, y_lru, y_sb, lru_out_g, sb_out_g, tm)
    gw_out = _matmul_tn("grad_w_out", mix16, dh1, D_MODEL, FF_CHUNK, tm, (D_MODEL, D_MODEL),
                        (D_MODEL, FF_CHUNK), lambda m, n, k: (0, n), b_prep=lambda u: u.astype(BF16))
    parts_out = gw_out.reshape(N_DEV, D_MODEL // N_DEV, D_MODEL)
    dq, dkt, dvt, r_out, r_up, r_down = _attn_bwd(qkv, run_tab, dy_sb, tri_suffix, tri_prefix,
                                                  [parts_out, gw_up, gw_down], [True] * 3, B, S)
    dk, dv = _untranspose(dkt, B, S), _untranspose(dvt, B, S)
    (dx_lru, dg_lru, d_conv_w, d_conv_b, d_wa, d_wx, d_b_a, d_b_x, d_lambda) = _lru_bwd(
        proj_lru, h, dy_lru, conv_w_full, conv_b, wa_bd, wx_bd, b_a, b_x, lru_lambda, B, S, lc)
    dx, dproj16, xn16, d_norm1_g = _bwd_in([dx_lru, dg_lru, dq, dk, dv], w_in16, x2, norm1_g, dh1, tm)
    gw_in = _matmul_tn("grad_w_in", xn16, dproj16, D_MODEL, FF_CHUNK, tm, (D_MODEL, IN_COLS),
                       (D_MODEL, FF_CHUNK), lambda m, n, k: (0, n))

    small_grads = {"norm1_g": d_norm1_g, "conv_b": d_conv_b, "lru_w_a": _diag_blocks(d_wa), "lru_b_a": d_b_a,
                   "lru_w_x": _diag_blocks(d_wx), "lru_b_x": d_b_x, "lru_lambda": d_lambda,
                   "lru_out_g": d_lru_out_g, "sb_out_g": d_sb_out_g, "norm2_g": d_norm2_g, "final_g": d_final_g}
    packed = _pack([small_grads[n] for n, _ in SMALL] + [d_conv_w, loss_part], EXCH_ROWS)
    parts_in = gw_in.reshape(D_MODEL, N_DEV, IN_COLS // N_DEV).transpose(1, 0, 2)
    r_in, r_small = _exchange("exchange_grads", [parts_in, packed], [True, False])

    g_w_in, d_w_in, nm_w_in, nv_w_in = _adam_shard("adam_w_in", r_in, w_in[0], m_w_in[0], v_w_in[0], 256)
    g_w_out, d_w_out, nm_w_out, nv_w_out = _adam_shard("adam_w_out", r_out, w_out[0], m_w_out[0], v_w_out[0], 64)
    g_w_up, d_w_up, nm_w_up, nv_w_up = _adam_shard("adam_w_up", r_up, w_up[0], m_w_up[0], v_w_up[0], 256)
    g_w_down, d_w_down, nm_w_down, nv_w_down = _adam_shard("adam_w_down", r_down, w_down[0], m_w_down[0], v_w_down[0], 128)

    total = _sum_parts("sum_small", r_small)
    sizes = [n for _, n in SMALL]
    small_g = _unpack(total, sizes + [CONV_K * LRU_W, 1])
    loss = small_g[-1][0]
    g_conv_w = lax.dynamic_slice_in_dim(small_g[-2].reshape(CONV_K, LRU_W), me * cw_cols, cw_cols, axis=1)
    given = dict(norm1_g=(norm1_g, m_norm1_g, v_norm1_g), conv_b=(conv_b, m_conv_b, v_conv_b),
                 lru_w_a=(lru_w_a, m_lru_w_a, v_lru_w_a), lru_b_a=(lru_b_a, m_lru_b_a, v_lru_b_a),
                 lru_w_x=(lru_w_x, m_lru_w_x, v_lru_w_x), lru_b_x=(lru_b_x, m_lru_b_x, v_lru_b_x),
                 lru_lambda=(lru_lambda, m_lru_lambda, v_lru_lambda), lru_out_g=(lru_out_g, m_lru_out_g, v_lru_out_g),
                 sb_out_g=(sb_out_g, m_sb_out_g, v_sb_out_g), norm2_g=(norm2_g, m_norm2_g, v_norm2_g),
                 final_g=(final_g, m_final_g, v_final_g))
    names = [n for n, _ in SMALL]
    pw = _pack([given[n][0] for n in names] + [conv_w], ADAM_ROWS)
    pm = _pack([given[n][1] for n in names] + [m_conv_w], ADAM_ROWS)
    pv = _pack([given[n][2] for n in names] + [v_conv_w], ADAM_ROWS)
    pg = _pack(small_g[:len(names)] + [g_conv_w], ADAM_ROWS)
    pd, pm2, pv2 = _adam_packed(pw, pg, pm, pv)
    asz = sizes + [conv_w.size]
    shapes = {n: given[n][0].shape for n in names}
    shapes["conv_w"] = conv_w.shape
    order = names + ["conv_w"]
    g_small = dict(zip(order, [a.reshape(shapes[n]) for n, a in zip(order, _unpack(pg, asz))]))
    d_small = dict(zip(order, [a.reshape(shapes[n]) for n, a in zip(order, _unpack(pd, asz))]))
    m_small = dict(zip(order, [a.reshape(shapes[n]) for n, a in zip(order, _unpack(pm2, asz))]))
    v_small = dict(zip(order, [a.reshape(shapes[n]) for n, a in zip(order, _unpack(pv2, asz))]))

    big = {"w_in": (g_w_in, d_w_in, nm_w_in, nv_w_in), "w_out": (g_w_out, d_w_out, nm_w_out, nv_w_out),
           "w_up": (g_w_up, d_w_up, nm_w_up, nv_w_up), "w_down": (g_w_down, d_w_down, nm_w_down, nv_w_down)}
    weights = ["norm1_g", "w_in", "conv_w", "conv_b", "lru_w_a", "lru_b_a", "lru_w_x", "lru_b_x", "lru_lambda",
               "lru_out_g", "sb_out_g", "w_out", "norm2_g", "w_up", "w_down", "final_g"]

    def leaf(n, kind):
        if n in big:
            return big[n][kind][None]
        return (g_small, d_small, m_small, v_small)[kind][n]

    return (loss, dx.reshape(B, S, D_MODEL), *[leaf(n, 0) for n in weights], *[leaf(n, 1) for n in weights],
            *[leaf(n, 2) for n in weights], *[leaf(n, 3) for n in weights])
```

```python
import jax
import jax.numpy as jnp
from jax import lax
from jax.experimental import pallas as pl
from jax.experimental.pallas import tpu as pltpu

F32 = jnp.float32
BF16 = jnp.bfloat16

D_MODEL = 1024
LRU_W = 512
SB_W = 512
HEAD_D = 64
D_FF = 4096
IN_COLS = 2 * LRU_W + 3 * SB_W
CONV_K = 4
LRU_BLOCKS = 8
LRU_C = 8.0
EPS = 1e-6
N_DEV = 8
LANES = 128
SUBLANES = 8
FF_CHUNK = 512
FF_PER_STEP = 4
GRAD_TK = 2048
Q_BLK = 256
K_BLK = 256
Q_PER_K = K_BLK // Q_BLK

ADAM_LR = 0.001
ADAM_B1 = 0.9
ADAM_B2 = 0.999
ADAM_EPS = 1e-08
ADAM_WD = 0.01
ADAM_STEP = 10

SMALL = (("norm1_g", 1024), ("conv_b", 512), ("lru_w_a", 32768), ("lru_b_a", 512), ("lru_w_x", 32768),
         ("lru_b_x", 512), ("lru_lambda", 512), ("lru_out_g", 512), ("sb_out_g", 512), ("norm2_g", 1024),
         ("final_g", 1024))
SMALL_ROWS = sum(n for _, n in SMALL) // LANES
EXCH_ROWS = SMALL_ROWS + (CONV_K * LRU_W) // LANES + 8
ADAM_ROWS = SMALL_ROWS + 8


def _params(sem=None, vmem_mb=None):
    kw = {}
    if sem is not None:
        kw["dimension_semantics"] = sem
    if vmem_mb is not None:
        kw["vmem_limit_bytes"] = vmem_mb << 20
    return pltpu.CompilerParams(**kw)


def _dot(a, b):
    return jnp.dot(a, b, preferred_element_type=F32)


def _dot_nt(a, b):
    return lax.dot_general(a, b, (((1,), (1,)), ((), ())), preferred_element_type=F32)


def _dot_tn(a, b):
    return lax.dot_general(a, b, (((0,), (0,)), ((), ())), preferred_element_type=F32)


def _rms_fwd(x, g):
    rstd = lax.rsqrt(jnp.mean(x * x, axis=-1, keepdims=True) + EPS)
    xhat = x * rstd
    return xhat * g, xhat, rstd


def _rms_bwd(dy, xhat, rstd, g):
    dxhat = dy * g
    return rstd * (dxhat - xhat * jnp.mean(dxhat * xhat, axis=-1, keepdims=True))


def _sigmoid(x):
    return 1.0 / (1.0 + jnp.exp(-x))


def _log1p_pos(e):
    series = e * (1.0 - e * (0.5 - e * (1.0 / 3.0 - e * 0.25)))
    return jnp.where(e < 1e-2, series, jnp.log(1.0 + e))


def _neg_expm1(x):
    series = -x * (1.0 + x * (0.5 + x * (1.0 / 6.0 + x * (1.0 / 24.0))))
    return jnp.where(x > -1e-2, series, 1.0 - jnp.exp(x))


def _gelu_parts(g):
    k0 = 0.7978845608028654
    k1 = 0.044715
    t = jnp.tanh(k0 * (g + k1 * g * g * g))
    val = 0.5 * g * (1.0 + t)
    grad = 0.5 * (1.0 + t) + 0.5 * g * (1.0 - t * t) * k0 * (1.0 + 3.0 * k1 * g * g)
    return val, grad


def _scan(a, b, reverse):
    n = a.shape[0]
    row = lax.broadcasted_iota(jnp.int32, a.shape, 0)
    s = 1
    while s < n:
        if reverse:
            keep = row < n - s
            shift = n - s
        else:
            keep = row >= s
            shift = s
        bs = jnp.where(keep, pltpu.roll(b, shift, 0), 0.0)
        a_s = jnp.where(keep, pltpu.roll(a, shift, 0), 1.0)
        b = a * bs + b
        a = a * a_s
        s *= 2
    return b, a


def _adamw(w, g, m, v):
    m = ADAM_B1 * m + (1.0 - ADAM_B1) * g
    v = ADAM_B2 * v + (1.0 - ADAM_B2) * (g * g)
    m_hat = m / (1.0 - ADAM_B1 ** ADAM_STEP)
    v_hat = v / (1.0 - ADAM_B2 ** ADAM_STEP)
    delta = -ADAM_LR * (m_hat / (jnp.sqrt(v_hat) + ADAM_EPS) + ADAM_WD * w)
    return delta, m, v


def _my_index():
    return 4 * lax.axis_index("x") + 2 * lax.axis_index("y") + lax.axis_index("c")


def _peer(k):
    x, y, c = lax.axis_index("x"), lax.axis_index("y"), lax.axis_index("c")
    px = 1 - x if (k >> 2) & 1 else x
    py = 1 - y if (k >> 1) & 1 else y
    pc = 1 - c if k & 1 else c
    return (px, py, pc), 4 * px + 2 * py + pc


def _exchange_shapes(srcs, sliced):
    n = len(srcs)
    out_shape = [jax.ShapeDtypeStruct(s.shape if sl else (N_DEV,) + s.shape, s.dtype) for s, sl in zip(srcs, sliced)]
    specs = [pl.BlockSpec(memory_space=pl.ANY)] * n
    sems = [pltpu.SemaphoreType.DMA((n, N_DEV - 1)), pltpu.SemaphoreType.DMA((n, N_DEV - 1)),
            pltpu.SemaphoreType.DMA((n,))]
    return out_shape, specs, sems


def _exchange_copies(ins, outs, sliced, send_sems, recv_sems, local_sems):
    n = len(ins)

    def part(a, p):
        return ins[a].at[p] if sliced[a] else ins[a]

    def copies(receiving):
        me = _my_index()
        local = [pltpu.make_async_copy(part(a, me), outs[a].at[me], local_sems.at[a]) for a in range(n)]
        remote = []
        for k in range(1, N_DEV):
            dev, idx = _peer(k)
            for a in range(n):
                remote.append(pltpu.make_async_remote_copy(
                    src_ref=part(a, idx), dst_ref=outs[a].at[idx if receiving else me],
                    send_sem=send_sems.at[a, k - 1], recv_sem=recv_sems.at[a, k - 1],
                    device_id=dev, device_id_type=pl.DeviceIdType.MESH))
        return local, remote

    def start():
        local, remote = copies(receiving=False)
        for cp in local + remote:
            cp.start()

    def wait():
        local, remote = copies(receiving=True)
        for cp in remote + local:
            cp.wait()

    return start, wait


def _exchange(name, srcs, sliced):
    n = len(srcs)
    out_shape, specs, sems = _exchange_shapes(srcs, sliced)

    def body(*refs):
        start, wait = _exchange_copies(refs[:n], refs[n:2 * n], sliced, *refs[2 * n:])
        start()
        wait()

    return pl.pallas_call(body, name=name, out_shape=out_shape, in_specs=specs, out_specs=specs,
                          scratch_shapes=sems)(*srcs)


def _cast_shards(ws):
    def body(*refs):
        for i in range(len(ws)):
            refs[len(ws) + i][...] = refs[i][...].astype(BF16)

    return pl.pallas_call(
        body, name="cast_shards", out_shape=[jax.ShapeDtypeStruct(w.shape, BF16) for w in ws],
        compiler_params=_params(vmem_mb=32),
    )(*ws)


def _fwd_in(x2, g1, w_in16, tm):
    T = x2.shape[0]

    def body(x_ref, g_ref, w_ref, lru_ref, qkv_ref):
        xn, _, _ = _rms_fwd(x_ref[...], g_ref[...])
        xn = xn.astype(BF16)
        lru_ref[...] = _dot(xn, w_ref[:, 0:2 * LRU_W])
        qkv_ref[...] = _dot(xn, w_ref[:, 2 * LRU_W:IN_COLS]).astype(BF16)

    return pl.pallas_call(
        body, name="fwd_in", grid=(T // tm,),
        in_specs=[pl.BlockSpec((tm, D_MODEL), lambda i: (i, 0)),
                  pl.BlockSpec((1, D_MODEL), lambda i: (0, 0)),
                  pl.BlockSpec((D_MODEL, IN_COLS), lambda i: (0, 0))],
        out_specs=[pl.BlockSpec((tm, 2 * LRU_W), lambda i: (i, 0)),
                   pl.BlockSpec((tm, 3 * SB_W), lambda i: (i, 0))],
        out_shape=[jax.ShapeDtypeStruct((T, 2 * LRU_W), F32), jax.ShapeDtypeStruct((T, 3 * SB_W), BF16)],
        compiler_params=_params(("arbitrary",), 48),
    )(x2, g1, w_in16)


def _lru_gates(c, wa_ref, wx_ref, ba_ref, bx_ref, lam_ref):
    c16 = c.astype(BF16)
    r = _sigmoid(_dot(c16, wa_ref[0]) + ba_ref[...])
    i = _sigmoid(_dot(c16, wx_ref[0]) + bx_ref[...])
    lam = lam_ref[...]
    e = jnp.exp(-jnp.abs(lam))
    sp = jnp.maximum(-lam, 0.0) + _log1p_pos(e)
    dsp_dlam = -jnp.where(lam >= 0.0, e, 1.0) / (1.0 + e)
    log_a = (-LRU_C) * r * sp
    a = jnp.exp(log_a)
    s = jnp.sqrt(_neg_expm1(2.0 * log_a))
    return r, i, sp, dsp_dlam, a, s


def _conv_taps(x, halo, lc):
    xe = jnp.concatenate([halo, x], axis=0)
    return [x] + [pltpu.roll(xe, k, 0)[SUBLANES:SUBLANES + lc] for k in range(1, CONV_K)]


def _lru_fwd(proj_lru, conv_w, conv_b, wa_bd, wx_bd, b_a, b_x, lam, B, S, lc):
    T = B * S
    nc = S // lc
    ncb = LRU_W // LANES

    def body(x_ref, g_ref, cw_ref, cb_ref, wa_ref, wx_ref, ba_ref, bx_ref, lam_ref, y_ref, h_ref, tail, carry):
        ci = pl.program_id(2)

        @pl.when(ci == 0)
        def _():
            tail[...] = jnp.zeros_like(tail)
            carry[...] = jnp.zeros_like(carry)

        x = x_ref[...]
        taps = _conv_taps(x, tail[...], lc)
        c = cb_ref[...] + sum(cw_ref[pl.ds(CONV_K - 1 - k, 1), :] * taps[k] for k in range(CONV_K))
        tail[...] = x_ref[pl.ds(lc - SUBLANES, SUBLANES), :]
        r, i, sp, _, a, s = _lru_gates(c, wa_ref, wx_ref, ba_ref, bx_ref, lam_ref)
        h_loc, a_run = _scan(a, s * (i * c), reverse=False)
        h_ref[...] = h_loc + a_run * carry[...]
        carry[...] = h_ref[pl.ds(lc - 1, 1), :]
        gelu, _ = _gelu_parts(g_ref[...])
        y_ref[...] = h_ref[...] * gelu

    chan = lambda b, cb, ci: (0, cb)
    return pl.pallas_call(
        body, name="lru_fwd", grid=(B, ncb, nc),
        in_specs=[pl.BlockSpec((lc, LANES), lambda b, cb, ci: (b * nc + ci, cb)),
                  pl.BlockSpec((lc, LANES), lambda b, cb, ci: (b * nc + ci, ncb + cb)),
                  pl.BlockSpec((CONV_K, LANES), chan), pl.BlockSpec((1, LANES), chan),
                  pl.BlockSpec((1, LANES, LANES), lambda b, cb, ci: (cb, 0, 0)),
                  pl.BlockSpec((1, LANES, LANES), lambda b, cb, ci: (cb, 0, 0)),
                  pl.BlockSpec((1, LANES), chan), pl.BlockSpec((1, LANES), chan), pl.BlockSpec((1, LANES), chan)],
        out_specs=[pl.BlockSpec((lc, LANES), lambda b, cb, ci: (b * nc + ci, cb))] * 2,
        out_shape=[jax.ShapeDtypeStruct((T, LRU_W), F32)] * 2,
        scratch_shapes=[pltpu.VMEM((SUBLANES, LANES), F32), pltpu.VMEM((1, LANES), F32)],
        compiler_params=_params(("arbitrary", "arbitrary", "arbitrary"), 32),
    )(proj_lru, proj_lru, conv_w, conv_b, wa_bd, wx_bd, b_a, b_x, lam)


def _cumsum_mm(v, tri):
    hi = v.astype(BF16)
    lo = (v - hi.astype(F32)).astype(BF16)
    return _dot(hi, tri) + _dot(lo, tri)


def _tri(prefix):
    r = lax.broadcasted_iota(jnp.int32, (K_BLK, K_BLK), 0)
    c = lax.broadcasted_iota(jnp.int32, (K_BLK, K_BLK), 1)
    return ((r <= c) if prefix else (r >= c)).astype(BF16)


def _attn_consts(qi):
    r = lax.broadcasted_iota(jnp.int32, (Q_BLK, K_BLK), 0)
    c = lax.broadcasted_iota(jnp.int32, (Q_BLK, K_BLK), 1)
    causal = c + ((qi // Q_PER_K) * K_BLK - qi * Q_BLK) < r
    lane = lax.broadcasted_iota(jnp.int32, (1, LANES), 1)
    return causal, c, lane, (lane < HEAD_D, lane >= HEAD_D)


def _log1m_beta(z, mask):
    lg = -(jnp.maximum(z, 0.0) + jnp.log(1.0 + jnp.exp(-jnp.abs(z))))
    return lg if mask is None else jnp.where(mask, lg, 0.0)


def _key_rows(j):
    return pl.ds(pl.multiple_of(j * K_BLK, K_BLK), K_BLK)


def _attn_fwd(qkv, tri_suffix, ride, ride_sliced, B, S):
    T = B * S
    nq = S // Q_BLK
    nhp = SB_W // LANES
    scale = HEAD_D ** -0.5
    assert S // K_BLK <= LANES
    nr = len(ride)
    ride_shape, ride_specs, ride_sems = _exchange_shapes(ride, ride_sliced)

    def body(q_ref, k_ref, v_ref, tri_ref, *rest):
        o_ref, run_ref = rest[nr:nr + 2]
        start_ride, wait_ride = _exchange_copies(rest[:nr], rest[nr + 2:2 * nr + 2], ride_sliced, *rest[2 * nr + 2:])
        qi = pl.program_id(2)
        step_no = (pl.program_id(0) * nhp + pl.program_id(1)) * nq + qi
        pl.when(step_no == 0)(start_ride)
        jd = qi // Q_PER_K
        causal, col, lane, halves = _attn_consts(qi)
        q = q_ref[...]
        qh = [jnp.where(hm, q, jnp.zeros_like(q)) * jnp.asarray(scale, BF16) for hm in halves]

        def group(blocks, carry):
            runs, tables, acc = carry
            runs, tables = list(runs), list(tables)
            ks = [k_ref[_key_rows(jl), :] for jl, _, _ in blocks]
            vs = [v_ref[_key_rows(jl), :] for jl, _, _ in blocks]
            chains = [(b, h) for b in range(len(blocks)) for h in range(2)]
            z = {c: _dot_nt(qh[c[1]], ks[c[0]]) for c in chains}
            lg = {c: _log1m_beta(z[c], blocks[c[0]][2]) for c in chains}
            suf = {c: _cumsum_mm(lg[c], tri_ref[...]) for c in chains}
            att = {}
            for b, h in chains:
                _, jlane, mask = blocks[b]
                a = jnp.exp(z[b, h] + suf[b, h] + runs[h])
                att[b, h] = (a if mask is None else jnp.where(mask, a, 0.0)).astype(BF16)
                tables[h] = jnp.where(lane == jlane, runs[h], tables[h])
                runs[h] = runs[h] + suf[b, h][:, 0:1]
            for b, h in chains:
                acc = acc + _dot(att[b, h], jnp.where(halves[h], vs[b], jnp.zeros_like(vs[b])))
            return tuple(runs), tuple(tables), acc

        col0 = jnp.zeros((Q_BLK, 1), F32)
        zero = jnp.zeros((Q_BLK, LANES), F32)
        two = jd % 2
        before = (jnp.maximum(jd - 1, 0), jnp.where(two == 1, jd - 1, -1), col < two * K_BLK)
        carry = group([(jd, jd, causal), before], ((col0, col0), (zero, zero), zero))
        top = jd - 1 - two

        def step(it, cr):
            ja = top - 2 * it
            return group([(ja, ja, None), (ja - 1, ja - 1, None)], cr)

        _, tables, acc = lax.fori_loop(0, (top + 1) // 2, step, carry)
        o_ref[...] = acc
        run_ref[:, 0:LANES] = tables[0]
        run_ref[:, LANES:2 * LANES] = tables[1]
        pl.when(step_no == B * nhp * nq - 1)(wait_ride)

    return pl.pallas_call(
        body, name="attn_fwd", grid=(B, nhp, nq),
        in_specs=[pl.BlockSpec((Q_BLK, LANES), lambda b, hp, qi: (b * nq + qi, hp)),
                  pl.BlockSpec((S, LANES), lambda b, hp, qi: (b, nhp + hp)),
                  pl.BlockSpec((S, LANES), lambda b, hp, qi: (b, 2 * nhp + hp)),
                  pl.BlockSpec((K_BLK, K_BLK), lambda b, hp, qi: (0, 0))] + ride_specs,
        out_specs=[pl.BlockSpec((Q_BLK, LANES), lambda b, hp, qi: (b * nq + qi, hp)),
                   pl.BlockSpec((Q_BLK, 2 * LANES), lambda b, hp, qi: (b * nq + qi, hp))] + ride_specs,
        out_shape=[jax.ShapeDtypeStruct((T, SB_W), F32), jax.ShapeDtypeStruct((T, 2 * SB_W), F32)] + ride_shape,
        scratch_shapes=ride_sems,
        compiler_params=_params(("arbitrary", "arbitrary", "arbitrary"), 48),
    )(qkv, qkv, qkv, tri_suffix, *ride)


def _fwd_mix(y_lru, y_sb, ga, gb, w_out16, x2, tm):
    T = x2.shape[0]

    def body(yl_ref, ys_ref, ga_ref, gb_ref, w_ref, x_ref, h1_ref, mix_ref):
        na, _, _ = _rms_fwd(yl_ref[...], ga_ref[...])
        nb, _, _ = _rms_fwd(ys_ref[...], gb_ref[...])
        na = na.astype(BF16)
        nb = nb.astype(BF16)
        mix_ref[:, 0:LRU_W] = na
        mix_ref[:, LRU_W:D_MODEL] = nb
        h1_ref[...] = x_ref[...] + _dot(na, w_ref[0:LRU_W, :]) + _dot(nb, w_ref[LRU_W:D_MODEL, :])

    row = lambda i: (i, 0)
    fix = lambda i: (0, 0)
    return pl.pallas_call(
        body, name="fwd_mix", grid=(T // tm,),
        in_specs=[pl.BlockSpec((tm, LRU_W), row), pl.BlockSpec((tm, SB_W), row),
                  pl.BlockSpec((1, LRU_W), fix), pl.BlockSpec((1, SB_W), fix),
                  pl.BlockSpec((D_MODEL, D_MODEL), fix), pl.BlockSpec((tm, D_MODEL), row)],
        out_specs=[pl.BlockSpec((tm, D_MODEL), row), pl.BlockSpec((tm, D_MODEL), row)],
        out_shape=[jax.ShapeDtypeStruct((T, D_MODEL), F32), jax.ShapeDtypeStruct((T, D_MODEL), BF16)],
        compiler_params=_params(("arbitrary",), 48),
    )(y_lru, y_sb, ga, gb, w_out16, x2)


def _fwd_mlp(h1, g2, w_up16, w_down16, gf, tgt, tm):
    T = h1.shape[0]
    nf = D_FF // (FF_CHUNK * FF_PER_STEP)

    def body(h1_ref, g2_ref, wu_ref, wd_ref, gf_ref, t_ref, up_ref, dh2_ref, dgf_ref, loss_ref, hn_s, acc):
        i, j = pl.program_id(0), pl.program_id(1)

        @pl.when(j == 0)
        def _():
            h1v = h1_ref[...]
            hn, _, _ = _rms_fwd(h1v, g2_ref[...])
            hn_s[...] = hn.astype(BF16)
            acc[...] = h1v

        down = None
        for c in range(FF_PER_STEP):
            cols = slice(c * FF_CHUNK, (c + 1) * FF_CHUNK)
            up = jnp.maximum(_dot(hn_s[...], wu_ref[c]), 0.0)
            up_ref[:, cols] = up.astype(BF16)
            part = _dot((up * up).astype(BF16), wd_ref[cols, :])
            down = part if down is None else down + part
        acc[...] += down

        @pl.when((i == 0) & (j == 0))
        def _():
            dgf_ref[...] = jnp.zeros_like(dgf_ref)
            loss_ref[...] = jnp.zeros_like(loss_ref)

        @pl.when(j == nf - 1)
        def _():
            gfv = gf_ref[...]
            y, xhat, rstd = _rms_fwd(acc[...], gfv)
            err = y - t_ref[...]
            loss_ref[...] += jnp.sum(0.5 * jnp.sum(err * err, axis=-1, keepdims=True) * (1.0 / D_MODEL))
            dy = err * (1.0 / D_MODEL)
            dgf_ref[...] += jnp.sum(dy * xhat, axis=0, keepdims=True)
            dh2_ref[...] = _rms_bwd(dy, xhat, rstd, gfv)

    row = lambda i, j: (i, 0)
    fix = lambda i, j: (0, 0)
    return pl.pallas_call(
        body, name="fwd_mlp", grid=(T // tm, nf),
        in_specs=[pl.BlockSpec((tm, D_MODEL), row), pl.BlockSpec((1, D_MODEL), fix),
                  pl.BlockSpec((FF_PER_STEP, D_MODEL, FF_CHUNK), lambda i, j: (j, 0, 0)),
                  pl.BlockSpec((FF_PER_STEP * FF_CHUNK, D_MODEL), lambda i, j: (j, 0)),
                  pl.BlockSpec((1, D_MODEL), fix), pl.BlockSpec((tm, D_MODEL), row)],
        out_specs=[pl.BlockSpec((tm, FF_PER_STEP * FF_CHUNK), lambda i, j: (i, j)), pl.BlockSpec((tm, D_MODEL), row),
                   pl.BlockSpec((1, D_MODEL), fix), pl.BlockSpec((1, LANES), fix)],
        out_shape=[jax.ShapeDtypeStruct((T, D_FF), BF16), jax.ShapeDtypeStruct((T, D_MODEL), F32),
                   jax.ShapeDtypeStruct((1, D_MODEL), F32), jax.ShapeDtypeStruct((1, LANES), F32)],
        scratch_shapes=[pltpu.VMEM((tm, D_MODEL), BF16), pltpu.VMEM((tm, D_MODEL), F32)],
        compiler_params=_params(("arbitrary", "arbitrary"), 48),
    )(h1, g2, w_up16, w_down16, gf, tgt)


def _bwd_mlp(dh2, up16, h1, g2, w_up16, w_down16, tm):
    T = h1.shape[0]
    nf = D_FF // (FF_CHUNK * FF_PER_STEP)

    def body(dh2_ref, up_ref, h1_ref, g2_ref, wu_ref, wd_ref, dup_ref, dh1_ref, hn_ref, dh2b_ref, dg2_ref, acc):
        i, j = pl.program_id(0), pl.program_id(1)

        @pl.when(j == 0)
        def _():
            hn, _, _ = _rms_fwd(h1_ref[...], g2_ref[...])
            hn_ref[...] = hn.astype(BF16)
            dh2b_ref[...] = dh2_ref[...].astype(BF16)
            acc[...] = jnp.zeros_like(acc)

        dhn = None
        for c in range(FF_PER_STEP):
            cols = slice(c * FF_CHUNK, (c + 1) * FF_CHUNK)
            u = up_ref[:, cols].astype(F32)
            dup = (2.0 * u * _dot_nt(dh2b_ref[...], wd_ref[cols, :])).astype(BF16)
            dup_ref[:, cols] = dup
            part = _dot_nt(dup, wu_ref[c])
            dhn = part if dhn is None else dhn + part
        acc[...] += dhn

        @pl.when((i == 0) & (j == 0))
        def _():
            dg2_ref[...] = jnp.zeros_like(dg2_ref)

        @pl.when(j == nf - 1)
        def _():
            g2v = g2_ref[...]
            _, xhat, rstd = _rms_fwd(h1_ref[...], g2v)
            dhn = acc[...]
            dg2_ref[...] += jnp.sum(dhn * xhat, axis=0, keepdims=True)
            dh1_ref[...] = dh2_ref[...] + _rms_bwd(dhn, xhat, rstd, g2v)

    row = lambda i, j: (i, 0)
    fix = lambda i, j: (0, 0)
    return pl.pallas_call(
        body, name="bwd_mlp", grid=(T // tm, nf),
        in_specs=[pl.BlockSpec((tm, D_MODEL), row), pl.BlockSpec((tm, FF_PER_STEP * FF_CHUNK), lambda i, j: (i, j)),
                  pl.BlockSpec((tm, D_MODEL), row), pl.BlockSpec((1, D_MODEL), fix),
                  pl.BlockSpec((FF_PER_STEP, D_MODEL, FF_CHUNK), lambda i, j: (j, 0, 0)),
                  pl.BlockSpec((FF_PER_STEP * FF_CHUNK, D_MODEL), lambda i, j: (j, 0))],
        out_specs=[pl.BlockSpec((tm, FF_PER_STEP * FF_CHUNK), lambda i, j: (i, j)), pl.BlockSpec((tm, D_MODEL), row),
                   pl.BlockSpec((tm, D_MODEL), row), pl.BlockSpec((tm, D_MODEL), row),
                   pl.BlockSpec((1, D_MODEL), fix)],
        out_shape=[jax.ShapeDtypeStruct((T, D_FF), BF16), jax.ShapeDtypeStruct((T, D_MODEL), F32),
                   jax.ShapeDtypeStruct((T, D_MODEL), BF16), jax.ShapeDtypeStruct((T, D_MODEL), BF16),
                   jax.ShapeDtypeStruct((1, D_MODEL), F32)],
        scratch_shapes=[pltpu.VMEM((tm, D_MODEL), F32)],
        compiler_params=_params(("arbitrary", "arbitrary"), 48),
    )(dh2, up16, h1, g2, w_up16, w_down16)


def _matmul_tn(name, a, b, bm, bn, tk, out_shape, out_block, out_index, a_prep=None, b_prep=None):
    T, M = a.shape
    N = b.shape[1]

    def body(a_ref, b_ref, o_ref):
        k = pl.program_id(2)
        av = a_ref[...] if a_prep is None else a_prep(a_ref[...])
        bv = b_ref[...] if b_prep is None else b_prep(b_ref[...])
        p = _dot_tn(av, bv)

        @pl.when(k == 0)
        def _():
            o_ref[...] = p

        @pl.when(k > 0)
        def _():
            o_ref[...] += p

    return pl.pallas_call(
        body, name=name, grid=(M // bm, N // bn, T // tk),
        in_specs=[pl.BlockSpec((tk, bm), lambda m, n, k: (k, m)), pl.BlockSpec((tk, bn), lambda m, n, k: (k, n))],
        out_specs=pl.BlockSpec(out_block, out_index),
        out_shape=jax.ShapeDtypeStruct(out_shape, F32),
        compiler_params=_params(("arbitrary", "arbitrary", "arbitrary"), 48),
    )(a, b)


def _bwd_mix(dh1, w_out16, y_lru, y_sb, ga, gb, tm):
    T = dh1.shape[0]

    def body(d_ref, w_ref, yl_ref, ys_ref, ga_ref, gb_ref, dyl_ref, dys_ref, dga_ref, dgb_ref):
        @pl.when(pl.program_id(0) == 0)
        def _():
            dga_ref[...] = jnp.zeros_like(dga_ref)
            dgb_ref[...] = jnp.zeros_like(dgb_ref)

        d16 = d_ref[...].astype(BF16)
        for y_ref, g_ref, lo, dy_ref, dg_ref in ((yl_ref, ga_ref, 0, dyl_ref, dga_ref),
                                                 (ys_ref, gb_ref, LRU_W, dys_ref, dgb_ref)):
            gv = g_ref[...]
            dn = _dot_nt(d16, w_ref[lo:lo + LRU_W, :])
            _, xhat, rstd = _rms_fwd(y_ref[...], gv)
            dg_ref[...] += jnp.sum(dn * xhat, axis=0, keepdims=True)
            dy_ref[...] = _rms_bwd(dn, xhat, rstd, gv)

    row = lambda i: (i, 0)
    fix = lambda i: (0, 0)
    return pl.pallas_call(
        body, name="bwd_mix", grid=(T // tm,),
        in_specs=[pl.BlockSpec((tm, D_MODEL), row), pl.BlockSpec((D_MODEL, D_MODEL), fix),
                  pl.BlockSpec((tm, LRU_W), row), pl.BlockSpec((tm, SB_W), row),
                  pl.BlockSpec((1, LRU_W), fix), pl.BlockSpec((1, SB_W), fix)],
        out_specs=[pl.BlockSpec((tm, LRU_W), row), pl.BlockSpec((tm, SB_W), row),
                   pl.BlockSpec((1, LRU_W), fix), pl.BlockSpec((1, SB_W), fix)],
        out_shape=[jax.ShapeDtypeStruct((T, LRU_W), F32), jax.ShapeDtypeStruct((T, SB_W), F32),
                   jax.ShapeDtypeStruct((1, LRU_W), F32), jax.ShapeDtypeStruct((1, SB_W), F32)],
        compiler_params=_params(("arbitrary",), 48),
    )(dh1, w_out16, y_lru, y_sb, ga, gb)


def _attn_bwd(qkv, run_tab, dy_sb, tri_suffix, tri_prefix, ride, ride_sliced, B, S):
    T = B * S
    nq = S // Q_BLK
    nkb = S // K_BLK
    nhp = SB_W // LANES
    scale = HEAD_D ** -0.5

    nr = len(ride)
    ride_shape, ride_specs, ride_sems = _exchange_shapes(ride, ride_sliced)

    def body(q_ref, k_ref, v_ref, run_ref, do_ref, ts_ref, tp_ref, *rest):
        dq_ref, dkt_ref, dvt_ref = rest[nr:nr + 3]
        start_ride, wait_ride = _exchange_copies(rest[:nr], rest[nr + 3:2 * nr + 3], ride_sliced, *rest[2 * nr + 3:])
        qi = pl.program_id(2)
        step_no = (pl.program_id(0) * nhp + pl.program_id(1)) * nq + qi
        pl.when(step_no == 0)(start_ride)
        jd = qi // Q_PER_K

        @pl.when(qi == 0)
        def _():
            dkt_ref[...] = jnp.zeros_like(dkt_ref)
            dvt_ref[...] = jnp.zeros_like(dvt_ref)

        causal, col, lane, halves = _attn_consts(qi)
        q = q_ref[...]
        do = do_ref[...]
        qh = [jnp.where(hm, q, jnp.zeros_like(q)) * jnp.asarray(scale, BF16) for hm in halves]
        doh = [jnp.where(hm, do, 0.0).astype(BF16) for hm in halves]
        qt = jnp.concatenate([h.astype(F32).T.astype(BF16) for h in qh], axis=1)
        dot_ = jnp.concatenate([h.astype(F32).T.astype(BF16) for h in doh], axis=1)
        tables = [run_ref[:, 0:LANES], run_ref[:, LANES:2 * LANES]]

        def group(blocks, carry):
            prefixes, dq = carry
            prefixes = list(prefixes)
            ks = [k_ref[_key_rows(jl), :] for jl, _, _ in blocks]
            vs = [v_ref[_key_rows(jl), :] for jl, _, _ in blocks]
            chains = [(b, h) for b in range(len(blocks)) for h in range(2)]
            z = {c: _dot_nt(qh[c[1]], ks[c[0]]) for c in chains}
            da = {c: _dot_nt(doh[c[1]], vs[c[0]]) for c in chains}
            lg = {c: _log1m_beta(z[c], blocks[c[0]][2]) for c in chains}
            suf = {c: _cumsum_mm(lg[c], ts_ref[...]) for c in chains}
            att, g = {}, {}
            for b, h in chains:
                _, jlane, mask = blocks[b]
                run = jnp.sum(jnp.where(lane == jlane, tables[h], 0.0), axis=1, keepdims=True)
                a = jnp.exp(z[b, h] + suf[b, h] + run)
                a = a if mask is None else jnp.where(mask, a, 0.0)
                g[b, h] = a * da[b, h]
                att[b, h] = a.astype(BF16)
            gpre = {c: _cumsum_mm(g[c], tp_ref[...]) for c in chains}
            dz = {}
            for b, h in chains:
                mask = blocks[b][2]
                d = g[b, h] - jnp.exp(z[b, h] + lg[b, h]) * (prefixes[h] + gpre[b, h])
                dz[b, h] = (d if mask is None else jnp.where(mask, d, 0.0)).astype(BF16)
                prefixes[h] = prefixes[h] + gpre[b, h][:, K_BLK - 1:K_BLK]
            for b, h in chains:
                dq = dq + _dot(dz[b, h], jnp.where(halves[h], ks[b], jnp.zeros_like(ks[b])))
            for b, (jl, _, _) in enumerate(blocks):
                dkt_ref[jl] += _dot(qt, jnp.concatenate([dz[b, 0], dz[b, 1]], axis=0))
                dvt_ref[jl] += _dot(dot_, jnp.concatenate([att[b, 0], att[b, 1]], axis=0))
            return tuple(prefixes), dq

        def step(it, cr):
            return group([(2 * it, 2 * it, None), (2 * it + 1, 2 * it + 1, None)], cr)

        col0 = jnp.zeros((Q_BLK, 1), F32)
        carry = lax.fori_loop(0, jd // 2, step, ((col0, col0), jnp.zeros((Q_BLK, LANES), F32)))
        two = jd % 2
        before = (jnp.maximum(jd - 1, 0), jnp.where(two == 1, jd - 1, -1), col < two * K_BLK)
        dq_ref[...] = group([before, (jd, jd, causal)], carry)[1] * scale
        pl.when(step_no == B * nhp * nq - 1)(wait_ride)

    qblk = pl.BlockSpec((Q_BLK, LANES), lambda b, hp, qi: (b * nq + qi, hp))
    tri = pl.BlockSpec((K_BLK, K_BLK), lambda b, hp, qi: (0, 0))
    acc = pl.BlockSpec((None, None, nkb, LANES, K_BLK), lambda b, hp, qi: (b, hp, 0, 0, 0))
    return pl.pallas_call(
        body, name="attn_bwd", grid=(B, nhp, nq),
        in_specs=[qblk, pl.BlockSpec((S, LANES), lambda b, hp, qi: (b, nhp + hp)),
                  pl.BlockSpec((S, LANES), lambda b, hp, qi: (b, 2 * nhp + hp)),
                  pl.BlockSpec((Q_BLK, 2 * LANES), lambda b, hp, qi: (b * nq + qi, hp)), qblk, tri, tri] + ride_specs,
        out_specs=[qblk, acc, acc] + ride_specs,
        out_shape=[jax.ShapeDtypeStruct((T, SB_W), F32)] + [jax.ShapeDtypeStruct((B, nhp, nkb, LANES, K_BLK), F32)] * 2
        + ride_shape,
        scratch_shapes=ride_sems,
        compiler_params=_params(("arbitrary", "arbitrary", "arbitrary"), 48),
    )(qkv, qkv, qkv, run_tab, dy_sb, tri_suffix, tri_prefix, *ride)


def _untranspose(t, B, S):
    return t.transpose(0, 2, 4, 1, 3).reshape(B * S, SB_W)


def _lru_bwd(proj_lru, h, dy_lru, conv_w, conv_b, wa_bd, wx_bd, b_a, b_x, lam, B, S, lc):
    T = B * S
    nc = S // lc
    ncb = LRU_W // LANES
    hpc = lc // SUBLANES

    def body(x_ref, xh_ref, g_ref, h_ref, hh_ref, dy_ref, cw_ref, cb_ref, wa_ref, wx_ref, ba_ref, bx_ref, lam_ref,
             dx_ref, dg_ref, dcw_ref, dcb_ref, dwa_ref, dwx_ref, dba_ref, dbx_ref, dlam_ref,
             lam_s, dc_s, a_first, lam_first, dc_head):
        b, ci = pl.program_id(1), pl.program_id(2)
        first_chunk = ci == nc - 1

        @pl.when(ci == 0)
        def _():
            a_first[...] = jnp.zeros_like(a_first)
            lam_first[...] = jnp.zeros_like(lam_first)
            dc_head[...] = jnp.zeros_like(dc_head)

        @pl.when((b == 0) & (ci == 0))
        def _():
            for ref in (dcw_ref, dcb_ref, dwa_ref, dwx_ref, dba_ref, dbx_ref, dlam_ref):
                ref[...] = jnp.zeros_like(ref)

        x = x_ref[...]
        taps = _conv_taps(x, jnp.where(first_chunk, 0.0, xh_ref[...]), lc)
        c = cb_ref[...] + sum(cw_ref[pl.ds(CONV_K - 1 - k, 1), :] * taps[k] for k in range(CONV_K))
        r, i, sp, dsp_dlam, a, s = _lru_gates(c, wa_ref, wx_ref, ba_ref, bx_ref, lam_ref)
        hv = h_ref[...]
        he = jnp.concatenate([jnp.where(first_chunk, 0.0, hh_ref[...]), hv], axis=0)
        h_prev = pltpu.roll(he, 1, 0)[SUBLANES:SUBLANES + lc]
        dy = dy_ref[...]
        gelu, dgelu = _gelu_parts(g_ref[...])
        dg_ref[...] = dy * hv * dgelu

        row = lax.broadcasted_iota(jnp.int32, (lc, LANES), 0)
        a_next = jnp.where(row < lc - 1, pltpu.roll(a, lc - 1, 0), a_first[...])
        lam_loc, a_run = _scan(a_next, dy * gelu, reverse=True)
        lam_s[...] = lam_loc + a_run * lam_first[...]
        lam_t = lam_s[...]
        lam_first[...] = lam_s[pl.ds(0, 1), :]
        lam_s[...] = a
        a_first[...] = lam_s[pl.ds(0, 1), :]

        ic = i * c
        dlog_a = lam_t * h_prev * a - (lam_t * ic) * (a * a) / s
        dpre_r = (dlog_a * ((-LRU_C) * sp)) * r * (1.0 - r)
        dpre_i = (lam_t * s * c) * i * (1.0 - i)
        dlam_ref[...] += jnp.sum(dlog_a * r, axis=0, keepdims=True) * ((-LRU_C) * dsp_dlam)
        dr16 = dpre_r.astype(BF16)
        di16 = dpre_i.astype(BF16)
        c16 = c.astype(BF16)
        dwa_ref[0] += _dot_tn(c16, dr16)
        dwx_ref[0] += _dot_tn(c16, di16)
        dba_ref[...] += jnp.sum(dpre_r, axis=0, keepdims=True)
        dbx_ref[...] += jnp.sum(dpre_i, axis=0, keepdims=True)
        dc = lam_t * s * i + _dot_nt(dr16, wa_ref[0]) + _dot_nt(di16, wx_ref[0])
        dcb_ref[...] += jnp.sum(dc, axis=0, keepdims=True)
        for k in range(CONV_K):
            dcw_ref[pl.ds(CONV_K - 1 - k, 1), :] += jnp.sum(dc * taps[k], axis=0, keepdims=True)
        dce = jnp.concatenate([dc, dc_head[...]], axis=0)
        dx = cw_ref[pl.ds(CONV_K - 1, 1), :] * dc
        for k in range(1, CONV_K):
            dx = dx + cw_ref[pl.ds(CONV_K - 1 - k, 1), :] * pltpu.roll(dce, lc + SUBLANES - k, 0)[0:lc]
        dx_ref[...] = dx
        dc_s[...] = dc
        dc_head[...] = dc_s[pl.ds(0, SUBLANES), :]

    def chunk(col):
        return pl.BlockSpec((lc, LANES), lambda cb, b, ci: (b * nc + nc - 1 - ci, col(cb)))

    def halo(col):
        return pl.BlockSpec((SUBLANES, LANES),
                            lambda cb, b, ci: (jnp.maximum((b * nc + nc - 1 - ci) * hpc - 1, 0), col(cb)))

    chan = lambda cb, b, ci: (0, cb)
    blk = lambda cb, b, ci: (cb, 0, 0)
    vec = pl.BlockSpec((1, LANES), chan)
    mat = pl.BlockSpec((1, LANES, LANES), blk)
    return pl.pallas_call(
        body, name="lru_bwd", grid=(ncb, B, nc),
        in_specs=[chunk(lambda cb: cb), halo(lambda cb: cb), chunk(lambda cb: ncb + cb),
                  chunk(lambda cb: cb), halo(lambda cb: cb), chunk(lambda cb: cb),
                  pl.BlockSpec((CONV_K, LANES), chan), vec, mat, mat, vec, vec, vec],
        out_specs=[chunk(lambda cb: cb), chunk(lambda cb: cb), pl.BlockSpec((CONV_K, LANES), chan), vec,
                   mat, mat, vec, vec, vec],
        out_shape=[jax.ShapeDtypeStruct((T, LRU_W), F32), jax.ShapeDtypeStruct((T, LRU_W), F32),
                   jax.ShapeDtypeStruct((CONV_K, LRU_W), F32), jax.ShapeDtypeStruct((1, LRU_W), F32),
                   jax.ShapeDtypeStruct((ncb, LANES, LANES), F32), jax.ShapeDtypeStruct((ncb, LANES, LANES), F32),
                   jax.ShapeDtypeStruct((1, LRU_W), F32), jax.ShapeDtypeStruct((1, LRU_W), F32),
                   jax.ShapeDtypeStruct((1, LRU_W), F32)],
        scratch_shapes=[pltpu.VMEM((lc, LANES), F32), pltpu.VMEM((lc, LANES), F32), pltpu.VMEM((1, LANES), F32),
                        pltpu.VMEM((1, LANES), F32), pltpu.VMEM((SUBLANES, LANES), F32)],
        compiler_params=_params(("arbitrary", "arbitrary", "arbitrary"), 32),
    )(proj_lru, proj_lru, proj_lru, h, h, dy_lru, conv_w, conv_b, wa_bd, wx_bd, b_a, b_x, lam)


def _bwd_in(pieces, w_in16, x2, g1, dh1, tm):
    T = x2.shape[0]
    npc = len(pieces)

    def body(*refs):
        p_refs = refs[:npc]
        w_ref, x_ref, g_ref, d_ref, dx_ref, dproj_ref, xn_ref, dg1_ref = refs[npc:]

        @pl.when(pl.program_id(0) == 0)
        def _():
            dg1_ref[...] = jnp.zeros_like(dg1_ref)

        dxn = jnp.zeros((tm, D_MODEL), F32)
        for n, p_ref in enumerate(p_refs):
            cols = slice(n * LRU_W, (n + 1) * LRU_W)
            p16 = p_ref[...].astype(BF16)
            dproj_ref[:, cols] = p16
            dxn = dxn + _dot_nt(p16, w_ref[:, cols])
        gv = g_ref[...]
        xn, xhat, rstd = _rms_fwd(x_ref[...], gv)
        xn_ref[...] = xn.astype(BF16)
        dg1_ref[...] += jnp.sum(dxn * xhat, axis=0, keepdims=True)
        dx_ref[...] = d_ref[...] + _rms_bwd(dxn, xhat, rstd, gv)

    row = lambda i: (i, 0)
    fix = lambda i: (0, 0)
    return pl.pallas_call(
        body, name="bwd_in", grid=(T // tm,),
        in_specs=[pl.BlockSpec((tm, LRU_W), row)] * npc + [
            pl.BlockSpec((D_MODEL, IN_COLS), fix), pl.BlockSpec((tm, D_MODEL), row),
            pl.BlockSpec((1, D_MODEL), fix), pl.BlockSpec((tm, D_MODEL), row)],
        out_specs=[pl.BlockSpec((tm, D_MODEL), row), pl.BlockSpec((tm, IN_COLS), row),
                   pl.BlockSpec((tm, D_MODEL), row), pl.BlockSpec((1, D_MODEL), fix)],
        out_shape=[jax.ShapeDtypeStruct((T, D_MODEL), F32), jax.ShapeDtypeStruct((T, IN_COLS), BF16),
                   jax.ShapeDtypeStruct((T, D_MODEL), BF16), jax.ShapeDtypeStruct((1, D_MODEL), F32)],
        compiler_params=_params(("arbitrary",), 56),
    )(*pieces, w_in16, x2, g1, dh1)


def _adam_shard(name, parts, w, m, v, tr):
    R, C = w.shape

    def body(p_ref, w_ref, m_ref, v_ref, g_ref, d_ref, m2_ref, v2_ref):
        g = p_ref[0]
        for p in range(1, N_DEV):
            g = g + p_ref[p]
        g_ref[...] = g
        d_ref[...], m2_ref[...], v2_ref[...] = _adamw(w_ref[...], g, m_ref[...], v_ref[...])

    blk = pl.BlockSpec((tr, C), lambda i: (i, 0))
    return pl.pallas_call(
        body, name=name, grid=(R // tr,),
        in_specs=[pl.BlockSpec((N_DEV, tr, C), lambda i: (0, i, 0)), blk, blk, blk],
        out_specs=[blk] * 4, out_shape=[jax.ShapeDtypeStruct((R, C), F32)] * 4,
        compiler_params=_params(("arbitrary",), 48),
    )(parts, w, m, v)


def _sum_parts(name, parts):
    def body(p_ref, g_ref):
        g = p_ref[0]
        for p in range(1, N_DEV):
            g = g + p_ref[p]
        g_ref[...] = g

    return pl.pallas_call(body, name=name, out_shape=jax.ShapeDtypeStruct(parts.shape[1:], F32))(parts)


def _adam_packed(w, g, m, v):
    def body(w_ref, g_ref, m_ref, v_ref, d_ref, m2_ref, v2_ref):
        d_ref[...], m2_ref[...], v2_ref[...] = _adamw(w_ref[...], g_ref[...], m_ref[...], v_ref[...])

    return pl.pallas_call(body, name="adam_small", out_shape=[jax.ShapeDtypeStruct(w.shape, F32)] * 3)(w, g, m, v)


def _pack(vals, rows):
    flat = jnp.concatenate([v.reshape(-1).astype(F32) for v in vals])
    return jnp.pad(flat, (0, rows * LANES - flat.shape[0])).reshape(rows, LANES)


def _unpack(packed, sizes):
    flat = packed.reshape(-1)
    out, off = [], 0
    for n in sizes:
        out.append(flat[off:off + n])
        off += n
    return out


def _block_diag_pairs(w):
    w = w.reshape(LRU_BLOCKS // 2, 2, HEAD_D, HEAD_D)
    out = jnp.zeros((LRU_BLOCKS // 2, LANES, LANES), w.dtype)
    out = out.at[:, :HEAD_D, :HEAD_D].set(w[:, 0])
    return out.at[:, HEAD_D:, HEAD_D:].set(w[:, 1])


def _diag_blocks(w):
    return jnp.stack([w[:, :HEAD_D, :HEAD_D], w[:, HEAD_D:, HEAD_D:]], axis=1).reshape(LRU_BLOCKS, HEAD_D, HEAD_D)


def kernel(x, norm1_g, w_in, conv_w, conv_b, lru_w_a, lru_b_a, lru_w_x, lru_b_x, lru_lambda, lru_out_g, sb_out_g, w_out, norm2_g, w_up, w_down, final_g, loss_target, m_norm1_g, m_w_in, m_conv_w, m_conv_b, m_lru_w_a, m_lru_b_a, m_lru_w_x, m_lru_b_x, m_lru_lambda, m_lru_out_g, m_sb_out_g, m_w_out, m_norm2_g, m_w_up, m_w_down, m_final_g, v_norm1_g, v_w_in, v_conv_w, v_conv_b, v_lru_w_a, v_lru_b_a, v_lru_w_x, v_lru_b_x, v_lru_lambda, v_lru_out_g, v_sb_out_g, v_w_out, v_norm2_g, v_w_up, v_w_down, v_final_g):
    B, S, _ = x.shape
    T = B * S
    tm = min(512, T)
    tk = min(GRAD_TK, T)
    lc = min(512, S)
    me = _my_index()
    x2 = x.reshape(T, D_MODEL)
    tgt = loss_target.reshape(T, D_MODEL)
    cw_cols = CONV_K * LRU_W // N_DEV // CONV_K

    shards16 = _cast_shards([w_in[0], w_out[0], w_up[0], w_down[0]])
    cw_pad = jnp.zeros((SUBLANES, LANES), F32).at[:CONV_K, :cw_cols].set(conv_w[0])
    g_in, g_cw = _exchange("gather_w_in", [shards16[0], cw_pad], [False, False])
    w_in16 = g_in.transpose(1, 0, 2).reshape(D_MODEL, IN_COLS)
    conv_w_full = g_cw[:, :CONV_K, :cw_cols].transpose(1, 0, 2).reshape(CONV_K, LRU_W)
    wa_bd = _block_diag_pairs(lru_w_a[0]).astype(BF16)
    wx_bd = _block_diag_pairs(lru_w_x[0]).astype(BF16)
    b_a = lru_b_a.reshape(1, LRU_W)
    b_x = lru_b_x.reshape(1, LRU_W)
    gf = final_g.reshape(1, D_MODEL)

    proj_lru, qkv = _fwd_in(x2, norm1_g, w_in16, tm)
    y_lru, h = _lru_fwd(proj_lru, conv_w_full, conv_b, wa_bd, wx_bd, b_a, b_x, lru_lambda, B, S, lc)
    tri_suffix, tri_prefix = _tri(False), _tri(True)
    y_sb, run_tab, g_out, g_up, g_down = _attn_fwd(qkv, tri_suffix, list(shards16[1:]), [False] * 3, B, S)
    w_out16 = g_out.reshape(D_MODEL, D_MODEL)
    w_down16 = g_down.reshape(D_FF, D_MODEL)
    h1, mix16 = _fwd_mix(y_lru, y_sb, lru_out_g, sb_out_g, w_out16, x2, tm)
    up16, dh2, d_final_g, loss_part = _fwd_mlp(h1, norm2_g, g_up, w_down16, gf, tgt, tm)

    dup16, dh1, hn16, dh2b, d_norm2_g = _bwd_mlp(dh2, up16, h1, norm2_g, g_up, w_down16, tm)
    sq = lambda u: (u.astype(F32) * u.astype(F32)).astype(BF16)
    gw_up = _matmul_tn("grad_w_up", hn16, dup16, D_MODEL, FF_CHUNK, tk, (N_DEV, D_MODEL, FF_CHUNK),
                       (None, D_MODEL, FF_CHUNK), lambda m, n, k: (n, 0, 0))
    gw_down = _matmul_tn("grad_w_down", up16, dh2b, FF_CHUNK, D_MODEL, tk, (N_DEV, FF_CHUNK, D_MODEL),
                         (None, FF_CHUNK, D_MODEL), lambda m, n, k: (m, 0, 0), a_prep=sq)
    dy_lru, dy_sb, d_lru_out_g, d_sb_out_g = _bwd_mix(dh1, w_out16, y_lru, y_sb, lru_out_g, sb_out_g, tm)
    gw_out = _matmul_tn("grad_w_out", mix16, dh1, D_MODEL, FF_CHUNK, tk, (D_MODEL, D_MODEL),
                        (D_MODEL, FF_CHUNK), lambda m, n, k: (0, n), b_prep=lambda u: u.astype(BF16))
    parts_out = gw_out.reshape(N_DEV, D_MODEL // N_DEV, D_MODEL)
    dq, dkt, dvt, r_out, r_up, r_down = _attn_bwd(qkv, run_tab, dy_sb, tri_suffix, tri_prefix,
                                                  [parts_out, gw_up, gw_down], [True] * 3, B, S)
    dk, dv = _untranspose(dkt, B, S), _untranspose(dvt, B, S)
    (dx_lru, dg_lru, d_conv_w, d_conv_b, d_wa, d_wx, d_b_a, d_b_x, d_lambda) = _lru_bwd(
        proj_lru, h, dy_lru, conv_w_full, conv_b, wa_bd, wx_bd, b_a, b_x, lru_lambda, B, S, lc)
    dx, dproj16, xn16, d_norm1_g = _bwd_in([dx_lru, dg_lru, dq, dk, dv], w_in16, x2, norm1_g, dh1, tm)
    gw_in = _matmul_tn("grad_w_in", xn16, dproj16, D_MODEL, FF_CHUNK, tk, (D_MODEL, IN_COLS),
                       (D_MODEL, FF_CHUNK), lambda m, n, k: (0, n))

    small_grads = {"norm1_g": d_norm1_g, "conv_b": d_conv_b, "lru_w_a": _diag_blocks(d_wa), "lru_b_a": d_b_a,
                   "lru_w_x": _diag_blocks(d_wx), "lru_b_x": d_b_x, "lru_lambda": d_lambda,
                   "lru_out_g": d_lru_out_g, "sb_out_g": d_sb_out_g, "norm2_g": d_norm2_g, "final_g": d_final_g}
    packed = _pack([small_grads[n] for n, _ in SMALL] + [d_conv_w, loss_part], EXCH_ROWS)
    parts_in = gw_in.reshape(D_MODEL, N_DEV, IN_COLS // N_DEV).transpose(1, 0, 2)
    r_in, r_small = _exchange("exchange_grads", [parts_in, packed], [True, False])

    g_w_in, d_w_in, nm_w_in, nv_w_in = _adam_shard("adam_w_in", r_in, w_in[0], m_w_in[0], v_w_in[0], 256)
    g_w_out, d_w_out, nm_w_out, nv_w_out = _adam_shard("adam_w_out", r_out, w_out[0], m_w_out[0], v_w_out[0], 64)
    g_w_up, d_w_up, nm_w_up, nv_w_up = _adam_shard("adam_w_up", r_up, w_up[0], m_w_up[0], v_w_up[0], 256)
    g_w_down, d_w_down, nm_w_down, nv_w_down = _adam_shard("adam_w_down", r_down, w_down[0], m_w_down[0], v_w_down[0], 128)

    total = _sum_parts("sum_small", r_small)
    sizes = [n for _, n in SMALL]
    small_g = _unpack(total, sizes + [CONV_K * LRU_W, 1])
    loss = small_g[-1][0]
    g_conv_w = lax.dynamic_slice_in_dim(small_g[-2].reshape(CONV_K, LRU_W), me * cw_cols, cw_cols, axis=1)
    given = dict(norm1_g=(norm1_g, m_norm1_g, v_norm1_g), conv_b=(conv_b, m_conv_b, v_conv_b),
                 lru_w_a=(lru_w_a, m_lru_w_a, v_lru_w_a), lru_b_a=(lru_b_a, m_lru_b_a, v_lru_b_a),
                 lru_w_x=(lru_w_x, m_lru_w_x, v_lru_w_x), lru_b_x=(lru_b_x, m_lru_b_x, v_lru_b_x),
                 lru_lambda=(lru_lambda, m_lru_lambda, v_lru_lambda), lru_out_g=(lru_out_g, m_lru_out_g, v_lru_out_g),
                 sb_out_g=(sb_out_g, m_sb_out_g, v_sb_out_g), norm2_g=(norm2_g, m_norm2_g, v_norm2_g),
                 final_g=(final_g, m_final_g, v_final_g))
    names = [n for n, _ in SMALL]
    pw = _pack([given[n][0] for n in names] + [conv_w], ADAM_ROWS)
    pm = _pack([given[n][1] for n in names] + [m_conv_w], ADAM_ROWS)
    pv = _pack([given[n][2] for n in names] + [v_conv_w], ADAM_ROWS)
    pg = _pack(small_g[:len(names)] + [g_conv_w], ADAM_ROWS)
    pd, pm2, pv2 = _adam_packed(pw, pg, pm, pv)
    asz = sizes + [conv_w.size]
    shapes = {n: given[n][0].shape for n in names}
    shapes["conv_w"] = conv_w.shape
    order = names + ["conv_w"]
    g_small = dict(zip(order, [a.reshape(shapes[n]) for n, a in zip(order, _unpack(pg, asz))]))
    d_small = dict(zip(order, [a.reshape(shapes[n]) for n, a in zip(order, _unpack(pd, asz))]))
    m_small = dict(zip(order, [a.reshape(shapes[n]) for n, a in zip(order, _unpack(pm2, asz))]))
    v_small = dict(zip(order, [a.reshape(shapes[n]) for n, a in zip(order, _unpack(pv2, asz))]))

    big = {"w_in": (g_w_in, d_w_in, nm_w_in, nv_w_in), "w_out": (g_w_out, d_w_out, nm_w_out, nv_w_out),
           "w_up": (g_w_up, d_w_up, nm_w_up, nv_w_up), "w_down": (g_w_down, d_w_down, nm_w_down, nv_w_down)}
    weights = ["norm1_g", "w_in", "conv_w", "conv_b", "lru_w_a", "lru_b_a", "lru_w_x", "lru_b_x", "lru_lambda",
               "lru_out_g", "sb_out_g", "w_out", "norm2_g", "w_up", "w_down", "final_g"]

    def leaf(n, kind):
        if n in big:
            return big[n][kind][None]
        return (g_small, d_small, m_small, v_small)[kind][n]

    return (loss, dx.reshape(B, S, D_MODEL), *[leaf(n, 0) for n in weights], *[leaf(n, 1) for n in weights],
            *[leaf(n, 2) for n in weights], *[leaf(n, 3) for n in weights])
```

```python
import jax
import jax.numpy as jnp
from jax import lax
from jax.experimental import pallas as pl
from jax.experimental.pallas import tpu as pltpu

F32 = jnp.float32
BF16 = jnp.bfloat16

D_MODEL = 1024
LRU_W = 512
SB_W = 512
HEAD_D = 64
D_FF = 4096
IN_COLS = 2 * LRU_W + 3 * SB_W
CONV_K = 4
LRU_BLOCKS = 8
LRU_C = 8.0
EPS = 1e-6
N_DEV = 8
LANES = 128
SUBLANES = 8
FF_CHUNK = 512
FF_PER_STEP = 4
GRAD_TK = 2048
Q_BLK = 256
K_BLK = 256
Q_PER_K = K_BLK // Q_BLK

ADAM_LR = 0.001
ADAM_B1 = 0.9
ADAM_B2 = 0.999
ADAM_EPS = 1e-08
ADAM_WD = 0.01
ADAM_STEP = 10

SMALL = (("norm1_g", 1024), ("conv_b", 512), ("lru_w_a", 32768), ("lru_b_a", 512), ("lru_w_x", 32768),
         ("lru_b_x", 512), ("lru_lambda", 512), ("lru_out_g", 512), ("sb_out_g", 512), ("norm2_g", 1024),
         ("final_g", 1024))
SMALL_ROWS = sum(n for _, n in SMALL) // LANES
EXCH_ROWS = SMALL_ROWS + (CONV_K * LRU_W) // LANES + 8
ADAM_ROWS = SMALL_ROWS + 8


def _params(sem=None, vmem_mb=None):
    kw = {}
    if sem is not None:
        kw["dimension_semantics"] = sem
    if vmem_mb is not None:
        kw["vmem_limit_bytes"] = vmem_mb << 20
    return pltpu.CompilerParams(**kw)


def _dot(a, b):
    return jnp.dot(a, b, preferred_element_type=F32)


def _dot_nt(a, b):
    return lax.dot_general(a, b, (((1,), (1,)), ((), ())), preferred_element_type=F32)


def _dot_tn(a, b):
    return lax.dot_general(a, b, (((0,), (0,)), ((), ())), preferred_element_type=F32)


def _rms_fwd(x, g):
    rstd = lax.rsqrt(jnp.mean(x * x, axis=-1, keepdims=True) + EPS)
    xhat = x * rstd
    return xhat * g, xhat, rstd


def _rms_bwd(dy, xhat, rstd, g):
    dxhat = dy * g
    return rstd * (dxhat - xhat * jnp.mean(dxhat * xhat, axis=-1, keepdims=True))


def _sigmoid(x):
    return 1.0 / (1.0 + jnp.exp(-x))


def _log1p_pos(e):
    series = e * (1.0 - e * (0.5 - e * (1.0 / 3.0 - e * 0.25)))
    return jnp.where(e < 1e-2, series, jnp.log(1.0 + e))


def _neg_expm1(x):
    series = -x * (1.0 + x * (0.5 + x * (1.0 / 6.0 + x * (1.0 / 24.0))))
    return jnp.where(x > -1e-2, series, 1.0 - jnp.exp(x))


def _gelu_parts(g):
    k0 = 0.7978845608028654
    k1 = 0.044715
    t = jnp.tanh(k0 * (g + k1 * g * g * g))
    val = 0.5 * g * (1.0 + t)
    grad = 0.5 * (1.0 + t) + 0.5 * g * (1.0 - t * t) * k0 * (1.0 + 3.0 * k1 * g * g)
    return val, grad


def _scan(a, b, reverse):
    n = a.shape[0]
    row = lax.broadcasted_iota(jnp.int32, a.shape, 0)
    s = 1
    while s < n:
        if reverse:
            keep = row < n - s
            shift = n - s
        else:
            keep = row >= s
            shift = s
        bs = jnp.where(keep, pltpu.roll(b, shift, 0), 0.0)
        a_s = jnp.where(keep, pltpu.roll(a, shift, 0), 1.0)
        b = a * bs + b
        a = a * a_s
        s *= 2
    return b, a


def _adamw(w, g, m, v):
    m = ADAM_B1 * m + (1.0 - ADAM_B1) * g
    v = ADAM_B2 * v + (1.0 - ADAM_B2) * (g * g)
    m_hat = m / (1.0 - ADAM_B1 ** ADAM_STEP)
    v_hat = v / (1.0 - ADAM_B2 ** ADAM_STEP)
    delta = -ADAM_LR * (m_hat / (jnp.sqrt(v_hat) + ADAM_EPS) + ADAM_WD * w)
    return delta, m, v


def _my_index():
    return 4 * lax.axis_index("x") + 2 * lax.axis_index("y") + lax.axis_index("c")


def _peer(k):
    x, y, c = lax.axis_index("x"), lax.axis_index("y"), lax.axis_index("c")
    px = 1 - x if (k >> 2) & 1 else x
    py = 1 - y if (k >> 1) & 1 else y
    pc = 1 - c if k & 1 else c
    return (px, py, pc), 4 * px + 2 * py + pc


def _exchange_shapes(srcs, sliced):
    n = len(srcs)
    out_shape = [jax.ShapeDtypeStruct(s.shape if sl else (N_DEV,) + s.shape, s.dtype) for s, sl in zip(srcs, sliced)]
    specs = [pl.BlockSpec(memory_space=pl.ANY)] * n
    sems = [pltpu.SemaphoreType.DMA((n, N_DEV - 1)), pltpu.SemaphoreType.DMA((n, N_DEV - 1)),
            pltpu.SemaphoreType.DMA((n,))]
    return out_shape, specs, sems


def _exchange_copies(ins, outs, sliced, send_sems, recv_sems, local_sems):
    n = len(ins)

    def part(a, p):
        return ins[a].at[p] if sliced[a] else ins[a]

    def copies(receiving):
        me = _my_index()
        local = [pltpu.make_async_copy(part(a, me), outs[a].at[me], local_sems.at[a]) for a in range(n)]
        remote = []
        for k in range(1, N_DEV):
            dev, idx = _peer(k)
            for a in range(n):
                remote.append(pltpu.make_async_remote_copy(
                    src_ref=part(a, idx), dst_ref=outs[a].at[idx if receiving else me],
                    send_sem=send_sems.at[a, k - 1], recv_sem=recv_sems.at[a, k - 1],
                    device_id=dev, device_id_type=pl.DeviceIdType.MESH))
        return local, remote

    def start():
        local, remote = copies(receiving=False)
        for cp in local + remote:
            cp.start()

    def wait():
        local, remote = copies(receiving=True)
        for cp in remote + local:
            cp.wait()

    return start, wait


def _exchange(name, srcs, sliced):
    n = len(srcs)
    out_shape, specs, sems = _exchange_shapes(srcs, sliced)

    def body(*refs):
        start, wait = _exchange_copies(refs[:n], refs[n:2 * n], sliced, *refs[2 * n:])
        start()
        wait()

    return pl.pallas_call(body, name=name, out_shape=out_shape, in_specs=specs, out_specs=specs,
                          scratch_shapes=sems)(*srcs)


def _cast_shards(ws):
    def body(*refs):
        for i in range(len(ws)):
            refs[len(ws) + i][...] = refs[i][...].astype(BF16)

    return pl.pallas_call(
        body, name="cast_shards", out_shape=[jax.ShapeDtypeStruct(w.shape, BF16) for w in ws],
        compiler_params=_params(vmem_mb=32),
    )(*ws)


def _fwd_in(x2, g1, w_in16, tm):
    T = x2.shape[0]

    def body(x_ref, g_ref, w_ref, lru_ref, qkv_ref):
        xn, _, _ = _rms_fwd(x_ref[...], g_ref[...])
        xn = xn.astype(BF16)
        lru_ref[...] = _dot(xn, w_ref[:, 0:2 * LRU_W])
        qkv_ref[...] = _dot(xn, w_ref[:, 2 * LRU_W:IN_COLS]).astype(BF16)

    return pl.pallas_call(
        body, name="fwd_in", grid=(T // tm,),
        in_specs=[pl.BlockSpec((tm, D_MODEL), lambda i: (i, 0)),
                  pl.BlockSpec((1, D_MODEL), lambda i: (0, 0)),
                  pl.BlockSpec((D_MODEL, IN_COLS), lambda i: (0, 0))],
        out_specs=[pl.BlockSpec((tm, 2 * LRU_W), lambda i: (i, 0)),
                   pl.BlockSpec((tm, 3 * SB_W), lambda i: (i, 0))],
        out_shape=[jax.ShapeDtypeStruct((T, 2 * LRU_W), F32), jax.ShapeDtypeStruct((T, 3 * SB_W), BF16)],
        compiler_params=_params(("arbitrary",), 48),
    )(x2, g1, w_in16)


def _lru_gates(c, wa_ref, wx_ref, ba_ref, bx_ref, lam_ref):
    c16 = c.astype(BF16)
    r = _sigmoid(_dot(c16, wa_ref[0]) + ba_ref[...])
    i = _sigmoid(_dot(c16, wx_ref[0]) + bx_ref[...])
    lam = lam_ref[...]
    e = jnp.exp(-jnp.abs(lam))
    sp = jnp.maximum(-lam, 0.0) + _log1p_pos(e)
    dsp_dlam = -jnp.where(lam >= 0.0, e, 1.0) / (1.0 + e)
    log_a = (-LRU_C) * r * sp
    a = jnp.exp(log_a)
    s = jnp.sqrt(_neg_expm1(2.0 * log_a))
    return r, i, sp, dsp_dlam, a, s


def _conv_taps(x, halo, lc):
    xe = jnp.concatenate([halo, x], axis=0)
    return [x] + [pltpu.roll(xe, k, 0)[SUBLANES:SUBLANES + lc] for k in range(1, CONV_K)]


def _lru_fwd(proj_lru, conv_w, conv_b, wa_bd, wx_bd, b_a, b_x, lam, B, S, lc):
    T = B * S
    nc = S // lc
    ncb = LRU_W // LANES

    def body(x_ref, g_ref, cw_ref, cb_ref, wa_ref, wx_ref, ba_ref, bx_ref, lam_ref, y_ref, h_ref, tail, carry):
        ci = pl.program_id(2)

        @pl.when(ci == 0)
        def _():
            tail[...] = jnp.zeros_like(tail)
            carry[...] = jnp.zeros_like(carry)

        x = x_ref[...]
        taps = _conv_taps(x, tail[...], lc)
        c = cb_ref[...] + sum(cw_ref[pl.ds(CONV_K - 1 - k, 1), :] * taps[k] for k in range(CONV_K))
        tail[...] = x_ref[pl.ds(lc - SUBLANES, SUBLANES), :]
        r, i, sp, _, a, s = _lru_gates(c, wa_ref, wx_ref, ba_ref, bx_ref, lam_ref)
        h_loc, a_run = _scan(a, s * (i * c), reverse=False)
        h_ref[...] = h_loc + a_run * carry[...]
        carry[...] = h_ref[pl.ds(lc - 1, 1), :]
        gelu, _ = _gelu_parts(g_ref[...])
        y_ref[...] = h_ref[...] * gelu

    chan = lambda b, cb, ci: (0, cb)
    return pl.pallas_call(
        body, name="lru_fwd", grid=(B, ncb, nc),
        in_specs=[pl.BlockSpec((lc, LANES), lambda b, cb, ci: (b * nc + ci, cb)),
                  pl.BlockSpec((lc, LANES), lambda b, cb, ci: (b * nc + ci, ncb + cb)),
                  pl.BlockSpec((CONV_K, LANES), chan), pl.BlockSpec((1, LANES), chan),
                  pl.BlockSpec((1, LANES, LANES), lambda b, cb, ci: (cb, 0, 0)),
                  pl.BlockSpec((1, LANES, LANES), lambda b, cb, ci: (cb, 0, 0)),
                  pl.BlockSpec((1, LANES), chan), pl.BlockSpec((1, LANES), chan), pl.BlockSpec((1, LANES), chan)],
        out_specs=[pl.BlockSpec((lc, LANES), lambda b, cb, ci: (b * nc + ci, cb))] * 2,
        out_shape=[jax.ShapeDtypeStruct((T, LRU_W), F32)] * 2,
        scratch_shapes=[pltpu.VMEM((SUBLANES, LANES), F32), pltpu.VMEM((1, LANES), F32)],
        compiler_params=_params(("arbitrary", "arbitrary", "arbitrary"), 32),
    )(proj_lru, proj_lru, conv_w, conv_b, wa_bd, wx_bd, b_a, b_x, lam)


def _tri(prefix):
    r = lax.broadcasted_iota(jnp.int32, (K_BLK, K_BLK), 0)
    c = lax.broadcasted_iota(jnp.int32, (K_BLK, K_BLK), 1)
    return ((r <= c) if prefix else (r >= c)).astype(BF16)


def _attn_consts(qi):
    r = lax.broadcasted_iota(jnp.int32, (Q_BLK, K_BLK), 0)
    c = lax.broadcasted_iota(jnp.int32, (Q_BLK, K_BLK), 1)
    causal = c + ((qi // Q_PER_K) * K_BLK - qi * Q_BLK) < r
    lane = lax.broadcasted_iota(jnp.int32, (1, LANES), 1)
    return causal, c, lane, (lane < HEAD_D, lane >= HEAD_D)


def _log1m_beta(z, mask):
    lg = -(jnp.maximum(z, 0.0) + jnp.log(1.0 + jnp.exp(-jnp.abs(z))))
    return lg if mask is None else jnp.where(mask, lg, 0.0)


def _key_rows(j):
    return pl.ds(pl.multiple_of(j * K_BLK, K_BLK), K_BLK)


def _attn_fwd(qkv, tri_suffix, ride, ride_sliced, B, S):
    T = B * S
    nq = S // Q_BLK
    nhp = SB_W // LANES
    scale = HEAD_D ** -0.5
    assert S // K_BLK <= LANES
    nr = len(ride)
    ride_shape, ride_specs, ride_sems = _exchange_shapes(ride, ride_sliced)

    def body(q_ref, k_ref, v_ref, tri_ref, *rest):
        o_ref, run_ref = rest[nr:nr + 2]
        start_ride, wait_ride = _exchange_copies(rest[:nr], rest[nr + 2:2 * nr + 2], ride_sliced, *rest[2 * nr + 2:])
        qi = pl.program_id(2)
        step_no = (pl.program_id(0) * nhp + pl.program_id(1)) * nq + qi
        pl.when(step_no == 0)(start_ride)
        jd = qi // Q_PER_K
        causal, col, lane, halves = _attn_consts(qi)
        q = q_ref[...]
        qh = [jnp.where(hm, q, jnp.zeros_like(q)) * jnp.asarray(scale, BF16) for hm in halves]

        def group(blocks, carry):
            runs, tables, acc = carry
            runs, tables = list(runs), list(tables)
            ks = [k_ref[_key_rows(jl), :] for jl, _, _ in blocks]
            vs = [v_ref[_key_rows(jl), :] for jl, _, _ in blocks]
            chains = [(b, h) for b in range(len(blocks)) for h in range(2)]
            z = {c: _dot_nt(qh[c[1]], ks[c[0]]) for c in chains}
            lg = {c: _log1m_beta(z[c], blocks[c[0]][2]) for c in chains}
            suf = {c: _dot(lg[c].astype(BF16), tri_ref[...]) for c in chains}
            att = {}
            for b, h in chains:
                _, jlane, mask = blocks[b]
                a = jnp.exp(z[b, h] + suf[b, h] + runs[h])
                att[b, h] = (a if mask is None else jnp.where(mask, a, 0.0)).astype(BF16)
                tables[h] = jnp.where(lane == jlane, runs[h], tables[h])
                runs[h] = runs[h] + suf[b, h][:, 0:1]
            for b, h in chains:
                acc = acc + _dot(att[b, h], jnp.where(halves[h], vs[b], jnp.zeros_like(vs[b])))
            return tuple(runs), tuple(tables), acc

        col0 = jnp.zeros((Q_BLK, 1), F32)
        zero = jnp.zeros((Q_BLK, LANES), F32)
        two = jd % 2
        before = (jnp.maximum(jd - 1, 0), jnp.where(two == 1, jd - 1, -1), col < two * K_BLK)
        carry = group([(jd, jd, causal), before], ((col0, col0), (zero, zero), zero))
        top = jd - 1 - two

        def step(it, cr):
            ja = top - 2 * it
            return group([(ja, ja, None), (ja - 1, ja - 1, None)], cr)

        _, tables, acc = lax.fori_loop(0, (top + 1) // 2, step, carry)
        o_ref[...] = acc
        run_ref[:, 0:LANES] = tables[0]
        run_ref[:, LANES:2 * LANES] = tables[1]
        pl.when(step_no == B * nhp * nq - 1)(wait_ride)

    return pl.pallas_call(
        body, name="attn_fwd", grid=(B, nhp, nq),
        in_specs=[pl.BlockSpec((Q_BLK, LANES), lambda b, hp, qi: (b * nq + qi, hp)),
                  pl.BlockSpec((S, LANES), lambda b, hp, qi: (b, nhp + hp)),
                  pl.BlockSpec((S, LANES), lambda b, hp, qi: (b, 2 * nhp + hp)),
                  pl.BlockSpec((K_BLK, K_BLK), lambda b, hp, qi: (0, 0))] + ride_specs,
        out_specs=[pl.BlockSpec((Q_BLK, LANES), lambda b, hp, qi: (b * nq + qi, hp)),
                   pl.BlockSpec((Q_BLK, 2 * LANES), lambda b, hp, qi: (b * nq + qi, hp))] + ride_specs,
        out_shape=[jax.ShapeDtypeStruct((T, SB_W), F32), jax.ShapeDtypeStruct((T, 2 * SB_W), F32)] + ride_shape,
        scratch_shapes=ride_sems,
        compiler_params=_params(("arbitrary", "arbitrary", "arbitrary"), 48),
    )(qkv, qkv, qkv, tri_suffix, *ride)


def _fwd_mix(y_lru, y_sb, ga, gb, w_out16, x2, tm):
    T = x2.shape[0]

    def body(yl_ref, ys_ref, ga_ref, gb_ref, w_ref, x_ref, h1_ref, mix_ref):
        na, _, _ = _rms_fwd(yl_ref[...], ga_ref[...])
        nb, _, _ = _rms_fwd(ys_ref[...], gb_ref[...])
        na = na.astype(BF16)
        nb = nb.astype(BF16)
        mix_ref[:, 0:LRU_W] = na
        mix_ref[:, LRU_W:D_MODEL] = nb
        h1_ref[...] = x_ref[...] + _dot(na, w_ref[0:LRU_W, :]) + _dot(nb, w_ref[LRU_W:D_MODEL, :])

    row = lambda i: (i, 0)
    fix = lambda i: (0, 0)
    return pl.pallas_call(
        body, name="fwd_mix", grid=(T // tm,),
        in_specs=[pl.BlockSpec((tm, LRU_W), row), pl.BlockSpec((tm, SB_W), row),
                  pl.BlockSpec((1, LRU_W), fix), pl.BlockSpec((1, SB_W), fix),
                  pl.BlockSpec((D_MODEL, D_MODEL), fix), pl.BlockSpec((tm, D_MODEL), row)],
        out_specs=[pl.BlockSpec((tm, D_MODEL), row), pl.BlockSpec((tm, D_MODEL), row)],
        out_shape=[jax.ShapeDtypeStruct((T, D_MODEL), F32), jax.ShapeDtypeStruct((T, D_MODEL), BF16)],
        compiler_params=_params(("arbitrary",), 48),
    )(y_lru, y_sb, ga, gb, w_out16, x2)


def _fwd_mlp(h1, g2, w_up16, w_down16, gf, tgt, tm):
    T = h1.shape[0]
    nf = D_FF // (FF_CHUNK * FF_PER_STEP)

    def body(h1_ref, g2_ref, wu_ref, wd_ref, gf_ref, t_ref, up_ref, dh2_ref, dgf_ref, loss_ref, hn_s, acc):
        i, j = pl.program_id(0), pl.program_id(1)

        @pl.when(j == 0)
        def _():
            h1v = h1_ref[...]
            hn, _, _ = _rms_fwd(h1v, g2_ref[...])
            hn_s[...] = hn.astype(BF16)
            acc[...] = h1v

        down = None
        for c in range(FF_PER_STEP):
            cols = slice(c * FF_CHUNK, (c + 1) * FF_CHUNK)
            up = jnp.maximum(_dot(hn_s[...], wu_ref[c]), 0.0)
            up_ref[:, cols] = up.astype(BF16)
            part = _dot((up * up).astype(BF16), wd_ref[cols, :])
            down = part if down is None else down + part
        acc[...] += down

        @pl.when((i == 0) & (j == 0))
        def _():
            dgf_ref[...] = jnp.zeros_like(dgf_ref)
            loss_ref[...] = jnp.zeros_like(loss_ref)

        @pl.when(j == nf - 1)
        def _():
            gfv = gf_ref[...]
            y, xhat, rstd = _rms_fwd(acc[...], gfv)
            err = y - t_ref[...]
            loss_ref[...] += jnp.sum(0.5 * jnp.sum(err * err, axis=-1, keepdims=True) * (1.0 / D_MODEL))
            dy = err * (1.0 / D_MODEL)
            dgf_ref[...] += jnp.sum(dy * xhat, axis=0, keepdims=True)
            dh2_ref[...] = _rms_bwd(dy, xhat, rstd, gfv)

    row = lambda i, j: (i, 0)
    fix = lambda i, j: (0, 0)
    return pl.pallas_call(
        body, name="fwd_mlp", grid=(T // tm, nf),
        in_specs=[pl.BlockSpec((tm, D_MODEL), row), pl.BlockSpec((1, D_MODEL), fix),
                  pl.BlockSpec((FF_PER_STEP, D_MODEL, FF_CHUNK), lambda i, j: (j, 0, 0)),
                  pl.BlockSpec((FF_PER_STEP * FF_CHUNK, D_MODEL), lambda i, j: (j, 0)),
                  pl.BlockSpec((1, D_MODEL), fix), pl.BlockSpec((tm, D_MODEL), row)],
        out_specs=[pl.BlockSpec((tm, FF_PER_STEP * FF_CHUNK), lambda i, j: (i, j)), pl.BlockSpec((tm, D_MODEL), row),
                   pl.BlockSpec((1, D_MODEL), fix), pl.BlockSpec((1, LANES), fix)],
        out_shape=[jax.ShapeDtypeStruct((T, D_FF), BF16), jax.ShapeDtypeStruct((T, D_MODEL), F32),
                   jax.ShapeDtypeStruct((1, D_MODEL), F32), jax.ShapeDtypeStruct((1, LANES), F32)],
        scratch_shapes=[pltpu.VMEM((tm, D_MODEL), BF16), pltpu.VMEM((tm, D_MODEL), F32)],
        compiler_params=_params(("arbitrary", "arbitrary"), 48),
    )(h1, g2, w_up16, w_down16, gf, tgt)


def _bwd_mlp(dh2, up16, h1, g2, w_up16, w_down16, tm):
    T = h1.shape[0]
    nf = D_FF // (FF_CHUNK * FF_PER_STEP)

    def body(dh2_ref, up_ref, h1_ref, g2_ref, wu_ref, wd_ref, dup_ref, dh1_ref, hn_ref, dh2b_ref, dg2_ref, acc):
        i, j = pl.program_id(0), pl.program_id(1)

        @pl.when(j == 0)
        def _():
            hn, _, _ = _rms_fwd(h1_ref[...], g2_ref[...])
            hn_ref[...] = hn.astype(BF16)
            dh2b_ref[...] = dh2_ref[...].astype(BF16)
            acc[...] = jnp.zeros_like(acc)

        dhn = None
        for c in range(FF_PER_STEP):
            cols = slice(c * FF_CHUNK, (c + 1) * FF_CHUNK)
            u = up_ref[:, cols].astype(F32)
            dup = (2.0 * u * _dot_nt(dh2b_ref[...], wd_ref[cols, :])).astype(BF16)
            dup_ref[:, cols] = dup
            part = _dot_nt(dup, wu_ref[c])
            dhn = part if dhn is None else dhn + part
        acc[...] += dhn

        @pl.when((i == 0) & (j == 0))
        def _():
            dg2_ref[...] = jnp.zeros_like(dg2_ref)

        @pl.when(j == nf - 1)
        def _():
            g2v = g2_ref[...]
            _, xhat, rstd = _rms_fwd(h1_ref[...], g2v)
            dhn = acc[...]
            dg2_ref[...] += jnp.sum(dhn * xhat, axis=0, keepdims=True)
            dh1_ref[...] = dh2_ref[...] + _rms_bwd(dhn, xhat, rstd, g2v)

    row = lambda i, j: (i, 0)
    fix = lambda i, j: (0, 0)
    return pl.pallas_call(
        body, name="bwd_mlp", grid=(T // tm, nf),
        in_specs=[pl.BlockSpec((tm, D_MODEL), row), pl.BlockSpec((tm, FF_PER_STEP * FF_CHUNK), lambda i, j: (i, j)),
                  pl.BlockSpec((tm, D_MODEL), row), pl.BlockSpec((1, D_MODEL), fix),
                  pl.BlockSpec((FF_PER_STEP, D_MODEL, FF_CHUNK), lambda i, j: (j, 0, 0)),
                  pl.BlockSpec((FF_PER_STEP * FF_CHUNK, D_MODEL), lambda i, j: (j, 0))],
        out_specs=[pl.BlockSpec((tm, FF_PER_STEP * FF_CHUNK), lambda i, j: (i, j)), pl.BlockSpec((tm, D_MODEL), row),
                   pl.BlockSpec((tm, D_MODEL), row), pl.BlockSpec((tm, D_MODEL), row),
                   pl.BlockSpec((1, D_MODEL), fix)],
        out_shape=[jax.ShapeDtypeStruct((T, D_FF), BF16), jax.ShapeDtypeStruct((T, D_MODEL), F32),
                   jax.ShapeDtypeStruct((T, D_MODEL), BF16), jax.ShapeDtypeStruct((T, D_MODEL), BF16),
                   jax.ShapeDtypeStruct((1, D_MODEL), F32)],
        scratch_shapes=[pltpu.VMEM((tm, D_MODEL), F32)],
        compiler_params=_params(("arbitrary", "arbitrary"), 48),
    )(dh2, up16, h1, g2, w_up16, w_down16)


def _matmul_tn(name, a, b, bm, bn, tk, out_shape, out_block, out_index, a_prep=None, b_prep=None, out_dtype=F32):
    T, M = a.shape
    N = b.shape[1]
    nk = T // tk

    def body(a_ref, b_ref, o_ref, acc):
        k = pl.program_id(2)
        av = a_ref[...] if a_prep is None else a_prep(a_ref[...])
        bv = b_ref[...] if b_prep is None else b_prep(b_ref[...])
        p = _dot_tn(av, bv)

        @pl.when(k == 0)
        def _():
            acc[...] = p

        @pl.when(k > 0)
        def _():
            acc[...] += p

        @pl.when(k == nk - 1)
        def _():
            o_ref[...] = acc[...].astype(out_dtype)

    return pl.pallas_call(
        body, name=name, grid=(M // bm, N // bn, nk),
        in_specs=[pl.BlockSpec((tk, bm), lambda m, n, k: (k, m)), pl.BlockSpec((tk, bn), lambda m, n, k: (k, n))],
        out_specs=pl.BlockSpec(out_block, out_index),
        out_shape=jax.ShapeDtypeStruct(out_shape, out_dtype),
        scratch_shapes=[pltpu.VMEM((bm, bn), F32)],
        compiler_params=_params(("arbitrary", "arbitrary", "arbitrary"), 48),
    )(a, b)


def _bwd_mix(dh1, w_out16, y_lru, y_sb, ga, gb, tm):
    T = dh1.shape[0]

    def body(d_ref, w_ref, yl_ref, ys_ref, ga_ref, gb_ref, dyl_ref, dys_ref, dga_ref, dgb_ref):
        @pl.when(pl.program_id(0) == 0)
        def _():
            dga_ref[...] = jnp.zeros_like(dga_ref)
            dgb_ref[...] = jnp.zeros_like(dgb_ref)

        d16 = d_ref[...].astype(BF16)
        for y_ref, g_ref, lo, dy_ref, dg_ref in ((yl_ref, ga_ref, 0, dyl_ref, dga_ref),
                                                 (ys_ref, gb_ref, LRU_W, dys_ref, dgb_ref)):
            gv = g_ref[...]
            dn = _dot_nt(d16, w_ref[lo:lo + LRU_W, :])
            _, xhat, rstd = _rms_fwd(y_ref[...], gv)
            dg_ref[...] += jnp.sum(dn * xhat, axis=0, keepdims=True)
            dy_ref[...] = _rms_bwd(dn, xhat, rstd, gv)

    row = lambda i: (i, 0)
    fix = lambda i: (0, 0)
    return pl.pallas_call(
        body, name="bwd_mix", grid=(T // tm,),
        in_specs=[pl.BlockSpec((tm, D_MODEL), row), pl.BlockSpec((D_MODEL, D_MODEL), fix),
                  pl.BlockSpec((tm, LRU_W), row), pl.BlockSpec((tm, SB_W), row),
                  pl.BlockSpec((1, LRU_W), fix), pl.BlockSpec((1, SB_W), fix)],
        out_specs=[pl.BlockSpec((tm, LRU_W), row), pl.BlockSpec((tm, SB_W), row),
                   pl.BlockSpec((1, LRU_W), fix), pl.BlockSpec((1, SB_W), fix)],
        out_shape=[jax.ShapeDtypeStruct((T, LRU_W), F32), jax.ShapeDtypeStruct((T, SB_W), F32),
                   jax.ShapeDtypeStruct((1, LRU_W), F32), jax.ShapeDtypeStruct((1, SB_W), F32)],
        compiler_params=_params(("arbitrary",), 48),
    )(dh1, w_out16, y_lru, y_sb, ga, gb)


def _attn_bwd(qkv, run_tab, dy_sb, tri_suffix, tri_prefix, ride, ride_sliced, B, S):
    T = B * S
    nq = S // Q_BLK
    nkb = S // K_BLK
    nhp = SB_W // LANES
    scale = HEAD_D ** -0.5

    nr = len(ride)
    ride_shape, ride_specs, ride_sems = _exchange_shapes(ride, ride_sliced)

    def body(q_ref, k_ref, v_ref, run_ref, do_ref, ts_ref, tp_ref, *rest):
        dq_ref, dkt_ref, dvt_ref = rest[nr:nr + 3]
        start_ride, wait_ride = _exchange_copies(rest[:nr], rest[nr + 3:2 * nr + 3], ride_sliced, *rest[2 * nr + 3:])
        qi = pl.program_id(2)
        step_no = (pl.program_id(0) * nhp + pl.program_id(1)) * nq + qi
        pl.when(step_no == 0)(start_ride)
        jd = qi // Q_PER_K

        @pl.when(qi == 0)
        def _():
            dkt_ref[...] = jnp.zeros_like(dkt_ref)
            dvt_ref[...] = jnp.zeros_like(dvt_ref)

        causal, col, lane, halves = _attn_consts(qi)
        q = q_ref[...]
        do = do_ref[...]
        qh = [jnp.where(hm, q, jnp.zeros_like(q)) * jnp.asarray(scale, BF16) for hm in halves]
        doh = [jnp.where(hm, do, 0.0).astype(BF16) for hm in halves]
        qt = jnp.concatenate([h.astype(F32).T.astype(BF16) for h in qh], axis=1)
        dot_ = jnp.concatenate([h.astype(F32).T.astype(BF16) for h in doh], axis=1)
        tables = [run_ref[:, 0:LANES], run_ref[:, LANES:2 * LANES]]

        def group(blocks, carry):
            prefixes, dq = carry
            prefixes = list(prefixes)
            ks = [k_ref[_key_rows(jl), :] for jl, _, _ in blocks]
            vs = [v_ref[_key_rows(jl), :] for jl, _, _ in blocks]
            chains = [(b, h) for b in range(len(blocks)) for h in range(2)]
            z = {c: _dot_nt(qh[c[1]], ks[c[0]]) for c in chains}
            da = {c: _dot_nt(doh[c[1]], vs[c[0]]) for c in chains}
            lg = {c: _log1m_beta(z[c], blocks[c[0]][2]) for c in chains}
            suf = {c: _dot(lg[c].astype(BF16), ts_ref[...]) for c in chains}
            att, g = {}, {}
            for b, h in chains:
                _, jlane, mask = blocks[b]
                run = jnp.sum(jnp.where(lane == jlane, tables[h], 0.0), axis=1, keepdims=True)
                a = jnp.exp(z[b, h] + suf[b, h] + run)
                a = a if mask is None else jnp.where(mask, a, 0.0)
                g[b, h] = a * da[b, h]
                att[b, h] = a.astype(BF16)
            gpre = {c: _dot(g[c].astype(BF16), tp_ref[...]) for c in chains}
            dz = {}
            for b, h in chains:
                mask = blocks[b][2]
                d = g[b, h] - jnp.exp(z[b, h] + lg[b, h]) * (prefixes[h] + gpre[b, h])
                dz[b, h] = (d if mask is None else jnp.where(mask, d, 0.0)).astype(BF16)
                prefixes[h] = prefixes[h] + gpre[b, h][:, K_BLK - 1:K_BLK]
            for b, h in chains:
                dq = dq + _dot(dz[b, h], jnp.where(halves[h], ks[b], jnp.zeros_like(ks[b])))
            for b, (jl, _, _) in enumerate(blocks):
                dkt_ref[jl] += _dot(qt, jnp.concatenate([dz[b, 0], dz[b, 1]], axis=0))
                dvt_ref[jl] += _dot(dot_, jnp.concatenate([att[b, 0], att[b, 1]], axis=0))
            return tuple(prefixes), dq

        def step(it, cr):
            return group([(2 * it, 2 * it, None), (2 * it + 1, 2 * it + 1, None)], cr)

        col0 = jnp.zeros((Q_BLK, 1), F32)
        carry = lax.fori_loop(0, jd // 2, step, ((col0, col0), jnp.zeros((Q_BLK, LANES), F32)))
        two = jd % 2
        before = (jnp.maximum(jd - 1, 0), jnp.where(two == 1, jd - 1, -1), col < two * K_BLK)
        dq_ref[...] = group([before, (jd, jd, causal)], carry)[1] * scale
        pl.when(step_no == B * nhp * nq - 1)(wait_ride)

    qblk = pl.BlockSpec((Q_BLK, LANES), lambda b, hp, qi: (b * nq + qi, hp))
    tri = pl.BlockSpec((K_BLK, K_BLK), lambda b, hp, qi: (0, 0))
    acc = pl.BlockSpec((None, None, nkb, LANES, K_BLK), lambda b, hp, qi: (b, hp, 0, 0, 0))
    return pl.pallas_call(
        body, name="attn_bwd", grid=(B, nhp, nq),
        in_specs=[qblk, pl.BlockSpec((S, LANES), lambda b, hp, qi: (b, nhp + hp)),
                  pl.BlockSpec((S, LANES), lambda b, hp, qi: (b, 2 * nhp + hp)),
                  pl.BlockSpec((Q_BLK, 2 * LANES), lambda b, hp, qi: (b * nq + qi, hp)), qblk, tri, tri] + ride_specs,
        out_specs=[qblk, acc, acc] + ride_specs,
        out_shape=[jax.ShapeDtypeStruct((T, SB_W), F32)] + [jax.ShapeDtypeStruct((B, nhp, nkb, LANES, K_BLK), F32)] * 2
        + ride_shape,
        scratch_shapes=ride_sems,
        compiler_params=_params(("arbitrary", "arbitrary", "arbitrary"), 48),
    )(qkv, qkv, qkv, run_tab, dy_sb, tri_suffix, tri_prefix, *ride)


def _untranspose(t, B, S):
    return t.transpose(0, 2, 4, 1, 3).reshape(B * S, SB_W)


def _lru_bwd(proj_lru, h, dy_lru, conv_w, conv_b, wa_bd, wx_bd, b_a, b_x, lam, B, S, lc):
    T = B * S
    nc = S // lc
    ncb = LRU_W // LANES
    hpc = lc // SUBLANES

    def body(x_ref, xh_ref, g_ref, h_ref, hh_ref, dy_ref, cw_ref, cb_ref, wa_ref, wx_ref, ba_ref, bx_ref, lam_ref,
             dx_ref, dg_ref, dcw_ref, dcb_ref, dwa_ref, dwx_ref, dba_ref, dbx_ref, dlam_ref,
             lam_s, dc_s, a_first, lam_first, dc_head):
        b, ci = pl.program_id(1), pl.program_id(2)
        first_chunk = ci == nc - 1

        @pl.when(ci == 0)
        def _():
            a_first[...] = jnp.zeros_like(a_first)
            lam_first[...] = jnp.zeros_like(lam_first)
            dc_head[...] = jnp.zeros_like(dc_head)

        @pl.when((b == 0) & (ci == 0))
        def _():
            for ref in (dcw_ref, dcb_ref, dwa_ref, dwx_ref, dba_ref, dbx_ref, dlam_ref):
                ref[...] = jnp.zeros_like(ref)

        x = x_ref[...]
        taps = _conv_taps(x, jnp.where(first_chunk, 0.0, xh_ref[...]), lc)
        c = cb_ref[...] + sum(cw_ref[pl.ds(CONV_K - 1 - k, 1), :] * taps[k] for k in range(CONV_K))
        r, i, sp, dsp_dlam, a, s = _lru_gates(c, wa_ref, wx_ref, ba_ref, bx_ref, lam_ref)
        hv = h_ref[...]
        he = jnp.concatenate([jnp.where(first_chunk, 0.0, hh_ref[...]), hv], axis=0)
        h_prev = pltpu.roll(he, 1, 0)[SUBLANES:SUBLANES + lc]
        dy = dy_ref[...]
        gelu, dgelu = _gelu_parts(g_ref[...])
        dg_ref[...] = dy * hv * dgelu

        row = lax.broadcasted_iota(jnp.int32, (lc, LANES), 0)
        a_next = jnp.where(row < lc - 1, pltpu.roll(a, lc - 1, 0), a_first[...])
        lam_loc, a_run = _scan(a_next, dy * gelu, reverse=True)
        lam_s[...] = lam_loc + a_run * lam_first[...]
        lam_t = lam_s[...]
        lam_first[...] = lam_s[pl.ds(0, 1), :]
        lam_s[...] = a
        a_first[...] = lam_s[pl.ds(0, 1), :]

        ic = i * c
        dlog_a = lam_t * h_prev * a - (lam_t * ic) * (a * a) / s
        dpre_r = (dlog_a * ((-LRU_C) * sp)) * r * (1.0 - r)
        dpre_i = (lam_t * s * c) * i * (1.0 - i)
        dlam_ref[...] += jnp.sum(dlog_a * r, axis=0, keepdims=True) * ((-LRU_C) * dsp_dlam)
        dr16 = dpre_r.astype(BF16)
        di16 = dpre_i.astype(BF16)
        c16 = c.astype(BF16)
        dwa_ref[0] += _dot_tn(c16, dr16)
        dwx_ref[0] += _dot_tn(c16, di16)
        dba_ref[...] += jnp.sum(dpre_r, axis=0, keepdims=True)
        dbx_ref[...] += jnp.sum(dpre_i, axis=0, keepdims=True)
        dc = lam_t * s * i + _dot_nt(dr16, wa_ref[0]) + _dot_nt(di16, wx_ref[0])
        dcb_ref[...] += jnp.sum(dc, axis=0, keepdims=True)
        for k in range(CONV_K):
            dcw_ref[pl.ds(CONV_K - 1 - k, 1), :] += jnp.sum(dc * taps[k], axis=0, keepdims=True)
        dce = jnp.concatenate([dc, dc_head[...]], axis=0)
        dx = cw_ref[pl.ds(CONV_K - 1, 1), :] * dc
        for k in range(1, CONV_K):
            dx = dx + cw_ref[pl.ds(CONV_K - 1 - k, 1), :] * pltpu.roll(dce, lc + SUBLANES - k, 0)[0:lc]
        dx_ref[...] = dx
        dc_s[...] = dc
        dc_head[...] = dc_s[pl.ds(0, SUBLANES), :]

    def chunk(col):
        return pl.BlockSpec((lc, LANES), lambda cb, b, ci: (b * nc + nc - 1 - ci, col(cb)))

    def halo(col):
        return pl.BlockSpec((SUBLANES, LANES),
                            lambda cb, b, ci: (jnp.maximum((b * nc + nc - 1 - ci) * hpc - 1, 0), col(cb)))

    chan = lambda cb, b, ci: (0, cb)
    blk = lambda cb, b, ci: (cb, 0, 0)
    vec = pl.BlockSpec((1, LANES), chan)
    mat = pl.BlockSpec((1, LANES, LANES), blk)
    return pl.pallas_call(
        body, name="lru_bwd", grid=(ncb, B, nc),
        in_specs=[chunk(lambda cb: cb), halo(lambda cb: cb), chunk(lambda cb: ncb + cb),
                  chunk(lambda cb: cb), halo(lambda cb: cb), chunk(lambda cb: cb),
                  pl.BlockSpec((CONV_K, LANES), chan), vec, mat, mat, vec, vec, vec],
        out_specs=[chunk(lambda cb: cb), chunk(lambda cb: cb), pl.BlockSpec((CONV_K, LANES), chan), vec,
                   mat, mat, vec, vec, vec],
        out_shape=[jax.ShapeDtypeStruct((T, LRU_W), F32), jax.ShapeDtypeStruct((T, LRU_W), F32),
                   jax.ShapeDtypeStruct((CONV_K, LRU_W), F32), jax.ShapeDtypeStruct((1, LRU_W), F32),
                   jax.ShapeDtypeStruct((ncb, LANES, LANES), F32), jax.ShapeDtypeStruct((ncb, LANES, LANES), F32),
                   jax.ShapeDtypeStruct((1, LRU_W), F32), jax.ShapeDtypeStruct((1, LRU_W), F32),
                   jax.ShapeDtypeStruct((1, LRU_W), F32)],
        scratch_shapes=[pltpu.VMEM((lc, LANES), F32), pltpu.VMEM((lc, LANES), F32), pltpu.VMEM((1, LANES), F32),
                        pltpu.VMEM((1, LANES), F32), pltpu.VMEM((SUBLANES, LANES), F32)],
        compiler_params=_params(("arbitrary", "arbitrary", "arbitrary"), 32),
    )(proj_lru, proj_lru, proj_lru, h, h, dy_lru, conv_w, conv_b, wa_bd, wx_bd, b_a, b_x, lam)


def _bwd_in(pieces, w_in16, x2, g1, dh1, tm):
    T = x2.shape[0]
    npc = len(pieces)

    def body(*refs):
        p_refs = refs[:npc]
        w_ref, x_ref, g_ref, d_ref, dx_ref, dproj_ref, xn_ref, dg1_ref = refs[npc:]

        @pl.when(pl.program_id(0) == 0)
        def _():
            dg1_ref[...] = jnp.zeros_like(dg1_ref)

        dxn = jnp.zeros((tm, D_MODEL), F32)
        for n, p_ref in enumerate(p_refs):
            cols = slice(n * LRU_W, (n + 1) * LRU_W)
            p16 = p_ref[...].astype(BF16)
            dproj_ref[:, cols] = p16
            dxn = dxn + _dot_nt(p16, w_ref[:, cols])
        gv = g_ref[...]
        xn, xhat, rstd = _rms_fwd(x_ref[...], gv)
        xn_ref[...] = xn.astype(BF16)
        dg1_ref[...] += jnp.sum(dxn * xhat, axis=0, keepdims=True)
        dx_ref[...] = d_ref[...] + _rms_bwd(dxn, xhat, rstd, gv)

    row = lambda i: (i, 0)
    fix = lambda i: (0, 0)
    return pl.pallas_call(
        body, name="bwd_in", grid=(T // tm,),
        in_specs=[pl.BlockSpec((tm, LRU_W), row)] * npc + [
            pl.BlockSpec((D_MODEL, IN_COLS), fix), pl.BlockSpec((tm, D_MODEL), row),
            pl.BlockSpec((1, D_MODEL), fix), pl.BlockSpec((tm, D_MODEL), row)],
        out_specs=[pl.BlockSpec((tm, D_MODEL), row), pl.BlockSpec((tm, IN_COLS), row),
                   pl.BlockSpec((tm, D_MODEL), row), pl.BlockSpec((1, D_MODEL), fix)],
        out_shape=[jax.ShapeDtypeStruct((T, D_MODEL), F32), jax.ShapeDtypeStruct((T, IN_COLS), BF16),
                   jax.ShapeDtypeStruct((T, D_MODEL), BF16), jax.ShapeDtypeStruct((1, D_MODEL), F32)],
        compiler_params=_params(("arbitrary",), 56),
    )(*pieces, w_in16, x2, g1, dh1)


def _adam_shard(name, parts, w, m, v, tr):
    R, C = w.shape

    def body(p_ref, w_ref, m_ref, v_ref, g_ref, d_ref, m2_ref, v2_ref):
        g = p_ref[0].astype(F32)
        for p in range(1, N_DEV):
            g = g + p_ref[p].astype(F32)
        g_ref[...] = g
        d_ref[...], m2_ref[...], v2_ref[...] = _adamw(w_ref[...], g, m_ref[...], v_ref[...])

    blk = pl.BlockSpec((tr, C), lambda i: (i, 0))
    return pl.pallas_call(
        body, name=name, grid=(R // tr,),
        in_specs=[pl.BlockSpec((N_DEV, tr, C), lambda i: (0, i, 0)), blk, blk, blk],
        out_specs=[blk] * 4, out_shape=[jax.ShapeDtypeStruct((R, C), F32)] * 4,
        compiler_params=_params(("arbitrary",), 48),
    )(parts, w, m, v)


def _sum_parts(name, parts):
    def body(p_ref, g_ref):
        g = p_ref[0]
        for p in range(1, N_DEV):
            g = g + p_ref[p]
        g_ref[...] = g

    return pl.pallas_call(body, name=name, out_shape=jax.ShapeDtypeStruct(parts.shape[1:], F32))(parts)


def _adam_packed(w, g, m, v):
    def body(w_ref, g_ref, m_ref, v_ref, d_ref, m2_ref, v2_ref):
        d_ref[...], m2_ref[...], v2_ref[...] = _adamw(w_ref[...], g_ref[...], m_ref[...], v_ref[...])

    return pl.pallas_call(body, name="adam_small", out_shape=[jax.ShapeDtypeStruct(w.shape, F32)] * 3)(w, g, m, v)


def _pack(vals, rows):
    flat = jnp.concatenate([v.reshape(-1).astype(F32) for v in vals])
    return jnp.pad(flat, (0, rows * LANES - flat.shape[0])).reshape(rows, LANES)


def _unpack(packed, sizes):
    flat = packed.reshape(-1)
    out, off = [], 0
    for n in sizes:
        out.append(flat[off:off + n])
        off += n
    return out


def _block_diag_pairs(w):
    w = w.reshape(LRU_BLOCKS // 2, 2, HEAD_D, HEAD_D)
    out = jnp.zeros((LRU_BLOCKS // 2, LANES, LANES), w.dtype)
    out = out.at[:, :HEAD_D, :HEAD_D].set(w[:, 0])
    return out.at[:, HEAD_D:, HEAD_D:].set(w[:, 1])


def _diag_blocks(w):
    return jnp.stack([w[:, :HEAD_D, :HEAD_D], w[:, HEAD_D:, HEAD_D:]], axis=1).reshape(LRU_BLOCKS, HEAD_D, HEAD_D)


def kernel(x, norm1_g, w_in, conv_w, conv_b, lru_w_a, lru_b_a, lru_w_x, lru_b_x, lru_lambda, lru_out_g, sb_out_g, w_out, norm2_g, w_up, w_down, final_g, loss_target, m_norm1_g, m_w_in, m_conv_w, m_conv_b, m_lru_w_a, m_lru_b_a, m_lru_w_x, m_lru_b_x, m_lru_lambda, m_lru_out_g, m_sb_out_g, m_w_out, m_norm2_g, m_w_up, m_w_down, m_final_g, v_norm1_g, v_w_in, v_conv_w, v_conv_b, v_lru_w_a, v_lru_b_a, v_lru_w_x, v_lru_b_x, v_lru_lambda, v_lru_out_g, v_sb_out_g, v_w_out, v_norm2_g, v_w_up, v_w_down, v_final_g):
    B, S, _ = x.shape
    T = B * S
    tm = min(512, T)
    tk = min(GRAD_TK, T)
    lc = min(512, S)
    me = _my_index()
    x2 = x.reshape(T, D_MODEL)
    tgt = loss_target.reshape(T, D_MODEL)
    cw_cols = CONV_K * LRU_W // N_DEV // CONV_K

    shards16 = _cast_shards([w_in[0], w_out[0], w_up[0], w_down[0]])
    cw_pad = jnp.zeros((SUBLANES, LANES), F32).at[:CONV_K, :cw_cols].set(conv_w[0])
    g_in, g_cw = _exchange("gather_w_in", [shards16[0], cw_pad], [False, False])
    w_in16 = g_in.transpose(1, 0, 2).reshape(D_MODEL, IN_COLS)
    conv_w_full = g_cw[:, :CONV_K, :cw_cols].transpose(1, 0, 2).reshape(CONV_K, LRU_W)
    wa_bd = _block_diag_pairs(lru_w_a[0]).astype(BF16)
    wx_bd = _block_diag_pairs(lru_w_x[0]).astype(BF16)
    b_a = lru_b_a.reshape(1, LRU_W)
    b_x = lru_b_x.reshape(1, LRU_W)
    gf = final_g.reshape(1, D_MODEL)

    proj_lru, qkv = _fwd_in(x2, norm1_g, w_in16, tm)
    y_lru, h = _lru_fwd(proj_lru, conv_w_full, conv_b, wa_bd, wx_bd, b_a, b_x, lru_lambda, B, S, lc)
    tri_suffix, tri_prefix = _tri(False), _tri(True)
    y_sb, run_tab, g_out, g_up, g_down = _attn_fwd(qkv, tri_suffix, list(shards16[1:]), [False] * 3, B, S)
    w_out16 = g_out.reshape(D_MODEL, D_MODEL)
    w_down16 = g_down.reshape(D_FF, D_MODEL)
    h1, mix16 = _fwd_mix(y_lru, y_sb, lru_out_g, sb_out_g, w_out16, x2, tm)
    up16, dh2, d_final_g, loss_part = _fwd_mlp(h1, norm2_g, g_up, w_down16, gf, tgt, tm)

    dup16, dh1, hn16, dh2b, d_norm2_g = _bwd_mlp(dh2, up16, h1, norm2_g, g_up, w_down16, tm)
    sq = lambda u: (u.astype(F32) * u.astype(F32)).astype(BF16)
    gw_up = _matmul_tn("grad_w_up", hn16, dup16, D_MODEL, FF_CHUNK, tk, (N_DEV, D_MODEL, FF_CHUNK),
                       (None, D_MODEL, FF_CHUNK), lambda m, n, k: (n, 0, 0))
    gw_down = _matmul_tn("grad_w_down", up16, dh2b, FF_CHUNK, D_MODEL, tk, (N_DEV, FF_CHUNK, D_MODEL),
                         (None, FF_CHUNK, D_MODEL), lambda m, n, k: (m, 0, 0), a_prep=sq)
    dy_lru, dy_sb, d_lru_out_g, d_sb_out_g = _bwd_mix(dh1, w_out16, y_lru, y_sb, lru_out_g, sb_out_g, tm)
    gw_out = _matmul_tn("grad_w_out", mix16, dh1, D_MODEL, FF_CHUNK, tk, (D_MODEL, D_MODEL),
                        (D_MODEL, FF_CHUNK), lambda m, n, k: (0, n), b_prep=lambda u: u.astype(BF16))
    parts_out = gw_out.reshape(N_DEV, D_MODEL // N_DEV, D_MODEL)
    dq, dkt, dvt, r_out, r_up, r_down = _attn_bwd(qkv, run_tab, dy_sb, tri_suffix, tri_prefix,
                                                  [parts_out, gw_up, gw_down], [True] * 3, B, S)
    dk, dv = _untranspose(dkt, B, S), _untranspose(dvt, B, S)
    (dx_lru, dg_lru, d_conv_w, d_conv_b, d_wa, d_wx, d_b_a, d_b_x, d_lambda) = _lru_bwd(
        proj_lru, h, dy_lru, conv_w_full, conv_b, wa_bd, wx_bd, b_a, b_x, lru_lambda, B, S, lc)
    dx, dproj16, xn16, d_norm1_g = _bwd_in([dx_lru, dg_lru, dq, dk, dv], w_in16, x2, norm1_g, dh1, tm)
    gw_in = _matmul_tn("grad_w_in", xn16, dproj16, D_MODEL, FF_CHUNK, tk, (D_MODEL, IN_COLS),
                       (D_MODEL, FF_CHUNK), lambda m, n, k: (0, n), out_dtype=BF16)

    small_grads = {"norm1_g": d_norm1_g, "conv_b": d_conv_b, "lru_w_a": _diag_blocks(d_wa), "lru_b_a": d_b_a,
                   "lru_w_x": _diag_blocks(d_wx), "lru_b_x": d_b_x, "lru_lambda": d_lambda,
                   "lru_out_g": d_lru_out_g, "sb_out_g": d_sb_out_g, "norm2_g": d_norm2_g, "final_g": d_final_g}
    packed = _pack([small_grads[n] for n, _ in SMALL] + [d_conv_w, loss_part], EXCH_ROWS)
    parts_in = gw_in.reshape(D_MODEL, N_DEV, IN_COLS // N_DEV).transpose(1, 0, 2)
    r_in, r_small = _exchange("exchange_grads", [parts_in, packed], [True, False])

    g_w_in, d_w_in, nm_w_in, nv_w_in = _adam_shard("adam_w_in", r_in, w_in[0], m_w_in[0], v_w_in[0], 256)
    g_w_out, d_w_out, nm_w_out, nv_w_out = _adam_shard("adam_w_out", r_out, w_out[0], m_w_out[0], v_w_out[0], 64)
    g_w_up, d_w_up, nm_w_up, nv_w_up = _adam_shard("adam_w_up", r_up, w_up[0], m_w_up[0], v_w_up[0], 256)
    g_w_down, d_w_down, nm_w_down, nv_w_down = _adam_shard("adam_w_down", r_down, w_down[0], m_w_down[0], v_w_down[0], 128)

    total = _sum_parts("sum_small", r_small)
    sizes = [n for _, n in SMALL]
    small_g = _unpack(total, sizes + [CONV_K * LRU_W, 1])
    loss = small_g[-1][0]
    g_conv_w = lax.dynamic_slice_in_dim(small_g[-2].reshape(CONV_K, LRU_W), me * cw_cols, cw_cols, axis=1)
    given = dict(norm1_g=(norm1_g, m_norm1_g, v_norm1_g), conv_b=(conv_b, m_conv_b, v_conv_b),
                 lru_w_a=(lru_w_a, m_lru_w_a, v_lru_w_a), lru_b_a=(lru_b_a, m_lru_b_a, v_lru_b_a),
                 lru_w_x=(lru_w_x, m_lru_w_x, v_lru_w_x), lru_b_x=(lru_b_x, m_lru_b_x, v_lru_b_x),
                 lru_lambda=(lru_lambda, m_lru_lambda, v_lru_lambda), lru_out_g=(lru_out_g, m_lru_out_g, v_lru_out_g),
                 sb_out_g=(sb_out_g, m_sb_out_g, v_sb_out_g), norm2_g=(norm2_g, m_norm2_g, v_norm2_g),
                 final_g=(final_g, m_final_g, v_final_g))
    names = [n for n, _ in SMALL]
    pw = _pack([given[n][0] for n in names] + [conv_w], ADAM_ROWS)
    pm = _pack([given[n][1] for n in names] + [m_conv_w], ADAM_ROWS)
    pv = _pack([given[n][2] for n in names] + [v_conv_w], ADAM_ROWS)
    pg = _pack(small_g[:len(names)] + [g_conv_w], ADAM_ROWS)
    pd, pm2, pv2 = _adam_packed(pw, pg, pm, pv)
    asz = sizes + [conv_w.size]
    shapes = {n: given[n][0].shape for n in names}
    shapes["conv_w"] = conv_w.shape
    order = names + ["conv_w"]
    g_small = dict(zip(order, [a.reshape(shapes[n]) for n, a in zip(order, _unpack(pg, asz))]))
    d_small = dict(zip(order, [a.reshape(shapes[n]) for n, a in zip(order, _unpack(pd, asz))]))
    m_small = dict(zip(order, [a.reshape(shapes[n]) for n, a in zip(order, _unpack(pm2, asz))]))
    v_small = dict(zip(order, [a.reshape(shapes[n]) for n, a in zip(order, _unpack(pv2, asz))]))

    big = {"w_in": (g_w_in, d_w_in, nm_w_in, nv_w_in), "w_out": (g_w_out, d_w_out, nm_w_out, nv_w_out),
           "w_up": (g_w_up, d_w_up, nm_w_up, nv_w_up), "w_down": (g_w_down, d_w_down, nm_w_down, nv_w_down)}
    weights = ["norm1_g", "w_in", "conv_w", "conv_b", "lru_w_a", "lru_b_a", "lru_w_x", "lru_b_x", "lru_lambda",
               "lru_out_g", "sb_out_g", "w_out", "norm2_g", "w_up", "w_down", "final_g"]

    def leaf(n, kind):
        if n in big:
            return big[n][kind][None]
        return (g_small, d_small, m_small, v_small)[kind][n]

    return (loss, dx.reshape(B, S, D_MODEL), *[leaf(n, 0) for n in weights], *[leaf(n, 1) for n in weights],
            *[leaf(n, 2) for n in weights], *[leaf(n, 3) for n in weights])
```

```python
import jax
import jax.numpy as jnp
from jax import lax
from jax.experimental import pallas as pl
from jax.experimental.pallas import tpu as pltpu

F32 = jnp.float32
BF16 = jnp.bfloat16

D_MODEL = 1024
LRU_W = 512
SB_W = 512
HEAD_D = 64
D_FF = 4096
IN_COLS = 2 * LRU_W + 3 * SB_W
CONV_K = 4
LRU_BLOCKS = 8
LRU_C = 8.0
EPS = 1e-6
N_DEV = 8
LANES = 128
SUBLANES = 8
FF_CHUNK = 512
FF_PER_STEP = 4
GRAD_TK = 2048
Q_BLK = 256
K_BLK = 256
Q_PER_K = K_BLK // Q_BLK

ADAM_LR = 0.001
ADAM_B1 = 0.9
ADAM_B2 = 0.999
ADAM_EPS = 1e-08
ADAM_WD = 0.01
ADAM_STEP = 10


def _params(sem=None, vmem_mb=None):
    kw = {}
    if sem is not None:
        kw["dimension_semantics"] = sem
    if vmem_mb is not None:
        kw["vmem_limit_bytes"] = vmem_mb << 20
    return pltpu.CompilerParams(**kw)


def _dot(a, b):
    return jnp.dot(a, b, preferred_element_type=F32)


def _dot_nt(a, b):
    return lax.dot_general(a, b, (((1,), (1,)), ((), ())), preferred_element_type=F32)


def _dot_tn(a, b):
    return lax.dot_general(a, b, (((0,), (0,)), ((), ())), preferred_element_type=F32)


def _rms_fwd(x, g):
    rstd = lax.rsqrt(jnp.mean(x * x, axis=-1, keepdims=True) + EPS)
    xhat = x * rstd
    return xhat * g, xhat, rstd


def _rms_bwd(dy, xhat, rstd, g):
    dxhat = dy * g
    return rstd * (dxhat - xhat * jnp.mean(dxhat * xhat, axis=-1, keepdims=True))


def _sigmoid(x):
    return 1.0 / (1.0 + jnp.exp(-x))


def _log1p_pos(e):
    series = e * (1.0 - e * (0.5 - e * (1.0 / 3.0 - e * 0.25)))
    return jnp.where(e < 1e-2, series, jnp.log(1.0 + e))


def _neg_expm1(x):
    series = -x * (1.0 + x * (0.5 + x * (1.0 / 6.0 + x * (1.0 / 24.0))))
    return jnp.where(x > -1e-2, series, 1.0 - jnp.exp(x))


def _gelu_parts(g):
    k0 = 0.7978845608028654
    k1 = 0.044715
    t = jnp.tanh(k0 * (g + k1 * g * g * g))
    val = 0.5 * g * (1.0 + t)
    grad = 0.5 * (1.0 + t) + 0.5 * g * (1.0 - t * t) * k0 * (1.0 + 3.0 * k1 * g * g)
    return val, grad


def _scan(a, b, reverse):
    n = a.shape[0]
    row = lax.broadcasted_iota(jnp.int32, a.shape, 0)
    s = 1
    while s < n:
        if reverse:
            keep = row < n - s
            shift = n - s
        else:
            keep = row >= s
            shift = s
        bs = jnp.where(keep, pltpu.roll(b, shift, 0), 0.0)
        a_s = jnp.where(keep, pltpu.roll(a, shift, 0), 1.0)
        b = a * bs + b
        a = a * a_s
        s *= 2
    return b, a


def _adamw(w, g, m, v):
    m = ADAM_B1 * m + (1.0 - ADAM_B1) * g
    v = ADAM_B2 * v + (1.0 - ADAM_B2) * (g * g)
    m_hat = m / (1.0 - ADAM_B1 ** ADAM_STEP)
    v_hat = v / (1.0 - ADAM_B2 ** ADAM_STEP)
    delta = -ADAM_LR * (m_hat / (jnp.sqrt(v_hat) + ADAM_EPS) + ADAM_WD * w)
    return delta, m, v


def _my_index():
    return 4 * lax.axis_index("x") + 2 * lax.axis_index("y") + lax.axis_index("c")


def _peer(k):
    x, y, c = lax.axis_index("x"), lax.axis_index("y"), lax.axis_index("c")
    px = 1 - x if (k >> 2) & 1 else x
    py = 1 - y if (k >> 1) & 1 else y
    pc = 1 - c if k & 1 else c
    return (px, py, pc), 4 * px + 2 * py + pc


def _exchange_shapes(srcs, sliced):
    n = len(srcs)
    out_shape = [jax.ShapeDtypeStruct(s.shape if sl else (N_DEV,) + s.shape, s.dtype) for s, sl in zip(srcs, sliced)]
    specs = [pl.BlockSpec(memory_space=pl.ANY)] * n
    sems = [pltpu.SemaphoreType.DMA((n, N_DEV - 1)), pltpu.SemaphoreType.DMA((n, N_DEV - 1)),
            pltpu.SemaphoreType.DMA((n,))]
    return out_shape, specs, sems


def _exchange_copies(ins, outs, sliced, send_sems, recv_sems, local_sems):
    n = len(ins)

    def part(a, p):
        return ins[a].at[p] if sliced[a] else ins[a]

    def copies(receiving):
        me = _my_index()
        local = [pltpu.make_async_copy(part(a, me), outs[a].at[me], local_sems.at[a]) for a in range(n)]
        remote = []
        for k in range(1, N_DEV):
            dev, idx = _peer(k)
            for a in range(n):
                remote.append(pltpu.make_async_remote_copy(
                    src_ref=part(a, idx), dst_ref=outs[a].at[idx if receiving else me],
                    send_sem=send_sems.at[a, k - 1], recv_sem=recv_sems.at[a, k - 1],
                    device_id=dev, device_id_type=pl.DeviceIdType.MESH))
        return local, remote

    def start():
        local, remote = copies(receiving=False)
        for cp in local + remote:
            cp.start()

    def wait():
        local, remote = copies(receiving=True)
        for cp in remote + local:
            cp.wait()

    return start, wait


def _exchange(name, srcs, sliced):
    n = len(srcs)
    out_shape, specs, sems = _exchange_shapes(srcs, sliced)

    def body(*refs):
        start, wait = _exchange_copies(refs[:n], refs[n:2 * n], sliced, *refs[2 * n:])
        start()
        wait()

    return pl.pallas_call(body, name=name, out_shape=out_shape, in_specs=specs, out_specs=specs,
                          scratch_shapes=sems)(*srcs)


def _cast_shards(ws):
    def body(*refs):
        for i in range(len(ws)):
            refs[len(ws) + i][...] = refs[i][...].astype(BF16)

    return pl.pallas_call(
        body, name="cast_shards", out_shape=[jax.ShapeDtypeStruct(w.shape, BF16) for w in ws],
        compiler_params=_params(vmem_mb=32),
    )(*ws)


def _fwd_in(x2, g1, w_in16, tm):
    T = x2.shape[0]

    def body(x_ref, g_ref, w_ref, lru_ref, qkv_ref):
        xn, _, _ = _rms_fwd(x_ref[...], g_ref[...])
        xn = xn.astype(BF16)
        lru_ref[...] = _dot(xn, w_ref[:, 0:2 * LRU_W])
        qkv_ref[...] = _dot(xn, w_ref[:, 2 * LRU_W:IN_COLS]).astype(BF16)

    return pl.pallas_call(
        body, name="fwd_in", grid=(T // tm,),
        in_specs=[pl.BlockSpec((tm, D_MODEL), lambda i: (i, 0)),
                  pl.BlockSpec((1, D_MODEL), lambda i: (0, 0)),
                  pl.BlockSpec((D_MODEL, IN_COLS), lambda i: (0, 0))],
        out_specs=[pl.BlockSpec((tm, 2 * LRU_W), lambda i: (i, 0)),
                   pl.BlockSpec((tm, 3 * SB_W), lambda i: (i, 0))],
        out_shape=[jax.ShapeDtypeStruct((T, 2 * LRU_W), F32), jax.ShapeDtypeStruct((T, 3 * SB_W), BF16)],
        compiler_params=_params(("arbitrary",), 48),
    )(x2, g1, w_in16)


def _lru_gates(c, wa_ref, wx_ref, ba_ref, bx_ref, lam_ref):
    c16 = c.astype(BF16)
    r = _sigmoid(_dot(c16, wa_ref[0]) + ba_ref[...])
    i = _sigmoid(_dot(c16, wx_ref[0]) + bx_ref[...])
    lam = lam_ref[...]
    e = jnp.exp(-jnp.abs(lam))
    sp = jnp.maximum(-lam, 0.0) + _log1p_pos(e)
    dsp_dlam = -jnp.where(lam >= 0.0, e, 1.0) / (1.0 + e)
    log_a = (-LRU_C) * r * sp
    a = jnp.exp(log_a)
    s = jnp.sqrt(_neg_expm1(2.0 * log_a))
    return r, i, sp, dsp_dlam, a, s


def _conv_taps(x, halo, lc):
    xe = jnp.concatenate([halo, x], axis=0)
    return [x] + [pltpu.roll(xe, k, 0)[SUBLANES:SUBLANES + lc] for k in range(1, CONV_K)]


def _lru_fwd(proj_lru, conv_w, conv_b, wa_bd, wx_bd, b_a, b_x, lam, B, S, lc):
    T = B * S
    nc = S // lc
    ncb = LRU_W // LANES

    def body(x_ref, g_ref, cw_ref, cb_ref, wa_ref, wx_ref, ba_ref, bx_ref, lam_ref, y_ref, h_ref, tail, carry):
        ci = pl.program_id(2)

        @pl.when(ci == 0)
        def _():
            tail[...] = jnp.zeros_like(tail)
            carry[...] = jnp.zeros_like(carry)

        x = x_ref[...]
        taps = _conv_taps(x, tail[...], lc)
        c = cb_ref[...] + sum(cw_ref[pl.ds(CONV_K - 1 - k, 1), :] * taps[k] for k in range(CONV_K))
        tail[...] = x_ref[pl.ds(lc - SUBLANES, SUBLANES), :]
        r, i, sp, _, a, s = _lru_gates(c, wa_ref, wx_ref, ba_ref, bx_ref, lam_ref)
        h_loc, a_run = _scan(a, s * (i * c), reverse=False)
        h_ref[...] = h_loc + a_run * carry[...]
        carry[...] = h_ref[pl.ds(lc - 1, 1), :]
        gelu, _ = _gelu_parts(g_ref[...])
        y_ref[...] = h_ref[...] * gelu

    chan = lambda b, cb, ci: (0, cb)
    return pl.pallas_call(
        body, name="lru_fwd", grid=(B, ncb, nc),
        in_specs=[pl.BlockSpec((lc, LANES), lambda b, cb, ci: (b * nc + ci, cb)),
                  pl.BlockSpec((lc, LANES), lambda b, cb, ci: (b * nc + ci, ncb + cb)),
                  pl.BlockSpec((CONV_K, LANES), chan), pl.BlockSpec((1, LANES), chan),
                  pl.BlockSpec((1, LANES, LANES), lambda b, cb, ci: (cb, 0, 0)),
                  pl.BlockSpec((1, LANES, LANES), lambda b, cb, ci: (cb, 0, 0)),
                  pl.BlockSpec((1, LANES), chan), pl.BlockSpec((1, LANES), chan), pl.BlockSpec((1, LANES), chan)],
        out_specs=[pl.BlockSpec((lc, LANES), lambda b, cb, ci: (b * nc + ci, cb))] * 2,
        out_shape=[jax.ShapeDtypeStruct((T, LRU_W), F32)] * 2,
        scratch_shapes=[pltpu.VMEM((SUBLANES, LANES), F32), pltpu.VMEM((1, LANES), F32)],
        compiler_params=_params(("arbitrary", "arbitrary", "arbitrary"), 32),
    )(proj_lru, proj_lru, conv_w, conv_b, wa_bd, wx_bd, b_a, b_x, lam)


def _tri(prefix):
    r = lax.broadcasted_iota(jnp.int32, (K_BLK, K_BLK), 0)
    c = lax.broadcasted_iota(jnp.int32, (K_BLK, K_BLK), 1)
    return ((r <= c) if prefix else (r >= c)).astype(BF16)


def _attn_consts(qi):
    r = lax.broadcasted_iota(jnp.int32, (Q_BLK, K_BLK), 0)
    c = lax.broadcasted_iota(jnp.int32, (Q_BLK, K_BLK), 1)
    causal = c + ((qi // Q_PER_K) * K_BLK - qi * Q_BLK) < r
    lane = lax.broadcasted_iota(jnp.int32, (1, LANES), 1)
    return causal, c, lane, (lane < HEAD_D, lane >= HEAD_D)


def _log1m_beta(z, mask):
    lg = -(jnp.maximum(z, 0.0) + jnp.log(1.0 + jnp.exp(-jnp.abs(z))))
    return lg if mask is None else jnp.where(mask, lg, 0.0)


def _key_rows(j):
    return pl.ds(pl.multiple_of(j * K_BLK, K_BLK), K_BLK)


def _attn_fwd(qkv, tri_suffix, ride, ride_sliced, B, S):
    T = B * S
    nq = S // Q_BLK
    nhp = SB_W // LANES
    scale = HEAD_D ** -0.5
    assert S // K_BLK <= LANES
    nr = len(ride)
    ride_shape, ride_specs, ride_sems = _exchange_shapes(ride, ride_sliced)

    def body(q_ref, k_ref, v_ref, tri_ref, *rest):
        o_ref, run_ref = rest[nr:nr + 2]
        start_ride, wait_ride = _exchange_copies(rest[:nr], rest[nr + 2:2 * nr + 2], ride_sliced, *rest[2 * nr + 2:])
        qi = pl.program_id(2)
        step_no = (pl.program_id(0) * nhp + pl.program_id(1)) * nq + qi
        pl.when(step_no == 0)(start_ride)
        jd = qi // Q_PER_K
        causal, col, lane, halves = _attn_consts(qi)
        q = q_ref[...]
        qh = [jnp.where(hm, q, jnp.zeros_like(q)) * jnp.asarray(scale, BF16) for hm in halves]

        def group(blocks, carry):
            runs, tables, acc = carry
            runs, tables = list(runs), list(tables)
            ks = [k_ref[_key_rows(jl), :] for jl, _, _ in blocks]
            vs = [v_ref[_key_rows(jl), :] for jl, _, _ in blocks]
            chains = [(b, h) for b in range(len(blocks)) for h in range(2)]
            z = {c: _dot_nt(qh[c[1]], ks[c[0]]) for c in chains}
            lg = {c: _log1m_beta(z[c], blocks[c[0]][2]) for c in chains}
            suf = {c: _dot(lg[c].astype(BF16), tri_ref[...]) for c in chains}
            att = {}
            for b, h in chains:
                _, jlane, mask = blocks[b]
                a = jnp.exp(z[b, h] + suf[b, h] + runs[h])
                att[b, h] = (a if mask is None else jnp.where(mask, a, 0.0)).astype(BF16)
                tables[h] = jnp.where(lane == jlane, runs[h], tables[h])
                runs[h] = runs[h] + suf[b, h][:, 0:1]
            for b, h in chains:
                acc = acc + _dot(att[b, h], jnp.where(halves[h], vs[b], jnp.zeros_like(vs[b])))
            return tuple(runs), tuple(tables), acc

        col0 = jnp.zeros((Q_BLK, 1), F32)
        zero = jnp.zeros((Q_BLK, LANES), F32)
        two = jd % 2
        before = (jnp.maximum(jd - 1, 0), jnp.where(two == 1, jd - 1, -1), col < two * K_BLK)
        carry = group([(jd, jd, causal), before], ((col0, col0), (zero, zero), zero))
        top = jd - 1 - two

        def run(ja, n):
            return [(ja - i, ja - i, None) for i in range(n)]

        odd_pair = ((top + 1) // 2) % 2
        carry = lax.cond(odd_pair == 1, lambda cr: group(run(top, 2), cr), lambda cr: cr, carry)
        top4 = top - 2 * odd_pair
        _, tables, acc = lax.fori_loop(0, (top4 + 1) // 4, lambda it, cr: group(run(top4 - 4 * it, 4), cr), carry)
        o_ref[...] = acc
        run_ref[:, 0:LANES] = tables[0]
        run_ref[:, LANES:2 * LANES] = tables[1]
        pl.when(step_no == B * nhp * nq - 1)(wait_ride)

    return pl.pallas_call(
        body, name="attn_fwd", grid=(B, nhp, nq),
        in_specs=[pl.BlockSpec((Q_BLK, LANES), lambda b, hp, qi: (b * nq + qi, hp)),
                  pl.BlockSpec((S, LANES), lambda b, hp, qi: (b, nhp + hp)),
                  pl.BlockSpec((S, LANES), lambda b, hp, qi: (b, 2 * nhp + hp)),
                  pl.BlockSpec((K_BLK, K_BLK), lambda b, hp, qi: (0, 0))] + ride_specs,
        out_specs=[pl.BlockSpec((Q_BLK, LANES), lambda b, hp, qi: (b * nq + qi, hp)),
                   pl.BlockSpec((Q_BLK, 2 * LANES), lambda b, hp, qi: (b * nq + qi, hp))] + ride_specs,
        out_shape=[jax.ShapeDtypeStruct((T, SB_W), F32), jax.ShapeDtypeStruct((T, 2 * SB_W), F32)] + ride_shape,
        scratch_shapes=ride_sems,
        compiler_params=_params(("arbitrary", "arbitrary", "arbitrary"), 48),
    )(qkv, qkv, qkv, tri_suffix, *ride)


def _fwd_mix(y_lru, y_sb, ga, gb, w_out16, x2, tm):
    T = x2.shape[0]

    def body(yl_ref, ys_ref, ga_ref, gb_ref, w_ref, x_ref, h1_ref, mix_ref):
        na, _, _ = _rms_fwd(yl_ref[...], ga_ref[...])
        nb, _, _ = _rms_fwd(ys_ref[...], gb_ref[...])
        na = na.astype(BF16)
        nb = nb.astype(BF16)
        mix_ref[:, 0:LRU_W] = na
        mix_ref[:, LRU_W:D_MODEL] = nb
        h1_ref[...] = x_ref[...] + _dot(na, w_ref[0:LRU_W, :]) + _dot(nb, w_ref[LRU_W:D_MODEL, :])

    row = lambda i: (i, 0)
    fix = lambda i: (0, 0)
    return pl.pallas_call(
        body, name="fwd_mix", grid=(T // tm,),
        in_specs=[pl.BlockSpec((tm, LRU_W), row), pl.BlockSpec((tm, SB_W), row),
                  pl.BlockSpec((1, LRU_W), fix), pl.BlockSpec((1, SB_W), fix),
                  pl.BlockSpec((D_MODEL, D_MODEL), fix), pl.BlockSpec((tm, D_MODEL), row)],
        out_specs=[pl.BlockSpec((tm, D_MODEL), row), pl.BlockSpec((tm, D_MODEL), row)],
        out_shape=[jax.ShapeDtypeStruct((T, D_MODEL), F32), jax.ShapeDtypeStruct((T, D_MODEL), BF16)],
        compiler_params=_params(("arbitrary",), 48),
    )(y_lru, y_sb, ga, gb, w_out16, x2)


def _fwd_mlp(h1, g2, w_up16, w_down16, gf, tgt, tm):
    T = h1.shape[0]
    nf = D_FF // (FF_CHUNK * FF_PER_STEP)

    def body(h1_ref, g2_ref, wu_ref, wd_ref, gf_ref, t_ref, up_ref, dh2_ref, dgf_ref, loss_ref, hn_s, acc):
        i, j = pl.program_id(0), pl.program_id(1)

        @pl.when(j == 0)
        def _():
            h1v = h1_ref[...]
            hn, _, _ = _rms_fwd(h1v, g2_ref[...])
            hn_s[...] = hn.astype(BF16)
            acc[...] = h1v

        down = None
        for c in range(FF_PER_STEP):
            cols = slice(c * FF_CHUNK, (c + 1) * FF_CHUNK)
            up = jnp.maximum(_dot(hn_s[...], wu_ref[c]), 0.0)
            up_ref[:, cols] = up.astype(BF16)
            part = _dot((up * up).astype(BF16), wd_ref[cols, :])
            down = part if down is None else down + part
        acc[...] += down

        @pl.when((i == 0) & (j == 0))
        def _():
            dgf_ref[...] = jnp.zeros_like(dgf_ref)
            loss_ref[...] = jnp.zeros_like(loss_ref)

        @pl.when(j == nf - 1)
        def _():
            gfv = gf_ref[...]
            y, xhat, rstd = _rms_fwd(acc[...], gfv)
            err = y - t_ref[...]
            loss_ref[...] += jnp.sum(0.5 * jnp.sum(err * err, axis=-1, keepdims=True) * (1.0 / D_MODEL))
            dy = err * (1.0 / D_MODEL)
            dgf_ref[...] += jnp.sum(dy * xhat, axis=0, keepdims=True)
            dh2_ref[...] = _rms_bwd(dy, xhat, rstd, gfv)

    row = lambda i, j: (i, 0)
    fix = lambda i, j: (0, 0)
    return pl.pallas_call(
        body, name="fwd_mlp", grid=(T // tm, nf),
        in_specs=[pl.BlockSpec((tm, D_MODEL), row), pl.BlockSpec((1, D_MODEL), fix),
                  pl.BlockSpec((FF_PER_STEP, D_MODEL, FF_CHUNK), lambda i, j: (j, 0, 0)),
                  pl.BlockSpec((FF_PER_STEP * FF_CHUNK, D_MODEL), lambda i, j: (j, 0)),
                  pl.BlockSpec((1, D_MODEL), fix), pl.BlockSpec((tm, D_MODEL), row)],
        out_specs=[pl.BlockSpec((tm, FF_PER_STEP * FF_CHUNK), lambda i, j: (i, j)), pl.BlockSpec((tm, D_MODEL), row),
                   pl.BlockSpec((1, D_MODEL), fix), pl.BlockSpec((1, LANES), fix)],
        out_shape=[jax.ShapeDtypeStruct((T, D_FF), BF16), jax.ShapeDtypeStruct((T, D_MODEL), F32),
                   jax.ShapeDtypeStruct((1, D_MODEL), F32), jax.ShapeDtypeStruct((1, LANES), F32)],
        scratch_shapes=[pltpu.VMEM((tm, D_MODEL), BF16), pltpu.VMEM((tm, D_MODEL), F32)],
        compiler_params=_params(("arbitrary", "arbitrary"), 48),
    )(h1, g2, w_up16, w_down16, gf, tgt)


def _bwd_mlp(dh2, up16, h1, g2, w_up16, w_down16, tm):
    T = h1.shape[0]
    nf = D_FF // (FF_CHUNK * FF_PER_STEP)

    def body(dh2_ref, up_ref, h1_ref, g2_ref, wu_ref, wd_ref, dup_ref, dh1_ref, hn_ref, dh2b_ref, dg2_ref, acc):
        i, j = pl.program_id(0), pl.program_id(1)

        @pl.when(j == 0)
        def _():
            hn, _, _ = _rms_fwd(h1_ref[...], g2_ref[...])
            hn_ref[...] = hn.astype(BF16)
            dh2b_ref[...] = dh2_ref[...].astype(BF16)
            acc[...] = jnp.zeros_like(acc)

        dhn = None
        for c in range(FF_PER_STEP):
            cols = slice(c * FF_CHUNK, (c + 1) * FF_CHUNK)
            u = up_ref[:, cols].astype(F32)
            dup = (2.0 * u * _dot_nt(dh2b_ref[...], wd_ref[cols, :])).astype(BF16)
            dup_ref[:, cols] = dup
            part = _dot_nt(dup, wu_ref[c])
            dhn = part if dhn is None else dhn + part
        acc[...] += dhn

        @pl.when((i == 0) & (j == 0))
        def _():
            dg2_ref[...] = jnp.zeros_like(dg2_ref)

        @pl.when(j == nf - 1)
        def _():
            g2v = g2_ref[...]
            _, xhat, rstd = _rms_fwd(h1_ref[...], g2v)
            dhn = acc[...]
            dg2_ref[...] += jnp.sum(dhn * xhat, axis=0, keepdims=True)
            dh1_ref[...] = dh2_ref[...] + _rms_bwd(dhn, xhat, rstd, g2v)

    row = lambda i, j: (i, 0)
    fix = lambda i, j: (0, 0)
    return pl.pallas_call(
        body, name="bwd_mlp", grid=(T // tm, nf),
        in_specs=[pl.BlockSpec((tm, D_MODEL), row), pl.BlockSpec((tm, FF_PER_STEP * FF_CHUNK), lambda i, j: (i, j)),
                  pl.BlockSpec((tm, D_MODEL), row), pl.BlockSpec((1, D_MODEL), fix),
                  pl.BlockSpec((FF_PER_STEP, D_MODEL, FF_CHUNK), lambda i, j: (j, 0, 0)),
                  pl.BlockSpec((FF_PER_STEP * FF_CHUNK, D_MODEL), lambda i, j: (j, 0))],
        out_specs=[pl.BlockSpec((tm, FF_PER_STEP * FF_CHUNK), lambda i, j: (i, j)), pl.BlockSpec((tm, D_MODEL), row),
                   pl.BlockSpec((tm, D_MODEL), row), pl.BlockSpec((tm, D_MODEL), row),
                   pl.BlockSpec((1, D_MODEL), fix)],
        out_shape=[jax.ShapeDtypeStruct((T, D_FF), BF16), jax.ShapeDtypeStruct((T, D_MODEL), F32),
                   jax.ShapeDtypeStruct((T, D_MODEL), BF16), jax.ShapeDtypeStruct((T, D_MODEL), BF16),
                   jax.ShapeDtypeStruct((1, D_MODEL), F32)],
        scratch_shapes=[pltpu.VMEM((tm, D_MODEL), F32)],
        compiler_params=_params(("arbitrary", "arbitrary"), 48),
    )(dh2, up16, h1, g2, w_up16, w_down16)


def _matmul_tn(name, a, b, bm, bn, tk, out_shape, out_block, out_index, a_prep=None, b_prep=None, out_dtype=F32):
    T, M = a.shape
    N = b.shape[1]
    nk = T // tk

    def body(a_ref, b_ref, o_ref, acc):
        k = pl.program_id(2)
        av = a_ref[...] if a_prep is None else a_prep(a_ref[...])
        bv = b_ref[...] if b_prep is None else b_prep(b_ref[...])
        p = _dot_tn(av, bv)

        @pl.when(k == 0)
        def _():
            acc[...] = p

        @pl.when(k > 0)
        def _():
            acc[...] += p

        @pl.when(k == nk - 1)
        def _():
            o_ref[...] = acc[...].astype(out_dtype)

    return pl.pallas_call(
        body, name=name, grid=(M // bm, N // bn, nk),
        in_specs=[pl.BlockSpec((tk, bm), lambda m, n, k: (k, m)), pl.BlockSpec((tk, bn), lambda m, n, k: (k, n))],
        out_specs=pl.BlockSpec(out_block, out_index),
        out_shape=jax.ShapeDtypeStruct(out_shape, out_dtype),
        scratch_shapes=[pltpu.VMEM((bm, bn), F32)],
        compiler_params=_params(("arbitrary", "arbitrary", "arbitrary"), 48),
    )(a, b)


def _bwd_mix(dh1, w_out16, y_lru, y_sb, ga, gb, tm):
    T = dh1.shape[0]

    def body(d_ref, w_ref, yl_ref, ys_ref, ga_ref, gb_ref, dyl_ref, dys_ref, dga_ref, dgb_ref):
        @pl.when(pl.program_id(0) == 0)
        def _():
            dga_ref[...] = jnp.zeros_like(dga_ref)
            dgb_ref[...] = jnp.zeros_like(dgb_ref)

        d16 = d_ref[...].astype(BF16)
        for y_ref, g_ref, lo, dy_ref, dg_ref in ((yl_ref, ga_ref, 0, dyl_ref, dga_ref),
                                                 (ys_ref, gb_ref, LRU_W, dys_ref, dgb_ref)):
            gv = g_ref[...]
            dn = _dot_nt(d16, w_ref[lo:lo + LRU_W, :])
            _, xhat, rstd = _rms_fwd(y_ref[...], gv)
            dg_ref[...] += jnp.sum(dn * xhat, axis=0, keepdims=True)
            dy_ref[...] = _rms_bwd(dn, xhat, rstd, gv)

    row = lambda i: (i, 0)
    fix = lambda i: (0, 0)
    return pl.pallas_call(
        body, name="bwd_mix", grid=(T // tm,),
        in_specs=[pl.BlockSpec((tm, D_MODEL), row), pl.BlockSpec((D_MODEL, D_MODEL), fix),
                  pl.BlockSpec((tm, LRU_W), row), pl.BlockSpec((tm, SB_W), row),
                  pl.BlockSpec((1, LRU_W), fix), pl.BlockSpec((1, SB_W), fix)],
        out_specs=[pl.BlockSpec((tm, LRU_W), row), pl.BlockSpec((tm, SB_W), row),
                   pl.BlockSpec((1, LRU_W), fix), pl.BlockSpec((1, SB_W), fix)],
        out_shape=[jax.ShapeDtypeStruct((T, LRU_W), F32), jax.ShapeDtypeStruct((T, SB_W), F32),
                   jax.ShapeDtypeStruct((1, LRU_W), F32), jax.ShapeDtypeStruct((1, SB_W), F32)],
        compiler_params=_params(("arbitrary",), 48),
    )(dh1, w_out16, y_lru, y_sb, ga, gb)


def _attn_bwd(qkv, run_tab, dy_sb, tri_suffix, tri_prefix, ride, ride_sliced, B, S):
    T = B * S
    nq = S // Q_BLK
    nkb = S // K_BLK
    nhp = SB_W // LANES
    scale = HEAD_D ** -0.5

    nr = len(ride)
    ride_shape, ride_specs, ride_sems = _exchange_shapes(ride, ride_sliced)

    def body(q_ref, k_ref, v_ref, run_ref, do_ref, ts_ref, tp_ref, *rest):
        dq_ref, dk_ref, dv_ref = rest[nr:nr + 3]
        dkt_ref, dvt_ref = rest[2 * nr + 3:2 * nr + 5]
        start_ride, wait_ride = _exchange_copies(rest[:nr], rest[nr + 3:2 * nr + 3], ride_sliced, *rest[2 * nr + 5:])
        qi = pl.program_id(2)
        step_no = (pl.program_id(0) * nhp + pl.program_id(1)) * nq + qi
        pl.when(step_no == 0)(start_ride)
        jd = qi // Q_PER_K

        @pl.when(qi == 0)
        def _():
            dkt_ref[...] = jnp.zeros_like(dkt_ref)
            dvt_ref[...] = jnp.zeros_like(dvt_ref)

        causal, col, lane, halves = _attn_consts(qi)
        q = q_ref[...]
        do = do_ref[...]
        qh = [jnp.where(hm, q, jnp.zeros_like(q)) * jnp.asarray(scale, BF16) for hm in halves]
        doh = [jnp.where(hm, do, 0.0).astype(BF16) for hm in halves]
        qt = jnp.concatenate([h.astype(F32).T.astype(BF16) for h in qh], axis=1)
        dot_ = jnp.concatenate([h.astype(F32).T.astype(BF16) for h in doh], axis=1)
        tables = [run_ref[:, 0:LANES], run_ref[:, LANES:2 * LANES]]

        def group(blocks, carry):
            prefixes, dq = carry
            prefixes = list(prefixes)
            ks = [k_ref[_key_rows(jl), :] for jl, _, _ in blocks]
            vs = [v_ref[_key_rows(jl), :] for jl, _, _ in blocks]
            chains = [(b, h) for b in range(len(blocks)) for h in range(2)]
            z = {c: _dot_nt(qh[c[1]], ks[c[0]]) for c in chains}
            da = {c: _dot_nt(doh[c[1]], vs[c[0]]) for c in chains}
            lg = {c: _log1m_beta(z[c], blocks[c[0]][2]) for c in chains}
            suf = {c: _dot(lg[c].astype(BF16), ts_ref[...]) for c in chains}
            att, g = {}, {}
            for b, h in chains:
                _, jlane, mask = blocks[b]
                run = jnp.sum(jnp.where(lane == jlane, tables[h], 0.0), axis=1, keepdims=True)
                a = jnp.exp(z[b, h] + suf[b, h] + run)
                a = a if mask is None else jnp.where(mask, a, 0.0)
                g[b, h] = a * da[b, h]
                att[b, h] = a.astype(BF16)
            gpre = {c: _dot(g[c].astype(BF16), tp_ref[...]) for c in chains}
            dz = {}
            for b, h in chains:
                mask = blocks[b][2]
                d = g[b, h] - jnp.exp(z[b, h] + lg[b, h]) * (prefixes[h] + gpre[b, h])
                dz[b, h] = (d if mask is None else jnp.where(mask, d, 0.0)).astype(BF16)
                prefixes[h] = prefixes[h] + gpre[b, h][:, K_BLK - 1:K_BLK]
            for b, h in chains:
                dq = dq + _dot(dz[b, h], jnp.where(halves[h], ks[b], jnp.zeros_like(ks[b])))
            for b, (jl, _, _) in enumerate(blocks):
                dkt_ref[jl] += _dot(qt, jnp.concatenate([dz[b, 0], dz[b, 1]], axis=0))
                dvt_ref[jl] += _dot(dot_, jnp.concatenate([att[b, 0], att[b, 1]], axis=0))
            return tuple(prefixes), dq

        def run(ja, n):
            return [(ja + i, ja + i, None) for i in range(n)]

        col0 = jnp.zeros((Q_BLK, 1), F32)
        fours = (jd // 2) // 2
        carry = lax.fori_loop(0, fours, lambda it, cr: group(run(4 * it, 4), cr),
                              ((col0, col0), jnp.zeros((Q_BLK, LANES), F32)))
        carry = lax.cond((jd // 2) % 2 == 1, lambda cr: group(run(4 * fours, 2), cr), lambda cr: cr, carry)
        two = jd % 2
        before = (jnp.maximum(jd - 1, 0), jnp.where(two == 1, jd - 1, -1), col < two * K_BLK)
        dq_ref[...] = group([before, (jd, jd, causal)], carry)[1] * scale

        @pl.when(qi == nq - 1)
        def _():
            for j in range(nkb):
                dk_ref[j * K_BLK:(j + 1) * K_BLK, :] = dkt_ref[j].T
                dv_ref[j * K_BLK:(j + 1) * K_BLK, :] = dvt_ref[j].T

        pl.when(step_no == B * nhp * nq - 1)(wait_ride)

    qblk = pl.BlockSpec((Q_BLK, LANES), lambda b, hp, qi: (b * nq + qi, hp))
    tri = pl.BlockSpec((K_BLK, K_BLK), lambda b, hp, qi: (0, 0))
    seq = pl.BlockSpec((S, LANES), lambda b, hp, qi: (b, hp))
    return pl.pallas_call(
        body, name="attn_bwd", grid=(B, nhp, nq),
        in_specs=[qblk, pl.BlockSpec((S, LANES), lambda b, hp, qi: (b, nhp + hp)),
                  pl.BlockSpec((S, LANES), lambda b, hp, qi: (b, 2 * nhp + hp)),
                  pl.BlockSpec((Q_BLK, 2 * LANES), lambda b, hp, qi: (b * nq + qi, hp)), qblk, tri, tri] + ride_specs,
        out_specs=[qblk, seq, seq] + ride_specs,
        out_shape=[jax.ShapeDtypeStruct((T, SB_W), F32)] * 3 + ride_shape,
        scratch_shapes=[pltpu.VMEM((nkb, LANES, K_BLK), F32)] * 2 + ride_sems,
        compiler_params=_params(("arbitrary", "arbitrary", "arbitrary"), 48),
    )(qkv, qkv, qkv, run_tab, dy_sb, tri_suffix, tri_prefix, *ride)


def _lru_bwd(proj_lru, h, dy_lru, conv_w, conv_b, wa_bd, wx_bd, b_a, b_x, lam, B, S, lc):
    T = B * S
    nc = S // lc
    ncb = LRU_W // LANES
    hpc = lc // SUBLANES

    def body(x_ref, xh_ref, g_ref, h_ref, hh_ref, dy_ref, cw_ref, cb_ref, wa_ref, wx_ref, ba_ref, bx_ref, lam_ref,
             dx_ref, dg_ref, dcw_ref, dcb_ref, dwa_ref, dwx_ref, dba_ref, dbx_ref, dlam_ref,
             lam_s, dc_s, a_first, lam_first, dc_head):
        b, ci = pl.program_id(1), pl.program_id(2)
        first_chunk = ci == nc - 1

        @pl.when(ci == 0)
        def _():
            a_first[...] = jnp.zeros_like(a_first)
            lam_first[...] = jnp.zeros_like(lam_first)
            dc_head[...] = jnp.zeros_like(dc_head)

        @pl.when((b == 0) & (ci == 0))
        def _():
            for ref in (dcw_ref, dcb_ref, dwa_ref, dwx_ref, dba_ref, dbx_ref, dlam_ref):
                ref[...] = jnp.zeros_like(ref)

        x = x_ref[...]
        taps = _conv_taps(x, jnp.where(first_chunk, 0.0, xh_ref[...]), lc)
        c = cb_ref[...] + sum(cw_ref[pl.ds(CONV_K - 1 - k, 1), :] * taps[k] for k in range(CONV_K))
        r, i, sp, dsp_dlam, a, s = _lru_gates(c, wa_ref, wx_ref, ba_ref, bx_ref, lam_ref)
        hv = h_ref[...]
        he = jnp.concatenate([jnp.where(first_chunk, 0.0, hh_ref[...]), hv], axis=0)
        h_prev = pltpu.roll(he, 1, 0)[SUBLANES:SUBLANES + lc]
        dy = dy_ref[...]
        gelu, dgelu = _gelu_parts(g_ref[...])
        dg_ref[...] = dy * hv * dgelu

        row = lax.broadcasted_iota(jnp.int32, (lc, LANES), 0)
        a_next = jnp.where(row < lc - 1, pltpu.roll(a, lc - 1, 0), a_first[...])
        lam_loc, a_run = _scan(a_next, dy * gelu, reverse=True)
        lam_s[...] = lam_loc + a_run * lam_first[...]
        lam_t = lam_s[...]
        lam_first[...] = lam_s[pl.ds(0, 1), :]
        lam_s[...] = a
        a_first[...] = lam_s[pl.ds(0, 1), :]

        ic = i * c
        dlog_a = lam_t * h_prev * a - (lam_t * ic) * (a * a) / s
        dpre_r = (dlog_a * ((-LRU_C) * sp)) * r * (1.0 - r)
        dpre_i = (lam_t * s * c) * i * (1.0 - i)
        dlam_ref[...] += jnp.sum(dlog_a * r, axis=0, keepdims=True) * ((-LRU_C) * dsp_dlam)
        dr16 = dpre_r.astype(BF16)
        di16 = dpre_i.astype(BF16)
        c16 = c.astype(BF16)
        dwa_ref[0] += _dot_tn(c16, dr16)
        dwx_ref[0] += _dot_tn(c16, di16)
        dba_ref[...] += jnp.sum(dpre_r, axis=0, keepdims=True)
        dbx_ref[...] += jnp.sum(dpre_i, axis=0, keepdims=True)
        dc = lam_t * s * i + _dot_nt(dr16, wa_ref[0]) + _dot_nt(di16, wx_ref[0])
        dcb_ref[...] += jnp.sum(dc, axis=0, keepdims=True)
        for k in range(CONV_K):
            dcw_ref[pl.ds(CONV_K - 1 - k, 1), :] += jnp.sum(dc * taps[k], axis=0, keepdims=True)
        dce = jnp.concatenate([dc, dc_head[...]], axis=0)
        dx = cw_ref[pl.ds(CONV_K - 1, 1), :] * dc
        for k in range(1, CONV_K):
            dx = dx + cw_ref[pl.ds(CONV_K - 1 - k, 1), :] * pltpu.roll(dce, lc + SUBLANES - k, 0)[0:lc]
        dx_ref[...] = dx
        dc_s[...] = dc
        dc_head[...] = dc_s[pl.ds(0, SUBLANES), :]

    def chunk(col):
        return pl.BlockSpec((lc, LANES), lambda cb, b, ci: (b * nc + nc - 1 - ci, col(cb)))

    def halo(col):
        return pl.BlockSpec((SUBLANES, LANES),
                            lambda cb, b, ci: (jnp.maximum((b * nc + nc - 1 - ci) * hpc - 1, 0), col(cb)))

    chan = lambda cb, b, ci: (0, cb)
    blk = lambda cb, b, ci: (cb, 0, 0)
    vec = pl.BlockSpec((1, LANES), chan)
    mat = pl.BlockSpec((1, LANES, LANES), blk)
    return pl.pallas_call(
        body, name="lru_bwd", grid=(ncb, B, nc),
        in_specs=[chunk(lambda cb: cb), halo(lambda cb: cb), chunk(lambda cb: ncb + cb),
                  chunk(lambda cb: cb), halo(lambda cb: cb), chunk(lambda cb: cb),
                  pl.BlockSpec((CONV_K, LANES), chan), vec, mat, mat, vec, vec, vec],
        out_specs=[chunk(lambda cb: cb), chunk(lambda cb: cb), pl.BlockSpec((CONV_K, LANES), chan), vec,
                   mat, mat, vec, vec, vec],
        out_shape=[jax.ShapeDtypeStruct((T, LRU_W), F32), jax.ShapeDtypeStruct((T, LRU_W), F32),
                   jax.ShapeDtypeStruct((CONV_K, LRU_W), F32), jax.ShapeDtypeStruct((1, LRU_W), F32),
                   jax.ShapeDtypeStruct((ncb, LANES, LANES), F32), jax.ShapeDtypeStruct((ncb, LANES, LANES), F32),
                   jax.ShapeDtypeStruct((1, LRU_W), F32), jax.ShapeDtypeStruct((1, LRU_W), F32),
                   jax.ShapeDtypeStruct((1, LRU_W), F32)],
        scratch_shapes=[pltpu.VMEM((lc, LANES), F32), pltpu.VMEM((lc, LANES), F32), pltpu.VMEM((1, LANES), F32),
                        pltpu.VMEM((1, LANES), F32), pltpu.VMEM((SUBLANES, LANES), F32)],
        compiler_params=_params(("arbitrary", "arbitrary", "arbitrary"), 32),
    )(proj_lru, proj_lru, proj_lru, h, h, dy_lru, conv_w, conv_b, wa_bd, wx_bd, b_a, b_x, lam)


def _bwd_in(pieces, w_in16, x2, g1, dh1, tm):
    T = x2.shape[0]
    npc = len(pieces)

    def body(*refs):
        p_refs = refs[:npc]
        w_ref, x_ref, g_ref, d_ref, dx_ref, dproj_ref, xn_ref, dg1_ref = refs[npc:]

        @pl.when(pl.program_id(0) == 0)
        def _():
            dg1_ref[...] = jnp.zeros_like(dg1_ref)

        dxn = jnp.zeros((tm, D_MODEL), F32)
        for n, p_ref in enumerate(p_refs):
            cols = slice(n * LRU_W, (n + 1) * LRU_W)
            p16 = p_ref[...].astype(BF16)
            dproj_ref[:, cols] = p16
            dxn = dxn + _dot_nt(p16, w_ref[:, cols])
        gv = g_ref[...]
        xn, xhat, rstd = _rms_fwd(x_ref[...], gv)
        xn_ref[...] = xn.astype(BF16)
        dg1_ref[...] += jnp.sum(dxn * xhat, axis=0, keepdims=True)
        dx_ref[...] = d_ref[...] + _rms_bwd(dxn, xhat, rstd, gv)

    row = lambda i: (i, 0)
    fix = lambda i: (0, 0)
    return pl.pallas_call(
        body, name="bwd_in", grid=(T // tm,),
        in_specs=[pl.BlockSpec((tm, LRU_W), row)] * npc + [
            pl.BlockSpec((D_MODEL, IN_COLS), fix), pl.BlockSpec((tm, D_MODEL), row),
            pl.BlockSpec((1, D_MODEL), fix), pl.BlockSpec((tm, D_MODEL), row)],
        out_specs=[pl.BlockSpec((tm, D_MODEL), row), pl.BlockSpec((tm, IN_COLS), row),
                   pl.BlockSpec((tm, D_MODEL), row), pl.BlockSpec((1, D_MODEL), fix)],
        out_shape=[jax.ShapeDtypeStruct((T, D_MODEL), F32), jax.ShapeDtypeStruct((T, IN_COLS), BF16),
                   jax.ShapeDtypeStruct((T, D_MODEL), BF16), jax.ShapeDtypeStruct((1, D_MODEL), F32)],
        compiler_params=_params(("arbitrary",), 56),
    )(*pieces, w_in16, x2, g1, dh1)


def _adam_shard(name, parts, w, m, v, tr):
    R, C = w.shape

    def body(p_ref, w_ref, m_ref, v_ref, g_ref, d_ref, m2_ref, v2_ref):
        g = p_ref[0].astype(F32)
        for p in range(1, N_DEV):
            g = g + p_ref[p].astype(F32)
        g_ref[...] = g
        d_ref[...], m2_ref[...], v2_ref[...] = _adamw(w_ref[...], g, m_ref[...], v_ref[...])

    blk = pl.BlockSpec((tr, C), lambda i: (i, 0))
    return pl.pallas_call(
        body, name=name, grid=(R // tr,),
        in_specs=[pl.BlockSpec((N_DEV, tr, C), lambda i: (0, i, 0)), blk, blk, blk],
        out_specs=[blk] * 4, out_shape=[jax.ShapeDtypeStruct((R, C), F32)] * 4,
        compiler_params=_params(("arbitrary",), 48),
    )(parts, w, m, v)


def _adam_param(name, parts, w, m, v):
    def body(p_ref, w_ref, m_ref, v_ref, g_ref, d_ref, m2_ref, v2_ref):
        g = p_ref[0]
        for p in range(1, N_DEV):
            g = g + p_ref[p]
        g_ref[...] = g
        d_ref[...], m2_ref[...], v2_ref[...] = _adamw(w_ref[...], g, m_ref[...], v_ref[...])

    return pl.pallas_call(body, name=name, out_shape=[jax.ShapeDtypeStruct(w.shape, F32)] * 4)(parts, w, m, v)


def _sum_parts(name, parts):
    def body(p_ref, g_ref):
        g = p_ref[0]
        for p in range(1, N_DEV):
            g = g + p_ref[p]
        g_ref[...] = g

    return pl.pallas_call(body, name=name, out_shape=jax.ShapeDtypeStruct(parts.shape[1:], F32))(parts)


def _block_diag_pairs(w):
    w = w.reshape(LRU_BLOCKS // 2, 2, HEAD_D, HEAD_D)
    out = jnp.zeros((LRU_BLOCKS // 2, LANES, LANES), w.dtype)
    out = out.at[:, :HEAD_D, :HEAD_D].set(w[:, 0])
    return out.at[:, HEAD_D:, HEAD_D:].set(w[:, 1])


def _diag_blocks(w):
    return jnp.stack([w[:, :HEAD_D, :HEAD_D], w[:, HEAD_D:, HEAD_D:]], axis=1).reshape(LRU_BLOCKS, HEAD_D, HEAD_D)


def kernel(x, norm1_g, w_in, conv_w, conv_b, lru_w_a, lru_b_a, lru_w_x, lru_b_x, lru_lambda, lru_out_g, sb_out_g, w_out, norm2_g, w_up, w_down, final_g, loss_target, m_norm1_g, m_w_in, m_conv_w, m_conv_b, m_lru_w_a, m_lru_b_a, m_lru_w_x, m_lru_b_x, m_lru_lambda, m_lru_out_g, m_sb_out_g, m_w_out, m_norm2_g, m_w_up, m_w_down, m_final_g, v_norm1_g, v_w_in, v_conv_w, v_conv_b, v_lru_w_a, v_lru_b_a, v_lru_w_x, v_lru_b_x, v_lru_lambda, v_lru_out_g, v_sb_out_g, v_w_out, v_norm2_g, v_w_up, v_w_down, v_final_g):
    B, S, _ = x.shape
    T = B * S
    tm = min(512, T)
    tk = min(GRAD_TK, T)
    lc = min(512, S)
    x2 = x.reshape(T, D_MODEL)
    tgt = loss_target.reshape(T, D_MODEL)
    cw_cols = CONV_K * LRU_W // N_DEV // CONV_K

    shards16 = _cast_shards([w_in[0], w_out[0], w_up[0], w_down[0]])
    cw_pad = jnp.zeros((SUBLANES, LANES), F32).at[:CONV_K, :cw_cols].set(conv_w[0])
    g_in, g_cw = _exchange("gather_w_in", [shards16[0], cw_pad], [False, False])
    w_in16 = g_in.transpose(1, 0, 2).reshape(D_MODEL, IN_COLS)
    conv_w_full = g_cw[:, :CONV_K, :cw_cols].transpose(1, 0, 2).reshape(CONV_K, LRU_W)
    wa_bd = _block_diag_pairs(lru_w_a[0]).astype(BF16)
    wx_bd = _block_diag_pairs(lru_w_x[0]).astype(BF16)
    b_a = lru_b_a.reshape(1, LRU_W)
    b_x = lru_b_x.reshape(1, LRU_W)
    gf = final_g.reshape(1, D_MODEL)

    proj_lru, qkv = _fwd_in(x2, norm1_g, w_in16, tm)
    y_lru, h = _lru_fwd(proj_lru, conv_w_full, conv_b, wa_bd, wx_bd, b_a, b_x, lru_lambda, B, S, lc)
    tri_suffix, tri_prefix = _tri(False), _tri(True)
    y_sb, run_tab, g_out, g_up, g_down = _attn_fwd(qkv, tri_suffix, list(shards16[1:]), [False] * 3, B, S)
    w_out16 = g_out.reshape(D_MODEL, D_MODEL)
    w_down16 = g_down.reshape(D_FF, D_MODEL)
    h1, mix16 = _fwd_mix(y_lru, y_sb, lru_out_g, sb_out_g, w_out16, x2, tm)
    up16, dh2, d_final_g, loss_part = _fwd_mlp(h1, norm2_g, g_up, w_down16, gf, tgt, tm)

    dup16, dh1, hn16, dh2b, d_norm2_g = _bwd_mlp(dh2, up16, h1, norm2_g, g_up, w_down16, tm)
    sq = lambda u: (u.astype(F32) * u.astype(F32)).astype(BF16)
    gw_up = _matmul_tn("grad_w_up", hn16, dup16, D_MODEL, FF_CHUNK, tk, (N_DEV, D_MODEL, FF_CHUNK),
                       (None, D_MODEL, FF_CHUNK), lambda m, n, k: (n, 0, 0))
    gw_down = _matmul_tn("grad_w_down", up16, dh2b, FF_CHUNK, D_MODEL, tk, (N_DEV, FF_CHUNK, D_MODEL),
                         (None, FF_CHUNK, D_MODEL), lambda m, n, k: (m, 0, 0), a_prep=sq)
    dy_lru, dy_sb, d_lru_out_g, d_sb_out_g = _bwd_mix(dh1, w_out16, y_lru, y_sb, lru_out_g, sb_out_g, tm)
    gw_out = _matmul_tn("grad_w_out", mix16, dh1, D_MODEL, FF_CHUNK, tk, (D_MODEL, D_MODEL),
                        (D_MODEL, FF_CHUNK), lambda m, n, k: (0, n), b_prep=lambda u: u.astype(BF16))
    parts_out = gw_out.reshape(N_DEV, D_MODEL // N_DEV, D_MODEL)
    dq, dk, dv, r_out, r_up, r_down = _attn_bwd(qkv, run_tab, dy_sb, tri_suffix, tri_prefix,
                                                [parts_out, gw_up, gw_down], [True] * 3, B, S)
    (dx_lru, dg_lru, d_conv_w, d_conv_b, d_wa, d_wx, d_b_a, d_b_x, d_lambda) = _lru_bwd(
        proj_lru, h, dy_lru, conv_w_full, conv_b, wa_bd, wx_bd, b_a, b_x, lru_lambda, B, S, lc)
    dx, dproj16, xn16, d_norm1_g = _bwd_in([dx_lru, dg_lru, dq, dk, dv], w_in16, x2, norm1_g, dh1, tm)
    gw_in = _matmul_tn("grad_w_in", xn16, dproj16, D_MODEL, FF_CHUNK, tk, (D_MODEL, IN_COLS),
                       (D_MODEL, FF_CHUNK), lambda m, n, k: (0, n), out_dtype=BF16)

    small = {"norm1_g": (d_norm1_g, norm1_g, m_norm1_g, v_norm1_g), "conv_b": (d_conv_b, conv_b, m_conv_b, v_conv_b),
             "lru_w_a": (_diag_blocks(d_wa), lru_w_a, m_lru_w_a, v_lru_w_a),
             "lru_b_a": (d_b_a, lru_b_a, m_lru_b_a, v_lru_b_a),
             "lru_w_x": (_diag_blocks(d_wx), lru_w_x, m_lru_w_x, v_lru_w_x),
             "lru_b_x": (d_b_x, lru_b_x, m_lru_b_x, v_lru_b_x),
             "lru_lambda": (d_lambda, lru_lambda, m_lru_lambda, v_lru_lambda),
             "lru_out_g": (d_lru_out_g, lru_out_g, m_lru_out_g, v_lru_out_g),
             "sb_out_g": (d_sb_out_g, sb_out_g, m_sb_out_g, v_sb_out_g),
             "norm2_g": (d_norm2_g, norm2_g, m_norm2_g, v_norm2_g),
             "final_g": (d_final_g, final_g, m_final_g, v_final_g)}
    names = list(small)

    def held(n, a):
        return a.reshape((1, D_MODEL) if n == "final_g" else small[n][1].shape)

    parts_in = gw_in.reshape(D_MODEL, N_DEV, IN_COLS // N_DEV).transpose(1, 0, 2)
    parts_cw = d_conv_w.reshape(CONV_K, N_DEV, cw_cols).transpose(1, 0, 2)[:, None]
    got = _exchange("exchange_grads", [parts_in, parts_cw] + [held(n, small[n][0]) for n in names] + [loss_part],
                    [True, True] + [False] * (len(names) + 1))

    out = {"w_in": _adam_shard("adam_w_in", got[0], w_in[0], m_w_in[0], v_w_in[0], 256),
           "w_out": _adam_shard("adam_w_out", r_out, w_out[0], m_w_out[0], v_w_out[0], 64),
           "w_up": _adam_shard("adam_w_up", r_up, w_up[0], m_w_up[0], v_w_up[0], 256),
           "w_down": _adam_shard("adam_w_down", r_down, w_down[0], m_w_down[0], v_w_down[0], 128)}
    out = {n: [a[None] for a in res] for n, res in out.items()}
    out["conv_w"] = _adam_param("adam_conv_w", got[1], conv_w, m_conv_w, v_conv_w)
    for n, parts in zip(names, got[2:-1]):
        _, w, m, v = small[n]
        res = _adam_param("adam_" + n, parts, held(n, w), held(n, m), held(n, v))
        out[n] = [a.reshape(w.shape) for a in res]
    loss = _sum_parts("sum_loss", got[-1])[0, 0]
    weights = ["norm1_g", "w_in", "conv_w", "conv_b", "lru_w_a", "lru_b_a", "lru_w_x", "lru_b_x", "lru_lambda",
               "lru_out_g", "sb_out_g", "w_out", "norm2_g", "w_up", "w_down", "final_g"]
    return (loss, dx.reshape(B, S, D_MODEL), *[out[n][0] for n in weights], *[out[n][1] for n in weights],
            *[out[n][2] for n in weights], *[out[n][3] for n in weights])
```

```python
import jax
import jax.numpy as jnp
from jax import lax
from jax.experimental import pallas as pl
from jax.experimental.pallas import tpu as pltpu

F32 = jnp.float32
BF16 = jnp.bfloat16

D_MODEL = 1024
LRU_W = 512
SB_W = 512
HEAD_D = 64
D_FF = 4096
IN_COLS = 2 * LRU_W + 3 * SB_W
CONV_K = 4
LRU_BLOCKS = 8
LRU_C = 8.0
EPS = 1e-6
N_DEV = 8
LANES = 128
SUBLANES = 8
FF_CHUNK = 512
FF_PER_STEP = 4
GRAD_TK = 2048
Q_BLK = 256
K_BLK = 256
Q_PER_K = K_BLK // Q_BLK

ADAM_LR = 0.001
ADAM_B1 = 0.9
ADAM_B2 = 0.999
ADAM_EPS = 1e-08
ADAM_WD = 0.01
ADAM_STEP = 10


def _params(sem=None, vmem_mb=None):
    kw = {}
    if sem is not None:
        kw["dimension_semantics"] = sem
    if vmem_mb is not None:
        kw["vmem_limit_bytes"] = vmem_mb << 20
    return pltpu.CompilerParams(**kw)


def _dot(a, b):
    return jnp.dot(a, b, preferred_element_type=F32)


def _dot_nt(a, b):
    return lax.dot_general(a, b, (((1,), (1,)), ((), ())), preferred_element_type=F32)


def _dot_tn(a, b):
    return lax.dot_general(a, b, (((0,), (0,)), ((), ())), preferred_element_type=F32)


def _rms_fwd(x, g):
    rstd = lax.rsqrt(jnp.mean(x * x, axis=-1, keepdims=True) + EPS)
    xhat = x * rstd
    return xhat * g, xhat, rstd


def _rms_bwd(dy, xhat, rstd, g):
    dxhat = dy * g
    return rstd * (dxhat - xhat * jnp.mean(dxhat * xhat, axis=-1, keepdims=True))


def _sigmoid(x):
    return 1.0 / (1.0 + jnp.exp(-x))


def _log1p_pos(e):
    series = e * (1.0 - e * (0.5 - e * (1.0 / 3.0 - e * 0.25)))
    return jnp.where(e < 1e-2, series, jnp.log(1.0 + e))


def _neg_expm1(x):
    series = -x * (1.0 + x * (0.5 + x * (1.0 / 6.0 + x * (1.0 / 24.0))))
    return jnp.where(x > -1e-2, series, 1.0 - jnp.exp(x))


def _gelu_parts(g):
    k0 = 0.7978845608028654
    k1 = 0.044715
    t = jnp.tanh(k0 * (g + k1 * g * g * g))
    val = 0.5 * g * (1.0 + t)
    grad = 0.5 * (1.0 + t) + 0.5 * g * (1.0 - t * t) * k0 * (1.0 + 3.0 * k1 * g * g)
    return val, grad


def _scan(a, b, reverse):
    n = a.shape[0]
    row = lax.broadcasted_iota(jnp.int32, a.shape, 0)
    s = 1
    while s < n:
        if reverse:
            keep = row < n - s
            shift = n - s
        else:
            keep = row >= s
            shift = s
        bs = jnp.where(keep, pltpu.roll(b, shift, 0), 0.0)
        a_s = jnp.where(keep, pltpu.roll(a, shift, 0), 1.0)
        b = a * bs + b
        a = a * a_s
        s *= 2
    return b, a


def _adamw(w, g, m, v):
    m = ADAM_B1 * m + (1.0 - ADAM_B1) * g
    v = ADAM_B2 * v + (1.0 - ADAM_B2) * (g * g)
    m_hat = m / (1.0 - ADAM_B1 ** ADAM_STEP)
    v_hat = v / (1.0 - ADAM_B2 ** ADAM_STEP)
    delta = -ADAM_LR * (m_hat / (jnp.sqrt(v_hat) + ADAM_EPS) + ADAM_WD * w)
    return delta, m, v


def _my_index():
    return 4 * lax.axis_index("x") + 2 * lax.axis_index("y") + lax.axis_index("c")


def _peer(k):
    x, y, c = lax.axis_index("x"), lax.axis_index("y"), lax.axis_index("c")
    px = 1 - x if (k >> 2) & 1 else x
    py = 1 - y if (k >> 1) & 1 else y
    pc = 1 - c if k & 1 else c
    return (px, py, pc), 4 * px + 2 * py + pc


def _exchange_shapes(srcs, sliced):
    n = len(srcs)
    out_shape = [jax.ShapeDtypeStruct(s.shape if sl else (N_DEV,) + s.shape, s.dtype) for s, sl in zip(srcs, sliced)]
    specs = [pl.BlockSpec(memory_space=pl.ANY)] * n
    sems = [pltpu.SemaphoreType.DMA((n, N_DEV - 1)), pltpu.SemaphoreType.DMA((n, N_DEV - 1)),
            pltpu.SemaphoreType.DMA((n,))]
    return out_shape, specs, sems


def _exchange_copies(ins, outs, sliced, send_sems, recv_sems, local_sems):
    n = len(ins)

    def part(a, p):
        return ins[a].at[p] if sliced[a] else ins[a]

    def copies(receiving):
        me = _my_index()
        local = [pltpu.make_async_copy(part(a, me), outs[a].at[me], local_sems.at[a]) for a in range(n)]
        remote = []
        for k in range(1, N_DEV):
            dev, idx = _peer(k)
            for a in range(n):
                remote.append(pltpu.make_async_remote_copy(
                    src_ref=part(a, idx), dst_ref=outs[a].at[idx if receiving else me],
                    send_sem=send_sems.at[a, k - 1], recv_sem=recv_sems.at[a, k - 1],
                    device_id=dev, device_id_type=pl.DeviceIdType.MESH))
        return local, remote

    def start():
        local, remote = copies(receiving=False)
        for cp in local + remote:
            cp.start()

    def wait():
        local, remote = copies(receiving=True)
        for cp in remote + local:
            cp.wait()

    return start, wait


def _exchange(name, srcs, sliced):
    n = len(srcs)
    out_shape, specs, sems = _exchange_shapes(srcs, sliced)

    def body(*refs):
        start, wait = _exchange_copies(refs[:n], refs[n:2 * n], sliced, *refs[2 * n:])
        start()
        wait()

    return pl.pallas_call(body, name=name, out_shape=out_shape, in_specs=specs, out_specs=specs,
                          scratch_shapes=sems)(*srcs)


def _cast_shards(ws):
    def body(*refs):
        for i in range(len(ws)):
            refs[len(ws) + i][...] = refs[i][...].astype(BF16)

    return pl.pallas_call(
        body, name="cast_shards", out_shape=[jax.ShapeDtypeStruct(w.shape, BF16) for w in ws],
        compiler_params=_params(vmem_mb=32),
    )(*ws)


def _fwd_in(x2, g1, w_in16t, tm):
    T = x2.shape[0]

    def body(x_ref, g_ref, w_ref, lru_ref, qkv_ref):
        xn, _, _ = _rms_fwd(x_ref[...], g_ref[...])
        xn = xn.astype(BF16)
        lru_ref[...] = _dot_nt(xn, w_ref[0:2 * LRU_W, :])
        qkv_ref[...] = _dot_nt(xn, w_ref[2 * LRU_W:IN_COLS, :]).astype(BF16)

    return pl.pallas_call(
        body, name="fwd_in", grid=(T // tm,),
        in_specs=[pl.BlockSpec((tm, D_MODEL), lambda i: (i, 0)),
                  pl.BlockSpec((1, D_MODEL), lambda i: (0, 0)),
                  pl.BlockSpec((IN_COLS, D_MODEL), lambda i: (0, 0))],
        out_specs=[pl.BlockSpec((tm, 2 * LRU_W), lambda i: (i, 0)),
                   pl.BlockSpec((tm, 3 * SB_W), lambda i: (i, 0))],
        out_shape=[jax.ShapeDtypeStruct((T, 2 * LRU_W), F32), jax.ShapeDtypeStruct((T, 3 * SB_W), BF16)],
        compiler_params=_params(("arbitrary",), 48),
    )(x2, g1, w_in16t)


def _lru_gates(c, wa_ref, wx_ref, ba_ref, bx_ref, lam_ref):
    c16 = c.astype(BF16)
    r = _sigmoid(_dot(c16, wa_ref[0]) + ba_ref[...])
    i = _sigmoid(_dot(c16, wx_ref[0]) + bx_ref[...])
    lam = lam_ref[...]
    e = jnp.exp(-jnp.abs(lam))
    sp = jnp.maximum(-lam, 0.0) + _log1p_pos(e)
    dsp_dlam = -jnp.where(lam >= 0.0, e, 1.0) / (1.0 + e)
    log_a = (-LRU_C) * r * sp
    a = jnp.exp(log_a)
    s = jnp.sqrt(_neg_expm1(2.0 * log_a))
    return r, i, sp, dsp_dlam, a, s


def _conv_taps(x, halo, lc):
    xe = jnp.concatenate([halo, x], axis=0)
    return [x] + [pltpu.roll(xe, k, 0)[SUBLANES:SUBLANES + lc] for k in range(1, CONV_K)]


def _lru_fwd(proj_lru, conv_w, conv_b, wa_bd, wx_bd, b_a, b_x, lam, B, S, lc):
    T = B * S
    nc = S // lc
    ncb = LRU_W // LANES

    def body(x_ref, g_ref, cw_ref, cb_ref, wa_ref, wx_ref, ba_ref, bx_ref, lam_ref, y_ref, h_ref, tail, carry):
        ci = pl.program_id(2)

        @pl.when(ci == 0)
        def _():
            tail[...] = jnp.zeros_like(tail)
            carry[...] = jnp.zeros_like(carry)

        x = x_ref[...]
        taps = _conv_taps(x, tail[...], lc)
        c = cb_ref[...] + sum(cw_ref[pl.ds(CONV_K - 1 - k, 1), :] * taps[k] for k in range(CONV_K))
        tail[...] = x_ref[pl.ds(lc - SUBLANES, SUBLANES), :]
        r, i, sp, _, a, s = _lru_gates(c, wa_ref, wx_ref, ba_ref, bx_ref, lam_ref)
        h_loc, a_run = _scan(a, s * (i * c), reverse=False)
        h_ref[...] = h_loc + a_run * carry[...]
        carry[...] = h_ref[pl.ds(lc - 1, 1), :]
        gelu, _ = _gelu_parts(g_ref[...])
        y_ref[...] = h_ref[...] * gelu

    chan = lambda b, cb, ci: (0, cb)
    return pl.pallas_call(
        body, name="lru_fwd", grid=(B, ncb, nc),
        in_specs=[pl.BlockSpec((lc, LANES), lambda b, cb, ci: (b * nc + ci, cb)),
                  pl.BlockSpec((lc, LANES), lambda b, cb, ci: (b * nc + ci, ncb + cb)),
                  pl.BlockSpec((CONV_K, LANES), chan), pl.BlockSpec((1, LANES), chan),
                  pl.BlockSpec((1, LANES, LANES), lambda b, cb, ci: (cb, 0, 0)),
                  pl.BlockSpec((1, LANES, LANES), lambda b, cb, ci: (cb, 0, 0)),
                  pl.BlockSpec((1, LANES), chan), pl.BlockSpec((1, LANES), chan), pl.BlockSpec((1, LANES), chan)],
        out_specs=[pl.BlockSpec((lc, LANES), lambda b, cb, ci: (b * nc + ci, cb))] * 2,
        out_shape=[jax.ShapeDtypeStruct((T, LRU_W), F32)] * 2,
        scratch_shapes=[pltpu.VMEM((SUBLANES, LANES), F32), pltpu.VMEM((1, LANES), F32)],
        compiler_params=_params(("arbitrary", "arbitrary", "arbitrary"), 32),
    )(proj_lru, proj_lru, conv_w, conv_b, wa_bd, wx_bd, b_a, b_x, lam)


def _tri(prefix):
    r = lax.broadcasted_iota(jnp.int32, (K_BLK, K_BLK), 0)
    c = lax.broadcasted_iota(jnp.int32, (K_BLK, K_BLK), 1)
    return ((r <= c) if prefix else (r >= c)).astype(BF16)


def _attn_consts(qi):
    r = lax.broadcasted_iota(jnp.int32, (Q_BLK, K_BLK), 0)
    c = lax.broadcasted_iota(jnp.int32, (Q_BLK, K_BLK), 1)
    causal = c + ((qi // Q_PER_K) * K_BLK - qi * Q_BLK) < r
    lane = lax.broadcasted_iota(jnp.int32, (1, LANES), 1)
    return causal, lane, (lane < HEAD_D, lane >= HEAD_D)


def _log1m_beta(z, mask):
    lg = -(jnp.maximum(z, 0.0) + jnp.log(1.0 + jnp.exp(-jnp.abs(z))))
    return lg if mask is None else jnp.where(mask, lg, 0.0)


def _key_rows(j):
    return pl.ds(pl.multiple_of(j * K_BLK, K_BLK), K_BLK)


def _attn_fwd(qkv, tri_suffix, ride, ride_sliced, B, S):
    T = B * S
    nq = S // Q_BLK
    nhp = SB_W // LANES
    scale = HEAD_D ** -0.5
    assert S // K_BLK <= LANES
    nr = len(ride)
    ride_shape, ride_specs, ride_sems = _exchange_shapes(ride, ride_sliced)

    def body(q_ref, k_ref, v_ref, tri_ref, *rest):
        o_ref, run_ref = rest[nr:nr + 2]
        start_ride, wait_ride = _exchange_copies(rest[:nr], rest[nr + 2:2 * nr + 2], ride_sliced, *rest[2 * nr + 2:])
        qi = pl.program_id(2)
        step_no = (pl.program_id(0) * nhp + pl.program_id(1)) * nq + qi
        pl.when(step_no == 0)(start_ride)
        jd = qi // Q_PER_K
        causal, lane, halves = _attn_consts(qi)
        q = q_ref[...]
        qh = [jnp.where(hm, q, jnp.zeros_like(q)) * jnp.asarray(scale, BF16) for hm in halves]

        def group(blocks, carry):
            runs, tables, acc = carry
            runs, tables = list(runs), list(tables)
            ks = [k_ref[_key_rows(jl), :] for jl, _, _ in blocks]
            vs = [v_ref[_key_rows(jl), :] for jl, _, _ in blocks]
            chains = [(b, h) for b in range(len(blocks)) for h in range(2)]
            z = {c: _dot_nt(qh[c[1]], ks[c[0]]) for c in chains}
            lg = {c: _log1m_beta(z[c], blocks[c[0]][2]) for c in chains}
            suf = {c: _dot(lg[c].astype(BF16), tri_ref[...]) for c in chains}
            att = {}
            for b, h in chains:
                _, jlane, mask = blocks[b]
                a = jnp.exp(z[b, h] + suf[b, h] + runs[h])
                att[b, h] = (a if mask is None else jnp.where(mask, a, 0.0)).astype(BF16)
                tables[h] = jnp.where(lane == jlane, runs[h], tables[h])
                runs[h] = runs[h] + suf[b, h][:, 0:1]
            for b, h in chains:
                acc = acc + _dot(att[b, h], jnp.where(halves[h], vs[b], jnp.zeros_like(vs[b])))
            return tuple(runs), tuple(tables), acc

        col0 = jnp.zeros((Q_BLK, 1), F32)
        zero = jnp.zeros((Q_BLK, LANES), F32)
        start = ((col0, col0), (zero, zero), zero)
        two = jd % 2
        carry = lax.cond(two == 1, lambda: group([(jd, jd, causal), (jd - 1, jd - 1, None)], start),
                         lambda: group([(jd, jd, causal)], start))
        top = jd - 1 - two

        def run(ja, n):
            return [(ja - i, ja - i, None) for i in range(n)]

        odd_pair = ((top + 1) // 2) % 2
        carry = lax.cond(odd_pair == 1, lambda cr: group(run(top, 2), cr), lambda cr: cr, carry)
        top4 = top - 2 * odd_pair
        _, tables, acc = lax.fori_loop(0, (top4 + 1) // 4, lambda it, cr: group(run(top4 - 4 * it, 4), cr), carry)
        o_ref[...] = acc
        run_ref[:, 0:LANES] = tables[0]
        run_ref[:, LANES:2 * LANES] = tables[1]
        pl.when(step_no == B * nhp * nq - 1)(wait_ride)

    return pl.pallas_call(
        body, name="attn_fwd", grid=(B, nhp, nq),
        in_specs=[pl.BlockSpec((Q_BLK, LANES), lambda b, hp, qi: (b * nq + qi, hp)),
                  pl.BlockSpec((S, LANES), lambda b, hp, qi: (b, nhp + hp)),
                  pl.BlockSpec((S, LANES), lambda b, hp, qi: (b, 2 * nhp + hp)),
                  pl.BlockSpec((K_BLK, K_BLK), lambda b, hp, qi: (0, 0))] + ride_specs,
        out_specs=[pl.BlockSpec((Q_BLK, LANES), lambda b, hp, qi: (b * nq + qi, hp)),
                   pl.BlockSpec((Q_BLK, 2 * LANES), lambda b, hp, qi: (b * nq + qi, hp))] + ride_specs,
        out_shape=[jax.ShapeDtypeStruct((T, SB_W), F32), jax.ShapeDtypeStruct((T, 2 * SB_W), F32)] + ride_shape,
        scratch_shapes=ride_sems,
        compiler_params=_params(("arbitrary", "arbitrary", "arbitrary"), 48),
    )(qkv, qkv, qkv, tri_suffix, *ride)


def _fwd_mix(y_lru, y_sb, ga, gb, w_out16, x2, tm):
    T = x2.shape[0]

    def body(yl_ref, ys_ref, ga_ref, gb_ref, w_ref, x_ref, h1_ref, mix_ref):
        na, _, _ = _rms_fwd(yl_ref[...], ga_ref[...])
        nb, _, _ = _rms_fwd(ys_ref[...], gb_ref[...])
        na = na.astype(BF16)
        nb = nb.astype(BF16)
        mix_ref[:, 0:LRU_W] = na
        mix_ref[:, LRU_W:D_MODEL] = nb
        h1_ref[...] = x_ref[...] + _dot(na, w_ref[0:LRU_W, :]) + _dot(nb, w_ref[LRU_W:D_MODEL, :])

    row = lambda i: (i, 0)
    fix = lambda i: (0, 0)
    return pl.pallas_call(
        body, name="fwd_mix", grid=(T // tm,),
        in_specs=[pl.BlockSpec((tm, LRU_W), row), pl.BlockSpec((tm, SB_W), row),
                  pl.BlockSpec((1, LRU_W), fix), pl.BlockSpec((1, SB_W), fix),
                  pl.BlockSpec((D_MODEL, D_MODEL), fix), pl.BlockSpec((tm, D_MODEL), row)],
        out_specs=[pl.BlockSpec((tm, D_MODEL), row), pl.BlockSpec((tm, D_MODEL), row)],
        out_shape=[jax.ShapeDtypeStruct((T, D_MODEL), F32), jax.ShapeDtypeStruct((T, D_MODEL), BF16)],
        compiler_params=_params(("arbitrary",), 48),
    )(y_lru, y_sb, ga, gb, w_out16, x2)


def _fwd_mlp(h1, g2, w_up16, w_down16, gf, tgt, tm):
    T = h1.shape[0]
    nf = D_FF // (FF_CHUNK * FF_PER_STEP)

    def body(h1_ref, g2_ref, wu_ref, wd_ref, gf_ref, t_ref, up_ref, dh2_ref, dgf_ref, loss_ref, hn_s, acc):
        i, j = pl.program_id(0), pl.program_id(1)

        @pl.when(j == 0)
        def _():
            h1v = h1_ref[...]
            hn, _, _ = _rms_fwd(h1v, g2_ref[...])
            hn_s[...] = hn.astype(BF16)
            acc[...] = h1v

        down = None
        for c in range(FF_PER_STEP):
            cols = slice(c * FF_CHUNK, (c + 1) * FF_CHUNK)
            up = jnp.maximum(_dot(hn_s[...], wu_ref[c]), 0.0)
            up_ref[:, cols] = up.astype(BF16)
            part = _dot((up * up).astype(BF16), wd_ref[cols, :])
            down = part if down is None else down + part
        acc[...] += down

        @pl.when((i == 0) & (j == 0))
        def _():
            dgf_ref[...] = jnp.zeros_like(dgf_ref)
            loss_ref[...] = jnp.zeros_like(loss_ref)

        @pl.when(j == nf - 1)
        def _():
            gfv = gf_ref[...]
            y, xhat, rstd = _rms_fwd(acc[...], gfv)
            err = y - t_ref[...]
            loss_ref[...] += jnp.sum(0.5 * jnp.sum(err * err, axis=-1, keepdims=True) * (1.0 / D_MODEL))
            dy = err * (1.0 / D_MODEL)
            dgf_ref[...] += jnp.sum(dy * xhat, axis=0, keepdims=True)
            dh2_ref[...] = _rms_bwd(dy, xhat, rstd, gfv)

    row = lambda i, j: (i, 0)
    fix = lambda i, j: (0, 0)
    return pl.pallas_call(
        body, name="fwd_mlp", grid=(T // tm, nf),
        in_specs=[pl.BlockSpec((tm, D_MODEL), row), pl.BlockSpec((1, D_MODEL), fix),
                  pl.BlockSpec((FF_PER_STEP, D_MODEL, FF_CHUNK), lambda i, j: (j, 0, 0)),
                  pl.BlockSpec((FF_PER_STEP * FF_CHUNK, D_MODEL), lambda i, j: (j, 0)),
                  pl.BlockSpec((1, D_MODEL), fix), pl.BlockSpec((tm, D_MODEL), row)],
        out_specs=[pl.BlockSpec((tm, FF_PER_STEP * FF_CHUNK), lambda i, j: (i, j)), pl.BlockSpec((tm, D_MODEL), row),
                   pl.BlockSpec((1, D_MODEL), fix), pl.BlockSpec((1, LANES), fix)],
        out_shape=[jax.ShapeDtypeStruct((T, D_FF), BF16), jax.ShapeDtypeStruct((T, D_MODEL), F32),
                   jax.ShapeDtypeStruct((1, D_MODEL), F32), jax.ShapeDtypeStruct((1, LANES), F32)],
        scratch_shapes=[pltpu.VMEM((tm, D_MODEL), BF16), pltpu.VMEM((tm, D_MODEL), F32)],
        compiler_params=_params(("arbitrary", "arbitrary"), 48),
    )(h1, g2, w_up16, w_down16, gf, tgt)


def _bwd_mlp(dh2, up16, h1, g2, w_up16, w_down16, tm):
    T = h1.shape[0]
    nf = D_FF // (FF_CHUNK * FF_PER_STEP)

    def body(dh2_ref, up_ref, h1_ref, g2_ref, wu_ref, wd_ref, dup_ref, dh1_ref, hn_ref, dh2b_ref, dg2_ref, acc):
        i, j = pl.program_id(0), pl.program_id(1)

        @pl.when(j == 0)
        def _():
            hn, _, _ = _rms_fwd(h1_ref[...], g2_ref[...])
            hn_ref[...] = hn.astype(BF16)
            dh2b_ref[...] = dh2_ref[...].astype(BF16)
            acc[...] = jnp.zeros_like(acc)

        dhn = None
        for c in range(FF_PER_STEP):
            cols = slice(c * FF_CHUNK, (c + 1) * FF_CHUNK)
            u = up_ref[:, cols].astype(F32)
            dup = (2.0 * u * _dot_nt(dh2b_ref[...], wd_ref[cols, :])).astype(BF16)
            dup_ref[:, cols] = dup
            part = _dot_nt(dup, wu_ref[c])
            dhn = part if dhn is None else dhn + part
        acc[...] += dhn

        @pl.when((i == 0) & (j == 0))
        def _():
            dg2_ref[...] = jnp.zeros_like(dg2_ref)

        @pl.when(j == nf - 1)
        def _():
            g2v = g2_ref[...]
            _, xhat, rstd = _rms_fwd(h1_ref[...], g2v)
            dhn = acc[...]
            dg2_ref[...] += jnp.sum(dhn * xhat, axis=0, keepdims=True)
            dh1_ref[...] = dh2_ref[...] + _rms_bwd(dhn, xhat, rstd, g2v)

    row = lambda i, j: (i, 0)
    fix = lambda i, j: (0, 0)
    return pl.pallas_call(
        body, name="bwd_mlp", grid=(T // tm, nf),
        in_specs=[pl.BlockSpec((tm, D_MODEL), row), pl.BlockSpec((tm, FF_PER_STEP * FF_CHUNK), lambda i, j: (i, j)),
                  pl.BlockSpec((tm, D_MODEL), row), pl.BlockSpec((1, D_MODEL), fix),
                  pl.BlockSpec((FF_PER_STEP, D_MODEL, FF_CHUNK), lambda i, j: (j, 0, 0)),
                  pl.BlockSpec((FF_PER_STEP * FF_CHUNK, D_MODEL), lambda i, j: (j, 0))],
        out_specs=[pl.BlockSpec((tm, FF_PER_STEP * FF_CHUNK), lambda i, j: (i, j)), pl.BlockSpec((tm, D_MODEL), row),
                   pl.BlockSpec((tm, D_MODEL), row), pl.BlockSpec((tm, D_MODEL), row),
                   pl.BlockSpec((1, D_MODEL), fix)],
        out_shape=[jax.ShapeDtypeStruct((T, D_FF), BF16), jax.ShapeDtypeStruct((T, D_MODEL), F32),
                   jax.ShapeDtypeStruct((T, D_MODEL), BF16), jax.ShapeDtypeStruct((T, D_MODEL), BF16),
                   jax.ShapeDtypeStruct((1, D_MODEL), F32)],
        scratch_shapes=[pltpu.VMEM((tm, D_MODEL), F32)],
        compiler_params=_params(("arbitrary", "arbitrary"), 48),
    )(dh2, up16, h1, g2, w_up16, w_down16)


def _matmul_tn(name, a, b, bm, bn, tk, out_shape, out_block, out_index, a_prep=None, b_prep=None, out_dtype=F32,
               ride=None, ride_sliced=None):
    T, M = a.shape
    N = b.shape[1]
    grid = (M // bm, N // bn, T // tk)
    ride, ride_sliced = list(ride or []), list(ride_sliced or [])
    nr = len(ride)
    ride_shape, ride_specs, ride_sems = _exchange_shapes(ride, ride_sliced)

    def body(a_ref, b_ref, *rest):
        o_ref, acc = rest[nr], rest[2 * nr + 1]
        k = pl.program_id(2)
        step_no = (pl.program_id(0) * grid[1] + pl.program_id(1)) * grid[2] + k
        if nr:
            start_ride, wait_ride = _exchange_copies(rest[:nr], rest[nr + 1:2 * nr + 1], ride_sliced,
                                                     *rest[2 * nr + 2:])
            pl.when(step_no == 0)(start_ride)
        av = a_ref[...] if a_prep is None else a_prep(a_ref[...])
        bv = b_ref[...] if b_prep is None else b_prep(b_ref[...])
        p = _dot_tn(av, bv)

        @pl.when(k == 0)
        def _():
            acc[...] = p

        @pl.when(k > 0)
        def _():
            acc[...] += p

        @pl.when(k == grid[2] - 1)
        def _():
            o_ref[...] = acc[...].astype(out_dtype)

        if nr:
            pl.when(step_no == grid[0] * grid[1] * grid[2] - 1)(wait_ride)

    return pl.pallas_call(
        body, name=name, grid=grid,
        in_specs=[pl.BlockSpec((tk, bm), lambda m, n, k: (k, m)), pl.BlockSpec((tk, bn), lambda m, n, k: (k, n))]
        + ride_specs,
        out_specs=[pl.BlockSpec(out_block, out_index)] + ride_specs,
        out_shape=[jax.ShapeDtypeStruct(out_shape, out_dtype)] + ride_shape,
        scratch_shapes=[pltpu.VMEM((bm, bn), F32)] + (ride_sems if nr else []),
        compiler_params=_params(("arbitrary", "arbitrary", "arbitrary"), 48),
    )(a, b, *ride)


def _bwd_mix(dh1, w_out16, y_lru, y_sb, ga, gb, tm):
    T = dh1.shape[0]

    def body(d_ref, w_ref, yl_ref, ys_ref, ga_ref, gb_ref, dyl_ref, dys_ref, dga_ref, dgb_ref):
        @pl.when(pl.program_id(0) == 0)
        def _():
            dga_ref[...] = jnp.zeros_like(dga_ref)
            dgb_ref[...] = jnp.zeros_like(dgb_ref)

        d16 = d_ref[...].astype(BF16)
        for y_ref, g_ref, lo, dy_ref, dg_ref in ((yl_ref, ga_ref, 0, dyl_ref, dga_ref),
                                                 (ys_ref, gb_ref, LRU_W, dys_ref, dgb_ref)):
            gv = g_ref[...]
            dn = _dot_nt(d16, w_ref[lo:lo + LRU_W, :])
            _, xhat, rstd = _rms_fwd(y_ref[...], gv)
            dg_ref[...] += jnp.sum(dn * xhat, axis=0, keepdims=True)
            dy_ref[...] = _rms_bwd(dn, xhat, rstd, gv)

    row = lambda i: (i, 0)
    fix = lambda i: (0, 0)
    return pl.pallas_call(
        body, name="bwd_mix", grid=(T // tm,),
        in_specs=[pl.BlockSpec((tm, D_MODEL), row), pl.BlockSpec((D_MODEL, D_MODEL), fix),
                  pl.BlockSpec((tm, LRU_W), row), pl.BlockSpec((tm, SB_W), row),
                  pl.BlockSpec((1, LRU_W), fix), pl.BlockSpec((1, SB_W), fix)],
        out_specs=[pl.BlockSpec((tm, LRU_W), row), pl.BlockSpec((tm, SB_W), row),
                   pl.BlockSpec((1, LRU_W), fix), pl.BlockSpec((1, SB_W), fix)],
        out_shape=[jax.ShapeDtypeStruct((T, LRU_W), F32), jax.ShapeDtypeStruct((T, SB_W), F32),
                   jax.ShapeDtypeStruct((1, LRU_W), F32), jax.ShapeDtypeStruct((1, SB_W), F32)],
        compiler_params=_params(("arbitrary",), 48),
    )(dh1, w_out16, y_lru, y_sb, ga, gb)


def _attn_bwd(qkv, run_tab, dy_sb, tri_suffix, tri_prefix, ride, ride_sliced, B, S):
    T = B * S
    nq = S // Q_BLK
    nkb = S // K_BLK
    nhp = SB_W // LANES
    scale = HEAD_D ** -0.5

    nr = len(ride)
    ride_shape, ride_specs, ride_sems = _exchange_shapes(ride, ride_sliced)

    def body(q_ref, k_ref, v_ref, run_ref, do_ref, ts_ref, tp_ref, *rest):
        dq_ref, dk_ref, dv_ref = rest[nr:nr + 3]
        dkt_ref, dvt_ref = rest[2 * nr + 3:2 * nr + 5]
        start_ride, wait_ride = _exchange_copies(rest[:nr], rest[nr + 3:2 * nr + 3], ride_sliced, *rest[2 * nr + 5:])
        qi = pl.program_id(2)
        step_no = (pl.program_id(0) * nhp + pl.program_id(1)) * nq + qi
        pl.when(step_no == 0)(start_ride)
        jd = qi // Q_PER_K

        @pl.when(qi == 0)
        def _():
            dkt_ref[...] = jnp.zeros_like(dkt_ref)
            dvt_ref[...] = jnp.zeros_like(dvt_ref)

        causal, lane, halves = _attn_consts(qi)
        q = q_ref[...]
        do = do_ref[...]
        qh = [jnp.where(hm, q, jnp.zeros_like(q)) * jnp.asarray(scale, BF16) for hm in halves]
        doh = [jnp.where(hm, do, 0.0).astype(BF16) for hm in halves]
        qt = jnp.concatenate([h.astype(F32).T.astype(BF16) for h in qh], axis=1)
        dot_ = jnp.concatenate([h.astype(F32).T.astype(BF16) for h in doh], axis=1)
        tables = [run_ref[:, 0:LANES], run_ref[:, LANES:2 * LANES]]

        def group(blocks, carry):
            prefixes, dq = carry
            prefixes = list(prefixes)
            ks = [k_ref[_key_rows(jl), :] for jl, _, _ in blocks]
            vs = [v_ref[_key_rows(jl), :] for jl, _, _ in blocks]
            chains = [(b, h) for b in range(len(blocks)) for h in range(2)]
            z = {c: _dot_nt(qh[c[1]], ks[c[0]]) for c in chains}
            da = {c: _dot_nt(doh[c[1]], vs[c[0]]) for c in chains}
            lg = {c: _log1m_beta(z[c], blocks[c[0]][2]) for c in chains}
            suf = {c: _dot(lg[c].astype(BF16), ts_ref[...]) for c in chains}
            att, g = {}, {}
            for b, h in chains:
                _, jlane, mask = blocks[b]
                run = jnp.sum(jnp.where(lane == jlane, tables[h], 0.0), axis=1, keepdims=True)
                a = jnp.exp(z[b, h] + suf[b, h] + run)
                a = a if mask is None else jnp.where(mask, a, 0.0)
                g[b, h] = a * da[b, h]
                att[b, h] = a.astype(BF16)
            gpre = {c: _dot(g[c].astype(BF16), tp_ref[...]) for c in chains}
            dz = {}
            for b, h in chains:
                mask = blocks[b][2]
                d = g[b, h] - jnp.exp(z[b, h] + lg[b, h]) * (prefixes[h] + gpre[b, h])
                dz[b, h] = (d if mask is None else jnp.where(mask, d, 0.0)).astype(BF16)
                prefixes[h] = prefixes[h] + gpre[b, h][:, K_BLK - 1:K_BLK]
            for b, h in chains:
                dq = dq + _dot(dz[b, h], jnp.where(halves[h], ks[b], jnp.zeros_like(ks[b])))
            for b, (jl, _, _) in enumerate(blocks):
                dkt_ref[jl] += _dot(qt, jnp.concatenate([dz[b, 0], dz[b, 1]], axis=0))
                dvt_ref[jl] += _dot(dot_, jnp.concatenate([att[b, 0], att[b, 1]], axis=0))
            return tuple(prefixes), dq

        def run(ja, n):
            return [(ja + i, ja + i, None) for i in range(n)]

        col0 = jnp.zeros((Q_BLK, 1), F32)
        fours = (jd // 2) // 2
        carry = lax.fori_loop(0, fours, lambda it, cr: group(run(4 * it, 4), cr),
                              ((col0, col0), jnp.zeros((Q_BLK, LANES), F32)))
        carry = lax.cond((jd // 2) % 2 == 1, lambda cr: group(run(4 * fours, 2), cr), lambda cr: cr, carry)
        carry = lax.cond(jd % 2 == 1, lambda cr: group([(jd - 1, jd - 1, None), (jd, jd, causal)], cr),
                         lambda cr: group([(jd, jd, causal)], cr), carry)
        dq_ref[...] = carry[1] * scale

        @pl.when(qi == nq - 1)
        def _():
            for j in range(nkb):
                dk_ref[j * K_BLK:(j + 1) * K_BLK, :] = dkt_ref[j].T
                dv_ref[j * K_BLK:(j + 1) * K_BLK, :] = dvt_ref[j].T

        pl.when(step_no == B * nhp * nq - 1)(wait_ride)

    qblk = pl.BlockSpec((Q_BLK, LANES), lambda b, hp, qi: (b * nq + qi, hp))
    tri = pl.BlockSpec((K_BLK, K_BLK), lambda b, hp, qi: (0, 0))
    seq = pl.BlockSpec((S, LANES), lambda b, hp, qi: (b, hp))
    return pl.pallas_call(
        body, name="attn_bwd", grid=(B, nhp, nq),
        in_specs=[qblk, pl.BlockSpec((S, LANES), lambda b, hp, qi: (b, nhp + hp)),
                  pl.BlockSpec((S, LANES), lambda b, hp, qi: (b, 2 * nhp + hp)),
                  pl.BlockSpec((Q_BLK, 2 * LANES), lambda b, hp, qi: (b * nq + qi, hp)), qblk, tri, tri] + ride_specs,
        out_specs=[qblk, seq, seq] + ride_specs,
        out_shape=[jax.ShapeDtypeStruct((T, SB_W), F32)] * 3 + ride_shape,
        scratch_shapes=[pltpu.VMEM((nkb, LANES, K_BLK), F32)] * 2 + ride_sems,
        compiler_params=_params(("arbitrary", "arbitrary", "arbitrary"), 48),
    )(qkv, qkv, qkv, run_tab, dy_sb, tri_suffix, tri_prefix, *ride)


def _lru_bwd(proj_lru, h, dy_lru, conv_w, conv_b, wa_bd, wx_bd, b_a, b_x, lam, B, S, lc):
    T = B * S
    nc = S // lc
    ncb = LRU_W // LANES
    hpc = lc // SUBLANES

    def body(x_ref, xh_ref, g_ref, h_ref, hh_ref, dy_ref, cw_ref, cb_ref, wa_ref, wx_ref, ba_ref, bx_ref, lam_ref,
             dx_ref, dg_ref, dcw_ref, dcb_ref, dwa_ref, dwx_ref, dba_ref, dbx_ref, dlam_ref,
             lam_s, dc_s, a_first, lam_first, dc_head):
        b, ci = pl.program_id(1), pl.program_id(2)
        first_chunk = ci == nc - 1

        @pl.when(ci == 0)
        def _():
            a_first[...] = jnp.zeros_like(a_first)
            lam_first[...] = jnp.zeros_like(lam_first)
            dc_head[...] = jnp.zeros_like(dc_head)

        @pl.when((b == 0) & (ci == 0))
        def _():
            for ref in (dcw_ref, dcb_ref, dwa_ref, dwx_ref, dba_ref, dbx_ref, dlam_ref):
                ref[...] = jnp.zeros_like(ref)

        x = x_ref[...]
        taps = _conv_taps(x, jnp.where(first_chunk, 0.0, xh_ref[...]), lc)
        c = cb_ref[...] + sum(cw_ref[pl.ds(CONV_K - 1 - k, 1), :] * taps[k] for k in range(CONV_K))
        r, i, sp, dsp_dlam, a, s = _lru_gates(c, wa_ref, wx_ref, ba_ref, bx_ref, lam_ref)
        hv = h_ref[...]
        he = jnp.concatenate([jnp.where(first_chunk, 0.0, hh_ref[...]), hv], axis=0)
        h_prev = pltpu.roll(he, 1, 0)[SUBLANES:SUBLANES + lc]
        dy = dy_ref[...]
        gelu, dgelu = _gelu_parts(g_ref[...])
        dg_ref[...] = dy * hv * dgelu

        row = lax.broadcasted_iota(jnp.int32, (lc, LANES), 0)
        a_next = jnp.where(row < lc - 1, pltpu.roll(a, lc - 1, 0), a_first[...])
        lam_loc, a_run = _scan(a_next, dy * gelu, reverse=True)
        lam_s[...] = lam_loc + a_run * lam_first[...]
        lam_t = lam_s[...]
        lam_first[...] = lam_s[pl.ds(0, 1), :]
        lam_s[...] = a
        a_first[...] = lam_s[pl.ds(0, 1), :]

        ic = i * c
        dlog_a = lam_t * h_prev * a - (lam_t * ic) * (a * a) / s
        dpre_r = (dlog_a * ((-LRU_C) * sp)) * r * (1.0 - r)
        dpre_i = (lam_t * s * c) * i * (1.0 - i)
        dlam_ref[...] += jnp.sum(dlog_a * r, axis=0, keepdims=True) * ((-LRU_C) * dsp_dlam)
        dr16 = dpre_r.astype(BF16)
        di16 = dpre_i.astype(BF16)
        c16 = c.astype(BF16)
        dwa_ref[0] += _dot_tn(c16, dr16)
        dwx_ref[0] += _dot_tn(c16, di16)
        dba_ref[...] += jnp.sum(dpre_r, axis=0, keepdims=True)
        dbx_ref[...] += jnp.sum(dpre_i, axis=0, keepdims=True)
        dc = lam_t * s * i + _dot_nt(dr16, wa_ref[0]) + _dot_nt(di16, wx_ref[0])
        dcb_ref[...] += jnp.sum(dc, axis=0, keepdims=True)
        for k in range(CONV_K):
            dcw_ref[pl.ds(CONV_K - 1 - k, 1), :] += jnp.sum(dc * taps[k], axis=0, keepdims=True)
        dce = jnp.concatenate([dc, dc_head[...]], axis=0)
        dx = cw_ref[pl.ds(CONV_K - 1, 1), :] * dc
        for k in range(1, CONV_K):
            dx = dx + cw_ref[pl.ds(CONV_K - 1 - k, 1), :] * pltpu.roll(dce, lc + SUBLANES - k, 0)[0:lc]
        dx_ref[...] = dx
        dc_s[...] = dc
        dc_head[...] = dc_s[pl.ds(0, SUBLANES), :]

    def chunk(col):
        return pl.BlockSpec((lc, LANES), lambda cb, b, ci: (b * nc + nc - 1 - ci, col(cb)))

    def halo(col):
        return pl.BlockSpec((SUBLANES, LANES),
                            lambda cb, b, ci: (jnp.maximum((b * nc + nc - 1 - ci) * hpc - 1, 0), col(cb)))

    chan = lambda cb, b, ci: (0, cb)
    blk = lambda cb, b, ci: (cb, 0, 0)
    vec = pl.BlockSpec((1, LANES), chan)
    mat = pl.BlockSpec((1, LANES, LANES), blk)
    return pl.pallas_call(
        body, name="lru_bwd", grid=(ncb, B, nc),
        in_specs=[chunk(lambda cb: cb), halo(lambda cb: cb), chunk(lambda cb: ncb + cb),
                  chunk(lambda cb: cb), halo(lambda cb: cb), chunk(lambda cb: cb),
                  pl.BlockSpec((CONV_K, LANES), chan), vec, mat, mat, vec, vec, vec],
        out_specs=[chunk(lambda cb: cb), chunk(lambda cb: cb), pl.BlockSpec((CONV_K, LANES), chan), vec,
                   mat, mat, vec, vec, vec],
        out_shape=[jax.ShapeDtypeStruct((T, LRU_W), F32), jax.ShapeDtypeStruct((T, LRU_W), F32),
                   jax.ShapeDtypeStruct((CONV_K, LRU_W), F32), jax.ShapeDtypeStruct((1, LRU_W), F32),
                   jax.ShapeDtypeStruct((ncb, LANES, LANES), F32), jax.ShapeDtypeStruct((ncb, LANES, LANES), F32),
                   jax.ShapeDtypeStruct((1, LRU_W), F32), jax.ShapeDtypeStruct((1, LRU_W), F32),
                   jax.ShapeDtypeStruct((1, LRU_W), F32)],
        scratch_shapes=[pltpu.VMEM((lc, LANES), F32), pltpu.VMEM((lc, LANES), F32), pltpu.VMEM((1, LANES), F32),
                        pltpu.VMEM((1, LANES), F32), pltpu.VMEM((SUBLANES, LANES), F32)],
        compiler_params=_params(("arbitrary", "arbitrary", "arbitrary"), 32),
    )(proj_lru, proj_lru, proj_lru, h, h, dy_lru, conv_w, conv_b, wa_bd, wx_bd, b_a, b_x, lam)


def _bwd_in(pieces, w_in16t, x2, g1, dh1, tm):
    T = x2.shape[0]
    npc = len(pieces)

    def body(*refs):
        p_refs = refs[:npc]
        w_ref, x_ref, g_ref, d_ref, dx_ref, dproj_ref, xn_ref, dg1_ref = refs[npc:]

        @pl.when(pl.program_id(0) == 0)
        def _():
            dg1_ref[...] = jnp.zeros_like(dg1_ref)

        dxn = jnp.zeros((tm, D_MODEL), F32)
        for n, p_ref in enumerate(p_refs):
            cols = slice(n * LRU_W, (n + 1) * LRU_W)
            p16 = p_ref[...].astype(BF16)
            dproj_ref[:, cols] = p16
            dxn = dxn + _dot(p16, w_ref[cols, :])
        gv = g_ref[...]
        xn, xhat, rstd = _rms_fwd(x_ref[...], gv)
        xn_ref[...] = xn.astype(BF16)
        dg1_ref[...] += jnp.sum(dxn * xhat, axis=0, keepdims=True)
        dx_ref[...] = d_ref[...] + _rms_bwd(dxn, xhat, rstd, gv)

    row = lambda i: (i, 0)
    fix = lambda i: (0, 0)
    return pl.pallas_call(
        body, name="bwd_in", grid=(T // tm,),
        in_specs=[pl.BlockSpec((tm, LRU_W), row)] * npc + [
            pl.BlockSpec((IN_COLS, D_MODEL), fix), pl.BlockSpec((tm, D_MODEL), row),
            pl.BlockSpec((1, D_MODEL), fix), pl.BlockSpec((tm, D_MODEL), row)],
        out_specs=[pl.BlockSpec((tm, D_MODEL), row), pl.BlockSpec((tm, IN_COLS), row),
                   pl.BlockSpec((tm, D_MODEL), row), pl.BlockSpec((1, D_MODEL), fix)],
        out_shape=[jax.ShapeDtypeStruct((T, D_MODEL), F32), jax.ShapeDtypeStruct((T, IN_COLS), BF16),
                   jax.ShapeDtypeStruct((T, D_MODEL), BF16), jax.ShapeDtypeStruct((1, D_MODEL), F32)],
        compiler_params=_params(("arbitrary",), 56),
    )(*pieces, w_in16t, x2, g1, dh1)


def _adam_shard(name, parts, w, m, v, tr):
    R, C = w.shape

    def body(p_ref, w_ref, m_ref, v_ref, g_ref, d_ref, m2_ref, v2_ref):
        g = p_ref[0].astype(F32)
        for p in range(1, N_DEV):
            g = g + p_ref[p].astype(F32)
        g_ref[...] = g
        d_ref[...], m2_ref[...], v2_ref[...] = _adamw(w_ref[...], g, m_ref[...], v_ref[...])

    blk = pl.BlockSpec((tr, C), lambda i: (i, 0))
    return pl.pallas_call(
        body, name=name, grid=(R // tr,),
        in_specs=[pl.BlockSpec((N_DEV, tr, C), lambda i: (0, i, 0)), blk, blk, blk],
        out_specs=[blk] * 4, out_shape=[jax.ShapeDtypeStruct((R, C), F32)] * 4,
        compiler_params=_params(("arbitrary",), 48),
    )(parts, w, m, v)


def _adam_param(name, parts, w, m, v):
    def body(p_ref, w_ref, m_ref, v_ref, g_ref, d_ref, m2_ref, v2_ref):
        g = p_ref[0]
        for p in range(1, N_DEV):
            g = g + p_ref[p]
        g_ref[...] = g
        d_ref[...], m2_ref[...], v2_ref[...] = _adamw(w_ref[...], g, m_ref[...], v_ref[...])

    return pl.pallas_call(body, name=name, out_shape=[jax.ShapeDtypeStruct(w.shape, F32)] * 4)(parts, w, m, v)


def _adam_given(name, g, w, m, v, tr):
    R, C = w.shape

    def body(g_ref, w_ref, m_ref, v_ref, d_ref, m2_ref, v2_ref):
        d_ref[...], m2_ref[...], v2_ref[...] = _adamw(w_ref[...], g_ref[...], m_ref[...], v_ref[...])

    blk = pl.BlockSpec((tr, C), lambda i: (i, 0))
    return [g] + list(pl.pallas_call(
        body, name=name, grid=(R // tr,), in_specs=[blk] * 4, out_specs=[blk] * 3,
        out_shape=[jax.ShapeDtypeStruct((R, C), F32)] * 3, compiler_params=_params(("arbitrary",), 32),
    )(g, w, m, v))


def _sum_parts(name, parts):
    def body(p_ref, g_ref):
        g = p_ref[0].astype(F32)
        for p in range(1, N_DEV):
            g = g + p_ref[p].astype(F32)
        g_ref[...] = g

    return pl.pallas_call(body, name=name, out_shape=jax.ShapeDtypeStruct(parts.shape[1:], F32))(parts)


def _block_diag_pairs(w):
    w = w.reshape(LRU_BLOCKS // 2, 2, HEAD_D, HEAD_D)
    out = jnp.zeros((LRU_BLOCKS // 2, LANES, LANES), w.dtype)
    out = out.at[:, :HEAD_D, :HEAD_D].set(w[:, 0])
    return out.at[:, HEAD_D:, HEAD_D:].set(w[:, 1])


def _diag_blocks(w):
    return jnp.stack([w[:, :HEAD_D, :HEAD_D], w[:, HEAD_D:, HEAD_D:]], axis=1).reshape(LRU_BLOCKS, HEAD_D, HEAD_D)


def kernel(x, norm1_g, w_in, conv_w, conv_b, lru_w_a, lru_b_a, lru_w_x, lru_b_x, lru_lambda, lru_out_g, sb_out_g, w_out, norm2_g, w_up, w_down, final_g, loss_target, m_norm1_g, m_w_in, m_conv_w, m_conv_b, m_lru_w_a, m_lru_b_a, m_lru_w_x, m_lru_b_x, m_lru_lambda, m_lru_out_g, m_sb_out_g, m_w_out, m_norm2_g, m_w_up, m_w_down, m_final_g, v_norm1_g, v_w_in, v_conv_w, v_conv_b, v_lru_w_a, v_lru_b_a, v_lru_w_x, v_lru_b_x, v_lru_lambda, v_lru_out_g, v_sb_out_g, v_w_out, v_norm2_g, v_w_up, v_w_down, v_final_g):
    B, S, _ = x.shape
    T = B * S
    tm = min(512, T)
    tk = min(GRAD_TK, T)
    lc = min(512, S)
    x2 = x.reshape(T, D_MODEL)
    tgt = loss_target.reshape(T, D_MODEL)
    cw_cols = CONV_K * LRU_W // N_DEV // CONV_K

    shards16 = _cast_shards([w_in[0].T, w_out[0], w_up[0], w_down[0]])
    cw_pad = jnp.zeros((SUBLANES, LANES), F32).at[:CONV_K, :cw_cols].set(conv_w[0])
    g_in, g_cw = _exchange("gather_w_in", [shards16[0], cw_pad], [False, False])
    w_in16t = g_in.reshape(IN_COLS, D_MODEL)
    conv_w_full = g_cw[:, :CONV_K, :cw_cols].transpose(1, 0, 2).reshape(CONV_K, LRU_W)
    wa_bd = _block_diag_pairs(lru_w_a[0]).astype(BF16)
    wx_bd = _block_diag_pairs(lru_w_x[0]).astype(BF16)
    b_a = lru_b_a.reshape(1, LRU_W)
    b_x = lru_b_x.reshape(1, LRU_W)
    gf = final_g.reshape(1, D_MODEL)

    proj_lru, qkv = _fwd_in(x2, norm1_g, w_in16t, tm)
    y_lru, h = _lru_fwd(proj_lru, conv_w_full, conv_b, wa_bd, wx_bd, b_a, b_x, lru_lambda, B, S, lc)
    tri_suffix, tri_prefix = _tri(False), _tri(True)
    y_sb, run_tab, g_out, g_up, g_down = _attn_fwd(qkv, tri_suffix, list(shards16[1:]), [False] * 3, B, S)
    w_out16 = g_out.reshape(D_MODEL, D_MODEL)
    w_down16 = g_down.reshape(D_FF, D_MODEL)
    h1, mix16 = _fwd_mix(y_lru, y_sb, lru_out_g, sb_out_g, w_out16, x2, tm)
    up16, dh2, d_final_g, loss_part = _fwd_mlp(h1, norm2_g, g_up, w_down16, gf, tgt, tm)

    dup16, dh1, hn16, dh2b, d_norm2_g = _bwd_mlp(dh2, up16, h1, norm2_g, g_up, w_down16, tm)
    sq = lambda u: (u.astype(F32) * u.astype(F32)).astype(BF16)
    gw_up, = _matmul_tn("grad_w_up", hn16, dup16, D_MODEL, FF_CHUNK, tk, (N_DEV, D_MODEL, FF_CHUNK),
                        (None, D_MODEL, FF_CHUNK), lambda m, n, k: (n, 0, 0))
    gw_down, = _matmul_tn("grad_w_down", up16, dh2b, FF_CHUNK, D_MODEL, tk, (N_DEV, FF_CHUNK, D_MODEL),
                          (None, FF_CHUNK, D_MODEL), lambda m, n, k: (m, 0, 0), a_prep=sq)
    dy_lru, dy_sb, d_lru_out_g, d_sb_out_g = _bwd_mix(dh1, w_out16, y_lru, y_sb, lru_out_g, sb_out_g, tm)
    gw_out, = _matmul_tn("grad_w_out", mix16, dh1, D_MODEL, FF_CHUNK, tk, (D_MODEL, D_MODEL),
                         (D_MODEL, FF_CHUNK), lambda m, n, k: (0, n), b_prep=lambda u: u.astype(BF16))
    parts_out = gw_out.reshape(N_DEV, D_MODEL // N_DEV, D_MODEL)
    dq, dk, dv, r_out, r_up, r_down = _attn_bwd(qkv, run_tab, dy_sb, tri_suffix, tri_prefix,
                                                [parts_out, gw_up, gw_down], [True] * 3, B, S)
    (dx_lru, dg_lru, d_conv_w, d_conv_b, d_wa, d_wx, d_b_a, d_b_x, d_lambda) = _lru_bwd(
        proj_lru, h, dy_lru, conv_w_full, conv_b, wa_bd, wx_bd, b_a, b_x, lru_lambda, B, S, lc)
    dx, dproj16, xn16, d_norm1_g = _bwd_in([dx_lru, dg_lru, dq, dk, dv], w_in16t, x2, norm1_g, dh1, tm)

    small = {"norm1_g": (d_norm1_g, norm1_g, m_norm1_g, v_norm1_g), "conv_b": (d_conv_b, conv_b, m_conv_b, v_conv_b),
             "lru_w_a": (_diag_blocks(d_wa), lru_w_a, m_lru_w_a, v_lru_w_a),
             "lru_b_a": (d_b_a, lru_b_a, m_lru_b_a, v_lru_b_a),
             "lru_w_x": (_diag_blocks(d_wx), lru_w_x, m_lru_w_x, v_lru_w_x),
             "lru_b_x": (d_b_x, lru_b_x, m_lru_b_x, v_lru_b_x),
             "lru_lambda": (d_lambda, lru_lambda, m_lru_lambda, v_lru_lambda),
             "lru_out_g": (d_lru_out_g, lru_out_g, m_lru_out_g, v_lru_out_g),
             "sb_out_g": (d_sb_out_g, sb_out_g, m_sb_out_g, v_sb_out_g),
             "norm2_g": (d_norm2_g, norm2_g, m_norm2_g, v_norm2_g),
             "final_g": (d_final_g, final_g, m_final_g, v_final_g)}
    names = list(small)

    def held(n, a):
        return a.reshape((1, D_MODEL) if n == "final_g" else small[n][1].shape)

    parts_cw = d_conv_w.reshape(CONV_K, N_DEV, cw_cols).transpose(1, 0, 2)[:, None]
    gw_in_t, *got = _matmul_tn(
        "grad_w_in", dproj16, xn16, FF_CHUNK, D_MODEL, tk, (IN_COLS, D_MODEL), (FF_CHUNK, D_MODEL),
        lambda m, n, k: (m, 0), out_dtype=BF16,
        ride=[parts_cw] + [held(n, small[n][0]) for n in names] + [loss_part],
        ride_sliced=[True] + [False] * (len(names) + 1))
    r_in, = _exchange("exchange_grads", [gw_in_t.reshape(N_DEV, IN_COLS // N_DEV, D_MODEL)], [True])

    g_in_t = _sum_parts("sum_w_in", r_in)
    out = {"w_in": _adam_given("adam_w_in", g_in_t.T, w_in[0], m_w_in[0], v_w_in[0], 256),
           "w_out": _adam_shard("adam_w_out", r_out, w_out[0], m_w_out[0], v_w_out[0], 64),
           "w_up": _adam_shard("adam_w_up", r_up, w_up[0], m_w_up[0], v_w_up[0], 256),
           "w_down": _adam_shard("adam_w_down", r_down, w_down[0], m_w_down[0], v_w_down[0], 128)}
    out = {n: [a[None] for a in res] for n, res in out.items()}
    out["conv_w"] = _adam_param("adam_conv_w", got[0], conv_w, m_conv_w, v_conv_w)
    for n, parts in zip(names, got[1:-1]):
        _, w, m, v = small[n]
        res = _adam_param("adam_" + n, parts, held(n, w), held(n, m), held(n, v))
        out[n] = [a.reshape(w.shape) for a in res]
    loss = _sum_parts("sum_loss", got[-1])[0, 0]
    weights = ["norm1_g", "w_in", "conv_w", "conv_b", "lru_w_a", "lru_b_a", "lru_w_x", "lru_b_x", "lru_lambda",
               "lru_out_g", "sb_out_g", "w_out", "norm2_g", "w_up", "w_down", "final_g"]
    return (loss, dx.reshape(B, S, D_MODEL), *[out[n][0] for n in weights], *[out[n][1] for n in weights],
            *[out[n][2] for n in weights], *[out[n][3] for n in weights])
```

```python
import jax
import jax.numpy as jnp
from jax import lax
from jax.experimental import pallas as pl
from jax.experimental.pallas import tpu as pltpu

F32 = jnp.float32
BF16 = jnp.bfloat16

D_MODEL = 1024
LRU_W = 512
SB_W = 512
HEAD_D = 64
D_FF = 4096
IN_COLS = 2 * LRU_W + 3 * SB_W
CONV_K = 4
LRU_BLOCKS = 8
LRU_C = 8.0
EPS = 1e-6
N_DEV = 8
LANES = 128
SUBLANES = 8
FF_CHUNK = 512
GRAD_TK = 2048
Q_BLK = 256
K_BLK = 256
Q_PER_K = K_BLK // Q_BLK

ADAM_LR = 0.001
ADAM_B1 = 0.9
ADAM_B2 = 0.999
ADAM_EPS = 1e-08
ADAM_WD = 0.01
ADAM_STEP = 10


def _params(sem=None, vmem_mb=None):
    kw = {}
    if sem is not None:
        kw["dimension_semantics"] = sem
    if vmem_mb is not None:
        kw["vmem_limit_bytes"] = vmem_mb << 20
    return pltpu.CompilerParams(**kw)


def _dot(a, b):
    return jnp.dot(a, b, preferred_element_type=F32)


def _dot_nt(a, b):
    return lax.dot_general(a, b, (((1,), (1,)), ((), ())), preferred_element_type=F32)


def _dot_tn(a, b):
    return lax.dot_general(a, b, (((0,), (0,)), ((), ())), preferred_element_type=F32)


def _rms_fwd(x, g):
    rstd = lax.rsqrt(jnp.mean(x * x, axis=-1, keepdims=True) + EPS)
    xhat = x * rstd
    return xhat * g, xhat, rstd


def _rms_bwd(dy, xhat, rstd, g):
    dxhat = dy * g
    return rstd * (dxhat - xhat * jnp.mean(dxhat * xhat, axis=-1, keepdims=True))


def _sigmoid(x):
    return 1.0 / (1.0 + jnp.exp(-x))


def _log1p_pos(e):
    series = e * (1.0 - e * (0.5 - e * (1.0 / 3.0 - e * 0.25)))
    return jnp.where(e < 1e-2, series, jnp.log(1.0 + e))


def _neg_expm1(x):
    series = -x * (1.0 + x * (0.5 + x * (1.0 / 6.0 + x * (1.0 / 24.0))))
    return jnp.where(x > -1e-2, series, 1.0 - jnp.exp(x))


def _gelu_parts(g):
    k0 = 0.7978845608028654
    k1 = 0.044715
    t = jnp.tanh(k0 * (g + k1 * g * g * g))
    val = 0.5 * g * (1.0 + t)
    grad = 0.5 * (1.0 + t) + 0.5 * g * (1.0 - t * t) * k0 * (1.0 + 3.0 * k1 * g * g)
    return val, grad


def _scan(a, b, reverse):
    n = a.shape[0]
    row = lax.broadcasted_iota(jnp.int32, a.shape, 0)
    s = 1
    while s < n:
        if reverse:
            keep = row < n - s
            shift = n - s
        else:
            keep = row >= s
            shift = s
        bs = jnp.where(keep, pltpu.roll(b, shift, 0), 0.0)
        a_s = jnp.where(keep, pltpu.roll(a, shift, 0), 1.0)
        b = a * bs + b
        a = a * a_s
        s *= 2
    return b, a


def _adamw(w, g, m, v):
    m = ADAM_B1 * m + (1.0 - ADAM_B1) * g
    v = ADAM_B2 * v + (1.0 - ADAM_B2) * (g * g)
    m_hat = m / (1.0 - ADAM_B1 ** ADAM_STEP)
    v_hat = v / (1.0 - ADAM_B2 ** ADAM_STEP)
    delta = -ADAM_LR * (m_hat / (jnp.sqrt(v_hat) + ADAM_EPS) + ADAM_WD * w)
    return delta, m, v


def _my_index():
    return 4 * lax.axis_index("x") + 2 * lax.axis_index("y") + lax.axis_index("c")


def _peer(k):
    x, y, c = lax.axis_index("x"), lax.axis_index("y"), lax.axis_index("c")
    px = 1 - x if (k >> 2) & 1 else x
    py = 1 - y if (k >> 1) & 1 else y
    pc = 1 - c if k & 1 else c
    return (px, py, pc), 4 * px + 2 * py + pc


def _exchange_shapes(srcs, sliced):
    n = len(srcs)
    out_shape = [jax.ShapeDtypeStruct(s.shape if sl else (N_DEV,) + s.shape, s.dtype) for s, sl in zip(srcs, sliced)]
    specs = [pl.BlockSpec(memory_space=pl.ANY)] * n
    sems = [pltpu.SemaphoreType.DMA((n, N_DEV - 1)), pltpu.SemaphoreType.DMA((n, N_DEV - 1)),
            pltpu.SemaphoreType.DMA((n,))]
    return out_shape, specs, sems


def _exchange_copies(ins, outs, sliced, send_sems, recv_sems, local_sems):
    n = len(ins)

    def part(a, p):
        return ins[a].at[p] if sliced[a] else ins[a]

    def copies(receiving):
        me = _my_index()
        local = [pltpu.make_async_copy(part(a, me), outs[a].at[me], local_sems.at[a]) for a in range(n)]
        remote = []
        for k in range(1, N_DEV):
            dev, idx = _peer(k)
            for a in range(n):
                remote.append(pltpu.make_async_remote_copy(
                    src_ref=part(a, idx), dst_ref=outs[a].at[idx if receiving else me],
                    send_sem=send_sems.at[a, k - 1], recv_sem=recv_sems.at[a, k - 1],
                    device_id=dev, device_id_type=pl.DeviceIdType.MESH))
        return local, remote

    def start():
        local, remote = copies(receiving=False)
        for cp in local + remote:
            cp.start()

    def wait():
        local, remote = copies(receiving=True)
        for cp in remote + local:
            cp.wait()

    return start, wait


def _exchange(name, srcs, sliced):
    n = len(srcs)
    out_shape, specs, sems = _exchange_shapes(srcs, sliced)

    def body(*refs):
        start, wait = _exchange_copies(refs[:n], refs[n:2 * n], sliced, *refs[2 * n:])
        start()
        wait()

    return pl.pallas_call(body, name=name, out_shape=out_shape, in_specs=specs, out_specs=specs,
                          scratch_shapes=sems)(*srcs)


def _cast_shards(ws):
    def body(*refs):
        for i in range(len(ws)):
            refs[len(ws) + i][...] = refs[i][...].astype(BF16)

    return pl.pallas_call(
        body, name="cast_shards", out_shape=[jax.ShapeDtypeStruct(w.shape, BF16) for w in ws],
        compiler_params=_params(vmem_mb=32),
    )(*ws)


def _fwd_in(x2, g1, w_in16t, tm):
    T = x2.shape[0]

    def body(x_ref, g_ref, w_ref, lru_ref, qkv_ref):
        xn, _, _ = _rms_fwd(x_ref[...], g_ref[...])
        xn = xn.astype(BF16)
        lru_ref[...] = _dot_nt(xn, w_ref[0:2 * LRU_W, :])
        qkv_ref[...] = _dot_nt(xn, w_ref[2 * LRU_W:IN_COLS, :]).astype(BF16)

    return pl.pallas_call(
        body, name="fwd_in", grid=(T // tm,),
        in_specs=[pl.BlockSpec((tm, D_MODEL), lambda i: (i, 0)),
                  pl.BlockSpec((1, D_MODEL), lambda i: (0, 0)),
                  pl.BlockSpec((IN_COLS, D_MODEL), lambda i: (0, 0))],
        out_specs=[pl.BlockSpec((tm, 2 * LRU_W), lambda i: (i, 0)),
                   pl.BlockSpec((tm, 3 * SB_W), lambda i: (i, 0))],
        out_shape=[jax.ShapeDtypeStruct((T, 2 * LRU_W), F32), jax.ShapeDtypeStruct((T, 3 * SB_W), BF16)],
        compiler_params=_params(("arbitrary",), 48),
    )(x2, g1, w_in16t)


def _lru_gates(c, wa_ref, wx_ref, ba_ref, bx_ref, lam_ref):
    c16 = c.astype(BF16)
    r = _sigmoid(_dot(c16, wa_ref[0]) + ba_ref[...])
    i = _sigmoid(_dot(c16, wx_ref[0]) + bx_ref[...])
    lam = lam_ref[...]
    e = jnp.exp(-jnp.abs(lam))
    sp = jnp.maximum(-lam, 0.0) + _log1p_pos(e)
    dsp_dlam = -jnp.where(lam >= 0.0, e, 1.0) / (1.0 + e)
    log_a = (-LRU_C) * r * sp
    a = jnp.exp(log_a)
    s = jnp.sqrt(_neg_expm1(2.0 * log_a))
    return r, i, sp, dsp_dlam, a, s


def _conv_taps(x, halo, lc):
    xe = jnp.concatenate([halo, x], axis=0)
    return [x] + [pltpu.roll(xe, k, 0)[SUBLANES:SUBLANES + lc] for k in range(1, CONV_K)]


def _lru_fwd(proj_lru, conv_w, conv_b, wa_bd, wx_bd, b_a, b_x, lam, B, S, lc):
    T = B * S
    nc = S // lc
    ncb = LRU_W // LANES

    def body(x_ref, g_ref, cw_ref, cb_ref, wa_ref, wx_ref, ba_ref, bx_ref, lam_ref, y_ref, h_ref, tail, carry):
        ci = pl.program_id(2)

        @pl.when(ci == 0)
        def _():
            tail[...] = jnp.zeros_like(tail)
            carry[...] = jnp.zeros_like(carry)

        x = x_ref[...]
        taps = _conv_taps(x, tail[...], lc)
        c = cb_ref[...] + sum(cw_ref[pl.ds(CONV_K - 1 - k, 1), :] * taps[k] for k in range(CONV_K))
        tail[...] = x_ref[pl.ds(lc - SUBLANES, SUBLANES), :]
        r, i, sp, _, a, s = _lru_gates(c, wa_ref, wx_ref, ba_ref, bx_ref, lam_ref)
        h_loc, a_run = _scan(a, s * (i * c), reverse=False)
        h_ref[...] = h_loc + a_run * carry[...]
        carry[...] = h_ref[pl.ds(lc - 1, 1), :]
        gelu, _ = _gelu_parts(g_ref[...])
        y_ref[...] = h_ref[...] * gelu

    chan = lambda b, cb, ci: (0, cb)
    return pl.pallas_call(
        body, name="lru_fwd", grid=(B, ncb, nc),
        in_specs=[pl.BlockSpec((lc, LANES), lambda b, cb, ci: (b * nc + ci, cb)),
                  pl.BlockSpec((lc, LANES), lambda b, cb, ci: (b * nc + ci, ncb + cb)),
                  pl.BlockSpec((CONV_K, LANES), chan), pl.BlockSpec((1, LANES), chan),
                  pl.BlockSpec((1, LANES, LANES), lambda b, cb, ci: (cb, 0, 0)),
                  pl.BlockSpec((1, LANES, LANES), lambda b, cb, ci: (cb, 0, 0)),
                  pl.BlockSpec((1, LANES), chan), pl.BlockSpec((1, LANES), chan), pl.BlockSpec((1, LANES), chan)],
        out_specs=[pl.BlockSpec((lc, LANES), lambda b, cb, ci: (b * nc + ci, cb))] * 2,
        out_shape=[jax.ShapeDtypeStruct((T, LRU_W), F32)] * 2,
        scratch_shapes=[pltpu.VMEM((SUBLANES, LANES), F32), pltpu.VMEM((1, LANES), F32)],
        compiler_params=_params(("arbitrary", "arbitrary", "arbitrary"), 32),
    )(proj_lru, proj_lru, conv_w, conv_b, wa_bd, wx_bd, b_a, b_x, lam)


def _tri(prefix):
    r = lax.broadcasted_iota(jnp.int32, (K_BLK, K_BLK), 0)
    c = lax.broadcasted_iota(jnp.int32, (K_BLK, K_BLK), 1)
    return ((r <= c) if prefix else (r >= c)).astype(BF16)


def _attn_consts(qi):
    r = lax.broadcasted_iota(jnp.int32, (Q_BLK, K_BLK), 0)
    c = lax.broadcasted_iota(jnp.int32, (Q_BLK, K_BLK), 1)
    causal = c + ((qi // Q_PER_K) * K_BLK - qi * Q_BLK) < r
    lane = lax.broadcasted_iota(jnp.int32, (1, LANES), 1)
    return causal, lane, (lane < HEAD_D, lane >= HEAD_D)


def _log1m_beta(z, mask):
    lg = -(jnp.maximum(z, 0.0) + jnp.log(1.0 + jnp.exp(-jnp.abs(z))))
    return lg if mask is None else jnp.where(mask, lg, 0.0)


def _key_rows(j):
    return pl.ds(pl.multiple_of(j * K_BLK, K_BLK), K_BLK)


def _attn_fwd(qkv, tri_suffix, ride, ride_sliced, B, S):
    T = B * S
    nq = S // Q_BLK
    nhp = SB_W // LANES
    scale = HEAD_D ** -0.5
    assert S // K_BLK <= LANES
    nr = len(ride)
    ride_shape, ride_specs, ride_sems = _exchange_shapes(ride, ride_sliced)

    def body(q_ref, k_ref, v_ref, tri_ref, *rest):
        o_ref, run_ref = rest[nr:nr + 2]
        start_ride, wait_ride = _exchange_copies(rest[:nr], rest[nr + 2:2 * nr + 2], ride_sliced, *rest[2 * nr + 2:])
        qi = pl.program_id(2)
        step_no = (pl.program_id(0) * nhp + pl.program_id(1)) * nq + qi
        pl.when(step_no == 0)(start_ride)
        jd = qi // Q_PER_K
        causal, lane, halves = _attn_consts(qi)
        q = q_ref[...]
        qh = [jnp.where(hm, q, jnp.zeros_like(q)) * jnp.asarray(scale, BF16) for hm in halves]

        def group(blocks, carry):
            runs, tables, acc = carry
            runs, tables = list(runs), list(tables)
            ks = [k_ref[_key_rows(jl), :] for jl, _, _ in blocks]
            vs = [v_ref[_key_rows(jl), :] for jl, _, _ in blocks]
            chains = [(b, h) for b in range(len(blocks)) for h in range(2)]
            z = {c: _dot_nt(qh[c[1]], ks[c[0]]) for c in chains}
            lg = {c: _log1m_beta(z[c], blocks[c[0]][2]) for c in chains}
            suf = {c: _dot(lg[c].astype(BF16), tri_ref[...]) for c in chains}
            att = {}
            for b, h in chains:
                _, jlane, mask = blocks[b]
                a = jnp.exp(z[b, h] + suf[b, h] + runs[h])
                att[b, h] = (a if mask is None else jnp.where(mask, a, 0.0)).astype(BF16)
                tables[h] = jnp.where(lane == jlane, runs[h], tables[h])
                runs[h] = runs[h] + suf[b, h][:, 0:1]
            for b, h in chains:
                acc = acc + _dot(att[b, h], jnp.where(halves[h], vs[b], jnp.zeros_like(vs[b])))
            return tuple(runs), tuple(tables), acc

        col0 = jnp.zeros((Q_BLK, 1), F32)
        zero = jnp.zeros((Q_BLK, LANES), F32)
        start = ((col0, col0), (zero, zero), zero)
        two = jd % 2
        carry = lax.cond(two == 1, lambda: group([(jd, jd, causal), (jd - 1, jd - 1, None)], start),
                         lambda: group([(jd, jd, causal)], start))
        top = jd - 1 - two

        def run(ja, n):
            return [(ja - i, ja - i, None) for i in range(n)]

        odd_pair = ((top + 1) // 2) % 2
        carry = lax.cond(odd_pair == 1, lambda cr: group(run(top, 2), cr), lambda cr: cr, carry)
        top4 = top - 2 * odd_pair
        _, tables, acc = lax.fori_loop(0, (top4 + 1) // 4, lambda it, cr: group(run(top4 - 4 * it, 4), cr), carry)
        o_ref[...] = acc
        run_ref[:, 0:LANES] = tables[0]
        run_ref[:, LANES:2 * LANES] = tables[1]
        pl.when(step_no == B * nhp * nq - 1)(wait_ride)

    return pl.pallas_call(
        body, name="attn_fwd", grid=(B, nhp, nq),
        in_specs=[pl.BlockSpec((Q_BLK, LANES), lambda b, hp, qi: (b * nq + qi, hp)),
                  pl.BlockSpec((S, LANES), lambda b, hp, qi: (b, nhp + hp)),
                  pl.BlockSpec((S, LANES), lambda b, hp, qi: (b, 2 * nhp + hp)),
                  pl.BlockSpec((K_BLK, K_BLK), lambda b, hp, qi: (0, 0))] + ride_specs,
        out_specs=[pl.BlockSpec((Q_BLK, LANES), lambda b, hp, qi: (b * nq + qi, hp)),
                   pl.BlockSpec((Q_BLK, 2 * LANES), lambda b, hp, qi: (b * nq + qi, hp))] + ride_specs,
        out_shape=[jax.ShapeDtypeStruct((T, SB_W), F32), jax.ShapeDtypeStruct((T, 2 * SB_W), F32)] + ride_shape,
        scratch_shapes=ride_sems,
        compiler_params=_params(("arbitrary", "arbitrary", "arbitrary"), 48),
    )(qkv, qkv, qkv, tri_suffix, *ride)


def _fwd_mix(y_lru, y_sb, ga, gb, w_out16, x2, tm):
    T = x2.shape[0]

    def body(yl_ref, ys_ref, ga_ref, gb_ref, w_ref, x_ref, h1_ref, mix_ref):
        na, _, _ = _rms_fwd(yl_ref[...], ga_ref[...])
        nb, _, _ = _rms_fwd(ys_ref[...], gb_ref[...])
        na = na.astype(BF16)
        nb = nb.astype(BF16)
        mix_ref[:, 0:LRU_W] = na
        mix_ref[:, LRU_W:D_MODEL] = nb
        h1_ref[...] = x_ref[...] + _dot(na, w_ref[0:LRU_W, :]) + _dot(nb, w_ref[LRU_W:D_MODEL, :])

    row = lambda i: (i, 0)
    fix = lambda i: (0, 0)
    return pl.pallas_call(
        body, name="fwd_mix", grid=(T // tm,),
        in_specs=[pl.BlockSpec((tm, LRU_W), row), pl.BlockSpec((tm, SB_W), row),
                  pl.BlockSpec((1, LRU_W), fix), pl.BlockSpec((1, SB_W), fix),
                  pl.BlockSpec((D_MODEL, D_MODEL), fix), pl.BlockSpec((tm, D_MODEL), row)],
        out_specs=[pl.BlockSpec((tm, D_MODEL), row), pl.BlockSpec((tm, D_MODEL), row)],
        out_shape=[jax.ShapeDtypeStruct((T, D_MODEL), F32), jax.ShapeDtypeStruct((T, D_MODEL), BF16)],
        compiler_params=_params(("arbitrary",), 48),
    )(y_lru, y_sb, ga, gb, w_out16, x2)


def _fwd_mlp(h1, g2, w_up16, w_down16, gf, tgt, tm, per_step):
    T = h1.shape[0]
    nf = D_FF // (FF_CHUNK * per_step)

    def body(h1_ref, g2_ref, wu_ref, wd_ref, gf_ref, t_ref, up_ref, dh2_ref, dgf_ref, loss_ref, hn_s, acc):
        i, j = pl.program_id(0), pl.program_id(1)

        @pl.when(j == 0)
        def _():
            h1v = h1_ref[...]
            hn, _, _ = _rms_fwd(h1v, g2_ref[...])
            hn_s[...] = hn.astype(BF16)
            acc[...] = h1v

        down = None
        for c in range(per_step):
            cols = slice(c * FF_CHUNK, (c + 1) * FF_CHUNK)
            up = jnp.maximum(_dot(hn_s[...], wu_ref[c]), 0.0)
            up_ref[:, cols] = up.astype(BF16)
            part = _dot((up * up).astype(BF16), wd_ref[cols, :])
            down = part if down is None else down + part
        acc[...] += down

        @pl.when((i == 0) & (j == 0))
        def _():
            dgf_ref[...] = jnp.zeros_like(dgf_ref)
            loss_ref[...] = jnp.zeros_like(loss_ref)

        @pl.when(j == nf - 1)
        def _():
            gfv = gf_ref[...]
            y, xhat, rstd = _rms_fwd(acc[...], gfv)
            err = y - t_ref[...]
            loss_ref[...] += jnp.sum(0.5 * jnp.sum(err * err, axis=-1, keepdims=True) * (1.0 / D_MODEL))
            dy = err * (1.0 / D_MODEL)
            dgf_ref[...] += jnp.sum(dy * xhat, axis=0, keepdims=True)
            dh2_ref[...] = _rms_bwd(dy, xhat, rstd, gfv)

    row = lambda i, j: (i, 0)
    fix = lambda i, j: (0, 0)
    return pl.pallas_call(
        body, name="fwd_mlp", grid=(T // tm, nf),
        in_specs=[pl.BlockSpec((tm, D_MODEL), row), pl.BlockSpec((1, D_MODEL), fix),
                  pl.BlockSpec((per_step, D_MODEL, FF_CHUNK), lambda i, j: (j, 0, 0)),
                  pl.BlockSpec((per_step * FF_CHUNK, D_MODEL), lambda i, j: (j, 0)),
                  pl.BlockSpec((1, D_MODEL), fix), pl.BlockSpec((tm, D_MODEL), row)],
        out_specs=[pl.BlockSpec((tm, per_step * FF_CHUNK), lambda i, j: (i, j)), pl.BlockSpec((tm, D_MODEL), row),
                   pl.BlockSpec((1, D_MODEL), fix), pl.BlockSpec((1, LANES), fix)],
        out_shape=[jax.ShapeDtypeStruct((T, D_FF), BF16), jax.ShapeDtypeStruct((T, D_MODEL), F32),
                   jax.ShapeDtypeStruct((1, D_MODEL), F32), jax.ShapeDtypeStruct((1, LANES), F32)],
        scratch_shapes=[pltpu.VMEM((tm, D_MODEL), BF16), pltpu.VMEM((tm, D_MODEL), F32)],
        compiler_params=_params(("arbitrary", "arbitrary"), 56),
    )(h1, g2, w_up16, w_down16, gf, tgt)


def _bwd_mlp(dh2, up16, h1, g2, w_up16, w_down16, tm, per_step):
    T = h1.shape[0]
    nf = D_FF // (FF_CHUNK * per_step)

    def body(dh2_ref, up_ref, h1_ref, g2_ref, wu_ref, wd_ref, dup_ref, dh1_ref, hn_ref, dh2b_ref, dg2_ref, acc):
        i, j = pl.program_id(0), pl.program_id(1)

        @pl.when(j == 0)
        def _():
            hn, _, _ = _rms_fwd(h1_ref[...], g2_ref[...])
            hn_ref[...] = hn.astype(BF16)
            dh2b_ref[...] = dh2_ref[...].astype(BF16)
            acc[...] = jnp.zeros_like(acc)

        dhn = None
        for c in range(per_step):
            cols = slice(c * FF_CHUNK, (c + 1) * FF_CHUNK)
            u = up_ref[:, cols].astype(F32)
            dup = (2.0 * u * _dot_nt(dh2b_ref[...], wd_ref[cols, :])).astype(BF16)
            dup_ref[:, cols] = dup
            part = _dot_nt(dup, wu_ref[c])
            dhn = part if dhn is None else dhn + part
        acc[...] += dhn

        @pl.when((i == 0) & (j == 0))
        def _():
            dg2_ref[...] = jnp.zeros_like(dg2_ref)

        @pl.when(j == nf - 1)
        def _():
            g2v = g2_ref[...]
            _, xhat, rstd = _rms_fwd(h1_ref[...], g2v)
            dhn = acc[...]
            dg2_ref[...] += jnp.sum(dhn * xhat, axis=0, keepdims=True)
            dh1_ref[...] = dh2_ref[...] + _rms_bwd(dhn, xhat, rstd, g2v)

    row = lambda i, j: (i, 0)
    fix = lambda i, j: (0, 0)
    return pl.pallas_call(
        body, name="bwd_mlp", grid=(T // tm, nf),
        in_specs=[pl.BlockSpec((tm, D_MODEL), row), pl.BlockSpec((tm, per_step * FF_CHUNK), lambda i, j: (i, j)),
                  pl.BlockSpec((tm, D_MODEL), row), pl.BlockSpec((1, D_MODEL), fix),
                  pl.BlockSpec((per_step, D_MODEL, FF_CHUNK), lambda i, j: (j, 0, 0)),
                  pl.BlockSpec((per_step * FF_CHUNK, D_MODEL), lambda i, j: (j, 0))],
        out_specs=[pl.BlockSpec((tm, per_step * FF_CHUNK), lambda i, j: (i, j)), pl.BlockSpec((tm, D_MODEL), row),
                   pl.BlockSpec((tm, D_MODEL), row), pl.BlockSpec((tm, D_MODEL), row),
                   pl.BlockSpec((1, D_MODEL), fix)],
        out_shape=[jax.ShapeDtypeStruct((T, D_FF), BF16), jax.ShapeDtypeStruct((T, D_MODEL), F32),
                   jax.ShapeDtypeStruct((T, D_MODEL), BF16), jax.ShapeDtypeStruct((T, D_MODEL), BF16),
                   jax.ShapeDtypeStruct((1, D_MODEL), F32)],
        scratch_shapes=[pltpu.VMEM((tm, D_MODEL), F32)],
        compiler_params=_params(("arbitrary", "arbitrary"), 48),
    )(dh2, up16, h1, g2, w_up16, w_down16)


def _matmul_tn(name, a, b, bm, bn, tk, out_shape, out_block, out_index, a_prep=None, b_prep=None, out_dtype=F32,
               ride=None, ride_sliced=None, split=None):
    T, M = a.shape
    N = b.shape[1]
    grid = (M // bm, N // bn, T // tk)
    ride, ride_sliced = list(ride or []), list(ride_sliced or [])
    nr = len(ride)
    ride_shape, ride_specs, ride_sems = _exchange_shapes(ride, ride_sliced)

    def body(a_ref, b_ref, *rest):
        o_ref, acc = rest[nr], rest[2 * nr + 1]
        k = pl.program_id(2)
        step_no = (pl.program_id(0) * grid[1] + pl.program_id(1)) * grid[2] + k
        if nr:
            start_ride, wait_ride = _exchange_copies(rest[:nr], rest[nr + 1:2 * nr + 1], ride_sliced,
                                                     *rest[2 * nr + 2:])
            pl.when(step_no == 0)(start_ride)
        av = a_ref[...] if a_prep is None else a_prep(a_ref[...])
        bv = b_ref[...] if b_prep is None else b_prep(b_ref[...])
        p = _dot_tn(av, bv)

        @pl.when(k == 0)
        def _():
            acc[...] = p

        @pl.when(k > 0)
        def _():
            acc[...] += p

        @pl.when(k == grid[2] - 1)
        def _():
            if split is None:
                o_ref[...] = acc[...].astype(out_dtype)
            else:
                axis, n = split
                w = (bm, bn)[axis] // n
                for c in range(n):
                    slab = acc[c * w:(c + 1) * w, :] if axis == 0 else acc[:, c * w:(c + 1) * w]
                    o_ref[c] = slab.astype(out_dtype)

        if nr:
            pl.when(step_no == grid[0] * grid[1] * grid[2] - 1)(wait_ride)

    return pl.pallas_call(
        body, name=name, grid=grid,
        in_specs=[pl.BlockSpec((tk, bm), lambda m, n, k: (k, m)), pl.BlockSpec((tk, bn), lambda m, n, k: (k, n))]
        + ride_specs,
        out_specs=[pl.BlockSpec(out_block, out_index)] + ride_specs,
        out_shape=[jax.ShapeDtypeStruct(out_shape, out_dtype)] + ride_shape,
        scratch_shapes=[pltpu.VMEM((bm, bn), F32)] + (ride_sems if nr else []),
        compiler_params=_params(("arbitrary", "arbitrary", "arbitrary"), 48),
    )(a, b, *ride)


def _bwd_mix(dh1, w_out16, y_lru, y_sb, ga, gb, tm):
    T = dh1.shape[0]

    def body(d_ref, w_ref, yl_ref, ys_ref, ga_ref, gb_ref, dyl_ref, dys_ref, dga_ref, dgb_ref):
        @pl.when(pl.program_id(0) == 0)
        def _():
            dga_ref[...] = jnp.zeros_like(dga_ref)
            dgb_ref[...] = jnp.zeros_like(dgb_ref)

        d16 = d_ref[...].astype(BF16)
        for y_ref, g_ref, lo, dy_ref, dg_ref in ((yl_ref, ga_ref, 0, dyl_ref, dga_ref),
                                                 (ys_ref, gb_ref, LRU_W, dys_ref, dgb_ref)):
            gv = g_ref[...]
            dn = _dot_nt(d16, w_ref[lo:lo + LRU_W, :])
            _, xhat, rstd = _rms_fwd(y_ref[...], gv)
            dg_ref[...] += jnp.sum(dn * xhat, axis=0, keepdims=True)
            dy_ref[...] = _rms_bwd(dn, xhat, rstd, gv)

    row = lambda i: (i, 0)
    fix = lambda i: (0, 0)
    return pl.pallas_call(
        body, name="bwd_mix", grid=(T // tm,),
        in_specs=[pl.BlockSpec((tm, D_MODEL), row), pl.BlockSpec((D_MODEL, D_MODEL), fix),
                  pl.BlockSpec((tm, LRU_W), row), pl.BlockSpec((tm, SB_W), row),
                  pl.BlockSpec((1, LRU_W), fix), pl.BlockSpec((1, SB_W), fix)],
        out_specs=[pl.BlockSpec((tm, LRU_W), row), pl.BlockSpec((tm, SB_W), row),
                   pl.BlockSpec((1, LRU_W), fix), pl.BlockSpec((1, SB_W), fix)],
        out_shape=[jax.ShapeDtypeStruct((T, LRU_W), F32), jax.ShapeDtypeStruct((T, SB_W), F32),
                   jax.ShapeDtypeStruct((1, LRU_W), F32), jax.ShapeDtypeStruct((1, SB_W), F32)],
        compiler_params=_params(("arbitrary",), 48),
    )(dh1, w_out16, y_lru, y_sb, ga, gb)


def _attn_bwd(qkv, run_tab, dy_sb, tri_suffix, tri_prefix, ride, ride_sliced, B, S):
    T = B * S
    nq = S // Q_BLK
    nkb = S // K_BLK
    nhp = SB_W // LANES
    scale = HEAD_D ** -0.5

    nr = len(ride)
    ride_shape, ride_specs, ride_sems = _exchange_shapes(ride, ride_sliced)

    def body(q_ref, k_ref, v_ref, run_ref, do_ref, ts_ref, tp_ref, *rest):
        dq_ref, dk_ref, dv_ref = rest[nr:nr + 3]
        dkt_ref, dvt_ref = rest[2 * nr + 3:2 * nr + 5]
        start_ride, wait_ride = _exchange_copies(rest[:nr], rest[nr + 3:2 * nr + 3], ride_sliced, *rest[2 * nr + 5:])
        qi = pl.program_id(2)
        step_no = (pl.program_id(0) * nhp + pl.program_id(1)) * nq + qi
        pl.when(step_no == 0)(start_ride)
        jd = qi // Q_PER_K

        @pl.when(qi == 0)
        def _():
            dkt_ref[...] = jnp.zeros_like(dkt_ref)
            dvt_ref[...] = jnp.zeros_like(dvt_ref)

        causal, lane, halves = _attn_consts(qi)
        q = q_ref[...]
        do = do_ref[...]
        qh = [jnp.where(hm, q, jnp.zeros_like(q)) * jnp.asarray(scale, BF16) for hm in halves]
        doh = [jnp.where(hm, do, 0.0).astype(BF16) for hm in halves]
        qt = jnp.concatenate([h.astype(F32).T.astype(BF16) for h in qh], axis=1)
        dot_ = jnp.concatenate([h.astype(F32).T.astype(BF16) for h in doh], axis=1)
        tables = [run_ref[:, 0:LANES], run_ref[:, LANES:2 * LANES]]

        def group(blocks, carry):
            prefixes, dq = carry
            prefixes = list(prefixes)
            ks = [k_ref[_key_rows(jl), :] for jl, _, _ in blocks]
            vs = [v_ref[_key_rows(jl), :] for jl, _, _ in blocks]
            chains = [(b, h) for b in range(len(blocks)) for h in range(2)]
            z = {c: _dot_nt(qh[c[1]], ks[c[0]]) for c in chains}
            da = {c: _dot_nt(doh[c[1]], vs[c[0]]) for c in chains}
            lg = {c: _log1m_beta(z[c], blocks[c[0]][2]) for c in chains}
            suf = {c: _dot(lg[c].astype(BF16), ts_ref[...]) for c in chains}
            att, g = {}, {}
            for b, h in chains:
                _, jlane, mask = blocks[b]
                run = jnp.sum(jnp.where(lane == jlane, tables[h], 0.0), axis=1, keepdims=True)
                a = jnp.exp(z[b, h] + suf[b, h] + run)
                a = a if mask is None else jnp.where(mask, a, 0.0)
                g[b, h] = a * da[b, h]
                att[b, h] = a.astype(BF16)
            gpre = {c: _dot(g[c].astype(BF16), tp_ref[...]) for c in chains}
            dz = {}
            for b, h in chains:
                mask = blocks[b][2]
                d = g[b, h] - jnp.exp(z[b, h] + lg[b, h]) * (prefixes[h] + gpre[b, h])
                dz[b, h] = (d if mask is None else jnp.where(mask, d, 0.0)).astype(BF16)
                prefixes[h] = prefixes[h] + gpre[b, h][:, K_BLK - 1:K_BLK]
            for b, h in chains:
                dq = dq + _dot(dz[b, h], jnp.where(halves[h], ks[b], jnp.zeros_like(ks[b])))
            for b, (jl, _, _) in enumerate(blocks):
                dkt_ref[jl] += _dot(qt, jnp.concatenate([dz[b, 0], dz[b, 1]], axis=0))
                dvt_ref[jl] += _dot(dot_, jnp.concatenate([att[b, 0], att[b, 1]], axis=0))
            return tuple(prefixes), dq

        def run(ja, n):
            return [(ja + i, ja + i, None) for i in range(n)]

        col0 = jnp.zeros((Q_BLK, 1), F32)
        fours = (jd // 2) // 2
        carry = lax.fori_loop(0, fours, lambda it, cr: group(run(4 * it, 4), cr),
                              ((col0, col0), jnp.zeros((Q_BLK, LANES), F32)))
        carry = lax.cond((jd // 2) % 2 == 1, lambda cr: group(run(4 * fours, 2), cr), lambda cr: cr, carry)
        carry = lax.cond(jd % 2 == 1, lambda cr: group([(jd - 1, jd - 1, None), (jd, jd, causal)], cr),
                         lambda cr: group([(jd, jd, causal)], cr), carry)
        dq_ref[...] = carry[1] * scale

        @pl.when(qi == nq - 1)
        def _():
            for j in range(nkb):
                dk_ref[j * K_BLK:(j + 1) * K_BLK, :] = dkt_ref[j].T
                dv_ref[j * K_BLK:(j + 1) * K_BLK, :] = dvt_ref[j].T

        pl.when(step_no == B * nhp * nq - 1)(wait_ride)

    qblk = pl.BlockSpec((Q_BLK, LANES), lambda b, hp, qi: (b * nq + qi, hp))
    tri = pl.BlockSpec((K_BLK, K_BLK), lambda b, hp, qi: (0, 0))
    seq = pl.BlockSpec((S, LANES), lambda b, hp, qi: (b, hp))
    return pl.pallas_call(
        body, name="attn_bwd", grid=(B, nhp, nq),
        in_specs=[qblk, pl.BlockSpec((S, LANES), lambda b, hp, qi: (b, nhp + hp)),
                  pl.BlockSpec((S, LANES), lambda b, hp, qi: (b, 2 * nhp + hp)),
                  pl.BlockSpec((Q_BLK, 2 * LANES), lambda b, hp, qi: (b * nq + qi, hp)), qblk, tri, tri] + ride_specs,
        out_specs=[qblk, seq, seq] + ride_specs,
        out_shape=[jax.ShapeDtypeStruct((T, SB_W), F32)] * 3 + ride_shape,
        scratch_shapes=[pltpu.VMEM((nkb, LANES, K_BLK), F32)] * 2 + ride_sems,
        compiler_params=_params(("arbitrary", "arbitrary", "arbitrary"), 48),
    )(qkv, qkv, qkv, run_tab, dy_sb, tri_suffix, tri_prefix, *ride)


def _lru_bwd(proj_lru, h, dy_lru, conv_w, conv_b, wa_bd, wx_bd, b_a, b_x, lam, B, S, lc):
    T = B * S
    nc = S // lc
    ncb = LRU_W // LANES
    hpc = lc // SUBLANES

    def body(x_ref, xh_ref, g_ref, h_ref, hh_ref, dy_ref, cw_ref, cb_ref, wa_ref, wx_ref, ba_ref, bx_ref, lam_ref,
             dx_ref, dg_ref, dcw_ref, dcb_ref, dwa_ref, dwx_ref, dba_ref, dbx_ref, dlam_ref,
             lam_s, dc_s, a_first, lam_first, dc_head):
        b, ci = pl.program_id(1), pl.program_id(2)
        first_chunk = ci == nc - 1

        @pl.when(ci == 0)
        def _():
            a_first[...] = jnp.zeros_like(a_first)
            lam_first[...] = jnp.zeros_like(lam_first)
            dc_head[...] = jnp.zeros_like(dc_head)

        @pl.when((b == 0) & (ci == 0))
        def _():
            for ref in (dcw_ref, dcb_ref, dwa_ref, dwx_ref, dba_ref, dbx_ref, dlam_ref):
                ref[...] = jnp.zeros_like(ref)

        x = x_ref[...]
        taps = _conv_taps(x, jnp.where(first_chunk, 0.0, xh_ref[...]), lc)
        c = cb_ref[...] + sum(cw_ref[pl.ds(CONV_K - 1 - k, 1), :] * taps[k] for k in range(CONV_K))
        r, i, sp, dsp_dlam, a, s = _lru_gates(c, wa_ref, wx_ref, ba_ref, bx_ref, lam_ref)
        hv = h_ref[...]
        he = jnp.concatenate([jnp.where(first_chunk, 0.0, hh_ref[...]), hv], axis=0)
        h_prev = pltpu.roll(he, 1, 0)[SUBLANES:SUBLANES + lc]
        dy = dy_ref[...]
        gelu, dgelu = _gelu_parts(g_ref[...])
        dg_ref[...] = dy * hv * dgelu

        row = lax.broadcasted_iota(jnp.int32, (lc, LANES), 0)
        a_next = jnp.where(row < lc - 1, pltpu.roll(a, lc - 1, 0), a_first[...])
        lam_loc, a_run = _scan(a_next, dy * gelu, reverse=True)
        lam_s[...] = lam_loc + a_run * lam_first[...]
        lam_t = lam_s[...]
        lam_first[...] = lam_s[pl.ds(0, 1), :]
        lam_s[...] = a
        a_first[...] = lam_s[pl.ds(0, 1), :]

        ic = i * c
        dlog_a = lam_t * h_prev * a - (lam_t * ic) * (a * a) / s
        dpre_r = (dlog_a * ((-LRU_C) * sp)) * r * (1.0 - r)
        dpre_i = (lam_t * s * c) * i * (1.0 - i)
        dlam_ref[...] += jnp.sum(dlog_a * r, axis=0, keepdims=True) * ((-LRU_C) * dsp_dlam)
        dr16 = dpre_r.astype(BF16)
        di16 = dpre_i.astype(BF16)
        c16 = c.astype(BF16)
        dwa_ref[0] += _dot_tn(c16, dr16)
        dwx_ref[0] += _dot_tn(c16, di16)
        dba_ref[...] += jnp.sum(dpre_r, axis=0, keepdims=True)
        dbx_ref[...] += jnp.sum(dpre_i, axis=0, keepdims=True)
        dc = lam_t * s * i + _dot_nt(dr16, wa_ref[0]) + _dot_nt(di16, wx_ref[0])
        dcb_ref[...] += jnp.sum(dc, axis=0, keepdims=True)
        for k in range(CONV_K):
            dcw_ref[pl.ds(CONV_K - 1 - k, 1), :] += jnp.sum(dc * taps[k], axis=0, keepdims=True)
        dce = jnp.concatenate([dc, dc_head[...]], axis=0)
        dx = cw_ref[pl.ds(CONV_K - 1, 1), :] * dc
        for k in range(1, CONV_K):
            dx = dx + cw_ref[pl.ds(CONV_K - 1 - k, 1), :] * pltpu.roll(dce, lc + SUBLANES - k, 0)[0:lc]
        dx_ref[...] = dx
        dc_s[...] = dc
        dc_head[...] = dc_s[pl.ds(0, SUBLANES), :]

    def chunk(col):
        return pl.BlockSpec((lc, LANES), lambda cb, b, ci: (b * nc + nc - 1 - ci, col(cb)))

    def halo(col):
        return pl.BlockSpec((SUBLANES, LANES),
                            lambda cb, b, ci: (jnp.maximum((b * nc + nc - 1 - ci) * hpc - 1, 0), col(cb)))

    chan = lambda cb, b, ci: (0, cb)
    blk = lambda cb, b, ci: (cb, 0, 0)
    vec = pl.BlockSpec((1, LANES), chan)
    mat = pl.BlockSpec((1, LANES, LANES), blk)
    return pl.pallas_call(
        body, name="lru_bwd", grid=(ncb, B, nc),
        in_specs=[chunk(lambda cb: cb), halo(lambda cb: cb), chunk(lambda cb: ncb + cb),
                  chunk(lambda cb: cb), halo(lambda cb: cb), chunk(lambda cb: cb),
                  pl.BlockSpec((CONV_K, LANES), chan), vec, mat, mat, vec, vec, vec],
        out_specs=[chunk(lambda cb: cb), chunk(lambda cb: cb), pl.BlockSpec((CONV_K, LANES), chan), vec,
                   mat, mat, vec, vec, vec],
        out_shape=[jax.ShapeDtypeStruct((T, LRU_W), F32), jax.ShapeDtypeStruct((T, LRU_W), F32),
                   jax.ShapeDtypeStruct((CONV_K, LRU_W), F32), jax.ShapeDtypeStruct((1, LRU_W), F32),
                   jax.ShapeDtypeStruct((ncb, LANES, LANES), F32), jax.ShapeDtypeStruct((ncb, LANES, LANES), F32),
                   jax.ShapeDtypeStruct((1, LRU_W), F32), jax.ShapeDtypeStruct((1, LRU_W), F32),
                   jax.ShapeDtypeStruct((1, LRU_W), F32)],
        scratch_shapes=[pltpu.VMEM((lc, LANES), F32), pltpu.VMEM((lc, LANES), F32), pltpu.VMEM((1, LANES), F32),
                        pltpu.VMEM((1, LANES), F32), pltpu.VMEM((SUBLANES, LANES), F32)],
        compiler_params=_params(("arbitrary", "arbitrary", "arbitrary"), 32),
    )(proj_lru, proj_lru, proj_lru, h, h, dy_lru, conv_w, conv_b, wa_bd, wx_bd, b_a, b_x, lam)


def _bwd_in(pieces, w_in16t, x2, g1, dh1, tm):
    T = x2.shape[0]
    npc = len(pieces)

    def body(*refs):
        p_refs = refs[:npc]
        w_ref, x_ref, g_ref, d_ref, dx_ref, dproj_ref, xn_ref, dg1_ref = refs[npc:]

        @pl.when(pl.program_id(0) == 0)
        def _():
            dg1_ref[...] = jnp.zeros_like(dg1_ref)

        dxn = jnp.zeros((tm, D_MODEL), F32)
        for n, p_ref in enumerate(p_refs):
            cols = slice(n * LRU_W, (n + 1) * LRU_W)
            p16 = p_ref[...].astype(BF16)
            dproj_ref[:, cols] = p16
            dxn = dxn + _dot(p16, w_ref[cols, :])
        gv = g_ref[...]
        xn, xhat, rstd = _rms_fwd(x_ref[...], gv)
        xn_ref[...] = xn.astype(BF16)
        dg1_ref[...] += jnp.sum(dxn * xhat, axis=0, keepdims=True)
        dx_ref[...] = d_ref[...] + _rms_bwd(dxn, xhat, rstd, gv)

    row = lambda i: (i, 0)
    fix = lambda i: (0, 0)
    return pl.pallas_call(
        body, name="bwd_in", grid=(T // tm,),
        in_specs=[pl.BlockSpec((tm, LRU_W), row)] * npc + [
            pl.BlockSpec((IN_COLS, D_MODEL), fix), pl.BlockSpec((tm, D_MODEL), row),
            pl.BlockSpec((1, D_MODEL), fix), pl.BlockSpec((tm, D_MODEL), row)],
        out_specs=[pl.BlockSpec((tm, D_MODEL), row), pl.BlockSpec((tm, IN_COLS), row),
                   pl.BlockSpec((tm, D_MODEL), row), pl.BlockSpec((1, D_MODEL), fix)],
        out_shape=[jax.ShapeDtypeStruct((T, D_MODEL), F32), jax.ShapeDtypeStruct((T, IN_COLS), BF16),
                   jax.ShapeDtypeStruct((T, D_MODEL), BF16), jax.ShapeDtypeStruct((1, D_MODEL), F32)],
        compiler_params=_params(("arbitrary",), 56),
    )(*pieces, w_in16t, x2, g1, dh1)


def _adam_shard(name, parts, w, m, v, tr):
    R, C = w.shape

    def body(p_ref, w_ref, m_ref, v_ref, g_ref, d_ref, m2_ref, v2_ref):
        g = p_ref[0].astype(F32)
        for p in range(1, N_DEV):
            g = g + p_ref[p].astype(F32)
        g_ref[...] = g
        d_ref[...], m2_ref[...], v2_ref[...] = _adamw(w_ref[...], g, m_ref[...], v_ref[...])

    blk = pl.BlockSpec((tr, C), lambda i: (i, 0))
    return pl.pallas_call(
        body, name=name, grid=(R // tr,),
        in_specs=[pl.BlockSpec((N_DEV, tr, C), lambda i: (0, i, 0)), blk, blk, blk],
        out_specs=[blk] * 4, out_shape=[jax.ShapeDtypeStruct((R, C), F32)] * 4,
        compiler_params=_params(("arbitrary",), 48),
    )(parts, w, m, v)


def _adam_param(name, parts, w, m, v):
    def body(p_ref, w_ref, m_ref, v_ref, g_ref, d_ref, m2_ref, v2_ref):
        g = p_ref[0]
        for p in range(1, N_DEV):
            g = g + p_ref[p]
        g_ref[...] = g
        d_ref[...], m2_ref[...], v2_ref[...] = _adamw(w_ref[...], g, m_ref[...], v_ref[...])

    return pl.pallas_call(body, name=name, out_shape=[jax.ShapeDtypeStruct(w.shape, F32)] * 4)(parts, w, m, v)


def _adam_given(name, g, w, m, v, tr):
    R, C = w.shape

    def body(g_ref, w_ref, m_ref, v_ref, d_ref, m2_ref, v2_ref):
        d_ref[...], m2_ref[...], v2_ref[...] = _adamw(w_ref[...], g_ref[...], m_ref[...], v_ref[...])

    blk = pl.BlockSpec((tr, C), lambda i: (i, 0))
    return [g] + list(pl.pallas_call(
        body, name=name, grid=(R // tr,), in_specs=[blk] * 4, out_specs=[blk] * 3,
        out_shape=[jax.ShapeDtypeStruct((R, C), F32)] * 3, compiler_params=_params(("arbitrary",), 32),
    )(g, w, m, v))


def _sum_parts(name, parts):
    def body(p_ref, g_ref):
        g = p_ref[0].astype(F32)
        for p in range(1, N_DEV):
            g = g + p_ref[p].astype(F32)
        g_ref[...] = g

    return pl.pallas_call(body, name=name, out_shape=jax.ShapeDtypeStruct(parts.shape[1:], F32))(parts)


def _block_diag_pairs(w):
    w = w.reshape(LRU_BLOCKS // 2, 2, HEAD_D, HEAD_D)
    out = jnp.zeros((LRU_BLOCKS // 2, LANES, LANES), w.dtype)
    out = out.at[:, :HEAD_D, :HEAD_D].set(w[:, 0])
    return out.at[:, HEAD_D:, HEAD_D:].set(w[:, 1])


def _diag_blocks(w):
    return jnp.stack([w[:, :HEAD_D, :HEAD_D], w[:, HEAD_D:, HEAD_D:]], axis=1).reshape(LRU_BLOCKS, HEAD_D, HEAD_D)


def kernel(x, norm1_g, w_in, conv_w, conv_b, lru_w_a, lru_b_a, lru_w_x, lru_b_x, lru_lambda, lru_out_g, sb_out_g, w_out, norm2_g, w_up, w_down, final_g, loss_target, m_norm1_g, m_w_in, m_conv_w, m_conv_b, m_lru_w_a, m_lru_b_a, m_lru_w_x, m_lru_b_x, m_lru_lambda, m_lru_out_g, m_sb_out_g, m_w_out, m_norm2_g, m_w_up, m_w_down, m_final_g, v_norm1_g, v_w_in, v_conv_w, v_conv_b, v_lru_w_a, v_lru_b_a, v_lru_w_x, v_lru_b_x, v_lru_lambda, v_lru_out_g, v_sb_out_g, v_w_out, v_norm2_g, v_w_up, v_w_down, v_final_g):
    B, S, _ = x.shape
    T = B * S
    tm = min(512, T)
    tk = min(GRAD_TK, T)
    lc = min(512, S)
    x2 = x.reshape(T, D_MODEL)
    tgt = loss_target.reshape(T, D_MODEL)
    cw_cols = CONV_K * LRU_W // N_DEV // CONV_K

    shards16 = _cast_shards([w_in[0].T, w_out[0], w_up[0], w_down[0]])
    cw_pad = jnp.zeros((SUBLANES, LANES), F32).at[:CONV_K, :cw_cols].set(conv_w[0])
    g_in, g_cw = _exchange("gather_w_in", [shards16[0], cw_pad], [False, False])
    w_in16t = g_in.reshape(IN_COLS, D_MODEL)
    conv_w_full = g_cw[:, :CONV_K, :cw_cols].transpose(1, 0, 2).reshape(CONV_K, LRU_W)
    wa_bd = _block_diag_pairs(lru_w_a[0]).astype(BF16)
    wx_bd = _block_diag_pairs(lru_w_x[0]).astype(BF16)
    b_a = lru_b_a.reshape(1, LRU_W)
    b_x = lru_b_x.reshape(1, LRU_W)
    gf = final_g.reshape(1, D_MODEL)

    proj_lru, qkv = _fwd_in(x2, norm1_g, w_in16t, tm)
    y_lru, h = _lru_fwd(proj_lru, conv_w_full, conv_b, wa_bd, wx_bd, b_a, b_x, lru_lambda, B, S, lc)
    tri_suffix, tri_prefix = _tri(False), _tri(True)
    y_sb, run_tab, g_out, g_up, g_down = _attn_fwd(qkv, tri_suffix, list(shards16[1:]), [False] * 3, B, S)
    w_out16 = g_out.reshape(D_MODEL, D_MODEL)
    w_down16 = g_down.reshape(D_FF, D_MODEL)
    h1, mix16 = _fwd_mix(y_lru, y_sb, lru_out_g, sb_out_g, w_out16, x2, tm)
    up16, dh2, d_final_g, loss_part = _fwd_mlp(h1, norm2_g, g_up, w_down16, gf, tgt, min(1024, T), 2)

    dup16, dh1, hn16, dh2b, d_norm2_g = _bwd_mlp(dh2, up16, h1, norm2_g, g_up, w_down16, tm, 4)
    sq = lambda u: (u.astype(F32) * u.astype(F32)).astype(BF16)
    wide = 4
    gw_up, = _matmul_tn("grad_w_up", hn16, dup16, D_MODEL, wide * FF_CHUNK, tk // 2, (N_DEV, D_MODEL, FF_CHUNK),
                        (wide, D_MODEL, FF_CHUNK), lambda m, n, k: (n, 0, 0), split=(1, wide))
    gw_down, = _matmul_tn("grad_w_down", up16, dh2b, wide * FF_CHUNK, D_MODEL, tk // 2, (N_DEV, FF_CHUNK, D_MODEL),
                          (wide, FF_CHUNK, D_MODEL), lambda m, n, k: (m, 0, 0), a_prep=sq, split=(0, wide))
    dy_lru, dy_sb, d_lru_out_g, d_sb_out_g = _bwd_mix(dh1, w_out16, y_lru, y_sb, lru_out_g, sb_out_g, tm)
    gw_out, = _matmul_tn("grad_w_out", mix16, dh1, D_MODEL, FF_CHUNK, tk, (D_MODEL, D_MODEL),
                         (D_MODEL, FF_CHUNK), lambda m, n, k: (0, n), b_prep=lambda u: u.astype(BF16))
    parts_out = gw_out.reshape(N_DEV, D_MODEL // N_DEV, D_MODEL)
    dq, dk, dv, r_out, r_up, r_down = _attn_bwd(qkv, run_tab, dy_sb, tri_suffix, tri_prefix,
                                                [parts_out, gw_up, gw_down], [True] * 3, B, S)
    (dx_lru, dg_lru, d_conv_w, d_conv_b, d_wa, d_wx, d_b_a, d_b_x, d_lambda) = _lru_bwd(
        proj_lru, h, dy_lru, conv_w_full, conv_b, wa_bd, wx_bd, b_a, b_x, lru_lambda, B, S, lc)
    dx, dproj16, xn16, d_norm1_g = _bwd_in([dx_lru, dg_lru, dq, dk, dv], w_in16t, x2, norm1_g, dh1, tm)

    small = {"norm1_g": (d_norm1_g, norm1_g, m_norm1_g, v_norm1_g), "conv_b": (d_conv_b, conv_b, m_conv_b, v_conv_b),
             "lru_w_a": (_diag_blocks(d_wa), lru_w_a, m_lru_w_a, v_lru_w_a),
             "lru_b_a": (d_b_a, lru_b_a, m_lru_b_a, v_lru_b_a),
             "lru_w_x": (_diag_blocks(d_wx), lru_w_x, m_lru_w_x, v_lru_w_x),
             "lru_b_x": (d_b_x, lru_b_x, m_lru_b_x, v_lru_b_x),
             "lru_lambda": (d_lambda, lru_lambda, m_lru_lambda, v_lru_lambda),
             "lru_out_g": (d_lru_out_g, lru_out_g, m_lru_out_g, v_lru_out_g),
             "sb_out_g": (d_sb_out_g, sb_out_g, m_sb_out_g, v_sb_out_g),
             "norm2_g": (d_norm2_g, norm2_g, m_norm2_g, v_norm2_g),
             "final_g": (d_final_g, final_g, m_final_g, v_final_g)}
    names = list(small)

    def held(n, a):
        return a.reshape((1, D_MODEL) if n == "final_g" else small[n][1].shape)

    parts_cw = d_conv_w.reshape(CONV_K, N_DEV, cw_cols).transpose(1, 0, 2)[:, None]
    gw_in_t, *got = _matmul_tn(
        "grad_w_in", dproj16, xn16, IN_COLS // 2, D_MODEL, tk // 2, (IN_COLS, D_MODEL), (IN_COLS // 2, D_MODEL),
        lambda m, n, k: (m, 0), out_dtype=BF16,
        ride=[parts_cw] + [held(n, small[n][0]) for n in names] + [loss_part],
        ride_sliced=[True] + [False] * (len(names) + 1))
    r_in, = _exchange("exchange_grads", [gw_in_t.reshape(N_DEV, IN_COLS // N_DEV, D_MODEL)], [True])

    g_in_t = _sum_parts("sum_w_in", r_in)
    out = {"w_in": _adam_given("adam_w_in", g_in_t.T, w_in[0], m_w_in[0], v_w_in[0], 256),
           "w_out": _adam_shard("adam_w_out", r_out, w_out[0], m_w_out[0], v_w_out[0], 64),
           "w_up": _adam_shard("adam_w_up", r_up, w_up[0], m_w_up[0], v_w_up[0], 256),
           "w_down": _adam_shard("adam_w_down", r_down, w_down[0], m_w_down[0], v_w_down[0], 128)}
    out = {n: [a[None] for a in res] for n, res in out.items()}
    out["conv_w"] = _adam_param("adam_conv_w", got[0], conv_w, m_conv_w, v_conv_w)
    for n, parts in zip(names, got[1:-1]):
        _, w, m, v = small[n]
        res = _adam_param("adam_" + n, parts, held(n, w), held(n, m), held(n, v))
        out[n] = [a.reshape(w.shape) for a in res]
    loss = _sum_parts("sum_loss", got[-1])[0, 0]
    weights = ["norm1_g", "w_in", "conv_w", "conv_b", "lru_w_a", "lru_b_a", "lru_w_x", "lru_b_x", "lru_lambda",
               "lru_out_g", "sb_out_g", "w_out", "norm2_g", "w_up", "w_down", "final_g"]
    return (loss, dx.reshape(B, S, D_MODEL), *[out[n][0] for n in weights], *[out[n][1] for n in weights],
            *[out[n][2] for n in weights], *[out[n][3] for n in weights])
```

```python
import jax
import jax.numpy as jnp
from jax import lax
from jax.experimental import pallas as pl
from jax.experimental.pallas import tpu as pltpu

F32 = jnp.float32
BF16 = jnp.bfloat16

D_MODEL = 1024
LRU_W = 512
SB_W = 512
HEAD_D = 64
D_FF = 4096
IN_COLS = 2 * LRU_W + 3 * SB_W
CONV_K = 4
LRU_BLOCKS = 8
LRU_C = 8.0
EPS = 1e-6
N_DEV = 8
LANES = 128
SUBLANES = 8
FF_CHUNK = 512
GRAD_TK = 2048
Q_BLK = 256
K_BLK = 256
Q_PER_K = K_BLK // Q_BLK

ADAM_LR = 0.001
ADAM_B1 = 0.9
ADAM_B2 = 0.999
ADAM_EPS = 1e-08
ADAM_WD = 0.01
ADAM_STEP = 10


def _params(sem=None, vmem_mb=None):
    kw = {}
    if sem is not None:
        kw["dimension_semantics"] = sem
    if vmem_mb is not None:
        kw["vmem_limit_bytes"] = vmem_mb << 20
    return pltpu.CompilerParams(**kw)


def _dot(a, b):
    return jnp.dot(a, b, preferred_element_type=F32)


def _dot_nt(a, b):
    return lax.dot_general(a, b, (((1,), (1,)), ((), ())), preferred_element_type=F32)


def _dot_tn(a, b):
    return lax.dot_general(a, b, (((0,), (0,)), ((), ())), preferred_element_type=F32)


def _rms_fwd(x, g):
    rstd = lax.rsqrt(jnp.mean(x * x, axis=-1, keepdims=True) + EPS)
    xhat = x * rstd
    return xhat * g, xhat, rstd


def _rms_bwd(dy, xhat, rstd, g):
    dxhat = dy * g
    return rstd * (dxhat - xhat * jnp.mean(dxhat * xhat, axis=-1, keepdims=True))


def _sigmoid(x):
    return 1.0 / (1.0 + jnp.exp(-x))


def _log1p_pos(e):
    series = e * (1.0 - e * (0.5 - e * (1.0 / 3.0 - e * 0.25)))
    return jnp.where(e < 1e-2, series, jnp.log(1.0 + e))


def _neg_expm1(x):
    series = -x * (1.0 + x * (0.5 + x * (1.0 / 6.0 + x * (1.0 / 24.0))))
    return jnp.where(x > -1e-2, series, 1.0 - jnp.exp(x))


def _gelu_parts(g):
    k0 = 0.7978845608028654
    k1 = 0.044715
    t = jnp.tanh(k0 * (g + k1 * g * g * g))
    val = 0.5 * g * (1.0 + t)
    grad = 0.5 * (1.0 + t) + 0.5 * g * (1.0 - t * t) * k0 * (1.0 + 3.0 * k1 * g * g)
    return val, grad


def _scan(a, b, reverse):
    n = a.shape[0]
    row = lax.broadcasted_iota(jnp.int32, a.shape, 0)
    s = 1
    while s < n:
        if reverse:
            keep = row < n - s
            shift = n - s
        else:
            keep = row >= s
            shift = s
        bs = jnp.where(keep, pltpu.roll(b, shift, 0), 0.0)
        a_s = jnp.where(keep, pltpu.roll(a, shift, 0), 1.0)
        b = a * bs + b
        a = a * a_s
        s *= 2
    return b, a


def _adamw(w, g, m, v):
    m = ADAM_B1 * m + (1.0 - ADAM_B1) * g
    v = ADAM_B2 * v + (1.0 - ADAM_B2) * (g * g)
    m_hat = m / (1.0 - ADAM_B1 ** ADAM_STEP)
    v_hat = v / (1.0 - ADAM_B2 ** ADAM_STEP)
    delta = -ADAM_LR * (m_hat / (jnp.sqrt(v_hat) + ADAM_EPS) + ADAM_WD * w)
    return delta, m, v


def _my_index():
    return 4 * lax.axis_index("x") + 2 * lax.axis_index("y") + lax.axis_index("c")


def _peer(k):
    x, y, c = lax.axis_index("x"), lax.axis_index("y"), lax.axis_index("c")
    px = 1 - x if (k >> 2) & 1 else x
    py = 1 - y if (k >> 1) & 1 else y
    pc = 1 - c if k & 1 else c
    return (px, py, pc), 4 * px + 2 * py + pc


def _exchange_shapes(srcs, sliced):
    n = len(srcs)
    out_shape = [jax.ShapeDtypeStruct(s.shape if sl else (N_DEV,) + s.shape, s.dtype) for s, sl in zip(srcs, sliced)]
    specs = [pl.BlockSpec(memory_space=pl.ANY)] * n
    sems = [pltpu.SemaphoreType.DMA((n, N_DEV - 1)), pltpu.SemaphoreType.DMA((n, N_DEV - 1)),
            pltpu.SemaphoreType.DMA((n,))]
    return out_shape, specs, sems


def _exchange_copies(ins, outs, sliced, send_sems, recv_sems, local_sems):
    n = len(ins)

    def part(a, p):
        return ins[a].at[p] if sliced[a] else ins[a]

    def copies(receiving):
        me = _my_index()
        local = [pltpu.make_async_copy(part(a, me), outs[a].at[me], local_sems.at[a]) for a in range(n)]
        remote = []
        for k in range(1, N_DEV):
            dev, idx = _peer(k)
            for a in range(n):
                remote.append(pltpu.make_async_remote_copy(
                    src_ref=part(a, idx), dst_ref=outs[a].at[idx if receiving else me],
                    send_sem=send_sems.at[a, k - 1], recv_sem=recv_sems.at[a, k - 1],
                    device_id=dev, device_id_type=pl.DeviceIdType.MESH))
        return local, remote

    def start():
        local, remote = copies(receiving=False)
        for cp in local + remote:
            cp.start()

    def wait():
        local, remote = copies(receiving=True)
        for cp in remote + local:
            cp.wait()

    return start, wait


def _gather_two_level(name, srcs):
    n = len(srcs)
    out_shape = [jax.ShapeDtypeStruct((N_DEV,) + s.shape, s.dtype) for s in srcs]

    def body(*refs):
        ins, outs = refs[:n], refs[n:2 * n]
        send_sems, recv_sems, local_sems = refs[2 * n:]
        x, y, c = lax.axis_index("x"), lax.axis_index("y"), lax.axis_index("c")
        here, sibling = (x, y, c), (x, y, 1 - c)
        chips = [(1 - x, y), (x, 1 - y), (1 - x, 1 - y)]

        def slot(px, py, pc):
            return 4 * px + 2 * py + pc

        def copy(a, k, block, to, src=None):
            return pltpu.make_async_remote_copy(
                src_ref=ins[a] if src is None else src, dst_ref=outs[a].at[slot(*block)],
                send_sem=send_sems.at[a, k], recv_sem=recv_sems.at[a, k], device_id=to,
                device_id_type=pl.DeviceIdType.MESH)

        local = [pltpu.make_async_copy(ins[a], outs[a].at[slot(*here)], local_sems.at[a]) for a in range(n)]
        first = [copy(a, 0, here, sibling) for a in range(n)]
        first += [copy(a, 1 + j, here, (*chip, c)) for j, chip in enumerate(chips) for a in range(n)]
        for cp in local + first:
            cp.start()
        passed = []
        for j, chip in enumerate(chips):
            for a in range(n):
                copy(a, 1 + j, (*chip, c), here).wait_recv()
                passed.append(copy(a, 4 + j, (*chip, c), sibling, src=outs[a].at[slot(*chip, c)]))
                passed[-1].start()
        for a in range(n):
            copy(a, 0, sibling, here).wait_recv()
        for j, chip in enumerate(chips):
            for a in range(n):
                copy(a, 4 + j, (*chip, 1 - c), here).wait_recv()
        for cp in first + passed:
            cp.wait_send()
        for cp in local:
            cp.wait()

    spec = [pl.BlockSpec(memory_space=pl.ANY)] * n
    return pl.pallas_call(
        body, name=name, out_shape=out_shape, in_specs=spec, out_specs=spec,
        scratch_shapes=[pltpu.SemaphoreType.DMA((n, N_DEV - 1)), pltpu.SemaphoreType.DMA((n, N_DEV - 1)),
                        pltpu.SemaphoreType.DMA((n,))],
    )(*srcs)


def _cast_shards(ws):
    def body(*refs):
        for i in range(len(ws)):
            refs[len(ws) + i][...] = refs[i][...].astype(BF16)

    return pl.pallas_call(
        body, name="cast_shards", out_shape=[jax.ShapeDtypeStruct(w.shape, BF16) for w in ws],
        compiler_params=_params(vmem_mb=32),
    )(*ws)


def _fwd_in(x2, g1, w_in16t, tm):
    T = x2.shape[0]

    def body(x_ref, g_ref, w_ref, lru_ref, qkv_ref):
        xn, _, _ = _rms_fwd(x_ref[...], g_ref[...])
        xn = xn.astype(BF16)
        lru_ref[...] = _dot_nt(xn, w_ref[0:2 * LRU_W, :])
        qkv_ref[...] = _dot_nt(xn, w_ref[2 * LRU_W:IN_COLS, :]).astype(BF16)

    return pl.pallas_call(
        body, name="fwd_in", grid=(T // tm,),
        in_specs=[pl.BlockSpec((tm, D_MODEL), lambda i: (i, 0)),
                  pl.BlockSpec((1, D_MODEL), lambda i: (0, 0)),
                  pl.BlockSpec((IN_COLS, D_MODEL), lambda i: (0, 0))],
        out_specs=[pl.BlockSpec((tm, 2 * LRU_W), lambda i: (i, 0)),
                   pl.BlockSpec((tm, 3 * SB_W), lambda i: (i, 0))],
        out_shape=[jax.ShapeDtypeStruct((T, 2 * LRU_W), F32), jax.ShapeDtypeStruct((T, 3 * SB_W), BF16)],
        compiler_params=_params(("arbitrary",), 48),
    )(x2, g1, w_in16t)


def _lru_gates(c, wa_ref, wx_ref, ba_ref, bx_ref, lam_ref):
    c16 = c.astype(BF16)
    r = _sigmoid(_dot(c16, wa_ref[0]) + ba_ref[...])
    i = _sigmoid(_dot(c16, wx_ref[0]) + bx_ref[...])
    lam = lam_ref[...]
    e = jnp.exp(-jnp.abs(lam))
    sp = jnp.maximum(-lam, 0.0) + _log1p_pos(e)
    dsp_dlam = -jnp.where(lam >= 0.0, e, 1.0) / (1.0 + e)
    log_a = (-LRU_C) * r * sp
    a = jnp.exp(log_a)
    s = jnp.sqrt(_neg_expm1(2.0 * log_a))
    return r, i, sp, dsp_dlam, a, s


def _conv_taps(x, halo, lc):
    xe = jnp.concatenate([halo, x], axis=0)
    return [x] + [pltpu.roll(xe, k, 0)[SUBLANES:SUBLANES + lc] for k in range(1, CONV_K)]


def _lru_fwd(proj_lru, conv_w, conv_b, wa_bd, wx_bd, b_a, b_x, lam, B, S, lc):
    T = B * S
    nc = S // lc
    ncb = LRU_W // LANES

    def body(x_ref, g_ref, cw_ref, cb_ref, wa_ref, wx_ref, ba_ref, bx_ref, lam_ref, y_ref, h_ref, tail, carry):
        ci = pl.program_id(2)

        @pl.when(ci == 0)
        def _():
            tail[...] = jnp.zeros_like(tail)
            carry[...] = jnp.zeros_like(carry)

        x = x_ref[...]
        taps = _conv_taps(x, tail[...], lc)
        c = cb_ref[...] + sum(cw_ref[pl.ds(CONV_K - 1 - k, 1), :] * taps[k] for k in range(CONV_K))
        tail[...] = x_ref[pl.ds(lc - SUBLANES, SUBLANES), :]
        r, i, sp, _, a, s = _lru_gates(c, wa_ref, wx_ref, ba_ref, bx_ref, lam_ref)
        h_loc, a_run = _scan(a, s * (i * c), reverse=False)
        h_ref[...] = h_loc + a_run * carry[...]
        carry[...] = h_ref[pl.ds(lc - 1, 1), :]
        gelu, _ = _gelu_parts(g_ref[...])
        y_ref[...] = h_ref[...] * gelu

    chan = lambda b, cb, ci: (0, cb)
    return pl.pallas_call(
        body, name="lru_fwd", grid=(B, ncb, nc),
        in_specs=[pl.BlockSpec((lc, LANES), lambda b, cb, ci: (b * nc + ci, cb)),
                  pl.BlockSpec((lc, LANES), lambda b, cb, ci: (b * nc + ci, ncb + cb)),
                  pl.BlockSpec((CONV_K, LANES), chan), pl.BlockSpec((1, LANES), chan),
                  pl.BlockSpec((1, LANES, LANES), lambda b, cb, ci: (cb, 0, 0)),
                  pl.BlockSpec((1, LANES, LANES), lambda b, cb, ci: (cb, 0, 0)),
                  pl.BlockSpec((1, LANES), chan), pl.BlockSpec((1, LANES), chan), pl.BlockSpec((1, LANES), chan)],
        out_specs=[pl.BlockSpec((lc, LANES), lambda b, cb, ci: (b * nc + ci, cb))] * 2,
        out_shape=[jax.ShapeDtypeStruct((T, LRU_W), F32)] * 2,
        scratch_shapes=[pltpu.VMEM((SUBLANES, LANES), F32), pltpu.VMEM((1, LANES), F32)],
        compiler_params=_params(("arbitrary", "arbitrary", "arbitrary"), 32),
    )(proj_lru, proj_lru, conv_w, conv_b, wa_bd, wx_bd, b_a, b_x, lam)


def _tri(prefix):
    r = lax.broadcasted_iota(jnp.int32, (K_BLK, K_BLK), 0)
    c = lax.broadcasted_iota(jnp.int32, (K_BLK, K_BLK), 1)
    return ((r <= c) if prefix else (r >= c)).astype(BF16)


def _attn_consts(qi):
    r = lax.broadcasted_iota(jnp.int32, (Q_BLK, K_BLK), 0)
    c = lax.broadcasted_iota(jnp.int32, (Q_BLK, K_BLK), 1)
    causal = c + ((qi // Q_PER_K) * K_BLK - qi * Q_BLK) < r
    lane = lax.broadcasted_iota(jnp.int32, (1, LANES), 1)
    return causal, lane, (lane < HEAD_D, lane >= HEAD_D)


def _log1m_beta(z, mask):
    lg = -(jnp.maximum(z, 0.0) + jnp.log(1.0 + jnp.exp(-jnp.abs(z))))
    return lg if mask is None else jnp.where(mask, lg, 0.0)


def _key_rows(j):
    return pl.ds(pl.multiple_of(j * K_BLK, K_BLK), K_BLK)


def _attn_fwd(qkv, tri_suffix, ride, ride_sliced, B, S):
    T = B * S
    nq = S // Q_BLK
    nhp = SB_W // LANES
    scale = HEAD_D ** -0.5
    assert S // K_BLK <= LANES
    nr = len(ride)
    ride_shape, ride_specs, ride_sems = _exchange_shapes(ride, ride_sliced)

    def body(q_ref, k_ref, v_ref, tri_ref, *rest):
        o_ref, run_ref = rest[nr:nr + 2]
        start_ride, wait_ride = _exchange_copies(rest[:nr], rest[nr + 2:2 * nr + 2], ride_sliced, *rest[2 * nr + 2:])
        qi = pl.program_id(2)
        step_no = (pl.program_id(0) * nhp + pl.program_id(1)) * nq + qi
        pl.when(step_no == 0)(start_ride)
        jd = qi // Q_PER_K
        causal, lane, halves = _attn_consts(qi)
        q = q_ref[...]
        qh = [jnp.where(hm, q, jnp.zeros_like(q)) * jnp.asarray(scale, BF16) for hm in halves]

        def group(blocks, carry):
            runs, tables, acc = carry
            runs, tables = list(runs), list(tables)
            ks = [k_ref[_key_rows(jl), :] for jl, _, _ in blocks]
            vs = [v_ref[_key_rows(jl), :] for jl, _, _ in blocks]
            chains = [(b, h) for b in range(len(blocks)) for h in range(2)]
            z = {c: _dot_nt(qh[c[1]], ks[c[0]]) for c in chains}
            lg = {c: _log1m_beta(z[c], blocks[c[0]][2]) for c in chains}
            suf = {c: _dot(lg[c].astype(BF16), tri_ref[...]) for c in chains}
            att = {}
            for b, h in chains:
                _, jlane, mask = blocks[b]
                a = jnp.exp(z[b, h] + suf[b, h] + runs[h])
                att[b, h] = (a if mask is None else jnp.where(mask, a, 0.0)).astype(BF16)
                tables[h] = jnp.where(lane == jlane, runs[h], tables[h])
                runs[h] = runs[h] + suf[b, h][:, 0:1]
            for b, h in chains:
                acc = acc + _dot(att[b, h], jnp.where(halves[h], vs[b], jnp.zeros_like(vs[b])))
            return tuple(runs), tuple(tables), acc

        col0 = jnp.zeros((Q_BLK, 1), F32)
        zero = jnp.zeros((Q_BLK, LANES), F32)
        start = ((col0, col0), (zero, zero), zero)
        two = jd % 2
        carry = lax.cond(two == 1, lambda: group([(jd, jd, causal), (jd - 1, jd - 1, None)], start),
                         lambda: group([(jd, jd, causal)], start))
        top = jd - 1 - two

        def run(ja, n):
            return [(ja - i, ja - i, None) for i in range(n)]

        odd_pair = ((top + 1) // 2) % 2
        carry = lax.cond(odd_pair == 1, lambda cr: group(run(top, 2), cr), lambda cr: cr, carry)
        top4 = top - 2 * odd_pair
        _, tables, acc = lax.fori_loop(0, (top4 + 1) // 4, lambda it, cr: group(run(top4 - 4 * it, 4), cr), carry)
        o_ref[...] = acc
        run_ref[:, 0:LANES] = tables[0]
        run_ref[:, LANES:2 * LANES] = tables[1]
        pl.when(step_no == B * nhp * nq - 1)(wait_ride)

    return pl.pallas_call(
        body, name="attn_fwd", grid=(B, nhp, nq),
        in_specs=[pl.BlockSpec((Q_BLK, LANES), lambda b, hp, qi: (b * nq + qi, hp)),
                  pl.BlockSpec((S, LANES), lambda b, hp, qi: (b, nhp + hp)),
                  pl.BlockSpec((S, LANES), lambda b, hp, qi: (b, 2 * nhp + hp)),
                  pl.BlockSpec((K_BLK, K_BLK), lambda b, hp, qi: (0, 0))] + ride_specs,
        out_specs=[pl.BlockSpec((Q_BLK, LANES), lambda b, hp, qi: (b * nq + qi, hp)),
                   pl.BlockSpec((Q_BLK, 2 * LANES), lambda b, hp, qi: (b * nq + qi, hp))] + ride_specs,
        out_shape=[jax.ShapeDtypeStruct((T, SB_W), F32), jax.ShapeDtypeStruct((T, 2 * SB_W), F32)] + ride_shape,
        scratch_shapes=ride_sems,
        compiler_params=_params(("arbitrary", "arbitrary", "arbitrary"), 48),
    )(qkv, qkv, qkv, tri_suffix, *ride)


def _fwd_mix(y_lru, y_sb, ga, gb, w_out16, x2, tm):
    T = x2.shape[0]

    def body(yl_ref, ys_ref, ga_ref, gb_ref, w_ref, x_ref, h1_ref, mix_ref):
        na, _, _ = _rms_fwd(yl_ref[...], ga_ref[...])
        nb, _, _ = _rms_fwd(ys_ref[...], gb_ref[...])
        na = na.astype(BF16)
        nb = nb.astype(BF16)
        mix_ref[:, 0:LRU_W] = na
        mix_ref[:, LRU_W:D_MODEL] = nb
        h1_ref[...] = x_ref[...] + _dot(na, w_ref[0:LRU_W, :]) + _dot(nb, w_ref[LRU_W:D_MODEL, :])

    row = lambda i: (i, 0)
    fix = lambda i: (0, 0)
    return pl.pallas_call(
        body, name="fwd_mix", grid=(T // tm,),
        in_specs=[pl.BlockSpec((tm, LRU_W), row), pl.BlockSpec((tm, SB_W), row),
                  pl.BlockSpec((1, LRU_W), fix), pl.BlockSpec((1, SB_W), fix),
                  pl.BlockSpec((D_MODEL, D_MODEL), fix), pl.BlockSpec((tm, D_MODEL), row)],
        out_specs=[pl.BlockSpec((tm, D_MODEL), row), pl.BlockSpec((tm, D_MODEL), row)],
        out_shape=[jax.ShapeDtypeStruct((T, D_MODEL), F32), jax.ShapeDtypeStruct((T, D_MODEL), BF16)],
        compiler_params=_params(("arbitrary",), 48),
    )(y_lru, y_sb, ga, gb, w_out16, x2)


def _fwd_mlp(h1, g2, w_up16, w_down16, gf, tgt, tm, per_step):
    T = h1.shape[0]
    nf = D_FF // (FF_CHUNK * per_step)

    def body(h1_ref, g2_ref, wu_ref, wd_ref, gf_ref, t_ref, up_ref, dh2_ref, dgf_ref, loss_ref, hn_s, acc):
        i, j = pl.program_id(0), pl.program_id(1)

        @pl.when(j == 0)
        def _():
            h1v = h1_ref[...]
            hn, _, _ = _rms_fwd(h1v, g2_ref[...])
            hn_s[...] = hn.astype(BF16)
            acc[...] = h1v

        down = None
        for c in range(per_step):
            cols = slice(c * FF_CHUNK, (c + 1) * FF_CHUNK)
            up = jnp.maximum(_dot(hn_s[...], wu_ref[c]), 0.0)
            up_ref[:, cols] = up.astype(BF16)
            part = _dot((up * up).astype(BF16), wd_ref[cols, :])
            down = part if down is None else down + part
        acc[...] += down

        @pl.when((i == 0) & (j == 0))
        def _():
            dgf_ref[...] = jnp.zeros_like(dgf_ref)
            loss_ref[...] = jnp.zeros_like(loss_ref)

        @pl.when(j == nf - 1)
        def _():
            gfv = gf_ref[...]
            y, xhat, rstd = _rms_fwd(acc[...], gfv)
            err = y - t_ref[...]
            loss_ref[...] += jnp.sum(0.5 * jnp.sum(err * err, axis=-1, keepdims=True) * (1.0 / D_MODEL))
            dy = err * (1.0 / D_MODEL)
            dgf_ref[...] += jnp.sum(dy * xhat, axis=0, keepdims=True)
            dh2_ref[...] = _rms_bwd(dy, xhat, rstd, gfv)

    row = lambda i, j: (i, 0)
    fix = lambda i, j: (0, 0)
    return pl.pallas_call(
        body, name="fwd_mlp", grid=(T // tm, nf),
        in_specs=[pl.BlockSpec((tm, D_MODEL), row), pl.BlockSpec((1, D_MODEL), fix),
                  pl.BlockSpec((per_step, D_MODEL, FF_CHUNK), lambda i, j: (j, 0, 0)),
                  pl.BlockSpec((per_step * FF_CHUNK, D_MODEL), lambda i, j: (j, 0)),
                  pl.BlockSpec((1, D_MODEL), fix), pl.BlockSpec((tm, D_MODEL), row)],
        out_specs=[pl.BlockSpec((tm, per_step * FF_CHUNK), lambda i, j: (i, j)), pl.BlockSpec((tm, D_MODEL), row),
                   pl.BlockSpec((1, D_MODEL), fix), pl.BlockSpec((1, LANES), fix)],
        out_shape=[jax.ShapeDtypeStruct((T, D_FF), BF16), jax.ShapeDtypeStruct((T, D_MODEL), F32),
                   jax.ShapeDtypeStruct((1, D_MODEL), F32), jax.ShapeDtypeStruct((1, LANES), F32)],
        scratch_shapes=[pltpu.VMEM((tm, D_MODEL), BF16), pltpu.VMEM((tm, D_MODEL), F32)],
        compiler_params=_params(("arbitrary", "arbitrary"), 56),
    )(h1, g2, w_up16, w_down16, gf, tgt)


def _bwd_mlp(dh2, up16, h1, g2, w_up16, w_down16, tm, per_step):
    T = h1.shape[0]
    nf = D_FF // (FF_CHUNK * per_step)

    def body(dh2_ref, up_ref, h1_ref, g2_ref, wu_ref, wd_ref, dup_ref, dh1_ref, hn_ref, dh2b_ref, dg2_ref, acc):
        i, j = pl.program_id(0), pl.program_id(1)

        @pl.when(j == 0)
        def _():
            hn, _, _ = _rms_fwd(h1_ref[...], g2_ref[...])
            hn_ref[...] = hn.astype(BF16)
            dh2b_ref[...] = dh2_ref[...].astype(BF16)
            acc[...] = jnp.zeros_like(acc)

        dhn = None
        for c in range(per_step):
            cols = slice(c * FF_CHUNK, (c + 1) * FF_CHUNK)
            u = up_ref[:, cols].astype(F32)
            dup = (2.0 * u * _dot_nt(dh2b_ref[...], wd_ref[cols, :])).astype(BF16)
            dup_ref[:, cols] = dup
            part = _dot_nt(dup, wu_ref[c])
            dhn = part if dhn is None else dhn + part
        acc[...] += dhn

        @pl.when((i == 0) & (j == 0))
        def _():
            dg2_ref[...] = jnp.zeros_like(dg2_ref)

        @pl.when(j == nf - 1)
        def _():
            g2v = g2_ref[...]
            _, xhat, rstd = _rms_fwd(h1_ref[...], g2v)
            dhn = acc[...]
            dg2_ref[...] += jnp.sum(dhn * xhat, axis=0, keepdims=True)
            dh1_ref[...] = dh2_ref[...] + _rms_bwd(dhn, xhat, rstd, g2v)

    row = lambda i, j: (i, 0)
    fix = lambda i, j: (0, 0)
    return pl.pallas_call(
        body, name="bwd_mlp", grid=(T // tm, nf),
        in_specs=[pl.BlockSpec((tm, D_MODEL), row), pl.BlockSpec((tm, per_step * FF_CHUNK), lambda i, j: (i, j)),
                  pl.BlockSpec((tm, D_MODEL), row), pl.BlockSpec((1, D_MODEL), fix),
                  pl.BlockSpec((per_step, D_MODEL, FF_CHUNK), lambda i, j: (j, 0, 0)),
                  pl.BlockSpec((per_step * FF_CHUNK, D_MODEL), lambda i, j: (j, 0))],
        out_specs=[pl.BlockSpec((tm, per_step * FF_CHUNK), lambda i, j: (i, j)), pl.BlockSpec((tm, D_MODEL), row),
                   pl.BlockSpec((tm, D_MODEL), row), pl.BlockSpec((tm, D_MODEL), row),
                   pl.BlockSpec((1, D_MODEL), fix)],
        out_shape=[jax.ShapeDtypeStruct((T, D_FF), BF16), jax.ShapeDtypeStruct((T, D_MODEL), F32),
                   jax.ShapeDtypeStruct((T, D_MODEL), BF16), jax.ShapeDtypeStruct((T, D_MODEL), BF16),
                   jax.ShapeDtypeStruct((1, D_MODEL), F32)],
        scratch_shapes=[pltpu.VMEM((tm, D_MODEL), F32)],
        compiler_params=_params(("arbitrary", "arbitrary"), 48),
    )(dh2, up16, h1, g2, w_up16, w_down16)


def _matmul_tn(name, a, b, bm, bn, tk, out_shape, out_block, out_index, a_prep=None, b_prep=None, out_dtype=F32,
               ride=None, ride_sliced=None, split=None):
    T, M = a.shape
    N = b.shape[1]
    grid = (M // bm, N // bn, T // tk)
    ride, ride_sliced = list(ride or []), list(ride_sliced or [])
    nr = len(ride)
    ride_shape, ride_specs, ride_sems = _exchange_shapes(ride, ride_sliced)

    def body(a_ref, b_ref, *rest):
        o_ref, acc = rest[nr], rest[2 * nr + 1]
        k = pl.program_id(2)
        step_no = (pl.program_id(0) * grid[1] + pl.program_id(1)) * grid[2] + k
        if nr:
            start_ride, wait_ride = _exchange_copies(rest[:nr], rest[nr + 1:2 * nr + 1], ride_sliced,
                                                     *rest[2 * nr + 2:])
            pl.when(step_no == 0)(start_ride)
        av = a_ref[...] if a_prep is None else a_prep(a_ref[...])
        bv = b_ref[...] if b_prep is None else b_prep(b_ref[...])
        p = _dot_tn(av, bv)

        @pl.when(k == 0)
        def _():
            acc[...] = p

        @pl.when(k > 0)
        def _():
            acc[...] += p

        @pl.when(k == grid[2] - 1)
        def _():
            if split is None:
                o_ref[...] = acc[...].astype(out_dtype)
            else:
                axis, n = split
                w = (bm, bn)[axis] // n
                for c in range(n):
                    slab = acc[c * w:(c + 1) * w, :] if axis == 0 else acc[:, c * w:(c + 1) * w]
                    o_ref[c] = slab.astype(out_dtype)

        if nr:
            pl.when(step_no == grid[0] * grid[1] * grid[2] - 1)(wait_ride)

    return pl.pallas_call(
        body, name=name, grid=grid,
        in_specs=[pl.BlockSpec((tk, bm), lambda m, n, k: (k, m)), pl.BlockSpec((tk, bn), lambda m, n, k: (k, n))]
        + ride_specs,
        out_specs=[pl.BlockSpec(out_block, out_index)] + ride_specs,
        out_shape=[jax.ShapeDtypeStruct(out_shape, out_dtype)] + ride_shape,
        scratch_shapes=[pltpu.VMEM((bm, bn), F32)] + (ride_sems if nr else []),
        compiler_params=_params(("arbitrary", "arbitrary", "arbitrary"), 48),
    )(a, b, *ride)


def _bwd_mix(dh1, w_out16, y_lru, y_sb, ga, gb, tm):
    T = dh1.shape[0]

    def body(d_ref, w_ref, yl_ref, ys_ref, ga_ref, gb_ref, dyl_ref, dys_ref, dga_ref, dgb_ref):
        @pl.when(pl.program_id(0) == 0)
        def _():
            dga_ref[...] = jnp.zeros_like(dga_ref)
            dgb_ref[...] = jnp.zeros_like(dgb_ref)

        d16 = d_ref[...].astype(BF16)
        for y_ref, g_ref, lo, dy_ref, dg_ref in ((yl_ref, ga_ref, 0, dyl_ref, dga_ref),
                                                 (ys_ref, gb_ref, LRU_W, dys_ref, dgb_ref)):
            gv = g_ref[...]
            dn = _dot_nt(d16, w_ref[lo:lo + LRU_W, :])
            _, xhat, rstd = _rms_fwd(y_ref[...], gv)
            dg_ref[...] += jnp.sum(dn * xhat, axis=0, keepdims=True)
            dy_ref[...] = _rms_bwd(dn, xhat, rstd, gv).astype(dy_ref.dtype)

    row = lambda i: (i, 0)
    fix = lambda i: (0, 0)
    return pl.pallas_call(
        body, name="bwd_mix", grid=(T // tm,),
        in_specs=[pl.BlockSpec((tm, D_MODEL), row), pl.BlockSpec((D_MODEL, D_MODEL), fix),
                  pl.BlockSpec((tm, LRU_W), row), pl.BlockSpec((tm, SB_W), row),
                  pl.BlockSpec((1, LRU_W), fix), pl.BlockSpec((1, SB_W), fix)],
        out_specs=[pl.BlockSpec((tm, LRU_W), row), pl.BlockSpec((tm, SB_W), row),
                   pl.BlockSpec((1, LRU_W), fix), pl.BlockSpec((1, SB_W), fix)],
        out_shape=[jax.ShapeDtypeStruct((T, LRU_W), F32), jax.ShapeDtypeStruct((T, SB_W), BF16),
                   jax.ShapeDtypeStruct((1, LRU_W), F32), jax.ShapeDtypeStruct((1, SB_W), F32)],
        compiler_params=_params(("arbitrary",), 48),
    )(dh1, w_out16, y_lru, y_sb, ga, gb)


def _attn_bwd(qkv, run_tab, dy_sb, tri_suffix, tri_prefix, ride, ride_sliced, B, S):
    T = B * S
    nq = S // Q_BLK
    nkb = S // K_BLK
    nhp = SB_W // LANES
    scale = HEAD_D ** -0.5

    nr = len(ride)
    ride_shape, ride_specs, ride_sems = _exchange_shapes(ride, ride_sliced)

    def body(q_ref, k_ref, v_ref, run_ref, do_ref, ts_ref, tp_ref, *rest):
        dq_ref, dk_ref, dv_ref = rest[nr:nr + 3]
        dkt_ref, dvt_ref = rest[2 * nr + 3:2 * nr + 5]
        start_ride, wait_ride = _exchange_copies(rest[:nr], rest[nr + 3:2 * nr + 3], ride_sliced, *rest[2 * nr + 5:])
        qi = pl.program_id(2)
        step_no = (pl.program_id(0) * nhp + pl.program_id(1)) * nq + qi
        pl.when(step_no == 0)(start_ride)
        jd = qi // Q_PER_K

        @pl.when(qi == 0)
        def _():
            dkt_ref[...] = jnp.zeros_like(dkt_ref)
            dvt_ref[...] = jnp.zeros_like(dvt_ref)

        causal, lane, halves = _attn_consts(qi)
        q = q_ref[...]
        do = do_ref[...]
        qh = [jnp.where(hm, q, jnp.zeros_like(q)) * jnp.asarray(scale, BF16) for hm in halves]
        doh = [jnp.where(hm, do, jnp.zeros_like(do)) for hm in halves]
        qt = jnp.concatenate([h.astype(F32).T.astype(BF16) for h in qh], axis=1)
        dot_ = jnp.concatenate([h.astype(F32).T.astype(BF16) for h in doh], axis=1)
        tables = [run_ref[:, 0:LANES], run_ref[:, LANES:2 * LANES]]

        def group(blocks, carry):
            prefixes, dq = carry
            prefixes = list(prefixes)
            ks = [k_ref[_key_rows(jl), :] for jl, _, _ in blocks]
            vs = [v_ref[_key_rows(jl), :] for jl, _, _ in blocks]
            chains = [(b, h) for b in range(len(blocks)) for h in range(2)]
            z = {c: _dot_nt(qh[c[1]], ks[c[0]]) for c in chains}
            da = {c: _dot_nt(doh[c[1]], vs[c[0]]) for c in chains}
            lg = {c: _log1m_beta(z[c], blocks[c[0]][2]) for c in chains}
            suf = {c: _dot(lg[c].astype(BF16), ts_ref[...]) for c in chains}
            att, g = {}, {}
            for b, h in chains:
                _, jlane, mask = blocks[b]
                run = jnp.sum(jnp.where(lane == jlane, tables[h], 0.0), axis=1, keepdims=True)
                a = jnp.exp(z[b, h] + suf[b, h] + run)
                a = a if mask is None else jnp.where(mask, a, 0.0)
                g[b, h] = a * da[b, h]
                att[b, h] = a.astype(BF16)
            gpre = {c: _dot(g[c].astype(BF16), tp_ref[...]) for c in chains}
            dz = {}
            for b, h in chains:
                mask = blocks[b][2]
                d = g[b, h] - jnp.exp(z[b, h] + lg[b, h]) * (prefixes[h] + gpre[b, h])
                dz[b, h] = (d if mask is None else jnp.where(mask, d, 0.0)).astype(BF16)
                prefixes[h] = prefixes[h] + gpre[b, h][:, K_BLK - 1:K_BLK]
            for b, h in chains:
                dq = dq + _dot(dz[b, h], jnp.where(halves[h], ks[b], jnp.zeros_like(ks[b])))
            for b, (jl, _, _) in enumerate(blocks):
                dkt_ref[jl] += _dot(qt, jnp.concatenate([dz[b, 0], dz[b, 1]], axis=0))
                dvt_ref[jl] += _dot(dot_, jnp.concatenate([att[b, 0], att[b, 1]], axis=0))
            return tuple(prefixes), dq

        def run(ja, n):
            return [(ja + i, ja + i, None) for i in range(n)]

        col0 = jnp.zeros((Q_BLK, 1), F32)
        fours = (jd // 2) // 2
        carry = lax.fori_loop(0, fours, lambda it, cr: group(run(4 * it, 4), cr),
                              ((col0, col0), jnp.zeros((Q_BLK, LANES), F32)))
        carry = lax.cond((jd // 2) % 2 == 1, lambda cr: group(run(4 * fours, 2), cr), lambda cr: cr, carry)
        carry = lax.cond(jd % 2 == 1, lambda cr: group([(jd - 1, jd - 1, None), (jd, jd, causal)], cr),
                         lambda cr: group([(jd, jd, causal)], cr), carry)
        dq_ref[...] = (carry[1] * scale).astype(BF16)

        @pl.when(qi == nq - 1)
        def _():
            for j in range(nkb):
                dk_ref[j * K_BLK:(j + 1) * K_BLK, :] = dkt_ref[j].T.astype(BF16)
                dv_ref[j * K_BLK:(j + 1) * K_BLK, :] = dvt_ref[j].T.astype(BF16)

        pl.when(step_no == B * nhp * nq - 1)(wait_ride)

    qblk = pl.BlockSpec((Q_BLK, LANES), lambda b, hp, qi: (b * nq + qi, hp))
    tri = pl.BlockSpec((K_BLK, K_BLK), lambda b, hp, qi: (0, 0))
    seq = pl.BlockSpec((S, LANES), lambda b, hp, qi: (b, hp))
    return pl.pallas_call(
        body, name="attn_bwd", grid=(B, nhp, nq),
        in_specs=[qblk, pl.BlockSpec((S, LANES), lambda b, hp, qi: (b, nhp + hp)),
                  pl.BlockSpec((S, LANES), lambda b, hp, qi: (b, 2 * nhp + hp)),
                  pl.BlockSpec((Q_BLK, 2 * LANES), lambda b, hp, qi: (b * nq + qi, hp)), qblk, tri, tri] + ride_specs,
        out_specs=[qblk, seq, seq] + ride_specs,
        out_shape=[jax.ShapeDtypeStruct((T, SB_W), BF16)] * 3 + ride_shape,
        scratch_shapes=[pltpu.VMEM((nkb, LANES, K_BLK), F32)] * 2 + ride_sems,
        compiler_params=_params(("arbitrary", "arbitrary", "arbitrary"), 48),
    )(qkv, qkv, qkv, run_tab, dy_sb, tri_suffix, tri_prefix, *ride)


def _lru_bwd(proj_lru, h, dy_lru, conv_w, conv_b, wa_bd, wx_bd, b_a, b_x, lam, B, S, lc):
    T = B * S
    nc = S // lc
    ncb = LRU_W // LANES
    hpc = lc // SUBLANES

    def body(x_ref, xh_ref, g_ref, h_ref, hh_ref, dy_ref, cw_ref, cb_ref, wa_ref, wx_ref, ba_ref, bx_ref, lam_ref,
             dx_ref, dg_ref, dcw_ref, dcb_ref, dwa_ref, dwx_ref, dba_ref, dbx_ref, dlam_ref,
             lam_s, dc_s, a_first, lam_first, dc_head):
        b, ci = pl.program_id(1), pl.program_id(2)
        first_chunk = ci == nc - 1

        @pl.when(ci == 0)
        def _():
            a_first[...] = jnp.zeros_like(a_first)
            lam_first[...] = jnp.zeros_like(lam_first)
            dc_head[...] = jnp.zeros_like(dc_head)

        @pl.when((b == 0) & (ci == 0))
        def _():
            for ref in (dcw_ref, dcb_ref, dwa_ref, dwx_ref, dba_ref, dbx_ref, dlam_ref):
                ref[...] = jnp.zeros_like(ref)

        x = x_ref[...]
        taps = _conv_taps(x, jnp.where(first_chunk, 0.0, xh_ref[...]), lc)
        c = cb_ref[...] + sum(cw_ref[pl.ds(CONV_K - 1 - k, 1), :] * taps[k] for k in range(CONV_K))
        r, i, sp, dsp_dlam, a, s = _lru_gates(c, wa_ref, wx_ref, ba_ref, bx_ref, lam_ref)
        hv = h_ref[...]
        he = jnp.concatenate([jnp.where(first_chunk, 0.0, hh_ref[...]), hv], axis=0)
        h_prev = pltpu.roll(he, 1, 0)[SUBLANES:SUBLANES + lc]
        dy = dy_ref[...]
        gelu, dgelu = _gelu_parts(g_ref[...])
        dg_ref[...] = (dy * hv * dgelu).astype(BF16)

        row = lax.broadcasted_iota(jnp.int32, (lc, LANES), 0)
        a_next = jnp.where(row < lc - 1, pltpu.roll(a, lc - 1, 0), a_first[...])
        lam_loc, a_run = _scan(a_next, dy * gelu, reverse=True)
        lam_s[...] = lam_loc + a_run * lam_first[...]
        lam_t = lam_s[...]
        lam_first[...] = lam_s[pl.ds(0, 1), :]
        lam_s[...] = a
        a_first[...] = lam_s[pl.ds(0, 1), :]

        ic = i * c
        dlog_a = lam_t * h_prev * a - (lam_t * ic) * (a * a) / s
        dpre_r = (dlog_a * ((-LRU_C) * sp)) * r * (1.0 - r)
        dpre_i = (lam_t * s * c) * i * (1.0 - i)
        dlam_ref[...] += jnp.sum(dlog_a * r, axis=0, keepdims=True) * ((-LRU_C) * dsp_dlam)
        dr16 = dpre_r.astype(BF16)
        di16 = dpre_i.astype(BF16)
        c16 = c.astype(BF16)
        dwa_ref[0] += _dot_tn(c16, dr16)
        dwx_ref[0] += _dot_tn(c16, di16)
        dba_ref[...] += jnp.sum(dpre_r, axis=0, keepdims=True)
        dbx_ref[...] += jnp.sum(dpre_i, axis=0, keepdims=True)
        dc = lam_t * s * i + _dot_nt(dr16, wa_ref[0]) + _dot_nt(di16, wx_ref[0])
        dcb_ref[...] += jnp.sum(dc, axis=0, keepdims=True)
        for k in range(CONV_K):
            dcw_ref[pl.ds(CONV_K - 1 - k, 1), :] += jnp.sum(dc * taps[k], axis=0, keepdims=True)
        dce = jnp.concatenate([dc, dc_head[...]], axis=0)
        dx = cw_ref[pl.ds(CONV_K - 1, 1), :] * dc
        for k in range(1, CONV_K):
            dx = dx + cw_ref[pl.ds(CONV_K - 1 - k, 1), :] * pltpu.roll(dce, lc + SUBLANES - k, 0)[0:lc]
        dx_ref[...] = dx.astype(BF16)
        dc_s[...] = dc
        dc_head[...] = dc_s[pl.ds(0, SUBLANES), :]

    def chunk(col):
        return pl.BlockSpec((lc, LANES), lambda cb, b, ci: (b * nc + nc - 1 - ci, col(cb)))

    def halo(col):
        return pl.BlockSpec((SUBLANES, LANES),
                            lambda cb, b, ci: (jnp.maximum((b * nc + nc - 1 - ci) * hpc - 1, 0), col(cb)))

    chan = lambda cb, b, ci: (0, cb)
    blk = lambda cb, b, ci: (cb, 0, 0)
    vec = pl.BlockSpec((1, LANES), chan)
    mat = pl.BlockSpec((1, LANES, LANES), blk)
    return pl.pallas_call(
        body, name="lru_bwd", grid=(ncb, B, nc),
        in_specs=[chunk(lambda cb: cb), halo(lambda cb: cb), chunk(lambda cb: ncb + cb),
                  chunk(lambda cb: cb), halo(lambda cb: cb), chunk(lambda cb: cb),
                  pl.BlockSpec((CONV_K, LANES), chan), vec, mat, mat, vec, vec, vec],
        out_specs=[chunk(lambda cb: cb), chunk(lambda cb: cb), pl.BlockSpec((CONV_K, LANES), chan), vec,
                   mat, mat, vec, vec, vec],
        out_shape=[jax.ShapeDtypeStruct((T, LRU_W), BF16), jax.ShapeDtypeStruct((T, LRU_W), BF16),
                   jax.ShapeDtypeStruct((CONV_K, LRU_W), F32), jax.ShapeDtypeStruct((1, LRU_W), F32),
                   jax.ShapeDtypeStruct((ncb, LANES, LANES), F32), jax.ShapeDtypeStruct((ncb, LANES, LANES), F32),
                   jax.ShapeDtypeStruct((1, LRU_W), F32), jax.ShapeDtypeStruct((1, LRU_W), F32),
                   jax.ShapeDtypeStruct((1, LRU_W), F32)],
        scratch_shapes=[pltpu.VMEM((lc, LANES), F32), pltpu.VMEM((lc, LANES), F32), pltpu.VMEM((1, LANES), F32),
                        pltpu.VMEM((1, LANES), F32), pltpu.VMEM((SUBLANES, LANES), F32)],
        compiler_params=_params(("arbitrary", "arbitrary", "arbitrary"), 32),
    )(proj_lru, proj_lru, proj_lru, h, h, dy_lru, conv_w, conv_b, wa_bd, wx_bd, b_a, b_x, lam)


def _bwd_in(pieces, w_in16t, x2, g1, dh1, tm):
    T = x2.shape[0]
    npc = len(pieces)

    def body(*refs):
        p_refs = refs[:npc]
        w_ref, x_ref, g_ref, d_ref, dx_ref, dproj_ref, xn_ref, dg1_ref = refs[npc:]

        @pl.when(pl.program_id(0) == 0)
        def _():
            dg1_ref[...] = jnp.zeros_like(dg1_ref)

        dxn = jnp.zeros((tm, D_MODEL), F32)
        for n, p_ref in enumerate(p_refs):
            cols = slice(n * LRU_W, (n + 1) * LRU_W)
            p16 = p_ref[...]
            dproj_ref[:, cols] = p16
            dxn = dxn + _dot(p16, w_ref[cols, :])
        gv = g_ref[...]
        xn, xhat, rstd = _rms_fwd(x_ref[...], gv)
        xn_ref[...] = xn.astype(BF16)
        dg1_ref[...] += jnp.sum(dxn * xhat, axis=0, keepdims=True)
        dx_ref[...] = d_ref[...] + _rms_bwd(dxn, xhat, rstd, gv)

    row = lambda i: (i, 0)
    fix = lambda i: (0, 0)
    return pl.pallas_call(
        body, name="bwd_in", grid=(T // tm,),
        in_specs=[pl.BlockSpec((tm, LRU_W), row)] * npc + [
            pl.BlockSpec((IN_COLS, D_MODEL), fix), pl.BlockSpec((tm, D_MODEL), row),
            pl.BlockSpec((1, D_MODEL), fix), pl.BlockSpec((tm, D_MODEL), row)],
        out_specs=[pl.BlockSpec((tm, D_MODEL), row), pl.BlockSpec((tm, IN_COLS), row),
                   pl.BlockSpec((tm, D_MODEL), row), pl.BlockSpec((1, D_MODEL), fix)],
        out_shape=[jax.ShapeDtypeStruct((T, D_MODEL), F32), jax.ShapeDtypeStruct((T, IN_COLS), BF16),
                   jax.ShapeDtypeStruct((T, D_MODEL), BF16), jax.ShapeDtypeStruct((1, D_MODEL), F32)],
        compiler_params=_params(("arbitrary",), 56),
    )(*pieces, w_in16t, x2, g1, dh1)


def _adam_shards(name, shards, steps, ride, ride_sliced):
    ns = len(shards)
    nr = len(ride)
    ride_shape, ride_specs, ride_sems = _exchange_shapes(ride, ride_sliced)

    def body(*refs):
        ins, rest = refs[:4 * ns], refs[4 * ns:]
        outs = rest[nr:nr + 4 * ns]
        start_ride, wait_ride = _exchange_copies(rest[:nr], rest[nr + 4 * ns:2 * nr + 4 * ns], ride_sliced,
                                                 *rest[2 * nr + 4 * ns:])
        pl.when(pl.program_id(0) == 0)(start_ride)
        for i in range(ns):
            p_ref, w_ref, m_ref, v_ref = ins[4 * i:4 * i + 4]
            g = p_ref[0]
            for p in range(1, N_DEV):
                g = g + p_ref[p]
            outs[4 * i][...] = g
            outs[4 * i + 1][...], outs[4 * i + 2][...], outs[4 * i + 3][...] = _adamw(w_ref[...], g, m_ref[...], v_ref[...])
        pl.when(pl.program_id(0) == steps - 1)(wait_ride)

    in_specs, out_specs, out_shape, args = [], [], [], []
    for parts, w, m, v in shards:
        R, C = w.shape
        blk = pl.BlockSpec((R // steps, C), lambda i: (i, 0))
        in_specs += [pl.BlockSpec((N_DEV, R // steps, C), lambda i: (0, i, 0)), blk, blk, blk]
        out_specs += [blk] * 4
        out_shape += [jax.ShapeDtypeStruct((R, C), F32)] * 4
        args += [parts, w, m, v]
    res = pl.pallas_call(
        body, name=name, grid=(steps,), in_specs=in_specs + ride_specs, out_specs=out_specs + ride_specs,
        out_shape=out_shape + ride_shape, scratch_shapes=ride_sems,
        compiler_params=_params(("arbitrary",), 48),
    )(*args, *ride)
    return [list(res[4 * i:4 * i + 4]) for i in range(ns)], list(res[4 * ns:])


def _adam_param(name, parts, w, m, v):
    def body(p_ref, w_ref, m_ref, v_ref, g_ref, d_ref, m2_ref, v2_ref):
        g = p_ref[0]
        for p in range(1, N_DEV):
            g = g + p_ref[p]
        g_ref[...] = g
        d_ref[...], m2_ref[...], v2_ref[...] = _adamw(w_ref[...], g, m_ref[...], v_ref[...])

    return pl.pallas_call(body, name=name, out_shape=[jax.ShapeDtypeStruct(w.shape, F32)] * 4)(parts, w, m, v)


def _adam_given(name, g, w, m, v, tr):
    R, C = w.shape

    def body(g_ref, w_ref, m_ref, v_ref, d_ref, m2_ref, v2_ref):
        d_ref[...], m2_ref[...], v2_ref[...] = _adamw(w_ref[...], g_ref[...], m_ref[...], v_ref[...])

    blk = pl.BlockSpec((tr, C), lambda i: (i, 0))
    return [g] + list(pl.pallas_call(
        body, name=name, grid=(R // tr,), in_specs=[blk] * 4, out_specs=[blk] * 3,
        out_shape=[jax.ShapeDtypeStruct((R, C), F32)] * 3, compiler_params=_params(("arbitrary",), 32),
    )(g, w, m, v))


def _sum_parts(name, parts):
    def body(p_ref, g_ref):
        g = p_ref[0].astype(F32)
        for p in range(1, N_DEV):
            g = g + p_ref[p].astype(F32)
        g_ref[...] = g

    return pl.pallas_call(body, name=name, out_shape=jax.ShapeDtypeStruct(parts.shape[1:], F32))(parts)


def _block_diag_pairs(w):
    w = w.reshape(LRU_BLOCKS // 2, 2, HEAD_D, HEAD_D)
    out = jnp.zeros((LRU_BLOCKS // 2, LANES, LANES), w.dtype)
    out = out.at[:, :HEAD_D, :HEAD_D].set(w[:, 0])
    return out.at[:, HEAD_D:, HEAD_D:].set(w[:, 1])


def _diag_blocks(w):
    return jnp.stack([w[:, :HEAD_D, :HEAD_D], w[:, HEAD_D:, HEAD_D:]], axis=1).reshape(LRU_BLOCKS, HEAD_D, HEAD_D)


def kernel(x, norm1_g, w_in, conv_w, conv_b, lru_w_a, lru_b_a, lru_w_x, lru_b_x, lru_lambda, lru_out_g, sb_out_g, w_out, norm2_g, w_up, w_down, final_g, loss_target, m_norm1_g, m_w_in, m_conv_w, m_conv_b, m_lru_w_a, m_lru_b_a, m_lru_w_x, m_lru_b_x, m_lru_lambda, m_lru_out_g, m_sb_out_g, m_w_out, m_norm2_g, m_w_up, m_w_down, m_final_g, v_norm1_g, v_w_in, v_conv_w, v_conv_b, v_lru_w_a, v_lru_b_a, v_lru_w_x, v_lru_b_x, v_lru_lambda, v_lru_out_g, v_sb_out_g, v_w_out, v_norm2_g, v_w_up, v_w_down, v_final_g):
    B, S, _ = x.shape
    T = B * S
    tm = min(512, T)
    tk = min(GRAD_TK, T)
    lc = min(512, S)
    x2 = x.reshape(T, D_MODEL)
    tgt = loss_target.reshape(T, D_MODEL)
    cw_cols = CONV_K * LRU_W // N_DEV // CONV_K

    shards16 = _cast_shards([w_in[0].T, w_out[0], w_up[0], w_down[0]])
    cw_pad = jnp.zeros((SUBLANES, LANES), F32).at[:CONV_K, :cw_cols].set(conv_w[0])
    g_in, g_cw = _gather_two_level("gather_w_in", [shards16[0], cw_pad])
    w_in16t = g_in.reshape(IN_COLS, D_MODEL)
    conv_w_full = g_cw[:, :CONV_K, :cw_cols].transpose(1, 0, 2).reshape(CONV_K, LRU_W)
    wa_bd = _block_diag_pairs(lru_w_a[0]).astype(BF16)
    wx_bd = _block_diag_pairs(lru_w_x[0]).astype(BF16)
    b_a = lru_b_a.reshape(1, LRU_W)
    b_x = lru_b_x.reshape(1, LRU_W)
    gf = final_g.reshape(1, D_MODEL)

    proj_lru, qkv = _fwd_in(x2, norm1_g, w_in16t, tm)
    y_lru, h = _lru_fwd(proj_lru, conv_w_full, conv_b, wa_bd, wx_bd, b_a, b_x, lru_lambda, B, S, lc)
    tri_suffix, tri_prefix = _tri(False), _tri(True)
    y_sb, run_tab, g_out, g_up, g_down = _attn_fwd(qkv, tri_suffix, list(shards16[1:]), [False] * 3, B, S)
    w_out16 = g_out.reshape(D_MODEL, D_MODEL)
    w_down16 = g_down.reshape(D_FF, D_MODEL)
    h1, mix16 = _fwd_mix(y_lru, y_sb, lru_out_g, sb_out_g, w_out16, x2, tm)
    up16, dh2, d_final_g, loss_part = _fwd_mlp(h1, norm2_g, g_up, w_down16, gf, tgt, min(1024, T), 2)

    dup16, dh1, hn16, dh2b, d_norm2_g = _bwd_mlp(dh2, up16, h1, norm2_g, g_up, w_down16, tm, 4)
    sq = lambda u: (u.astype(F32) * u.astype(F32)).astype(BF16)
    wide = 4
    gw_up, = _matmul_tn("grad_w_up", hn16, dup16, D_MODEL, wide * FF_CHUNK, tk // 2, (N_DEV, D_MODEL, FF_CHUNK),
                        (wide, D_MODEL, FF_CHUNK), lambda m, n, k: (n, 0, 0), split=(1, wide))
    gw_down, = _matmul_tn("grad_w_down", up16, dh2b, wide * FF_CHUNK, D_MODEL, tk // 2, (N_DEV, FF_CHUNK, D_MODEL),
                          (wide, FF_CHUNK, D_MODEL), lambda m, n, k: (m, 0, 0), a_prep=sq, split=(0, wide))
    dy_lru, dy_sb, d_lru_out_g, d_sb_out_g = _bwd_mix(dh1, w_out16, y_lru, y_sb, lru_out_g, sb_out_g, tm)
    gw_out, = _matmul_tn("grad_w_out", mix16, dh1, D_MODEL, FF_CHUNK, tk, (D_MODEL, D_MODEL),
                         (D_MODEL, FF_CHUNK), lambda m, n, k: (0, n), b_prep=lambda u: u.astype(BF16))
    parts_out = gw_out.reshape(N_DEV, D_MODEL // N_DEV, D_MODEL)
    dq, dk, dv, r_out, r_up, r_down = _attn_bwd(qkv, run_tab, dy_sb, tri_suffix, tri_prefix,
                                                [parts_out, gw_up, gw_down], [True] * 3, B, S)
    (dx_lru, dg_lru, d_conv_w, d_conv_b, d_wa, d_wx, d_b_a, d_b_x, d_lambda) = _lru_bwd(
        proj_lru, h, dy_lru, conv_w_full, conv_b, wa_bd, wx_bd, b_a, b_x, lru_lambda, B, S, lc)
    dx, dproj16, xn16, d_norm1_g = _bwd_in([dx_lru, dg_lru, dq, dk, dv], w_in16t, x2, norm1_g, dh1, tm)

    small = {"norm1_g": (d_norm1_g, norm1_g, m_norm1_g, v_norm1_g), "conv_b": (d_conv_b, conv_b, m_conv_b, v_conv_b),
             "lru_w_a": (_diag_blocks(d_wa), lru_w_a, m_lru_w_a, v_lru_w_a),
             "lru_b_a": (d_b_a, lru_b_a, m_lru_b_a, v_lru_b_a),
             "lru_w_x": (_diag_blocks(d_wx), lru_w_x, m_lru_w_x, v_lru_w_x),
             "lru_b_x": (d_b_x, lru_b_x, m_lru_b_x, v_lru_b_x),
             "lru_lambda": (d_lambda, lru_lambda, m_lru_lambda, v_lru_lambda),
             "lru_out_g": (d_lru_out_g, lru_out_g, m_lru_out_g, v_lru_out_g),
             "sb_out_g": (d_sb_out_g, sb_out_g, m_sb_out_g, v_sb_out_g),
             "norm2_g": (d_norm2_g, norm2_g, m_norm2_g, v_norm2_g),
             "final_g": (d_final_g, final_g, m_final_g, v_final_g)}
    names = list(small)

    def held(n, a):
        return a.reshape((1, D_MODEL) if n == "final_g" else small[n][1].shape)

    parts_cw = d_conv_w.reshape(CONV_K, N_DEV, cw_cols).transpose(1, 0, 2)[:, None]
    gw_in_t, *got = _matmul_tn(
        "grad_w_in", dproj16, xn16, IN_COLS // 2, D_MODEL, tk // 2, (IN_COLS, D_MODEL), (IN_COLS // 2, D_MODEL),
        lambda m, n, k: (m, 0), out_dtype=BF16,
        ride=[parts_cw] + [held(n, small[n][0]) for n in names] + [loss_part],
        ride_sliced=[True] + [False] * (len(names) + 1))

    big, (r_in,) = _adam_shards(
        "adam_out_up_down", [(r_out, w_out[0], m_w_out[0], v_w_out[0]), (r_up, w_up[0], m_w_up[0], v_w_up[0]),
                             (r_down, w_down[0], m_w_down[0], v_w_down[0])],
        4, [gw_in_t.reshape(N_DEV, IN_COLS // N_DEV, D_MODEL)], [True])
    g_in_t = _sum_parts("sum_w_in", r_in)
    out = dict(zip(("w_out", "w_up", "w_down"), big))
    out["w_in"] = _adam_given("adam_w_in", g_in_t.T, w_in[0], m_w_in[0], v_w_in[0], 256)
    out = {n: [a[None] for a in res] for n, res in out.items()}
    out["conv_w"] = _adam_param("adam_conv_w", got[0], conv_w, m_conv_w, v_conv_w)
    for n, parts in zip(names, got[1:-1]):
        _, w, m, v = small[n]
        res = _adam_param("adam_" + n, parts, held(n, w), held(n, m), held(n, v))
        out[n] = [a.reshape(w.shape) for a in res]
    loss = _sum_parts("sum_loss", got[-1])[0, 0]
    weights = ["norm1_g", "w_in", "conv_w", "conv_b", "lru_w_a", "lru_b_a", "lru_w_x", "lru_b_x", "lru_lambda",
               "lru_out_g", "sb_out_g", "w_out", "norm2_g", "w_up", "w_down", "final_g"]
    return (loss, dx.reshape(B, S, D_MODEL), *[out[n][0] for n in weights], *[out[n][1] for n in weights],
            *[out[n][2] for n in weights], *[out[n][3] for n in weights])
```

```python
import jax
import jax.numpy as jnp
from jax import lax
from jax.experimental import pallas as pl
from jax.experimental.pallas import tpu as pltpu

F32 = jnp.float32
BF16 = jnp.bfloat16

D_MODEL = 1024
LRU_W = 512
SB_W = 512
HEAD_D = 64
D_FF = 4096
IN_COLS = 2 * LRU_W + 3 * SB_W
CONV_K = 4
LRU_BLOCKS = 8
LRU_C = 8.0
EPS = 1e-6
N_DEV = 8
LANES = 128
SUBLANES = 8
FF_CHUNK = 512
GRAD_TK = 2048
Q_BLK = 256
K_BLK = 256
Q_PER_K = K_BLK // Q_BLK

ADAM_LR = 0.001
ADAM_B1 = 0.9
ADAM_B2 = 0.999
ADAM_EPS = 1e-08
ADAM_WD = 0.01
ADAM_STEP = 10


def _params(sem=None, vmem_mb=None):
    kw = {}
    if sem is not None:
        kw["dimension_semantics"] = sem
    if vmem_mb is not None:
        kw["vmem_limit_bytes"] = vmem_mb << 20
    return pltpu.CompilerParams(**kw)


def _dot(a, b):
    return jnp.dot(a, b, preferred_element_type=F32)


def _dot_nt(a, b):
    return lax.dot_general(a, b, (((1,), (1,)), ((), ())), preferred_element_type=F32)


def _dot_tn(a, b):
    return lax.dot_general(a, b, (((0,), (0,)), ((), ())), preferred_element_type=F32)


def _rms_fwd(x, g):
    rstd = lax.rsqrt(jnp.mean(x * x, axis=-1, keepdims=True) + EPS)
    xhat = x * rstd
    return xhat * g, xhat, rstd


def _rms_bwd(dy, xhat, rstd, g):
    dxhat = dy * g
    return rstd * (dxhat - xhat * jnp.mean(dxhat * xhat, axis=-1, keepdims=True))


def _sigmoid(x):
    return 1.0 / (1.0 + jnp.exp(-x))


def _log1p_pos(e):
    series = e * (1.0 - e * (0.5 - e * (1.0 / 3.0 - e * 0.25)))
    return jnp.where(e < 1e-2, series, jnp.log(1.0 + e))


def _neg_expm1(x):
    series = -x * (1.0 + x * (0.5 + x * (1.0 / 6.0 + x * (1.0 / 24.0))))
    return jnp.where(x > -1e-2, series, 1.0 - jnp.exp(x))


def _gelu_parts(g):
    k0 = 0.7978845608028654
    k1 = 0.044715
    t = jnp.tanh(k0 * (g + k1 * g * g * g))
    val = 0.5 * g * (1.0 + t)
    grad = 0.5 * (1.0 + t) + 0.5 * g * (1.0 - t * t) * k0 * (1.0 + 3.0 * k1 * g * g)
    return val, grad


def _scan(a, b, reverse):
    n = a.shape[0]
    row = lax.broadcasted_iota(jnp.int32, a.shape, 0)
    s = 1
    while s < n:
        if reverse:
            keep = row < n - s
            shift = n - s
        else:
            keep = row >= s
            shift = s
        bs = jnp.where(keep, pltpu.roll(b, shift, 0), 0.0)
        a_s = jnp.where(keep, pltpu.roll(a, shift, 0), 1.0)
        b = a * bs + b
        a = a * a_s
        s *= 2
    return b, a


def _adamw(w, g, m, v):
    m = ADAM_B1 * m + (1.0 - ADAM_B1) * g
    v = ADAM_B2 * v + (1.0 - ADAM_B2) * (g * g)
    m_hat = m / (1.0 - ADAM_B1 ** ADAM_STEP)
    v_hat = v / (1.0 - ADAM_B2 ** ADAM_STEP)
    delta = -ADAM_LR * (m_hat / (jnp.sqrt(v_hat) + ADAM_EPS) + ADAM_WD * w)
    return delta, m, v


def _my_index():
    return 4 * lax.axis_index("x") + 2 * lax.axis_index("y") + lax.axis_index("c")


def _peer(k):
    x, y, c = lax.axis_index("x"), lax.axis_index("y"), lax.axis_index("c")
    px = 1 - x if (k >> 2) & 1 else x
    py = 1 - y if (k >> 1) & 1 else y
    pc = 1 - c if k & 1 else c
    return (px, py, pc), 4 * px + 2 * py + pc


def _owners(sl):
    return sl if isinstance(sl, tuple) else (0, N_DEV)


def _exchange_shapes(srcs, sliced):
    n = len(srcs)
    out_shape = [jax.ShapeDtypeStruct((N_DEV,) + (s.shape[1:] if sl else s.shape), s.dtype) for s, sl in zip(srcs, sliced)]
    specs = [pl.BlockSpec(memory_space=pl.ANY)] * n
    sems = [pltpu.SemaphoreType.DMA((n, N_DEV - 1)), pltpu.SemaphoreType.DMA((n, N_DEV - 1)),
            pltpu.SemaphoreType.DMA((n,))]
    return out_shape, specs, sems


def _exchange_copies(ins, outs, sliced, send_sems, recv_sems, local_sems):
    n = len(ins)

    def part(a, p):
        return ins[a].at[p - _owners(sliced[a])[0]] if sliced[a] else ins[a]

    def owner(a, p):
        lo, hi = _owners(sliced[a])
        return None if (lo, hi) == (0, N_DEV) else (p >= lo) & (p < hi)

    def run(cond, fn):
        fn() if cond is None else pl.when(cond)(fn)

    def local(a, me):
        return pltpu.make_async_copy(part(a, me), outs[a].at[me], local_sems.at[a])

    def remote(a, k, receiving, me):
        dev, idx = _peer(k)
        return pltpu.make_async_remote_copy(
            src_ref=part(a, idx), dst_ref=outs[a].at[idx if receiving else me],
            send_sem=send_sems.at[a, k - 1], recv_sem=recv_sems.at[a, k - 1],
            device_id=dev, device_id_type=pl.DeviceIdType.MESH), idx

    def start():
        me = _my_index()
        for a in range(n):
            run(owner(a, me), local(a, me).start)
        for k in range(1, N_DEV):
            for a in range(n):
                cp, idx = remote(a, k, False, me)
                run(owner(a, idx), cp.start)

    def wait():
        me = _my_index()
        for k in range(1, N_DEV):
            for a in range(n):
                cp, idx = remote(a, k, True, me)
                run(owner(a, idx), cp.wait_send)
                run(owner(a, me), cp.wait_recv)
        for a in range(n):
            run(owner(a, me), local(a, me).wait)

    return start, wait


def _gather_two_level(name, srcs):
    n = len(srcs)
    out_shape = [jax.ShapeDtypeStruct((N_DEV,) + s.shape, s.dtype) for s in srcs]

    def body(*refs):
        ins, outs = refs[:n], refs[n:2 * n]
        send_sems, recv_sems, local_sems = refs[2 * n:]
        x, y, c = lax.axis_index("x"), lax.axis_index("y"), lax.axis_index("c")
        here, sibling = (x, y, c), (x, y, 1 - c)
        chips = [(1 - x, y), (x, 1 - y), (1 - x, 1 - y)]

        def slot(px, py, pc):
            return 4 * px + 2 * py + pc

        def copy(a, k, block, to, src=None):
            return pltpu.make_async_remote_copy(
                src_ref=ins[a] if src is None else src, dst_ref=outs[a].at[slot(*block)],
                send_sem=send_sems.at[a, k], recv_sem=recv_sems.at[a, k], device_id=to,
                device_id_type=pl.DeviceIdType.MESH)

        local = [pltpu.make_async_copy(ins[a], outs[a].at[slot(*here)], local_sems.at[a]) for a in range(n)]
        first = [copy(a, 0, here, sibling) for a in range(n)]
        first += [copy(a, 1 + j, here, (*chip, c)) for j, chip in enumerate(chips) for a in range(n)]
        for cp in local + first:
            cp.start()
        passed = []
        for j, chip in enumerate(chips):
            for a in range(n):
                copy(a, 1 + j, (*chip, c), here).wait_recv()
                passed.append(copy(a, 4 + j, (*chip, c), sibling, src=outs[a].at[slot(*chip, c)]))
                passed[-1].start()
        for a in range(n):
            copy(a, 0, sibling, here).wait_recv()
        for j, chip in enumerate(chips):
            for a in range(n):
                copy(a, 4 + j, (*chip, 1 - c), here).wait_recv()
        for cp in first + passed:
            cp.wait_send()
        for cp in local:
            cp.wait()

    spec = [pl.BlockSpec(memory_space=pl.ANY)] * n
    return pl.pallas_call(
        body, name=name, out_shape=out_shape, in_specs=spec, out_specs=spec,
        scratch_shapes=[pltpu.SemaphoreType.DMA((n, N_DEV - 1)), pltpu.SemaphoreType.DMA((n, N_DEV - 1)),
                        pltpu.SemaphoreType.DMA((n,))],
    )(*srcs)


def _whole(shape):
    return pl.BlockSpec(shape, lambda *_: (0,) * len(shape))


def _cast_shards(ws, steps=4):
    def body(*refs):
        for i in range(len(ws)):
            refs[len(ws) + i][...] = refs[i][...].astype(BF16)

    specs = [pl.BlockSpec((w.shape[0] // steps, w.shape[1]), lambda i: (i, 0)) for w in ws]
    return pl.pallas_call(
        body, name="cast_shards", grid=(steps,), in_specs=specs, out_specs=specs,
        out_shape=[jax.ShapeDtypeStruct(w.shape, BF16) for w in ws],
        compiler_params=_params(("arbitrary",), 32),
    )(*ws)


def _fwd_in(x2, g1, w_in16t, tm):
    T = x2.shape[0]

    def body(x_ref, g_ref, w_ref, lru_ref, qkv_ref):
        xn, _, _ = _rms_fwd(x_ref[...], g_ref[...])
        xn = xn.astype(BF16)
        lru_ref[...] = _dot_nt(xn, w_ref[0:2 * LRU_W, :])
        qkv_ref[...] = _dot_nt(xn, w_ref[2 * LRU_W:IN_COLS, :]).astype(BF16)

    return pl.pallas_call(
        body, name="fwd_in", grid=(T // tm,),
        in_specs=[pl.BlockSpec((tm, D_MODEL), lambda i: (i, 0)),
                  pl.BlockSpec((1, D_MODEL), lambda i: (0, 0)),
                  pl.BlockSpec((IN_COLS, D_MODEL), lambda i: (0, 0))],
        out_specs=[pl.BlockSpec((tm, 2 * LRU_W), lambda i: (i, 0)),
                   pl.BlockSpec((tm, 3 * SB_W), lambda i: (i, 0))],
        out_shape=[jax.ShapeDtypeStruct((T, 2 * LRU_W), F32), jax.ShapeDtypeStruct((T, 3 * SB_W), BF16)],
        compiler_params=_params(("arbitrary",), 48),
    )(x2, g1, w_in16t)


def _lru_gates(c, wa_ref, wx_ref, ba_ref, bx_ref, lam_ref):
    c16 = c.astype(BF16)
    r = _sigmoid(_dot(c16, wa_ref[0]) + ba_ref[...])
    i = _sigmoid(_dot(c16, wx_ref[0]) + bx_ref[...])
    lam = lam_ref[...]
    e = jnp.exp(-jnp.abs(lam))
    sp = jnp.maximum(-lam, 0.0) + _log1p_pos(e)
    dsp_dlam = -jnp.where(lam >= 0.0, e, 1.0) / (1.0 + e)
    log_a = (-LRU_C) * r * sp
    a = jnp.exp(log_a)
    s = jnp.sqrt(_neg_expm1(2.0 * log_a))
    return r, i, sp, dsp_dlam, a, s


def _conv_taps(x, halo, lc):
    xe = jnp.concatenate([halo, x], axis=0)
    return [x] + [pltpu.roll(xe, k, 0)[SUBLANES:SUBLANES + lc] for k in range(1, CONV_K)]


def _lru_fwd(proj_lru, conv_w, conv_b, wa_bd, wx_bd, b_a, b_x, lam, B, S, lc):
    T = B * S
    nc = S // lc
    ncb = LRU_W // LANES

    def body(x_ref, g_ref, cw_ref, cb_ref, wa_ref, wx_ref, ba_ref, bx_ref, lam_ref, y_ref, h_ref, tail, carry):
        ci = pl.program_id(2)

        @pl.when(ci == 0)
        def _():
            tail[...] = jnp.zeros_like(tail)
            carry[...] = jnp.zeros_like(carry)

        x = x_ref[...]
        taps = _conv_taps(x, tail[...], lc)
        c = cb_ref[...] + sum(cw_ref[pl.ds(CONV_K - 1 - k, 1), :] * taps[k] for k in range(CONV_K))
        tail[...] = x_ref[pl.ds(lc - SUBLANES, SUBLANES), :]
        r, i, sp, _, a, s = _lru_gates(c, wa_ref, wx_ref, ba_ref, bx_ref, lam_ref)
        h_loc, a_run = _scan(a, s * (i * c), reverse=False)
        h_ref[...] = h_loc + a_run * carry[...]
        carry[...] = h_ref[pl.ds(lc - 1, 1), :]
        gelu, _ = _gelu_parts(g_ref[...])
        y_ref[...] = h_ref[...] * gelu

    chan = lambda b, cb, ci: (0, cb)
    return pl.pallas_call(
        body, name="lru_fwd", grid=(B, ncb, nc),
        in_specs=[pl.BlockSpec((lc, LANES), lambda b, cb, ci: (b * nc + ci, cb)),
                  pl.BlockSpec((lc, LANES), lambda b, cb, ci: (b * nc + ci, ncb + cb)),
                  pl.BlockSpec((CONV_K, LANES), chan), pl.BlockSpec((1, LANES), chan),
                  pl.BlockSpec((1, LANES, LANES), lambda b, cb, ci: (cb, 0, 0)),
                  pl.BlockSpec((1, LANES, LANES), lambda b, cb, ci: (cb, 0, 0)),
                  pl.BlockSpec((1, LANES), chan), pl.BlockSpec((1, LANES), chan), pl.BlockSpec((1, LANES), chan)],
        out_specs=[pl.BlockSpec((lc, LANES), lambda b, cb, ci: (b * nc + ci, cb))] * 2,
        out_shape=[jax.ShapeDtypeStruct((T, LRU_W), F32)] * 2,
        scratch_shapes=[pltpu.VMEM((SUBLANES, LANES), F32), pltpu.VMEM((1, LANES), F32)],
        compiler_params=_params(("arbitrary", "arbitrary", "arbitrary"), 32),
    )(proj_lru, proj_lru, conv_w, conv_b, wa_bd, wx_bd, b_a, b_x, lam)


def _tri(prefix):
    r = lax.broadcasted_iota(jnp.int32, (K_BLK, K_BLK), 0)
    c = lax.broadcasted_iota(jnp.int32, (K_BLK, K_BLK), 1)
    return ((r <= c) if prefix else (r >= c)).astype(BF16)


def _attn_consts(qi):
    r = lax.broadcasted_iota(jnp.int32, (Q_BLK, K_BLK), 0)
    c = lax.broadcasted_iota(jnp.int32, (Q_BLK, K_BLK), 1)
    causal = c + ((qi // Q_PER_K) * K_BLK - qi * Q_BLK) < r
    lane = lax.broadcasted_iota(jnp.int32, (1, LANES), 1)
    return causal, lane, (lane < HEAD_D, lane >= HEAD_D)


def _log1m_beta(z, mask):
    lg = -(jnp.maximum(z, 0.0) + jnp.log(1.0 + jnp.exp(-jnp.abs(z))))
    return lg if mask is None else jnp.where(mask, lg, 0.0)


def _key_rows(j):
    return pl.ds(pl.multiple_of(j * K_BLK, K_BLK), K_BLK)


def _attn_fwd(qkv, tri_suffix, ride, ride_sliced, B, S):
    T = B * S
    nq = S // Q_BLK
    nhp = SB_W // LANES
    scale = HEAD_D ** -0.5
    assert S // K_BLK <= LANES
    nr = len(ride)
    ride_shape, ride_specs, ride_sems = _exchange_shapes(ride, ride_sliced)

    def body(q_ref, k_ref, v_ref, tri_ref, *rest):
        o_ref, run_ref = rest[nr:nr + 2]
        start_ride, wait_ride = _exchange_copies(rest[:nr], rest[nr + 2:2 * nr + 2], ride_sliced, *rest[2 * nr + 2:])
        qi = pl.program_id(2)
        step_no = (pl.program_id(0) * nhp + pl.program_id(1)) * nq + qi
        pl.when(step_no == 0)(start_ride)
        jd = qi // Q_PER_K
        causal, lane, halves = _attn_consts(qi)
        q = q_ref[...]
        qh = [jnp.where(hm, q, jnp.zeros_like(q)) * jnp.asarray(scale, BF16) for hm in halves]

        def group(blocks, carry):
            runs, tables, acc = carry
            runs, tables = list(runs), list(tables)
            ks = [k_ref[_key_rows(jl), :] for jl, _, _ in blocks]
            vs = [v_ref[_key_rows(jl), :] for jl, _, _ in blocks]
            chains = [(b, h) for b in range(len(blocks)) for h in range(2)]
            z = {c: _dot_nt(qh[c[1]], ks[c[0]]) for c in chains}
            lg = {c: _log1m_beta(z[c], blocks[c[0]][2]) for c in chains}
            suf = {c: _dot(lg[c].astype(BF16), tri_ref[...]) for c in chains}
            att = {}
            for b, h in chains:
                _, jlane, mask = blocks[b]
                a = jnp.exp(z[b, h] + suf[b, h] + runs[h])
                att[b, h] = (a if mask is None else jnp.where(mask, a, 0.0)).astype(BF16)
                tables[h] = jnp.where(lane == jlane, runs[h], tables[h])
                runs[h] = runs[h] + suf[b, h][:, 0:1]
            for b, h in chains:
                acc = acc + _dot(att[b, h], jnp.where(halves[h], vs[b], jnp.zeros_like(vs[b])))
            return tuple(runs), tuple(tables), acc

        col0 = jnp.zeros((Q_BLK, 1), F32)
        zero = jnp.zeros((Q_BLK, LANES), F32)
        start = ((col0, col0), (zero, zero), zero)
        two = jd % 2
        carry = lax.cond(two == 1, lambda: group([(jd, jd, causal), (jd - 1, jd - 1, None)], start),
                         lambda: group([(jd, jd, causal)], start))
        top = jd - 1 - two

        def run(ja, n):
            return [(ja - i, ja - i, None) for i in range(n)]

        odd_pair = ((top + 1) // 2) % 2
        carry = lax.cond(odd_pair == 1, lambda cr: group(run(top, 2), cr), lambda cr: cr, carry)
        top4 = top - 2 * odd_pair
        _, tables, acc = lax.fori_loop(0, (top4 + 1) // 4, lambda it, cr: group(run(top4 - 4 * it, 4), cr), carry)
        o_ref[...] = acc
        run_ref[:, 0:LANES] = tables[0]
        run_ref[:, LANES:2 * LANES] = tables[1]
        pl.when(step_no == B * nhp * nq - 1)(wait_ride)

    return pl.pallas_call(
        body, name="attn_fwd", grid=(B, nhp, nq),
        in_specs=[pl.BlockSpec((Q_BLK, LANES), lambda b, hp, qi: (b * nq + qi, hp)),
                  pl.BlockSpec((S, LANES), lambda b, hp, qi: (b, nhp + hp)),
                  pl.BlockSpec((S, LANES), lambda b, hp, qi: (b, 2 * nhp + hp)),
                  pl.BlockSpec((K_BLK, K_BLK), lambda b, hp, qi: (0, 0))] + ride_specs,
        out_specs=[pl.BlockSpec((Q_BLK, LANES), lambda b, hp, qi: (b * nq + qi, hp)),
                   pl.BlockSpec((Q_BLK, 2 * LANES), lambda b, hp, qi: (b * nq + qi, hp))] + ride_specs,
        out_shape=[jax.ShapeDtypeStruct((T, SB_W), F32), jax.ShapeDtypeStruct((T, 2 * SB_W), F32)] + ride_shape,
        scratch_shapes=ride_sems,
        compiler_params=_params(("arbitrary", "arbitrary", "arbitrary"), 48),
    )(qkv, qkv, qkv, tri_suffix, *ride)


def _fwd_mix(y_lru, y_sb, ga, gb, w_out16, x2, tm):
    T = x2.shape[0]

    def body(yl_ref, ys_ref, ga_ref, gb_ref, w_ref, x_ref, h1_ref, mix_ref):
        na, _, _ = _rms_fwd(yl_ref[...], ga_ref[...])
        nb, _, _ = _rms_fwd(ys_ref[...], gb_ref[...])
        na = na.astype(BF16)
        nb = nb.astype(BF16)
        mix_ref[:, 0:LRU_W] = na
        mix_ref[:, LRU_W:D_MODEL] = nb
        h1_ref[...] = x_ref[...] + _dot(na, w_ref[0:LRU_W, :]) + _dot(nb, w_ref[LRU_W:D_MODEL, :])

    row = lambda i: (i, 0)
    fix = lambda i: (0, 0)
    return pl.pallas_call(
        body, name="fwd_mix", grid=(T // tm,),
        in_specs=[pl.BlockSpec((tm, LRU_W), row), pl.BlockSpec((tm, SB_W), row),
                  pl.BlockSpec((1, LRU_W), fix), pl.BlockSpec((1, SB_W), fix),
                  pl.BlockSpec((D_MODEL, D_MODEL), fix), pl.BlockSpec((tm, D_MODEL), row)],
        out_specs=[pl.BlockSpec((tm, D_MODEL), row), pl.BlockSpec((tm, D_MODEL), row)],
        out_shape=[jax.ShapeDtypeStruct((T, D_MODEL), F32), jax.ShapeDtypeStruct((T, D_MODEL), BF16)],
        compiler_params=_params(("arbitrary",), 48),
    )(y_lru, y_sb, ga, gb, w_out16, x2)


def _fwd_mlp(h1, g2, w_up16, w_down16, gf, tgt, tm, per_step):
    T = h1.shape[0]
    nf = D_FF // (FF_CHUNK * per_step)

    def body(h1_ref, g2_ref, wu_ref, wd_ref, gf_ref, t_ref, up_ref, dh2_ref, dgf_ref, loss_ref, hn_s, acc):
        i, j = pl.program_id(0), pl.program_id(1)

        @pl.when(j == 0)
        def _():
            h1v = h1_ref[...]
            hn, _, _ = _rms_fwd(h1v, g2_ref[...])
            hn_s[...] = hn.astype(BF16)
            acc[...] = h1v

        down = None
        for c in range(per_step):
            cols = slice(c * FF_CHUNK, (c + 1) * FF_CHUNK)
            up = jnp.maximum(_dot(hn_s[...], wu_ref[c]), 0.0)
            up_ref[:, cols] = up.astype(BF16)
            part = _dot((up * up).astype(BF16), wd_ref[cols, :])
            down = part if down is None else down + part
        acc[...] += down

        @pl.when((i == 0) & (j == 0))
        def _():
            dgf_ref[...] = jnp.zeros_like(dgf_ref)
            loss_ref[...] = jnp.zeros_like(loss_ref)

        @pl.when(j == nf - 1)
        def _():
            gfv = gf_ref[...]
            y, xhat, rstd = _rms_fwd(acc[...], gfv)
            err = y - t_ref[...]
            loss_ref[...] += jnp.sum(0.5 * jnp.sum(err * err, axis=-1, keepdims=True) * (1.0 / D_MODEL))
            dy = err * (1.0 / D_MODEL)
            dgf_ref[...] += jnp.sum(dy * xhat, axis=0, keepdims=True)
            dh2_ref[...] = _rms_bwd(dy, xhat, rstd, gfv)

    row = lambda i, j: (i, 0)
    fix = lambda i, j: (0, 0)
    return pl.pallas_call(
        body, name="fwd_mlp", grid=(T // tm, nf),
        in_specs=[pl.BlockSpec((tm, D_MODEL), row), pl.BlockSpec((1, D_MODEL), fix),
                  pl.BlockSpec((per_step, D_MODEL, FF_CHUNK), lambda i, j: (j, 0, 0)),
                  pl.BlockSpec((per_step * FF_CHUNK, D_MODEL), lambda i, j: (j, 0)),
                  pl.BlockSpec((1, D_MODEL), fix), pl.BlockSpec((tm, D_MODEL), row)],
        out_specs=[pl.BlockSpec((tm, per_step * FF_CHUNK), lambda i, j: (i, j)), pl.BlockSpec((tm, D_MODEL), row),
                   pl.BlockSpec((1, D_MODEL), fix), pl.BlockSpec((1, LANES), fix)],
        out_shape=[jax.ShapeDtypeStruct((T, D_FF), BF16), jax.ShapeDtypeStruct((T, D_MODEL), F32),
                   jax.ShapeDtypeStruct((1, D_MODEL), F32), jax.ShapeDtypeStruct((1, LANES), F32)],
        scratch_shapes=[pltpu.VMEM((tm, D_MODEL), BF16), pltpu.VMEM((tm, D_MODEL), F32)],
        compiler_params=_params(("arbitrary", "arbitrary"), 56),
    )(h1, g2, w_up16, w_down16, gf, tgt)


def _bwd_mlp(dh2, up16, h1, g2, w_up16, w_down16, tm, per_step):
    T = h1.shape[0]
    nf = D_FF // (FF_CHUNK * per_step)

    def body(dh2_ref, up_ref, h1_ref, g2_ref, wu_ref, wd_ref, dup_ref, dh1_ref, hn_ref, dh2b_ref, dg2_ref, acc):
        i, j = pl.program_id(0), pl.program_id(1)

        @pl.when(j == 0)
        def _():
            hn, _, _ = _rms_fwd(h1_ref[...], g2_ref[...])
            hn_ref[...] = hn.astype(BF16)
            dh2b_ref[...] = dh2_ref[...].astype(BF16)
            acc[...] = jnp.zeros_like(acc)

        dhn = None
        for c in range(per_step):
            cols = slice(c * FF_CHUNK, (c + 1) * FF_CHUNK)
            u = up_ref[:, cols].astype(F32)
            dup = (2.0 * u * _dot_nt(dh2b_ref[...], wd_ref[cols, :])).astype(BF16)
            dup_ref[:, cols] = dup
            part = _dot_nt(dup, wu_ref[c])
            dhn = part if dhn is None else dhn + part
        acc[...] += dhn

        @pl.when((i == 0) & (j == 0))
        def _():
            dg2_ref[...] = jnp.zeros_like(dg2_ref)

        @pl.when(j == nf - 1)
        def _():
            g2v = g2_ref[...]
            _, xhat, rstd = _rms_fwd(h1_ref[...], g2v)
            dhn = acc[...]
            dg2_ref[...] += jnp.sum(dhn * xhat, axis=0, keepdims=True)
            dh1_ref[...] = dh2_ref[...] + _rms_bwd(dhn, xhat, rstd, g2v)

    row = lambda i, j: (i, 0)
    fix = lambda i, j: (0, 0)
    return pl.pallas_call(
        body, name="bwd_mlp", grid=(T // tm, nf),
        in_specs=[pl.BlockSpec((tm, D_MODEL), row), pl.BlockSpec((tm, per_step * FF_CHUNK), lambda i, j: (i, j)),
                  pl.BlockSpec((tm, D_MODEL), row), pl.BlockSpec((1, D_MODEL), fix),
                  pl.BlockSpec((per_step, D_MODEL, FF_CHUNK), lambda i, j: (j, 0, 0)),
                  pl.BlockSpec((per_step * FF_CHUNK, D_MODEL), lambda i, j: (j, 0))],
        out_specs=[pl.BlockSpec((tm, per_step * FF_CHUNK), lambda i, j: (i, j)), pl.BlockSpec((tm, D_MODEL), row),
                   pl.BlockSpec((tm, D_MODEL), row), pl.BlockSpec((tm, D_MODEL), row),
                   pl.BlockSpec((1, D_MODEL), fix)],
        out_shape=[jax.ShapeDtypeStruct((T, D_FF), BF16), jax.ShapeDtypeStruct((T, D_MODEL), F32),
                   jax.ShapeDtypeStruct((T, D_MODEL), BF16), jax.ShapeDtypeStruct((T, D_MODEL), BF16),
                   jax.ShapeDtypeStruct((1, D_MODEL), F32)],
        scratch_shapes=[pltpu.VMEM((tm, D_MODEL), F32)],
        compiler_params=_params(("arbitrary", "arbitrary"), 48),
    )(dh2, up16, h1, g2, w_up16, w_down16)


def _matmul_tn(name, a, b, bm, bn, tk, out_shape, out_block, out_index, a_prep=None, b_prep=None, out_dtype=F32,
               ride=None, ride_sliced=None, split=None, a_blocks=None):
    T, M = a.shape
    N = b.shape[1]
    first, count = a_blocks or (0, M // bm)
    grid = (count, N // bn, T // tk)
    ride, ride_sliced = list(ride or []), list(ride_sliced or [])
    nr = len(ride)
    ride_shape, ride_specs, ride_sems = _exchange_shapes(ride, ride_sliced)

    def body(a_ref, b_ref, *rest):
        o_ref, acc = rest[nr], rest[2 * nr + 1]
        k = pl.program_id(2)
        step_no = (pl.program_id(0) * grid[1] + pl.program_id(1)) * grid[2] + k
        if nr:
            start_ride, wait_ride = _exchange_copies(rest[:nr], rest[nr + 1:2 * nr + 1], ride_sliced,
                                                     *rest[2 * nr + 2:])
            pl.when(step_no == 0)(start_ride)
        av = a_ref[...] if a_prep is None else a_prep(a_ref[...])
        bv = b_ref[...] if b_prep is None else b_prep(b_ref[...])
        p = _dot_tn(av, bv)

        @pl.when(k == 0)
        def _():
            acc[...] = p

        @pl.when(k > 0)
        def _():
            acc[...] += p

        @pl.when(k == grid[2] - 1)
        def _():
            if split is None:
                o_ref[...] = acc[...].astype(out_dtype)
            else:
                axis, n = split
                w = (bm, bn)[axis] // n
                for c in range(n):
                    slab = acc[c * w:(c + 1) * w, :] if axis == 0 else acc[:, c * w:(c + 1) * w]
                    o_ref[c] = slab.astype(out_dtype)

        if nr:
            pl.when(step_no == grid[0] * grid[1] * grid[2] - 1)(wait_ride)

    return pl.pallas_call(
        body, name=name, grid=grid,
        in_specs=[pl.BlockSpec((tk, bm), lambda m, n, k: (k, first + m)),
                  pl.BlockSpec((tk, bn), lambda m, n, k: (k, n))] + ride_specs,
        out_specs=[pl.BlockSpec(out_block, out_index)] + ride_specs,
        out_shape=[jax.ShapeDtypeStruct(out_shape, out_dtype)] + ride_shape,
        scratch_shapes=[pltpu.VMEM((bm, bn), F32)] + (ride_sems if nr else []),
        compiler_params=_params(("arbitrary", "arbitrary", "arbitrary"), 48),
    )(a, b, *ride)


def _bwd_mix(dh1, w_out16, y_lru, y_sb, ga, gb, tm):
    T = dh1.shape[0]

    def body(d_ref, w_ref, yl_ref, ys_ref, ga_ref, gb_ref, dyl_ref, dys_ref, dga_ref, dgb_ref):
        @pl.when(pl.program_id(0) == 0)
        def _():
            dga_ref[...] = jnp.zeros_like(dga_ref)
            dgb_ref[...] = jnp.zeros_like(dgb_ref)

        d16 = d_ref[...].astype(BF16)
        for y_ref, g_ref, lo, dy_ref, dg_ref in ((yl_ref, ga_ref, 0, dyl_ref, dga_ref),
                                                 (ys_ref, gb_ref, LRU_W, dys_ref, dgb_ref)):
            gv = g_ref[...]
            dn = _dot_nt(d16, w_ref[lo:lo + LRU_W, :])
            _, xhat, rstd = _rms_fwd(y_ref[...], gv)
            dg_ref[...] += jnp.sum(dn * xhat, axis=0, keepdims=True)
            dy_ref[...] = _rms_bwd(dn, xhat, rstd, gv).astype(dy_ref.dtype)

    row = lambda i: (i, 0)
    fix = lambda i: (0, 0)
    return pl.pallas_call(
        body, name="bwd_mix", grid=(T // tm,),
        in_specs=[pl.BlockSpec((tm, D_MODEL), row), pl.BlockSpec((D_MODEL, D_MODEL), fix),
                  pl.BlockSpec((tm, LRU_W), row), pl.BlockSpec((tm, SB_W), row),
                  pl.BlockSpec((1, LRU_W), fix), pl.BlockSpec((1, SB_W), fix)],
        out_specs=[pl.BlockSpec((tm, LRU_W), row), pl.BlockSpec((tm, SB_W), row),
                   pl.BlockSpec((1, LRU_W), fix), pl.BlockSpec((1, SB_W), fix)],
        out_shape=[jax.ShapeDtypeStruct((T, LRU_W), F32), jax.ShapeDtypeStruct((T, SB_W), BF16),
                   jax.ShapeDtypeStruct((1, LRU_W), F32), jax.ShapeDtypeStruct((1, SB_W), F32)],
        compiler_params=_params(("arbitrary",), 48),
    )(dh1, w_out16, y_lru, y_sb, ga, gb)


def _attn_bwd(qkv, run_tab, dy_sb, tri_suffix, tri_prefix, ride, ride_sliced, B, S):
    T = B * S
    nq = S // Q_BLK
    nkb = S // K_BLK
    nhp = SB_W // LANES
    scale = HEAD_D ** -0.5

    nr = len(ride)
    ride_shape, ride_specs, ride_sems = _exchange_shapes(ride, ride_sliced)

    def body(q_ref, k_ref, v_ref, run_ref, do_ref, ts_ref, tp_ref, *rest):
        dq_ref, dk_ref, dv_ref = rest[nr:nr + 3]
        dkt_ref, dvt_ref = rest[2 * nr + 3:2 * nr + 5]
        start_ride, wait_ride = _exchange_copies(rest[:nr], rest[nr + 3:2 * nr + 3], ride_sliced, *rest[2 * nr + 5:])
        qi = pl.program_id(2)
        step_no = (pl.program_id(0) * nhp + pl.program_id(1)) * nq + qi
        pl.when(step_no == 0)(start_ride)
        jd = qi // Q_PER_K

        @pl.when(qi == 0)
        def _():
            dkt_ref[...] = jnp.zeros_like(dkt_ref)
            dvt_ref[...] = jnp.zeros_like(dvt_ref)

        causal, lane, halves = _attn_consts(qi)
        q = q_ref[...]
        do = do_ref[...]
        qh = [jnp.where(hm, q, jnp.zeros_like(q)) * jnp.asarray(scale, BF16) for hm in halves]
        doh = [jnp.where(hm, do, jnp.zeros_like(do)) for hm in halves]
        qt = jnp.concatenate([h.astype(F32).T.astype(BF16) for h in qh], axis=1)
        dot_ = jnp.concatenate([h.astype(F32).T.astype(BF16) for h in doh], axis=1)
        tables = [run_ref[:, 0:LANES], run_ref[:, LANES:2 * LANES]]

        def group(blocks, carry):
            prefixes, dq = carry
            prefixes = list(prefixes)
            ks = [k_ref[_key_rows(jl), :] for jl, _, _ in blocks]
            vs = [v_ref[_key_rows(jl), :] for jl, _, _ in blocks]
            chains = [(b, h) for b in range(len(blocks)) for h in range(2)]
            z = {c: _dot_nt(qh[c[1]], ks[c[0]]) for c in chains}
            da = {c: _dot_nt(doh[c[1]], vs[c[0]]) for c in chains}
            lg = {c: _log1m_beta(z[c], blocks[c[0]][2]) for c in chains}
            suf = {c: _dot(lg[c].astype(BF16), ts_ref[...]) for c in chains}
            att, g = {}, {}
            for b, h in chains:
                _, jlane, mask = blocks[b]
                run = jnp.sum(jnp.where(lane == jlane, tables[h], 0.0), axis=1, keepdims=True)
                a = jnp.exp(z[b, h] + suf[b, h] + run)
                a = a if mask is None else jnp.where(mask, a, 0.0)
                g[b, h] = a * da[b, h]
                att[b, h] = a.astype(BF16)
            gpre = {c: _dot(g[c].astype(BF16), tp_ref[...]) for c in chains}
            dz = {}
            for b, h in chains:
                mask = blocks[b][2]
                d = g[b, h] - jnp.exp(z[b, h] + lg[b, h]) * (prefixes[h] + gpre[b, h])
                dz[b, h] = (d if mask is None else jnp.where(mask, d, 0.0)).astype(BF16)
                prefixes[h] = prefixes[h] + gpre[b, h][:, K_BLK - 1:K_BLK]
            for b, h in chains:
                dq = dq + _dot(dz[b, h], jnp.where(halves[h], ks[b], jnp.zeros_like(ks[b])))
            for b, (jl, _, _) in enumerate(blocks):
                dkt_ref[jl] += _dot(qt, jnp.concatenate([dz[b, 0], dz[b, 1]], axis=0))
                dvt_ref[jl] += _dot(dot_, jnp.concatenate([att[b, 0], att[b, 1]], axis=0))
            return tuple(prefixes), dq

        def run(ja, n):
            return [(ja + i, ja + i, None) for i in range(n)]

        col0 = jnp.zeros((Q_BLK, 1), F32)
        fours = (jd // 2) // 2
        carry = lax.fori_loop(0, fours, lambda it, cr: group(run(4 * it, 4), cr),
                              ((col0, col0), jnp.zeros((Q_BLK, LANES), F32)))
        carry = lax.cond((jd // 2) % 2 == 1, lambda cr: group(run(4 * fours, 2), cr), lambda cr: cr, carry)
        carry = lax.cond(jd % 2 == 1, lambda cr: group([(jd - 1, jd - 1, None), (jd, jd, causal)], cr),
                         lambda cr: group([(jd, jd, causal)], cr), carry)
        dq_ref[...] = (carry[1] * scale).astype(BF16)

        @pl.when(qi == nq - 1)
        def _():
            for j in range(nkb):
                dk_ref[j * K_BLK:(j + 1) * K_BLK, :] = dkt_ref[j].T.astype(BF16)
                dv_ref[j * K_BLK:(j + 1) * K_BLK, :] = dvt_ref[j].T.astype(BF16)

        pl.when(step_no == B * nhp * nq - 1)(wait_ride)

    qblk = pl.BlockSpec((Q_BLK, LANES), lambda b, hp, qi: (b * nq + qi, hp))
    tri = pl.BlockSpec((K_BLK, K_BLK), lambda b, hp, qi: (0, 0))
    seq = pl.BlockSpec((S, LANES), lambda b, hp, qi: (b, hp))
    return pl.pallas_call(
        body, name="attn_bwd", grid=(B, nhp, nq),
        in_specs=[qblk, pl.BlockSpec((S, LANES), lambda b, hp, qi: (b, nhp + hp)),
                  pl.BlockSpec((S, LANES), lambda b, hp, qi: (b, 2 * nhp + hp)),
                  pl.BlockSpec((Q_BLK, 2 * LANES), lambda b, hp, qi: (b * nq + qi, hp)), qblk, tri, tri] + ride_specs,
        out_specs=[qblk, seq, seq] + ride_specs,
        out_shape=[jax.ShapeDtypeStruct((T, SB_W), BF16)] * 3 + ride_shape,
        scratch_shapes=[pltpu.VMEM((nkb, LANES, K_BLK), F32)] * 2 + ride_sems,
        compiler_params=_params(("arbitrary", "arbitrary", "arbitrary"), 48),
    )(qkv, qkv, qkv, run_tab, dy_sb, tri_suffix, tri_prefix, *ride)


def _lru_bwd(proj_lru, h, dy_lru, conv_w, conv_b, wa_bd, wx_bd, b_a, b_x, lam, B, S, lc):
    T = B * S
    nc = S // lc
    ncb = LRU_W // LANES
    hpc = lc // SUBLANES

    def body(x_ref, xh_ref, g_ref, h_ref, hh_ref, dy_ref, cw_ref, cb_ref, wa_ref, wx_ref, ba_ref, bx_ref, lam_ref,
             dx_ref, dg_ref, dcw_ref, dcb_ref, dwa_ref, dwx_ref, dba_ref, dbx_ref, dlam_ref,
             lam_s, dc_s, a_first, lam_first, dc_head):
        b, ci = pl.program_id(1), pl.program_id(2)
        first_chunk = ci == nc - 1

        @pl.when(ci == 0)
        def _():
            a_first[...] = jnp.zeros_like(a_first)
            lam_first[...] = jnp.zeros_like(lam_first)
            dc_head[...] = jnp.zeros_like(dc_head)

        @pl.when((b == 0) & (ci == 0))
        def _():
            for ref in (dcw_ref, dcb_ref, dwa_ref, dwx_ref, dba_ref, dbx_ref, dlam_ref):
                ref[...] = jnp.zeros_like(ref)

        x = x_ref[...]
        taps = _conv_taps(x, jnp.where(first_chunk, 0.0, xh_ref[...]), lc)
        c = cb_ref[...] + sum(cw_ref[pl.ds(CONV_K - 1 - k, 1), :] * taps[k] for k in range(CONV_K))
        r, i, sp, dsp_dlam, a, s = _lru_gates(c, wa_ref, wx_ref, ba_ref, bx_ref, lam_ref)
        hv = h_ref[...]
        he = jnp.concatenate([jnp.where(first_chunk, 0.0, hh_ref[...]), hv], axis=0)
        h_prev = pltpu.roll(he, 1, 0)[SUBLANES:SUBLANES + lc]
        dy = dy_ref[...]
        gelu, dgelu = _gelu_parts(g_ref[...])
        dg_ref[...] = (dy * hv * dgelu).astype(BF16)

        row = lax.broadcasted_iota(jnp.int32, (lc, LANES), 0)
        a_next = jnp.where(row < lc - 1, pltpu.roll(a, lc - 1, 0), a_first[...])
        lam_loc, a_run = _scan(a_next, dy * gelu, reverse=True)
        lam_s[...] = lam_loc + a_run * lam_first[...]
        lam_t = lam_s[...]
        lam_first[...] = lam_s[pl.ds(0, 1), :]
        lam_s[...] = a
        a_first[...] = lam_s[pl.ds(0, 1), :]

        ic = i * c
        dlog_a = lam_t * h_prev * a - (lam_t * ic) * (a * a) / s
        dpre_r = (dlog_a * ((-LRU_C) * sp)) * r * (1.0 - r)
        dpre_i = (lam_t * s * c) * i * (1.0 - i)
        dlam_ref[...] += jnp.sum(dlog_a * r, axis=0, keepdims=True) * ((-LRU_C) * dsp_dlam)
        dr16 = dpre_r.astype(BF16)
        di16 = dpre_i.astype(BF16)
        c16 = c.astype(BF16)
        dwa_ref[0] += _dot_tn(c16, dr16)
        dwx_ref[0] += _dot_tn(c16, di16)
        dba_ref[...] += jnp.sum(dpre_r, axis=0, keepdims=True)
        dbx_ref[...] += jnp.sum(dpre_i, axis=0, keepdims=True)
        dc = lam_t * s * i + _dot_nt(dr16, wa_ref[0]) + _dot_nt(di16, wx_ref[0])
        dcb_ref[...] += jnp.sum(dc, axis=0, keepdims=True)
        for k in range(CONV_K):
            dcw_ref[pl.ds(CONV_K - 1 - k, 1), :] += jnp.sum(dc * taps[k], axis=0, keepdims=True)
        dce = jnp.concatenate([dc, dc_head[...]], axis=0)
        dx = cw_ref[pl.ds(CONV_K - 1, 1), :] * dc
        for k in range(1, CONV_K):
            dx = dx + cw_ref[pl.ds(CONV_K - 1 - k, 1), :] * pltpu.roll(dce, lc + SUBLANES - k, 0)[0:lc]
        dx_ref[...] = dx.astype(BF16)
        dc_s[...] = dc
        dc_head[...] = dc_s[pl.ds(0, SUBLANES), :]

    def chunk(col):
        return pl.BlockSpec((lc, LANES), lambda cb, b, ci: (b * nc + nc - 1 - ci, col(cb)))

    def halo(col):
        return pl.BlockSpec((SUBLANES, LANES),
                            lambda cb, b, ci: (jnp.maximum((b * nc + nc - 1 - ci) * hpc - 1, 0), col(cb)))

    chan = lambda cb, b, ci: (0, cb)
    blk = lambda cb, b, ci: (cb, 0, 0)
    vec = pl.BlockSpec((1, LANES), chan)
    mat = pl.BlockSpec((1, LANES, LANES), blk)
    return pl.pallas_call(
        body, name="lru_bwd", grid=(ncb, B, nc),
        in_specs=[chunk(lambda cb: cb), halo(lambda cb: cb), chunk(lambda cb: ncb + cb),
                  chunk(lambda cb: cb), halo(lambda cb: cb), chunk(lambda cb: cb),
                  pl.BlockSpec((CONV_K, LANES), chan), vec, mat, mat, vec, vec, vec],
        out_specs=[chunk(lambda cb: cb), chunk(lambda cb: cb), pl.BlockSpec((CONV_K, LANES), chan), vec,
                   mat, mat, vec, vec, vec],
        out_shape=[jax.ShapeDtypeStruct((T, LRU_W), BF16), jax.ShapeDtypeStruct((T, LRU_W), BF16),
                   jax.ShapeDtypeStruct((CONV_K, LRU_W), F32), jax.ShapeDtypeStruct((1, LRU_W), F32),
                   jax.ShapeDtypeStruct((ncb, LANES, LANES), F32), jax.ShapeDtypeStruct((ncb, LANES, LANES), F32),
                   jax.ShapeDtypeStruct((1, LRU_W), F32), jax.ShapeDtypeStruct((1, LRU_W), F32),
                   jax.ShapeDtypeStruct((1, LRU_W), F32)],
        scratch_shapes=[pltpu.VMEM((lc, LANES), F32), pltpu.VMEM((lc, LANES), F32), pltpu.VMEM((1, LANES), F32),
                        pltpu.VMEM((1, LANES), F32), pltpu.VMEM((SUBLANES, LANES), F32)],
        compiler_params=_params(("arbitrary", "arbitrary", "arbitrary"), 32),
    )(proj_lru, proj_lru, proj_lru, h, h, dy_lru, conv_w, conv_b, wa_bd, wx_bd, b_a, b_x, lam)


def _bwd_in(pieces, w_in16t, x2, g1, dh1, tm):
    T = x2.shape[0]
    npc = len(pieces)

    def body(*refs):
        p_refs = refs[:npc]
        w_ref, x_ref, g_ref, d_ref, dx_ref, dproj_ref, xn_ref, dg1_ref = refs[npc:]

        @pl.when(pl.program_id(0) == 0)
        def _():
            dg1_ref[...] = jnp.zeros_like(dg1_ref)

        dxn = jnp.zeros((tm, D_MODEL), F32)
        for n, p_ref in enumerate(p_refs):
            cols = slice(n * LRU_W, (n + 1) * LRU_W)
            p16 = p_ref[...]
            dproj_ref[:, cols] = p16
            dxn = dxn + _dot(p16, w_ref[cols, :])
        gv = g_ref[...]
        xn, xhat, rstd = _rms_fwd(x_ref[...], gv)
        xn_ref[...] = xn.astype(BF16)
        dg1_ref[...] += jnp.sum(dxn * xhat, axis=0, keepdims=True)
        dx_ref[...] = d_ref[...] + _rms_bwd(dxn, xhat, rstd, gv)

    row = lambda i: (i, 0)
    fix = lambda i: (0, 0)
    return pl.pallas_call(
        body, name="bwd_in", grid=(T // tm,),
        in_specs=[pl.BlockSpec((tm, LRU_W), row)] * npc + [
            pl.BlockSpec((IN_COLS, D_MODEL), fix), pl.BlockSpec((tm, D_MODEL), row),
            pl.BlockSpec((1, D_MODEL), fix), pl.BlockSpec((tm, D_MODEL), row)],
        out_specs=[pl.BlockSpec((tm, D_MODEL), row), pl.BlockSpec((tm, IN_COLS), row),
                   pl.BlockSpec((tm, D_MODEL), row), pl.BlockSpec((1, D_MODEL), fix)],
        out_shape=[jax.ShapeDtypeStruct((T, D_MODEL), F32), jax.ShapeDtypeStruct((T, IN_COLS), BF16),
                   jax.ShapeDtypeStruct((T, D_MODEL), BF16), jax.ShapeDtypeStruct((1, D_MODEL), F32)],
        compiler_params=_params(("arbitrary",), 56),
    )(*pieces, w_in16t, x2, g1, dh1)


def _adam_shards(name, shards, steps, ride, ride_sliced):
    ns = len(shards)
    nr = len(ride)
    ride_shape, ride_specs, ride_sems = _exchange_shapes(ride, ride_sliced)

    def body(*refs):
        ins, rest = refs[:4 * ns], refs[4 * ns:]
        outs = rest[nr:nr + 4 * ns]
        start_ride, wait_ride = _exchange_copies(rest[:nr], rest[nr + 4 * ns:2 * nr + 4 * ns], ride_sliced,
                                                 *rest[2 * nr + 4 * ns:])
        pl.when(pl.program_id(0) == 0)(start_ride)
        for i in range(ns):
            p_ref, w_ref, m_ref, v_ref = ins[4 * i:4 * i + 4]
            g = p_ref[0]
            for p in range(1, N_DEV):
                g = g + p_ref[p]
            outs[4 * i][...] = g
            outs[4 * i + 1][...], outs[4 * i + 2][...], outs[4 * i + 3][...] = _adamw(w_ref[...], g, m_ref[...], v_ref[...])
        pl.when(pl.program_id(0) == steps - 1)(wait_ride)

    in_specs, out_specs, out_shape, args = [], [], [], []
    for parts, w, m, v in shards:
        R, C = w.shape
        blk = pl.BlockSpec((R // steps, C), lambda i: (i, 0))
        in_specs += [pl.BlockSpec((N_DEV, R // steps, C), lambda i: (0, i, 0)), blk, blk, blk]
        out_specs += [blk] * 4
        out_shape += [jax.ShapeDtypeStruct((R, C), F32)] * 4
        args += [parts, w, m, v]
    res = pl.pallas_call(
        body, name=name, grid=(steps,), in_specs=in_specs + ride_specs, out_specs=out_specs + ride_specs,
        out_shape=out_shape + ride_shape, scratch_shapes=ride_sems,
        compiler_params=_params(("arbitrary",), 48),
    )(*args, *ride)
    return [list(res[4 * i:4 * i + 4]) for i in range(ns)], list(res[4 * ns:])


def _adam_params(name, items):
    def body(*refs):
        ins, outs = refs[:4 * len(items)], refs[4 * len(items):]
        for i in range(len(items)):
            p_ref, w_ref, m_ref, v_ref = ins[4 * i:4 * i + 4]
            g = p_ref[0]
            for p in range(1, N_DEV):
                g = g + p_ref[p]
            outs[4 * i][...] = g
            outs[4 * i + 1][...], outs[4 * i + 2][...], outs[4 * i + 3][...] = _adamw(w_ref[...], g, m_ref[...], v_ref[...])

    args = [a for item in items for a in item]
    outs = [w for _, w, _, _ in items for _ in range(4)]
    res = pl.pallas_call(
        body, name=name, grid=(1,), in_specs=[_whole(a.shape) for a in args], out_specs=[_whole(w.shape) for w in outs],
        out_shape=[jax.ShapeDtypeStruct(w.shape, F32) for w in outs], compiler_params=_params(("arbitrary",), 32),
    )(*args)
    return [list(res[4 * i:4 * i + 4]) for i in range(len(items))]


def _adam_given(name, g, w, m, v, tr):
    R, C = w.shape

    def body(g_ref, w_ref, m_ref, v_ref, d_ref, m2_ref, v2_ref):
        d_ref[...], m2_ref[...], v2_ref[...] = _adamw(w_ref[...], g_ref[...], m_ref[...], v_ref[...])

    blk = pl.BlockSpec((tr, C), lambda i: (i, 0))
    return [g] + list(pl.pallas_call(
        body, name=name, grid=(R // tr,), in_specs=[blk] * 4, out_specs=[blk] * 3,
        out_shape=[jax.ShapeDtypeStruct((R, C), F32)] * 3, compiler_params=_params(("arbitrary",), 32),
    )(g, w, m, v))


def _sum_parts(name, parts):
    def body(p_ref, g_ref):
        g = p_ref[0].astype(F32)
        for p in range(1, N_DEV):
            g = g + p_ref[p].astype(F32)
        g_ref[...] = g

    return pl.pallas_call(
        body, name=name, grid=(1,), in_specs=[_whole(parts.shape)], out_specs=_whole(parts.shape[1:]),
        out_shape=jax.ShapeDtypeStruct(parts.shape[1:], F32), compiler_params=_params(("arbitrary",), 32),
    )(parts)


def _block_diag_pairs(w):
    w = w.reshape(LRU_BLOCKS // 2, 2, HEAD_D, HEAD_D)
    out = jnp.zeros((LRU_BLOCKS // 2, LANES, LANES), w.dtype)
    out = out.at[:, :HEAD_D, :HEAD_D].set(w[:, 0])
    return out.at[:, HEAD_D:, HEAD_D:].set(w[:, 1])


def _diag_blocks(w):
    return jnp.stack([w[:, :HEAD_D, :HEAD_D], w[:, HEAD_D:, HEAD_D:]], axis=1).reshape(LRU_BLOCKS, HEAD_D, HEAD_D)


def kernel(x, norm1_g, w_in, conv_w, conv_b, lru_w_a, lru_b_a, lru_w_x, lru_b_x, lru_lambda, lru_out_g, sb_out_g, w_out, norm2_g, w_up, w_down, final_g, loss_target, m_norm1_g, m_w_in, m_conv_w, m_conv_b, m_lru_w_a, m_lru_b_a, m_lru_w_x, m_lru_b_x, m_lru_lambda, m_lru_out_g, m_sb_out_g, m_w_out, m_norm2_g, m_w_up, m_w_down, m_final_g, v_norm1_g, v_w_in, v_conv_w, v_conv_b, v_lru_w_a, v_lru_b_a, v_lru_w_x, v_lru_b_x, v_lru_lambda, v_lru_out_g, v_sb_out_g, v_w_out, v_norm2_g, v_w_up, v_w_down, v_final_g):
    B, S, _ = x.shape
    T = B * S
    tm = min(512, T)
    tk = min(GRAD_TK, T)
    lc = min(512, S)
    x2 = x.reshape(T, D_MODEL)
    tgt = loss_target.reshape(T, D_MODEL)
    cw_cols = CONV_K * LRU_W // N_DEV // CONV_K

    shards16 = _cast_shards([w_in[0].T, w_out[0], w_up[0], w_down[0]])
    cw_pad = jnp.zeros((SUBLANES, LANES), F32).at[:CONV_K, :cw_cols].set(conv_w[0])
    g_in, g_cw = _gather_two_level("gather_w_in", [shards16[0], cw_pad])
    w_in16t = g_in.reshape(IN_COLS, D_MODEL)
    conv_w_full = g_cw[:, :CONV_K, :cw_cols].transpose(1, 0, 2).reshape(CONV_K, LRU_W)
    wa_bd = _block_diag_pairs(lru_w_a[0]).astype(BF16)
    wx_bd = _block_diag_pairs(lru_w_x[0]).astype(BF16)
    b_a = lru_b_a.reshape(1, LRU_W)
    b_x = lru_b_x.reshape(1, LRU_W)
    gf = final_g.reshape(1, D_MODEL)

    proj_lru, qkv = _fwd_in(x2, norm1_g, w_in16t, tm)
    y_lru, h = _lru_fwd(proj_lru, conv_w_full, conv_b, wa_bd, wx_bd, b_a, b_x, lru_lambda, B, S, lc)
    tri_suffix, tri_prefix = _tri(False), _tri(True)
    y_sb, run_tab, g_out, g_up, g_down = _attn_fwd(qkv, tri_suffix, list(shards16[1:]), [False] * 3, B, S)
    w_out16 = g_out.reshape(D_MODEL, D_MODEL)
    w_down16 = g_down.reshape(D_FF, D_MODEL)
    h1, mix16 = _fwd_mix(y_lru, y_sb, lru_out_g, sb_out_g, w_out16, x2, tm)
    up16, dh2, d_final_g, loss_part = _fwd_mlp(h1, norm2_g, g_up, w_down16, gf, tgt, min(1024, T), 2)

    dup16, dh1, hn16, dh2b, d_norm2_g = _bwd_mlp(dh2, up16, h1, norm2_g, g_up, w_down16, tm, 4)
    sq = lambda u: (u.astype(F32) * u.astype(F32)).astype(BF16)
    wide = 4
    gw_up, = _matmul_tn("grad_w_up", hn16, dup16, D_MODEL, wide * FF_CHUNK, tk // 2, (N_DEV, D_MODEL, FF_CHUNK),
                        (wide, D_MODEL, FF_CHUNK), lambda m, n, k: (n, 0, 0), split=(1, wide))
    gw_down, = _matmul_tn("grad_w_down", up16, dh2b, wide * FF_CHUNK, D_MODEL, tk // 2, (N_DEV, FF_CHUNK, D_MODEL),
                          (wide, FF_CHUNK, D_MODEL), lambda m, n, k: (m, 0, 0), a_prep=sq, split=(0, wide))
    dy_lru, dy_sb, d_lru_out_g, d_sb_out_g = _bwd_mix(dh1, w_out16, y_lru, y_sb, lru_out_g, sb_out_g, tm)
    gw_out, = _matmul_tn("grad_w_out", mix16, dh1, D_MODEL, FF_CHUNK, tk, (D_MODEL, D_MODEL),
                         (D_MODEL, FF_CHUNK), lambda m, n, k: (0, n), b_prep=lambda u: u.astype(BF16))
    parts_out = gw_out.reshape(N_DEV, D_MODEL // N_DEV, D_MODEL)
    dq, dk, dv, r_out, r_up, r_down = _attn_bwd(qkv, run_tab, dy_sb, tri_suffix, tri_prefix,
                                                [parts_out, gw_up, gw_down], [True] * 3, B, S)
    (dx_lru, dg_lru, d_conv_w, d_conv_b, d_wa, d_wx, d_b_a, d_b_x, d_lambda) = _lru_bwd(
        proj_lru, h, dy_lru, conv_w_full, conv_b, wa_bd, wx_bd, b_a, b_x, lru_lambda, B, S, lc)
    dx, dproj16, xn16, d_norm1_g = _bwd_in([dx_lru, dg_lru, dq, dk, dv], w_in16t, x2, norm1_g, dh1, tm)

    small = {"norm1_g": (d_norm1_g, norm1_g, m_norm1_g, v_norm1_g), "conv_b": (d_conv_b, conv_b, m_conv_b, v_conv_b),
             "lru_w_a": (_diag_blocks(d_wa), lru_w_a, m_lru_w_a, v_lru_w_a),
             "lru_b_a": (d_b_a, lru_b_a, m_lru_b_a, v_lru_b_a),
             "lru_w_x": (_diag_blocks(d_wx), lru_w_x, m_lru_w_x, v_lru_w_x),
             "lru_b_x": (d_b_x, lru_b_x, m_lru_b_x, v_lru_b_x),
             "lru_lambda": (d_lambda, lru_lambda, m_lru_lambda, v_lru_lambda),
             "lru_out_g": (d_lru_out_g, lru_out_g, m_lru_out_g, v_lru_out_g),
             "sb_out_g": (d_sb_out_g, sb_out_g, m_sb_out_g, v_sb_out_g),
             "norm2_g": (d_norm2_g, norm2_g, m_norm2_g, v_norm2_g),
             "final_g": (d_final_g, final_g, m_final_g, v_final_g)}
    names = list(small)

    def held(n, a):
        return a.reshape((1, D_MODEL) if n == "final_g" else small[n][1].shape)

    parts_cw = d_conv_w.reshape(CONV_K, N_DEV, cw_cols).transpose(1, 0, 2)[:, None]
    half, per = IN_COLS // 2, IN_COLS // N_DEV

    def grad_w_in_half(i, ride, ride_sliced):
        return _matmul_tn("grad_w_in_%d" % i, dproj16, xn16, half, D_MODEL, tk // 2, (half, D_MODEL),
                          (half, D_MODEL), lambda m, n, k: (0, 0), out_dtype=BF16, ride=ride, ride_sliced=ride_sliced,
                          a_blocks=(i, 1))

    lo_t, *got = grad_w_in_half(0, [parts_cw] + [held(n, small[n][0]) for n in names] + [loss_part],
                                [True] + [False] * (len(names) + 1))
    hi_t, r_lo = grad_w_in_half(1, [lo_t.reshape(N_DEV // 2, per, D_MODEL)], [(0, N_DEV // 2)])

    big, (r_hi,) = _adam_shards(
        "adam_out_up_down", [(r_out, w_out[0], m_w_out[0], v_w_out[0]), (r_up, w_up[0], m_w_up[0], v_w_up[0]),
                             (r_down, w_down[0], m_w_down[0], v_w_down[0])],
        4, [hi_t.reshape(N_DEV // 2, per, D_MODEL)], [(N_DEV // 2, N_DEV)])
    r_in = jnp.where(_my_index() < N_DEV // 2, r_lo, r_hi)
    g_in_t = _sum_parts("sum_w_in", r_in)
    out = dict(zip(("w_out", "w_up", "w_down"), big))
    out["w_in"] = _adam_given("adam_w_in", g_in_t.T, w_in[0], m_w_in[0], v_w_in[0], 256)
    out = {n: [a[None] for a in res] for n, res in out.items()}
    items = [(got[0], conv_w, m_conv_w, v_conv_w)]
    items += [(parts, *[held(n, a) for a in small[n][1:]]) for n, parts in zip(names, got[1:-1])]
    for n, res in zip(["conv_w"] + names, _adam_params("adam_small", items)):
        out[n] = [a.reshape((conv_w if n == "conv_w" else small[n][1]).shape) for a in res]
    loss = _sum_parts("sum_loss", got[-1])[0, 0]
    weights = ["norm1_g", "w_in", "conv_w", "conv_b", "lru_w_a", "lru_b_a", "lru_w_x", "lru_b_x", "lru_lambda",
               "lru_out_g", "sb_out_g", "w_out", "norm2_g", "w_up", "w_down", "final_g"]
    return (loss, dx.reshape(B, S, D_MODEL), *[out[n][0] for n in weights], *[out[n][1] for n in weights],
            *[out[n][2] for n in weights], *[out[n][3] for n in weights])
```

```python
import jax
import jax.numpy as jnp
from jax import lax
from jax.experimental import pallas as pl
from jax.experimental.pallas import tpu as pltpu

F32 = jnp.float32
BF16 = jnp.bfloat16

D_MODEL = 1024
LRU_W = 512
SB_W = 512
HEAD_D = 64
D_FF = 4096
IN_COLS = 2 * LRU_W + 3 * SB_W
CONV_K = 4
LRU_BLOCKS = 8
LRU_C = 8.0
EPS = 1e-6
N_DEV = 8
LANES = 128
SUBLANES = 8
FF_CHUNK = 512
GRAD_TK = 2048
Q_BLK = 256
K_BLK = 256
Q_PER_K = K_BLK // Q_BLK

ADAM_LR = 0.001
ADAM_B1 = 0.9
ADAM_B2 = 0.999
ADAM_EPS = 1e-08
ADAM_WD = 0.01
ADAM_STEP = 10


def _params(sem=None, vmem_mb=None):
    kw = {}
    if sem is not None:
        kw["dimension_semantics"] = sem
    if vmem_mb is not None:
        kw["vmem_limit_bytes"] = vmem_mb << 20
    return pltpu.CompilerParams(**kw)


def _dot(a, b):
    return jnp.dot(a, b, preferred_element_type=F32)


def _dot_nt(a, b):
    return lax.dot_general(a, b, (((1,), (1,)), ((), ())), preferred_element_type=F32)


def _dot_tn(a, b):
    return lax.dot_general(a, b, (((0,), (0,)), ((), ())), preferred_element_type=F32)


def _rms_fwd(x, g):
    rstd = lax.rsqrt(jnp.mean(x * x, axis=-1, keepdims=True) + EPS)
    xhat = x * rstd
    return xhat * g, xhat, rstd


def _rms_bwd(dy, xhat, rstd, g):
    dxhat = dy * g
    return rstd * (dxhat - xhat * jnp.mean(dxhat * xhat, axis=-1, keepdims=True))


def _sigmoid(x):
    return 1.0 / (1.0 + jnp.exp(-x))


def _log1p_pos(e):
    series = e * (1.0 - e * (0.5 - e * (1.0 / 3.0 - e * 0.25)))
    return jnp.where(e < 1e-2, series, jnp.log(1.0 + e))


def _neg_expm1(x):
    series = -x * (1.0 + x * (0.5 + x * (1.0 / 6.0 + x * (1.0 / 24.0))))
    return jnp.where(x > -1e-2, series, 1.0 - jnp.exp(x))


def _gelu_parts(g):
    k0 = 0.7978845608028654
    k1 = 0.044715
    t = jnp.tanh(k0 * (g + k1 * g * g * g))
    val = 0.5 * g * (1.0 + t)
    grad = 0.5 * (1.0 + t) + 0.5 * g * (1.0 - t * t) * k0 * (1.0 + 3.0 * k1 * g * g)
    return val, grad


def _scan(a, b, reverse):
    n = a.shape[0]
    row = lax.broadcasted_iota(jnp.int32, a.shape, 0)
    s = 1
    while s < n:
        if reverse:
            keep = row < n - s
            shift = n - s
        else:
            keep = row >= s
            shift = s
        bs = jnp.where(keep, pltpu.roll(b, shift, 0), 0.0)
        a_s = jnp.where(keep, pltpu.roll(a, shift, 0), 1.0)
        b = a * bs + b
        a = a * a_s
        s *= 2
    return b, a


def _adamw(w, g, m, v):
    m = ADAM_B1 * m + (1.0 - ADAM_B1) * g
    v = ADAM_B2 * v + (1.0 - ADAM_B2) * (g * g)
    m_hat = m / (1.0 - ADAM_B1 ** ADAM_STEP)
    v_hat = v / (1.0 - ADAM_B2 ** ADAM_STEP)
    delta = -ADAM_LR * (m_hat / (jnp.sqrt(v_hat) + ADAM_EPS) + ADAM_WD * w)
    return delta, m, v


def _my_index():
    return 4 * lax.axis_index("x") + 2 * lax.axis_index("y") + lax.axis_index("c")


def _peer(k):
    x, y, c = lax.axis_index("x"), lax.axis_index("y"), lax.axis_index("c")
    px = 1 - x if (k >> 2) & 1 else x
    py = 1 - y if (k >> 1) & 1 else y
    pc = 1 - c if k & 1 else c
    return (px, py, pc), 4 * px + 2 * py + pc


def _exchange_shapes(srcs, sliced):
    n = len(srcs)
    out_shape = [jax.ShapeDtypeStruct(s.shape if sl else (N_DEV,) + s.shape, s.dtype) for s, sl in zip(srcs, sliced)]
    specs = [pl.BlockSpec(memory_space=pl.ANY)] * n
    sems = [pltpu.SemaphoreType.DMA((n, N_DEV - 1)), pltpu.SemaphoreType.DMA((n, N_DEV - 1)),
            pltpu.SemaphoreType.DMA((n,))]
    return out_shape, specs, sems


def _exchange_copies(ins, outs, sliced, send_sems, recv_sems, local_sems):
    n = len(ins)

    def part(a, p):
        return ins[a].at[p] if sliced[a] else ins[a]

    def copies(receiving):
        me = _my_index()
        local = [pltpu.make_async_copy(part(a, me), outs[a].at[me], local_sems.at[a]) for a in range(n)]
        remote = []
        for k in range(1, N_DEV):
            dev, idx = _peer(k)
            for a in range(n):
                remote.append(pltpu.make_async_remote_copy(
                    src_ref=part(a, idx), dst_ref=outs[a].at[idx if receiving else me],
                    send_sem=send_sems.at[a, k - 1], recv_sem=recv_sems.at[a, k - 1],
                    device_id=dev, device_id_type=pl.DeviceIdType.MESH))
        return local, remote

    def start():
        local, remote = copies(receiving=False)
        for cp in local + remote:
            cp.start()

    def wait():
        local, remote = copies(receiving=True)
        for cp in remote + local:
            cp.wait()

    return start, wait


def _gather_two_level(name, srcs):
    n = len(srcs)
    out_shape = [jax.ShapeDtypeStruct((N_DEV,) + s.shape, s.dtype) for s in srcs]

    def body(*refs):
        ins, outs = refs[:n], refs[n:2 * n]
        send_sems, recv_sems, local_sems = refs[2 * n:]
        x, y, c = lax.axis_index("x"), lax.axis_index("y"), lax.axis_index("c")
        here, sibling = (x, y, c), (x, y, 1 - c)
        chips = [(1 - x, y), (x, 1 - y), (1 - x, 1 - y)]

        def slot(px, py, pc):
            return 4 * px + 2 * py + pc

        def copy(a, k, block, to, src=None):
            return pltpu.make_async_remote_copy(
                src_ref=ins[a] if src is None else src, dst_ref=outs[a].at[slot(*block)],
                send_sem=send_sems.at[a, k], recv_sem=recv_sems.at[a, k], device_id=to,
                device_id_type=pl.DeviceIdType.MESH)

        local = [pltpu.make_async_copy(ins[a], outs[a].at[slot(*here)], local_sems.at[a]) for a in range(n)]
        first = [copy(a, 0, here, sibling) for a in range(n)]
        first += [copy(a, 1 + j, here, (*chip, c)) for j, chip in enumerate(chips) for a in range(n)]
        for cp in local + first:
            cp.start()
        passed = []
        for j, chip in enumerate(chips):
            for a in range(n):
                copy(a, 1 + j, (*chip, c), here).wait_recv()
                passed.append(copy(a, 4 + j, (*chip, c), sibling, src=outs[a].at[slot(*chip, c)]))
                passed[-1].start()
        for a in range(n):
            copy(a, 0, sibling, here).wait_recv()
        for j, chip in enumerate(chips):
            for a in range(n):
                copy(a, 4 + j, (*chip, 1 - c), here).wait_recv()
        for cp in first + passed:
            cp.wait_send()
        for cp in local:
            cp.wait()

    spec = [pl.BlockSpec(memory_space=pl.ANY)] * n
    return pl.pallas_call(
        body, name=name, out_shape=out_shape, in_specs=spec, out_specs=spec,
        scratch_shapes=[pltpu.SemaphoreType.DMA((n, N_DEV - 1)), pltpu.SemaphoreType.DMA((n, N_DEV - 1)),
                        pltpu.SemaphoreType.DMA((n,))],
    )(*srcs)


def _whole(shape):
    return pl.BlockSpec(shape, lambda *_: (0,) * len(shape))


def _cast_shards(ws, steps=4):
    def body(*refs):
        for i in range(len(ws)):
            refs[len(ws) + i][...] = refs[i][...].astype(BF16)

    specs = [pl.BlockSpec((w.shape[0] // steps, w.shape[1]), lambda i: (i, 0)) for w in ws]
    return pl.pallas_call(
        body, name="cast_shards", grid=(steps,), in_specs=specs, out_specs=specs,
        out_shape=[jax.ShapeDtypeStruct(w.shape, BF16) for w in ws],
        compiler_params=_params(("arbitrary",), 32),
    )(*ws)


def _fwd_in(x2, g1, w_in16t, tm):
    T = x2.shape[0]

    def body(x_ref, g_ref, w_ref, lru_ref, qkv_ref):
        xn, _, _ = _rms_fwd(x_ref[...], g_ref[...])
        xn = xn.astype(BF16)
        lru_ref[...] = _dot_nt(xn, w_ref[0:2 * LRU_W, :])
        qkv_ref[...] = _dot_nt(xn, w_ref[2 * LRU_W:IN_COLS, :]).astype(BF16)

    return pl.pallas_call(
        body, name="fwd_in", grid=(T // tm,),
        in_specs=[pl.BlockSpec((tm, D_MODEL), lambda i: (i, 0)),
                  pl.BlockSpec((1, D_MODEL), lambda i: (0, 0)),
                  pl.BlockSpec((IN_COLS, D_MODEL), lambda i: (0, 0))],
        out_specs=[pl.BlockSpec((tm, 2 * LRU_W), lambda i: (i, 0)),
                   pl.BlockSpec((tm, 3 * SB_W), lambda i: (i, 0))],
        out_shape=[jax.ShapeDtypeStruct((T, 2 * LRU_W), F32), jax.ShapeDtypeStruct((T, 3 * SB_W), BF16)],
        compiler_params=_params(("arbitrary",), 48),
    )(x2, g1, w_in16t)


def _lru_gates(c, wa_ref, wx_ref, ba_ref, bx_ref, lam_ref):
    c16 = c.astype(BF16)
    r = _sigmoid(_dot(c16, wa_ref[0]) + ba_ref[...])
    i = _sigmoid(_dot(c16, wx_ref[0]) + bx_ref[...])
    lam = lam_ref[...]
    e = jnp.exp(-jnp.abs(lam))
    sp = jnp.maximum(-lam, 0.0) + _log1p_pos(e)
    dsp_dlam = -jnp.where(lam >= 0.0, e, 1.0) / (1.0 + e)
    log_a = (-LRU_C) * r * sp
    a = jnp.exp(log_a)
    s = jnp.sqrt(_neg_expm1(2.0 * log_a))
    return r, i, sp, dsp_dlam, a, s


def _conv_taps(x, halo, lc):
    xe = jnp.concatenate([halo, x], axis=0)
    return [x] + [pltpu.roll(xe, k, 0)[SUBLANES:SUBLANES + lc] for k in range(1, CONV_K)]


def _lru_fwd(proj_lru, conv_w, conv_b, wa_bd, wx_bd, b_a, b_x, lam, B, S, lc):
    T = B * S
    nc = S // lc
    ncb = LRU_W // LANES

    def body(x_ref, g_ref, cw_ref, cb_ref, wa_ref, wx_ref, ba_ref, bx_ref, lam_ref, y_ref, h_ref, tail, carry):
        ci = pl.program_id(2)

        @pl.when(ci == 0)
        def _():
            tail[...] = jnp.zeros_like(tail)
            carry[...] = jnp.zeros_like(carry)

        x = x_ref[...]
        taps = _conv_taps(x, tail[...], lc)
        c = cb_ref[...] + sum(cw_ref[pl.ds(CONV_K - 1 - k, 1), :] * taps[k] for k in range(CONV_K))
        tail[...] = x_ref[pl.ds(lc - SUBLANES, SUBLANES), :]
        r, i, sp, _, a, s = _lru_gates(c, wa_ref, wx_ref, ba_ref, bx_ref, lam_ref)
        h_loc, a_run = _scan(a, s * (i * c), reverse=False)
        h_ref[...] = h_loc + a_run * carry[...]
        carry[...] = h_ref[pl.ds(lc - 1, 1), :]
        gelu, _ = _gelu_parts(g_ref[...])
        y_ref[...] = h_ref[...] * gelu

    chan = lambda b, cb, ci: (0, cb)
    return pl.pallas_call(
        body, name="lru_fwd", grid=(B, ncb, nc),
        in_specs=[pl.BlockSpec((lc, LANES), lambda b, cb, ci: (b * nc + ci, cb)),
                  pl.BlockSpec((lc, LANES), lambda b, cb, ci: (b * nc + ci, ncb + cb)),
                  pl.BlockSpec((CONV_K, LANES), chan), pl.BlockSpec((1, LANES), chan),
                  pl.BlockSpec((1, LANES, LANES), lambda b, cb, ci: (cb, 0, 0)),
                  pl.BlockSpec((1, LANES, LANES), lambda b, cb, ci: (cb, 0, 0)),
                  pl.BlockSpec((1, LANES), chan), pl.BlockSpec((1, LANES), chan), pl.BlockSpec((1, LANES), chan)],
        out_specs=[pl.BlockSpec((lc, LANES), lambda b, cb, ci: (b * nc + ci, cb))] * 2,
        out_shape=[jax.ShapeDtypeStruct((T, LRU_W), F32)] * 2,
        scratch_shapes=[pltpu.VMEM((SUBLANES, LANES), F32), pltpu.VMEM((1, LANES), F32)],
        compiler_params=_params(("arbitrary", "arbitrary", "arbitrary"), 32),
    )(proj_lru, proj_lru, conv_w, conv_b, wa_bd, wx_bd, b_a, b_x, lam)


def _tri(prefix):
    r = lax.broadcasted_iota(jnp.int32, (K_BLK, K_BLK), 0)
    c = lax.broadcasted_iota(jnp.int32, (K_BLK, K_BLK), 1)
    return ((r <= c) if prefix else (r >= c)).astype(BF16)


def _attn_consts(qi):
    r = lax.broadcasted_iota(jnp.int32, (Q_BLK, K_BLK), 0)
    c = lax.broadcasted_iota(jnp.int32, (Q_BLK, K_BLK), 1)
    causal = c + ((qi // Q_PER_K) * K_BLK - qi * Q_BLK) < r
    lane = lax.broadcasted_iota(jnp.int32, (1, LANES), 1)
    return causal, lane, (lane < HEAD_D, lane >= HEAD_D)


def _log1m_beta(z, mask):
    lg = -(jnp.maximum(z, 0.0) + jnp.log(1.0 + jnp.exp(-jnp.abs(z))))
    return lg if mask is None else jnp.where(mask, lg, 0.0)


def _key_rows(j):
    return pl.ds(pl.multiple_of(j * K_BLK, K_BLK), K_BLK)


def _attn_fwd(qkv, tri_suffix, ride, ride_sliced, B, S):
    T = B * S
    nq = S // Q_BLK
    nhp = SB_W // LANES
    scale = HEAD_D ** -0.5
    assert S // K_BLK <= LANES
    nr = len(ride)
    ride_shape, ride_specs, ride_sems = _exchange_shapes(ride, ride_sliced)

    def body(q_ref, k_ref, v_ref, tri_ref, *rest):
        o_ref, run_ref = rest[nr:nr + 2]
        start_ride, wait_ride = _exchange_copies(rest[:nr], rest[nr + 2:2 * nr + 2], ride_sliced, *rest[2 * nr + 2:])
        qi = pl.program_id(2)
        step_no = (pl.program_id(0) * nhp + pl.program_id(1)) * nq + qi
        pl.when(step_no == 0)(start_ride)
        jd = qi // Q_PER_K
        causal, lane, halves = _attn_consts(qi)
        q = q_ref[...]
        qh = [jnp.where(hm, q, jnp.zeros_like(q)) * jnp.asarray(scale, BF16) for hm in halves]

        def group(blocks, carry):
            runs, tables, acc = carry
            runs, tables = list(runs), list(tables)
            ks = [k_ref[_key_rows(jl), :] for jl, _, _ in blocks]
            vs = [v_ref[_key_rows(jl), :] for jl, _, _ in blocks]
            chains = [(b, h) for b in range(len(blocks)) for h in range(2)]
            z = {c: _dot_nt(qh[c[1]], ks[c[0]]) for c in chains}
            lg = {c: _log1m_beta(z[c], blocks[c[0]][2]) for c in chains}
            suf = {c: _dot(lg[c].astype(BF16), tri_ref[...]) for c in chains}
            att = {}
            for b, h in chains:
                _, jlane, mask = blocks[b]
                a = jnp.exp(z[b, h] + suf[b, h] + runs[h])
                att[b, h] = (a if mask is None else jnp.where(mask, a, 0.0)).astype(BF16)
                tables[h] = jnp.where(lane == jlane, runs[h], tables[h])
                runs[h] = runs[h] + suf[b, h][:, 0:1]
            for b, h in chains:
                acc = acc + _dot(att[b, h], jnp.where(halves[h], vs[b], jnp.zeros_like(vs[b])))
            return tuple(runs), tuple(tables), acc

        col0 = jnp.zeros((Q_BLK, 1), F32)
        zero = jnp.zeros((Q_BLK, LANES), F32)
        start = ((col0, col0), (zero, zero), zero)
        two = jd % 2
        carry = lax.cond(two == 1, lambda: group([(jd, jd, causal), (jd - 1, jd - 1, None)], start),
                         lambda: group([(jd, jd, causal)], start))
        top = jd - 1 - two

        def run(ja, n):
            return [(ja - i, ja - i, None) for i in range(n)]

        odd_pair = ((top + 1) // 2) % 2
        carry = lax.cond(odd_pair == 1, lambda cr: group(run(top, 2), cr), lambda cr: cr, carry)
        top4 = top - 2 * odd_pair
        _, tables, acc = lax.fori_loop(0, (top4 + 1) // 4, lambda it, cr: group(run(top4 - 4 * it, 4), cr), carry)
        o_ref[...] = acc
        run_ref[:, 0:LANES] = tables[0]
        run_ref[:, LANES:2 * LANES] = tables[1]
        pl.when(step_no == B * nhp * nq - 1)(wait_ride)

    return pl.pallas_call(
        body, name="attn_fwd", grid=(B, nhp, nq),
        in_specs=[pl.BlockSpec((Q_BLK, LANES), lambda b, hp, qi: (b * nq + qi, hp)),
                  pl.BlockSpec((S, LANES), lambda b, hp, qi: (b, nhp + hp)),
                  pl.BlockSpec((S, LANES), lambda b, hp, qi: (b, 2 * nhp + hp)),
                  pl.BlockSpec((K_BLK, K_BLK), lambda b, hp, qi: (0, 0))] + ride_specs,
        out_specs=[pl.BlockSpec((Q_BLK, LANES), lambda b, hp, qi: (b * nq + qi, hp)),
                   pl.BlockSpec((Q_BLK, 2 * LANES), lambda b, hp, qi: (b * nq + qi, hp))] + ride_specs,
        out_shape=[jax.ShapeDtypeStruct((T, SB_W), F32), jax.ShapeDtypeStruct((T, 2 * SB_W), F32)] + ride_shape,
        scratch_shapes=ride_sems,
        compiler_params=_params(("arbitrary", "arbitrary", "arbitrary"), 48),
    )(qkv, qkv, qkv, tri_suffix, *ride)


def _fwd_mix(y_lru, y_sb, ga, gb, w_out16, x2, tm):
    T = x2.shape[0]

    def body(yl_ref, ys_ref, ga_ref, gb_ref, w_ref, x_ref, h1_ref, mix_ref):
        na, _, _ = _rms_fwd(yl_ref[...], ga_ref[...])
        nb, _, _ = _rms_fwd(ys_ref[...], gb_ref[...])
        na = na.astype(BF16)
        nb = nb.astype(BF16)
        mix_ref[:, 0:LRU_W] = na
        mix_ref[:, LRU_W:D_MODEL] = nb
        h1_ref[...] = x_ref[...] + _dot(na, w_ref[0:LRU_W, :]) + _dot(nb, w_ref[LRU_W:D_MODEL, :])

    row = lambda i: (i, 0)
    fix = lambda i: (0, 0)
    return pl.pallas_call(
        body, name="fwd_mix", grid=(T // tm,),
        in_specs=[pl.BlockSpec((tm, LRU_W), row), pl.BlockSpec((tm, SB_W), row),
                  pl.BlockSpec((1, LRU_W), fix), pl.BlockSpec((1, SB_W), fix),
                  pl.BlockSpec((D_MODEL, D_MODEL), fix), pl.BlockSpec((tm, D_MODEL), row)],
        out_specs=[pl.BlockSpec((tm, D_MODEL), row), pl.BlockSpec((tm, D_MODEL), row)],
        out_shape=[jax.ShapeDtypeStruct((T, D_MODEL), F32), jax.ShapeDtypeStruct((T, D_MODEL), BF16)],
        compiler_params=_params(("arbitrary",), 48),
    )(y_lru, y_sb, ga, gb, w_out16, x2)


def _fwd_mlp(h1, g2, w_up16, w_down16, gf, tgt, tm, per_step):
    T = h1.shape[0]
    nf = D_FF // (FF_CHUNK * per_step)

    def body(h1_ref, g2_ref, wu_ref, wd_ref, gf_ref, t_ref, up_ref, dh2_ref, dgf_ref, loss_ref, hn_s, acc):
        i, j = pl.program_id(0), pl.program_id(1)

        @pl.when(j == 0)
        def _():
            h1v = h1_ref[...]
            hn, _, _ = _rms_fwd(h1v, g2_ref[...])
            hn_s[...] = hn.astype(BF16)
            acc[...] = h1v

        down = None
        for c in range(per_step):
            cols = slice(c * FF_CHUNK, (c + 1) * FF_CHUNK)
            up = jnp.maximum(_dot(hn_s[...], wu_ref[c]), 0.0)
            up_ref[:, cols] = up.astype(BF16)
            part = _dot((up * up).astype(BF16), wd_ref[cols, :])
            down = part if down is None else down + part
        acc[...] += down

        @pl.when((i == 0) & (j == 0))
        def _():
            dgf_ref[...] = jnp.zeros_like(dgf_ref)
            loss_ref[...] = jnp.zeros_like(loss_ref)

        @pl.when(j == nf - 1)
        def _():
            gfv = gf_ref[...]
            y, xhat, rstd = _rms_fwd(acc[...], gfv)
            err = y - t_ref[...]
            loss_ref[...] += jnp.sum(0.5 * jnp.sum(err * err, axis=-1, keepdims=True) * (1.0 / D_MODEL))
            dy = err * (1.0 / D_MODEL)
            dgf_ref[...] += jnp.sum(dy * xhat, axis=0, keepdims=True)
            dh2_ref[...] = _rms_bwd(dy, xhat, rstd, gfv)

    row = lambda i, j: (i, 0)
    fix = lambda i, j: (0, 0)
    return pl.pallas_call(
        body, name="fwd_mlp", grid=(T // tm, nf),
        in_specs=[pl.BlockSpec((tm, D_MODEL), row), pl.BlockSpec((1, D_MODEL), fix),
                  pl.BlockSpec((per_step, D_MODEL, FF_CHUNK), lambda i, j: (j, 0, 0)),
                  pl.BlockSpec((per_step * FF_CHUNK, D_MODEL), lambda i, j: (j, 0)),
                  pl.BlockSpec((1, D_MODEL), fix), pl.BlockSpec((tm, D_MODEL), row)],
        out_specs=[pl.BlockSpec((tm, per_step * FF_CHUNK), lambda i, j: (i, j)), pl.BlockSpec((tm, D_MODEL), row),
                   pl.BlockSpec((1, D_MODEL), fix), pl.BlockSpec((1, LANES), fix)],
        out_shape=[jax.ShapeDtypeStruct((T, D_FF), BF16), jax.ShapeDtypeStruct((T, D_MODEL), F32),
                   jax.ShapeDtypeStruct((1, D_MODEL), F32), jax.ShapeDtypeStruct((1, LANES), F32)],
        scratch_shapes=[pltpu.VMEM((tm, D_MODEL), BF16), pltpu.VMEM((tm, D_MODEL), F32)],
        compiler_params=_params(("arbitrary", "arbitrary"), 56),
    )(h1, g2, w_up16, w_down16, gf, tgt)


def _bwd_mlp(dh2, up16, h1, g2, w_up16, w_down16, tm, per_step):
    T = h1.shape[0]
    nf = D_FF // (FF_CHUNK * per_step)

    def body(dh2_ref, up_ref, h1_ref, g2_ref, wu_ref, wd_ref, dup_ref, dh1_ref, hn_ref, dh2b_ref, dg2_ref, acc):
        i, j = pl.program_id(0), pl.program_id(1)

        @pl.when(j == 0)
        def _():
            hn, _, _ = _rms_fwd(h1_ref[...], g2_ref[...])
            hn_ref[...] = hn.astype(BF16)
            dh2b_ref[...] = dh2_ref[...].astype(BF16)
            acc[...] = jnp.zeros_like(acc)

        dhn = None
        for c in range(per_step):
            cols = slice(c * FF_CHUNK, (c + 1) * FF_CHUNK)
            u = up_ref[:, cols].astype(F32)
            dup = (2.0 * u * _dot_nt(dh2b_ref[...], wd_ref[cols, :])).astype(BF16)
            dup_ref[:, cols] = dup
            part = _dot_nt(dup, wu_ref[c])
            dhn = part if dhn is None else dhn + part
        acc[...] += dhn

        @pl.when((i == 0) & (j == 0))
        def _():
            dg2_ref[...] = jnp.zeros_like(dg2_ref)

        @pl.when(j == nf - 1)
        def _():
            g2v = g2_ref[...]
            _, xhat, rstd = _rms_fwd(h1_ref[...], g2v)
            dhn = acc[...]
            dg2_ref[...] += jnp.sum(dhn * xhat, axis=0, keepdims=True)
            dh1_ref[...] = dh2_ref[...] + _rms_bwd(dhn, xhat, rstd, g2v)

    row = lambda i, j: (i, 0)
    fix = lambda i, j: (0, 0)
    return pl.pallas_call(
        body, name="bwd_mlp", grid=(T // tm, nf),
        in_specs=[pl.BlockSpec((tm, D_MODEL), row), pl.BlockSpec((tm, per_step * FF_CHUNK), lambda i, j: (i, j)),
                  pl.BlockSpec((tm, D_MODEL), row), pl.BlockSpec((1, D_MODEL), fix),
                  pl.BlockSpec((per_step, D_MODEL, FF_CHUNK), lambda i, j: (j, 0, 0)),
                  pl.BlockSpec((per_step * FF_CHUNK, D_MODEL), lambda i, j: (j, 0))],
        out_specs=[pl.BlockSpec((tm, per_step * FF_CHUNK), lambda i, j: (i, j)), pl.BlockSpec((tm, D_MODEL), row),
                   pl.BlockSpec((tm, D_MODEL), row), pl.BlockSpec((tm, D_MODEL), row),
                   pl.BlockSpec((1, D_MODEL), fix)],
        out_shape=[jax.ShapeDtypeStruct((T, D_FF), BF16), jax.ShapeDtypeStruct((T, D_MODEL), F32),
                   jax.ShapeDtypeStruct((T, D_MODEL), BF16), jax.ShapeDtypeStruct((T, D_MODEL), BF16),
                   jax.ShapeDtypeStruct((1, D_MODEL), F32)],
        scratch_shapes=[pltpu.VMEM((tm, D_MODEL), F32)],
        compiler_params=_params(("arbitrary", "arbitrary"), 48),
    )(dh2, up16, h1, g2, w_up16, w_down16)


def _matmul_tn(name, a, b, bm, bn, tk, out_shape, out_block, out_index, a_prep=None, b_prep=None, out_dtype=F32,
               ride=None, ride_sliced=None, split=None):
    T, M = a.shape
    N = b.shape[1]
    grid = (M // bm, N // bn, T // tk)
    ride, ride_sliced = list(ride or []), list(ride_sliced or [])
    nr = len(ride)
    ride_shape, ride_specs, ride_sems = _exchange_shapes(ride, ride_sliced)

    def body(a_ref, b_ref, *rest):
        o_ref, acc = rest[nr], rest[2 * nr + 1]
        k = pl.program_id(2)
        step_no = (pl.program_id(0) * grid[1] + pl.program_id(1)) * grid[2] + k
        if nr:
            start_ride, wait_ride = _exchange_copies(rest[:nr], rest[nr + 1:2 * nr + 1], ride_sliced,
                                                     *rest[2 * nr + 2:])
            pl.when(step_no == 0)(start_ride)
        av = a_ref[...] if a_prep is None else a_prep(a_ref[...])
        bv = b_ref[...] if b_prep is None else b_prep(b_ref[...])
        p = _dot_tn(av, bv)

        @pl.when(k == 0)
        def _():
            acc[...] = p

        @pl.when(k > 0)
        def _():
            acc[...] += p

        @pl.when(k == grid[2] - 1)
        def _():
            if split is None:
                o_ref[...] = acc[...].astype(out_dtype)
            else:
                axis, n = split
                w = (bm, bn)[axis] // n
                for c in range(n):
                    slab = acc[c * w:(c + 1) * w, :] if axis == 0 else acc[:, c * w:(c + 1) * w]
                    o_ref[c] = slab.astype(out_dtype)

        if nr:
            pl.when(step_no == grid[0] * grid[1] * grid[2] - 1)(wait_ride)

    return pl.pallas_call(
        body, name=name, grid=grid,
        in_specs=[pl.BlockSpec((tk, bm), lambda m, n, k: (k, m)), pl.BlockSpec((tk, bn), lambda m, n, k: (k, n))]
        + ride_specs,
        out_specs=[pl.BlockSpec(out_block, out_index)] + ride_specs,
        out_shape=[jax.ShapeDtypeStruct(out_shape, out_dtype)] + ride_shape,
        scratch_shapes=[pltpu.VMEM((bm, bn), F32)] + (ride_sems if nr else []),
        compiler_params=_params(("arbitrary", "arbitrary", "arbitrary"), 48),
    )(a, b, *ride)


def _bwd_mix(dh1, w_out16, y_lru, y_sb, ga, gb, tm):
    T = dh1.shape[0]

    def body(d_ref, w_ref, yl_ref, ys_ref, ga_ref, gb_ref, dyl_ref, dys_ref, dga_ref, dgb_ref):
        @pl.when(pl.program_id(0) == 0)
        def _():
            dga_ref[...] = jnp.zeros_like(dga_ref)
            dgb_ref[...] = jnp.zeros_like(dgb_ref)

        d16 = d_ref[...].astype(BF16)
        for y_ref, g_ref, lo, dy_ref, dg_ref in ((yl_ref, ga_ref, 0, dyl_ref, dga_ref),
                                                 (ys_ref, gb_ref, LRU_W, dys_ref, dgb_ref)):
            gv = g_ref[...]
            dn = _dot_nt(d16, w_ref[lo:lo + LRU_W, :])
            _, xhat, rstd = _rms_fwd(y_ref[...], gv)
            dg_ref[...] += jnp.sum(dn * xhat, axis=0, keepdims=True)
            dy_ref[...] = _rms_bwd(dn, xhat, rstd, gv).astype(dy_ref.dtype)

    row = lambda i: (i, 0)
    fix = lambda i: (0, 0)
    return pl.pallas_call(
        body, name="bwd_mix", grid=(T // tm,),
        in_specs=[pl.BlockSpec((tm, D_MODEL), row), pl.BlockSpec((D_MODEL, D_MODEL), fix),
                  pl.BlockSpec((tm, LRU_W), row), pl.BlockSpec((tm, SB_W), row),
                  pl.BlockSpec((1, LRU_W), fix), pl.BlockSpec((1, SB_W), fix)],
        out_specs=[pl.BlockSpec((tm, LRU_W), row), pl.BlockSpec((tm, SB_W), row),
                   pl.BlockSpec((1, LRU_W), fix), pl.BlockSpec((1, SB_W), fix)],
        out_shape=[jax.ShapeDtypeStruct((T, LRU_W), F32), jax.ShapeDtypeStruct((T, SB_W), BF16),
                   jax.ShapeDtypeStruct((1, LRU_W), F32), jax.ShapeDtypeStruct((1, SB_W), F32)],
        compiler_params=_params(("arbitrary",), 48),
    )(dh1, w_out16, y_lru, y_sb, ga, gb)


def _attn_bwd(qkv, run_tab, dy_sb, tri_suffix, tri_prefix, ride, ride_sliced, B, S):
    T = B * S
    nq = S // Q_BLK
    nkb = S // K_BLK
    nhp = SB_W // LANES
    scale = HEAD_D ** -0.5

    nr = len(ride)
    ride_shape, ride_specs, ride_sems = _exchange_shapes(ride, ride_sliced)

    def body(q_ref, k_ref, v_ref, run_ref, do_ref, ts_ref, tp_ref, *rest):
        dq_ref, dk_ref, dv_ref = rest[nr:nr + 3]
        dkt_ref, dvt_ref = rest[2 * nr + 3:2 * nr + 5]
        start_ride, wait_ride = _exchange_copies(rest[:nr], rest[nr + 3:2 * nr + 3], ride_sliced, *rest[2 * nr + 5:])
        qi = pl.program_id(2)
        step_no = (pl.program_id(0) * nhp + pl.program_id(1)) * nq + qi
        pl.when(step_no == 0)(start_ride)
        jd = qi // Q_PER_K

        @pl.when(qi == 0)
        def _():
            dkt_ref[...] = jnp.zeros_like(dkt_ref)
            dvt_ref[...] = jnp.zeros_like(dvt_ref)

        causal, lane, halves = _attn_consts(qi)
        q = q_ref[...]
        do = do_ref[...]
        qh = [jnp.where(hm, q, jnp.zeros_like(q)) * jnp.asarray(scale, BF16) for hm in halves]
        doh = [jnp.where(hm, do, jnp.zeros_like(do)) for hm in halves]
        qt = jnp.concatenate([h.astype(F32).T.astype(BF16) for h in qh], axis=1)
        dot_ = jnp.concatenate([h.astype(F32).T.astype(BF16) for h in doh], axis=1)
        tables = [run_ref[:, 0:LANES], run_ref[:, LANES:2 * LANES]]

        def group(blocks, carry):
            prefixes, dq = carry
            prefixes = list(prefixes)
            ks = [k_ref[_key_rows(jl), :] for jl, _, _ in blocks]
            vs = [v_ref[_key_rows(jl), :] for jl, _, _ in blocks]
            chains = [(b, h) for b in range(len(blocks)) for h in range(2)]
            z = {c: _dot_nt(qh[c[1]], ks[c[0]]) for c in chains}
            da = {c: _dot_nt(doh[c[1]], vs[c[0]]) for c in chains}
            lg = {c: _log1m_beta(z[c], blocks[c[0]][2]) for c in chains}
            suf = {c: _dot(lg[c].astype(BF16), ts_ref[...]) for c in chains}
            att, g = {}, {}
            for b, h in chains:
                _, jlane, mask = blocks[b]
                run = jnp.sum(jnp.where(lane == jlane, tables[h], 0.0), axis=1, keepdims=True)
                a = jnp.exp(z[b, h] + suf[b, h] + run)
                a = a if mask is None else jnp.where(mask, a, 0.0)
                g[b, h] = a * da[b, h]
                att[b, h] = a.astype(BF16)
            gpre = {c: _dot(g[c].astype(BF16), tp_ref[...]) for c in chains}
            dz = {}
            for b, h in chains:
                mask = blocks[b][2]
                d = g[b, h] - jnp.exp(z[b, h] + lg[b, h]) * (prefixes[h] + gpre[b, h])
                dz[b, h] = (d if mask is None else jnp.where(mask, d, 0.0)).astype(BF16)
                prefixes[h] = prefixes[h] + gpre[b, h][:, K_BLK - 1:K_BLK]
            for b, h in chains:
                dq = dq + _dot(dz[b, h], jnp.where(halves[h], ks[b], jnp.zeros_like(ks[b])))
            for b, (jl, _, _) in enumerate(blocks):
                dkt_ref[jl] += _dot(qt, jnp.concatenate([dz[b, 0], dz[b, 1]], axis=0))
                dvt_ref[jl] += _dot(dot_, jnp.concatenate([att[b, 0], att[b, 1]], axis=0))
            return tuple(prefixes), dq

        def run(ja, n):
            return [(ja + i, ja + i, None) for i in range(n)]

        col0 = jnp.zeros((Q_BLK, 1), F32)
        fours = (jd // 2) // 2
        carry = lax.fori_loop(0, fours, lambda it, cr: group(run(4 * it, 4), cr),
                              ((col0, col0), jnp.zeros((Q_BLK, LANES), F32)))
        carry = lax.cond((jd // 2) % 2 == 1, lambda cr: group(run(4 * fours, 2), cr), lambda cr: cr, carry)
        carry = lax.cond(jd % 2 == 1, lambda cr: group([(jd - 1, jd - 1, None), (jd, jd, causal)], cr),
                         lambda cr: group([(jd, jd, causal)], cr), carry)
        dq_ref[...] = (carry[1] * scale).astype(BF16)

        @pl.when(qi == nq - 1)
        def _():
            for j in range(nkb):
                dk_ref[j * K_BLK:(j + 1) * K_BLK, :] = dkt_ref[j].T.astype(BF16)
                dv_ref[j * K_BLK:(j + 1) * K_BLK, :] = dvt_ref[j].T.astype(BF16)

        pl.when(step_no == B * nhp * nq - 1)(wait_ride)

    qblk = pl.BlockSpec((Q_BLK, LANES), lambda b, hp, qi: (b * nq + qi, hp))
    tri = pl.BlockSpec((K_BLK, K_BLK), lambda b, hp, qi: (0, 0))
    seq = pl.BlockSpec((S, LANES), lambda b, hp, qi: (b, hp))
    return pl.pallas_call(
        body, name="attn_bwd", grid=(B, nhp, nq),
        in_specs=[qblk, pl.BlockSpec((S, LANES), lambda b, hp, qi: (b, nhp + hp)),
                  pl.BlockSpec((S, LANES), lambda b, hp, qi: (b, 2 * nhp + hp)),
                  pl.BlockSpec((Q_BLK, 2 * LANES), lambda b, hp, qi: (b * nq + qi, hp)), qblk, tri, tri] + ride_specs,
        out_specs=[qblk, seq, seq] + ride_specs,
        out_shape=[jax.ShapeDtypeStruct((T, SB_W), BF16)] * 3 + ride_shape,
        scratch_shapes=[pltpu.VMEM((nkb, LANES, K_BLK), F32)] * 2 + ride_sems,
        compiler_params=_params(("arbitrary", "arbitrary", "arbitrary"), 48),
    )(qkv, qkv, qkv, run_tab, dy_sb, tri_suffix, tri_prefix, *ride)


def _lru_bwd(proj_lru, h, dy_lru, conv_w, conv_b, wa_bd, wx_bd, b_a, b_x, lam, B, S, lc):
    T = B * S
    nc = S // lc
    ncb = LRU_W // LANES
    hpc = lc // SUBLANES

    def body(x_ref, xh_ref, g_ref, h_ref, hh_ref, dy_ref, cw_ref, cb_ref, wa_ref, wx_ref, ba_ref, bx_ref, lam_ref,
             dx_ref, dg_ref, dcw_ref, dcb_ref, dwa_ref, dwx_ref, dba_ref, dbx_ref, dlam_ref,
             lam_s, dc_s, a_first, lam_first, dc_head):
        b, ci = pl.program_id(1), pl.program_id(2)
        first_chunk = ci == nc - 1

        @pl.when(ci == 0)
        def _():
            a_first[...] = jnp.zeros_like(a_first)
            lam_first[...] = jnp.zeros_like(lam_first)
            dc_head[...] = jnp.zeros_like(dc_head)

        @pl.when((b == 0) & (ci == 0))
        def _():
            for ref in (dcw_ref, dcb_ref, dwa_ref, dwx_ref, dba_ref, dbx_ref, dlam_ref):
                ref[...] = jnp.zeros_like(ref)

        x = x_ref[...]
        taps = _conv_taps(x, jnp.where(first_chunk, 0.0, xh_ref[...]), lc)
        c = cb_ref[...] + sum(cw_ref[pl.ds(CONV_K - 1 - k, 1), :] * taps[k] for k in range(CONV_K))
        r, i, sp, dsp_dlam, a, s = _lru_gates(c, wa_ref, wx_ref, ba_ref, bx_ref, lam_ref)
        hv = h_ref[...]
        he = jnp.concatenate([jnp.where(first_chunk, 0.0, hh_ref[...]), hv], axis=0)
        h_prev = pltpu.roll(he, 1, 0)[SUBLANES:SUBLANES + lc]
        dy = dy_ref[...]
        gelu, dgelu = _gelu_parts(g_ref[...])
        dg_ref[...] = (dy * hv * dgelu).astype(BF16)

        row = lax.broadcasted_iota(jnp.int32, (lc, LANES), 0)
        a_next = jnp.where(row < lc - 1, pltpu.roll(a, lc - 1, 0), a_first[...])
        lam_loc, a_run = _scan(a_next, dy * gelu, reverse=True)
        lam_s[...] = lam_loc + a_run * lam_first[...]
        lam_t = lam_s[...]
        lam_first[...] = lam_s[pl.ds(0, 1), :]
        lam_s[...] = a
        a_first[...] = lam_s[pl.ds(0, 1), :]

        ic = i * c
        dlog_a = lam_t * h_prev * a - (lam_t * ic) * (a * a) / s
        dpre_r = (dlog_a * ((-LRU_C) * sp)) * r * (1.0 - r)
        dpre_i = (lam_t * s * c) * i * (1.0 - i)
        dlam_ref[...] += jnp.sum(dlog_a * r, axis=0, keepdims=True) * ((-LRU_C) * dsp_dlam)
        dr16 = dpre_r.astype(BF16)
        di16 = dpre_i.astype(BF16)
        c16 = c.astype(BF16)
        dwa_ref[0] += _dot_tn(c16, dr16)
        dwx_ref[0] += _dot_tn(c16, di16)
        dba_ref[...] += jnp.sum(dpre_r, axis=0, keepdims=True)
        dbx_ref[...] += jnp.sum(dpre_i, axis=0, keepdims=True)
        dc = lam_t * s * i + _dot_nt(dr16, wa_ref[0]) + _dot_nt(di16, wx_ref[0])
        dcb_ref[...] += jnp.sum(dc, axis=0, keepdims=True)
        for k in range(CONV_K):
            dcw_ref[pl.ds(CONV_K - 1 - k, 1), :] += jnp.sum(dc * taps[k], axis=0, keepdims=True)
        dce = jnp.concatenate([dc, dc_head[...]], axis=0)
        dx = cw_ref[pl.ds(CONV_K - 1, 1), :] * dc
        for k in range(1, CONV_K):
            dx = dx + cw_ref[pl.ds(CONV_K - 1 - k, 1), :] * pltpu.roll(dce, lc + SUBLANES - k, 0)[0:lc]
        dx_ref[...] = dx.astype(BF16)
        dc_s[...] = dc
        dc_head[...] = dc_s[pl.ds(0, SUBLANES), :]

    def chunk(col):
        return pl.BlockSpec((lc, LANES), lambda cb, b, ci: (b * nc + nc - 1 - ci, col(cb)))

    def halo(col):
        return pl.BlockSpec((SUBLANES, LANES),
                            lambda cb, b, ci: (jnp.maximum((b * nc + nc - 1 - ci) * hpc - 1, 0), col(cb)))

    chan = lambda cb, b, ci: (0, cb)
    blk = lambda cb, b, ci: (cb, 0, 0)
    vec = pl.BlockSpec((1, LANES), chan)
    mat = pl.BlockSpec((1, LANES, LANES), blk)
    return pl.pallas_call(
        body, name="lru_bwd", grid=(ncb, B, nc),
        in_specs=[chunk(lambda cb: cb), halo(lambda cb: cb), chunk(lambda cb: ncb + cb),
                  chunk(lambda cb: cb), halo(lambda cb: cb), chunk(lambda cb: cb),
                  pl.BlockSpec((CONV_K, LANES), chan), vec, mat, mat, vec, vec, vec],
        out_specs=[chunk(lambda cb: cb), chunk(lambda cb: cb), pl.BlockSpec((CONV_K, LANES), chan), vec,
                   mat, mat, vec, vec, vec],
        out_shape=[jax.ShapeDtypeStruct((T, LRU_W), BF16), jax.ShapeDtypeStruct((T, LRU_W), BF16),
                   jax.ShapeDtypeStruct((CONV_K, LRU_W), F32), jax.ShapeDtypeStruct((1, LRU_W), F32),
                   jax.ShapeDtypeStruct((ncb, LANES, LANES), F32), jax.ShapeDtypeStruct((ncb, LANES, LANES), F32),
                   jax.ShapeDtypeStruct((1, LRU_W), F32), jax.ShapeDtypeStruct((1, LRU_W), F32),
                   jax.ShapeDtypeStruct((1, LRU_W), F32)],
        scratch_shapes=[pltpu.VMEM((lc, LANES), F32), pltpu.VMEM((lc, LANES), F32), pltpu.VMEM((1, LANES), F32),
                        pltpu.VMEM((1, LANES), F32), pltpu.VMEM((SUBLANES, LANES), F32)],
        compiler_params=_params(("arbitrary", "arbitrary", "arbitrary"), 32),
    )(proj_lru, proj_lru, proj_lru, h, h, dy_lru, conv_w, conv_b, wa_bd, wx_bd, b_a, b_x, lam)


def _bwd_in(pieces, w_in16t, x2, g1, dh1, tm):
    T = x2.shape[0]
    npc = len(pieces)

    def body(*refs):
        p_refs = refs[:npc]
        w_ref, x_ref, g_ref, d_ref, dx_ref, dproj_ref, xn_ref, dg1_ref = refs[npc:]

        @pl.when(pl.program_id(0) == 0)
        def _():
            dg1_ref[...] = jnp.zeros_like(dg1_ref)

        dxn = jnp.zeros((tm, D_MODEL), F32)
        for n, p_ref in enumerate(p_refs):
            cols = slice(n * LRU_W, (n + 1) * LRU_W)
            p16 = p_ref[...]
            dproj_ref[:, cols] = p16
            dxn = dxn + _dot(p16, w_ref[cols, :])
        gv = g_ref[...]
        xn, xhat, rstd = _rms_fwd(x_ref[...], gv)
        xn_ref[...] = xn.astype(BF16)
        dg1_ref[...] += jnp.sum(dxn * xhat, axis=0, keepdims=True)
        dx_ref[...] = d_ref[...] + _rms_bwd(dxn, xhat, rstd, gv)

    row = lambda i: (i, 0)
    fix = lambda i: (0, 0)
    return pl.pallas_call(
        body, name="bwd_in", grid=(T // tm,),
        in_specs=[pl.BlockSpec((tm, LRU_W), row)] * npc + [
            pl.BlockSpec((IN_COLS, D_MODEL), fix), pl.BlockSpec((tm, D_MODEL), row),
            pl.BlockSpec((1, D_MODEL), fix), pl.BlockSpec((tm, D_MODEL), row)],
        out_specs=[pl.BlockSpec((tm, D_MODEL), row), pl.BlockSpec((tm, IN_COLS), row),
                   pl.BlockSpec((tm, D_MODEL), row), pl.BlockSpec((1, D_MODEL), fix)],
        out_shape=[jax.ShapeDtypeStruct((T, D_MODEL), F32), jax.ShapeDtypeStruct((T, IN_COLS), BF16),
                   jax.ShapeDtypeStruct((T, D_MODEL), BF16), jax.ShapeDtypeStruct((1, D_MODEL), F32)],
        compiler_params=_params(("arbitrary",), 56),
    )(*pieces, w_in16t, x2, g1, dh1)


def _adam_shards(name, shards, steps, ride, ride_sliced):
    ns = len(shards)
    nr = len(ride)
    ride_shape, ride_specs, ride_sems = _exchange_shapes(ride, ride_sliced)

    def body(*refs):
        ins, rest = refs[:4 * ns], refs[4 * ns:]
        outs = rest[nr:nr + 4 * ns]
        start_ride, wait_ride = _exchange_copies(rest[:nr], rest[nr + 4 * ns:2 * nr + 4 * ns], ride_sliced,
                                                 *rest[2 * nr + 4 * ns:])
        pl.when(pl.program_id(0) == 0)(start_ride)
        for i in range(ns):
            p_ref, w_ref, m_ref, v_ref = ins[4 * i:4 * i + 4]
            g = p_ref[0]
            for p in range(1, N_DEV):
                g = g + p_ref[p]
            outs[4 * i][...] = g
            outs[4 * i + 1][...], outs[4 * i + 2][...], outs[4 * i + 3][...] = _adamw(w_ref[...], g, m_ref[...], v_ref[...])
        pl.when(pl.program_id(0) == steps - 1)(wait_ride)

    in_specs, out_specs, out_shape, args = [], [], [], []
    for parts, w, m, v in shards:
        R, C = w.shape
        blk = pl.BlockSpec((R // steps, C), lambda i: (i, 0))
        in_specs += [pl.BlockSpec((N_DEV, R // steps, C), lambda i: (0, i, 0)), blk, blk, blk]
        out_specs += [blk] * 4
        out_shape += [jax.ShapeDtypeStruct((R, C), F32)] * 4
        args += [parts, w, m, v]
    res = pl.pallas_call(
        body, name=name, grid=(steps,), in_specs=in_specs + ride_specs, out_specs=out_specs + ride_specs,
        out_shape=out_shape + ride_shape, scratch_shapes=ride_sems,
        compiler_params=_params(("arbitrary",), 48),
    )(*args, *ride)
    return [list(res[4 * i:4 * i + 4]) for i in range(ns)], list(res[4 * ns:])


def _adam_params(name, items):
    def body(*refs):
        ins, outs = refs[:4 * len(items)], refs[4 * len(items):]
        for i in range(len(items)):
            p_ref, w_ref, m_ref, v_ref = ins[4 * i:4 * i + 4]
            g = p_ref[0]
            for p in range(1, N_DEV):
                g = g + p_ref[p]
            outs[4 * i][...] = g
            outs[4 * i + 1][...], outs[4 * i + 2][...], outs[4 * i + 3][...] = _adamw(w_ref[...], g, m_ref[...], v_ref[...])

    args = [a for item in items for a in item]
    outs = [w for _, w, _, _ in items for _ in range(4)]
    res = pl.pallas_call(
        body, name=name, grid=(1,), in_specs=[_whole(a.shape) for a in args], out_specs=[_whole(w.shape) for w in outs],
        out_shape=[jax.ShapeDtypeStruct(w.shape, F32) for w in outs], compiler_params=_params(("arbitrary",), 32),
    )(*args)
    return [list(res[4 * i:4 * i + 4]) for i in range(len(items))]


def _adam_given(name, g, w, m, v, tr):
    R, C = w.shape

    def body(g_ref, w_ref, m_ref, v_ref, d_ref, m2_ref, v2_ref):
        d_ref[...], m2_ref[...], v2_ref[...] = _adamw(w_ref[...], g_ref[...], m_ref[...], v_ref[...])

    blk = pl.BlockSpec((tr, C), lambda i: (i, 0))
    return [g] + list(pl.pallas_call(
        body, name=name, grid=(R // tr,), in_specs=[blk] * 4, out_specs=[blk] * 3,
        out_shape=[jax.ShapeDtypeStruct((R, C), F32)] * 3, compiler_params=_params(("arbitrary",), 32),
    )(g, w, m, v))


def _sum_parts(name, parts):
    def body(p_ref, g_ref):
        g = p_ref[0].astype(F32)
        for p in range(1, N_DEV):
            g = g + p_ref[p].astype(F32)
        g_ref[...] = g

    return pl.pallas_call(
        body, name=name, grid=(1,), in_specs=[_whole(parts.shape)], out_specs=_whole(parts.shape[1:]),
        out_shape=jax.ShapeDtypeStruct(parts.shape[1:], F32), compiler_params=_params(("arbitrary",), 32),
    )(parts)


def _block_diag_pairs(w):
    w = w.reshape(LRU_BLOCKS // 2, 2, HEAD_D, HEAD_D)
    out = jnp.zeros((LRU_BLOCKS // 2, LANES, LANES), w.dtype)
    out = out.at[:, :HEAD_D, :HEAD_D].set(w[:, 0])
    return out.at[:, HEAD_D:, HEAD_D:].set(w[:, 1])


def _diag_blocks(w):
    return jnp.stack([w[:, :HEAD_D, :HEAD_D], w[:, HEAD_D:, HEAD_D:]], axis=1).reshape(LRU_BLOCKS, HEAD_D, HEAD_D)


def kernel(x, norm1_g, w_in, conv_w, conv_b, lru_w_a, lru_b_a, lru_w_x, lru_b_x, lru_lambda, lru_out_g, sb_out_g, w_out, norm2_g, w_up, w_down, final_g, loss_target, m_norm1_g, m_w_in, m_conv_w, m_conv_b, m_lru_w_a, m_lru_b_a, m_lru_w_x, m_lru_b_x, m_lru_lambda, m_lru_out_g, m_sb_out_g, m_w_out, m_norm2_g, m_w_up, m_w_down, m_final_g, v_norm1_g, v_w_in, v_conv_w, v_conv_b, v_lru_w_a, v_lru_b_a, v_lru_w_x, v_lru_b_x, v_lru_lambda, v_lru_out_g, v_sb_out_g, v_w_out, v_norm2_g, v_w_up, v_w_down, v_final_g):
    B, S, _ = x.shape
    T = B * S
    tm = min(512, T)
    tk = min(GRAD_TK, T)
    lc = min(512, S)
    x2 = x.reshape(T, D_MODEL)
    tgt = loss_target.reshape(T, D_MODEL)
    cw_cols = CONV_K * LRU_W // N_DEV // CONV_K

    shards16 = _cast_shards([w_in[0].T, w_out[0], w_up[0], w_down[0]])
    cw_pad = jnp.zeros((SUBLANES, LANES), F32).at[:CONV_K, :cw_cols].set(conv_w[0])
    g_in, g_cw = _gather_two_level("gather_w_in", [shards16[0], cw_pad])
    w_in16t = g_in.reshape(IN_COLS, D_MODEL)
    conv_w_full = g_cw[:, :CONV_K, :cw_cols].transpose(1, 0, 2).reshape(CONV_K, LRU_W)
    wa_bd = _block_diag_pairs(lru_w_a[0]).astype(BF16)
    wx_bd = _block_diag_pairs(lru_w_x[0]).astype(BF16)
    b_a = lru_b_a.reshape(1, LRU_W)
    b_x = lru_b_x.reshape(1, LRU_W)
    gf = final_g.reshape(1, D_MODEL)

    proj_lru, qkv = _fwd_in(x2, norm1_g, w_in16t, tm)
    y_lru, h = _lru_fwd(proj_lru, conv_w_full, conv_b, wa_bd, wx_bd, b_a, b_x, lru_lambda, B, S, lc)
    tri_suffix, tri_prefix = _tri(False), _tri(True)
    y_sb, run_tab, g_out, g_up, g_down = _attn_fwd(qkv, tri_suffix, list(shards16[1:]), [False] * 3, B, S)
    w_out16 = g_out.reshape(D_MODEL, D_MODEL)
    w_down16 = g_down.reshape(D_FF, D_MODEL)
    h1, mix16 = _fwd_mix(y_lru, y_sb, lru_out_g, sb_out_g, w_out16, x2, tm)
    up16, dh2, d_final_g, loss_part = _fwd_mlp(h1, norm2_g, g_up, w_down16, gf, tgt, min(1024, T), 2)

    dup16, dh1, hn16, dh2b, d_norm2_g = _bwd_mlp(dh2, up16, h1, norm2_g, g_up, w_down16, tm, 4)
    sq = lambda u: (u.astype(F32) * u.astype(F32)).astype(BF16)
    wide = 4
    gw_up, = _matmul_tn("grad_w_up", hn16, dup16, D_MODEL, wide * FF_CHUNK, tk // 2, (N_DEV, D_MODEL, FF_CHUNK),
                        (wide, D_MODEL, FF_CHUNK), lambda m, n, k: (n, 0, 0), split=(1, wide))
    gw_down, = _matmul_tn("grad_w_down", up16, dh2b, wide * FF_CHUNK, D_MODEL, tk // 2, (N_DEV, FF_CHUNK, D_MODEL),
                          (wide, FF_CHUNK, D_MODEL), lambda m, n, k: (m, 0, 0), a_prep=sq, split=(0, wide))
    dy_lru, dy_sb, d_lru_out_g, d_sb_out_g = _bwd_mix(dh1, w_out16, y_lru, y_sb, lru_out_g, sb_out_g, tm)
    gw_out, = _matmul_tn("grad_w_out", mix16, dh1, D_MODEL, FF_CHUNK, tk, (D_MODEL, D_MODEL),
                         (D_MODEL, FF_CHUNK), lambda m, n, k: (0, n), b_prep=lambda u: u.astype(BF16))
    parts_out = gw_out.reshape(N_DEV, D_MODEL // N_DEV, D_MODEL)
    dq, dk, dv, r_out, r_up, r_down = _attn_bwd(qkv, run_tab, dy_sb, tri_suffix, tri_prefix,
                                                [parts_out, gw_up, gw_down], [True] * 3, B, S)
    (dx_lru, dg_lru, d_conv_w, d_conv_b, d_wa, d_wx, d_b_a, d_b_x, d_lambda) = _lru_bwd(
        proj_lru, h, dy_lru, conv_w_full, conv_b, wa_bd, wx_bd, b_a, b_x, lru_lambda, B, S, lc)
    dx, dproj16, xn16, d_norm1_g = _bwd_in([dx_lru, dg_lru, dq, dk, dv], w_in16t, x2, norm1_g, dh1, tm)

    small = {"norm1_g": (d_norm1_g, norm1_g, m_norm1_g, v_norm1_g), "conv_b": (d_conv_b, conv_b, m_conv_b, v_conv_b),
             "lru_w_a": (_diag_blocks(d_wa), lru_w_a, m_lru_w_a, v_lru_w_a),
             "lru_b_a": (d_b_a, lru_b_a, m_lru_b_a, v_lru_b_a),
             "lru_w_x": (_diag_blocks(d_wx), lru_w_x, m_lru_w_x, v_lru_w_x),
             "lru_b_x": (d_b_x, lru_b_x, m_lru_b_x, v_lru_b_x),
             "lru_lambda": (d_lambda, lru_lambda, m_lru_lambda, v_lru_lambda),
             "lru_out_g": (d_lru_out_g, lru_out_g, m_lru_out_g, v_lru_out_g),
             "sb_out_g": (d_sb_out_g, sb_out_g, m_sb_out_g, v_sb_out_g),
             "norm2_g": (d_norm2_g, norm2_g, m_norm2_g, v_norm2_g),
             "final_g": (d_final_g, final_g, m_final_g, v_final_g)}
    names = list(small)

    def held(n, a):
        return a.reshape((1, D_MODEL) if n == "final_g" else small[n][1].shape)

    parts_cw = d_conv_w.reshape(CONV_K, N_DEV, cw_cols).transpose(1, 0, 2)[:, None]
    gw_in_t, *got = _matmul_tn(
        "grad_w_in", dproj16, xn16, IN_COLS // 2, D_MODEL, tk // 2, (IN_COLS, D_MODEL), (IN_COLS // 2, D_MODEL),
        lambda m, n, k: (m, 0), out_dtype=BF16,
        ride=[parts_cw] + [held(n, small[n][0]) for n in names] + [loss_part],
        ride_sliced=[True] + [False] * (len(names) + 1))

    big, (r_in,) = _adam_shards(
        "adam_out_up_down", [(r_out, w_out[0], m_w_out[0], v_w_out[0]), (r_up, w_up[0], m_w_up[0], v_w_up[0]),
                             (r_down, w_down[0], m_w_down[0], v_w_down[0])],
        4, [gw_in_t.reshape(N_DEV, IN_COLS // N_DEV, D_MODEL)], [True])
    g_in_t = _sum_parts("sum_w_in", r_in)
    out = dict(zip(("w_out", "w_up", "w_down"), big))
    out["w_in"] = _adam_given("adam_w_in", g_in_t.T, w_in[0], m_w_in[0], v_w_in[0], 256)
    out = {n: [a[None] for a in res] for n, res in out.items()}
    items = [(got[0], conv_w, m_conv_w, v_conv_w)]
    items += [(parts, *[held(n, a) for a in small[n][1:]]) for n, parts in zip(names, got[1:-1])]
    for n, res in zip(["conv_w"] + names, _adam_params("adam_small", items)):
        out[n] = [a.reshape((conv_w if n == "conv_w" else small[n][1]).shape) for a in res]
    loss = _sum_parts("sum_loss", got[-1])[0, 0]
    weights = ["norm1_g", "w_in", "conv_w", "conv_b", "lru_w_a", "lru_b_a", "lru_w_x", "lru_b_x", "lru_lambda",
               "lru_out_g", "sb_out_g", "w_out", "norm2_g", "w_up", "w_down", "final_g"]
    return (loss, dx.reshape(B, S, D_MODEL), *[out[n][0] for n in weights], *[out[n][1] for n in weights],
            *[out[n][2] for n in weights], *[out[n][3] for n in weights])
```

```python
import jax
import jax.numpy as jnp
from jax import lax
from jax.experimental import pallas as pl
from jax.experimental.pallas import tpu as pltpu

F32 = jnp.float32
BF16 = jnp.bfloat16

D_MODEL = 1024
LRU_W = 512
SB_W = 512
HEAD_D = 64
D_FF = 4096
IN_COLS = 2 * LRU_W + 3 * SB_W
CONV_K = 4
LRU_BLOCKS = 8
LRU_C = 8.0
EPS = 1e-6
N_DEV = 8
LANES = 128
SUBLANES = 8
FF_CHUNK = 512
GRAD_TK = 2048
Q_BLK = 256
K_BLK = 256
Q_PER_K = K_BLK // Q_BLK

ADAM_LR = 0.001
ADAM_B1 = 0.9
ADAM_B2 = 0.999
ADAM_EPS = 1e-08
ADAM_WD = 0.01
ADAM_STEP = 10


def _params(sem=None, vmem_mb=None):
    kw = {}
    if sem is not None:
        kw["dimension_semantics"] = sem
    if vmem_mb is not None:
        kw["vmem_limit_bytes"] = vmem_mb << 20
    return pltpu.CompilerParams(**kw)


def _dot(a, b):
    return jnp.dot(a, b, preferred_element_type=F32)


def _dot_nt(a, b):
    return lax.dot_general(a, b, (((1,), (1,)), ((), ())), preferred_element_type=F32)


def _dot_tn(a, b):
    return lax.dot_general(a, b, (((0,), (0,)), ((), ())), preferred_element_type=F32)


def _rms_fwd(x, g):
    rstd = lax.rsqrt(jnp.mean(x * x, axis=-1, keepdims=True) + EPS)
    xhat = x * rstd
    return xhat * g, xhat, rstd


def _rms_bwd(dy, xhat, rstd, g):
    dxhat = dy * g
    return rstd * (dxhat - xhat * jnp.mean(dxhat * xhat, axis=-1, keepdims=True))


def _sigmoid(x):
    return 1.0 / (1.0 + jnp.exp(-x))


def _log1p_pos(e):
    series = e * (1.0 - e * (0.5 - e * (1.0 / 3.0 - e * 0.25)))
    return jnp.where(e < 1e-2, series, jnp.log(1.0 + e))


def _neg_expm1(x):
    series = -x * (1.0 + x * (0.5 + x * (1.0 / 6.0 + x * (1.0 / 24.0))))
    return jnp.where(x > -1e-2, series, 1.0 - jnp.exp(x))


def _gelu_parts(g):
    k0 = 0.7978845608028654
    k1 = 0.044715
    t = jnp.tanh(k0 * (g + k1 * g * g * g))
    val = 0.5 * g * (1.0 + t)
    grad = 0.5 * (1.0 + t) + 0.5 * g * (1.0 - t * t) * k0 * (1.0 + 3.0 * k1 * g * g)
    return val, grad


def _scan(a, b, reverse):
    n = a.shape[0]
    row = lax.broadcasted_iota(jnp.int32, a.shape, 0)
    s = 1
    while s < n:
        if reverse:
            keep = row < n - s
            shift = n - s
        else:
            keep = row >= s
            shift = s
        bs = jnp.where(keep, pltpu.roll(b, shift, 0), 0.0)
        a_s = jnp.where(keep, pltpu.roll(a, shift, 0), 1.0)
        b = a * bs + b
        a = a * a_s
        s *= 2
    return b, a


def _adamw(w, g, m, v):
    m = ADAM_B1 * m + (1.0 - ADAM_B1) * g
    v = ADAM_B2 * v + (1.0 - ADAM_B2) * (g * g)
    m_hat = m / (1.0 - ADAM_B1 ** ADAM_STEP)
    v_hat = v / (1.0 - ADAM_B2 ** ADAM_STEP)
    delta = -ADAM_LR * (m_hat / (jnp.sqrt(v_hat) + ADAM_EPS) + ADAM_WD * w)
    return delta, m, v


def _my_index():
    return 4 * lax.axis_index("x") + 2 * lax.axis_index("y") + lax.axis_index("c")


def _peer(k):
    x, y, c = lax.axis_index("x"), lax.axis_index("y"), lax.axis_index("c")
    px = 1 - x if (k >> 2) & 1 else x
    py = 1 - y if (k >> 1) & 1 else y
    pc = 1 - c if k & 1 else c
    return (px, py, pc), 4 * px + 2 * py + pc


def _exchange_shapes(srcs, sliced):
    n = len(srcs)
    out_shape = [jax.ShapeDtypeStruct(s.shape if sl else (N_DEV,) + s.shape, s.dtype) for s, sl in zip(srcs, sliced)]
    specs = [pl.BlockSpec(memory_space=pl.ANY)] * n
    sems = [pltpu.SemaphoreType.DMA((n, N_DEV - 1)), pltpu.SemaphoreType.DMA((n, N_DEV - 1)),
            pltpu.SemaphoreType.DMA((n,))]
    return out_shape, specs, sems


def _exchange_copies(ins, outs, sliced, send_sems, recv_sems, local_sems):
    n = len(ins)

    def part(a, p):
        return ins[a].at[p] if sliced[a] else ins[a]

    def copies(receiving):
        me = _my_index()
        local = [pltpu.make_async_copy(part(a, me), outs[a].at[me], local_sems.at[a]) for a in range(n)]
        remote = []
        for k in range(1, N_DEV):
            dev, idx = _peer(k)
            for a in range(n):
                remote.append(pltpu.make_async_remote_copy(
                    src_ref=part(a, idx), dst_ref=outs[a].at[idx if receiving else me],
                    send_sem=send_sems.at[a, k - 1], recv_sem=recv_sems.at[a, k - 1],
                    device_id=dev, device_id_type=pl.DeviceIdType.MESH))
        return local, remote

    def start():
        local, remote = copies(receiving=False)
        for cp in local + remote:
            cp.start()

    def wait():
        local, remote = copies(receiving=True)
        for cp in remote + local:
            cp.wait()

    return start, wait


def _gather_two_level(name, srcs):
    n = len(srcs)
    out_shape = [jax.ShapeDtypeStruct((N_DEV,) + s.shape, s.dtype) for s in srcs]

    def body(*refs):
        ins, outs = refs[:n], refs[n:2 * n]
        send_sems, recv_sems, local_sems = refs[2 * n:]
        x, y, c = lax.axis_index("x"), lax.axis_index("y"), lax.axis_index("c")
        here, sibling = (x, y, c), (x, y, 1 - c)
        chips = [(1 - x, y), (x, 1 - y), (1 - x, 1 - y)]

        def slot(px, py, pc):
            return 4 * px + 2 * py + pc

        def copy(a, k, block, to, src=None):
            return pltpu.make_async_remote_copy(
                src_ref=ins[a] if src is None else src, dst_ref=outs[a].at[slot(*block)],
                send_sem=send_sems.at[a, k], recv_sem=recv_sems.at[a, k], device_id=to,
                device_id_type=pl.DeviceIdType.MESH)

        local = [pltpu.make_async_copy(ins[a], outs[a].at[slot(*here)], local_sems.at[a]) for a in range(n)]
        first = [copy(a, 0, here, sibling) for a in range(n)]
        first += [copy(a, 1 + j, here, (*chip, c)) for j, chip in enumerate(chips) for a in range(n)]
        for cp in local + first:
            cp.start()
        passed = []
        for j, chip in enumerate(chips):
            for a in range(n):
                copy(a, 1 + j, (*chip, c), here).wait_recv()
                passed.append(copy(a, 4 + j, (*chip, c), sibling, src=outs[a].at[slot(*chip, c)]))
                passed[-1].start()
        for a in range(n):
            copy(a, 0, sibling, here).wait_recv()
        for j, chip in enumerate(chips):
            for a in range(n):
                copy(a, 4 + j, (*chip, 1 - c), here).wait_recv()
        for cp in first + passed:
            cp.wait_send()
        for cp in local:
            cp.wait()

    spec = [pl.BlockSpec(memory_space=pl.ANY)] * n
    return pl.pallas_call(
        body, name=name, out_shape=out_shape, in_specs=spec, out_specs=spec,
        scratch_shapes=[pltpu.SemaphoreType.DMA((n, N_DEV - 1)), pltpu.SemaphoreType.DMA((n, N_DEV - 1)),
                        pltpu.SemaphoreType.DMA((n,))],
    )(*srcs)


def _whole(shape):
    return pl.BlockSpec(shape, lambda *_: (0,) * len(shape))


def _cast_shards(ws, steps=4):
    def body(*refs):
        for i in range(len(ws)):
            refs[len(ws) + i][...] = refs[i][...].astype(BF16)

    specs = [pl.BlockSpec((w.shape[0] // steps, w.shape[1]), lambda i: (i, 0)) for w in ws]
    return pl.pallas_call(
        body, name="cast_shards", grid=(steps,), in_specs=specs, out_specs=specs,
        out_shape=[jax.ShapeDtypeStruct(w.shape, BF16) for w in ws],
        compiler_params=_params(("arbitrary",), 32),
    )(*ws)


def _fwd_in(x2, g1, w_in16t, tm):
    T = x2.shape[0]

    def body(x_ref, g_ref, w_ref, lru_ref, qkv_ref):
        xn, _, _ = _rms_fwd(x_ref[...], g_ref[...])
        xn = xn.astype(BF16)
        lru_ref[...] = _dot_nt(xn, w_ref[0:2 * LRU_W, :])
        qkv_ref[...] = _dot_nt(xn, w_ref[2 * LRU_W:IN_COLS, :]).astype(BF16)

    return pl.pallas_call(
        body, name="fwd_in", grid=(T // tm,),
        in_specs=[pl.BlockSpec((tm, D_MODEL), lambda i: (i, 0)),
                  pl.BlockSpec((1, D_MODEL), lambda i: (0, 0)),
                  pl.BlockSpec((IN_COLS, D_MODEL), lambda i: (0, 0))],
        out_specs=[pl.BlockSpec((tm, 2 * LRU_W), lambda i: (i, 0)),
                   pl.BlockSpec((tm, 3 * SB_W), lambda i: (i, 0))],
        out_shape=[jax.ShapeDtypeStruct((T, 2 * LRU_W), F32), jax.ShapeDtypeStruct((T, 3 * SB_W), BF16)],
        compiler_params=_params(("arbitrary",), 48),
    )(x2, g1, w_in16t)


def _lru_gates(c, wa_ref, wx_ref, ba_ref, bx_ref, lam_ref):
    c16 = c.astype(BF16)
    r = _sigmoid(_dot(c16, wa_ref[0]) + ba_ref[...])
    i = _sigmoid(_dot(c16, wx_ref[0]) + bx_ref[...])
    lam = lam_ref[...]
    e = jnp.exp(-jnp.abs(lam))
    sp = jnp.maximum(-lam, 0.0) + _log1p_pos(e)
    dsp_dlam = -jnp.where(lam >= 0.0, e, 1.0) / (1.0 + e)
    log_a = (-LRU_C) * r * sp
    a = jnp.exp(log_a)
    s = jnp.sqrt(_neg_expm1(2.0 * log_a))
    return r, i, sp, dsp_dlam, a, s


def _conv_taps(x, halo, lc):
    xe = jnp.concatenate([halo, x], axis=0)
    return [x] + [pltpu.roll(xe, k, 0)[SUBLANES:SUBLANES + lc] for k in range(1, CONV_K)]


def _lru_fwd(proj_lru, conv_w, conv_b, wa_bd, wx_bd, b_a, b_x, lam, B, S, lc):
    T = B * S
    nc = S // lc
    ncb = LRU_W // LANES

    def body(x_ref, g_ref, cw_ref, cb_ref, wa_ref, wx_ref, ba_ref, bx_ref, lam_ref, y_ref, h_ref, tail, carry):
        ci = pl.program_id(2)

        @pl.when(ci == 0)
        def _():
            tail[...] = jnp.zeros_like(tail)
            carry[...] = jnp.zeros_like(carry)

        x = x_ref[...]
        taps = _conv_taps(x, tail[...], lc)
        c = cb_ref[...] + sum(cw_ref[pl.ds(CONV_K - 1 - k, 1), :] * taps[k] for k in range(CONV_K))
        tail[...] = x_ref[pl.ds(lc - SUBLANES, SUBLANES), :]
        r, i, sp, _, a, s = _lru_gates(c, wa_ref, wx_ref, ba_ref, bx_ref, lam_ref)
        h_loc, a_run = _scan(a, s * (i * c), reverse=False)
        h_ref[...] = h_loc + a_run * carry[...]
        carry[...] = h_ref[pl.ds(lc - 1, 1), :]
        gelu, _ = _gelu_parts(g_ref[...])
        y_ref[...] = h_ref[...] * gelu

    chan = lambda b, cb, ci: (0, cb)
    return pl.pallas_call(
        body, name="lru_fwd", grid=(B, ncb, nc),
        in_specs=[pl.BlockSpec((lc, LANES), lambda b, cb, ci: (b * nc + ci, cb)),
                  pl.BlockSpec((lc, LANES), lambda b, cb, ci: (b * nc + ci, ncb + cb)),
                  pl.BlockSpec((CONV_K, LANES), chan), pl.BlockSpec((1, LANES), chan),
                  pl.BlockSpec((1, LANES, LANES), lambda b, cb, ci: (cb, 0, 0)),
                  pl.BlockSpec((1, LANES, LANES), lambda b, cb, ci: (cb, 0, 0)),
                  pl.BlockSpec((1, LANES), chan), pl.BlockSpec((1, LANES), chan), pl.BlockSpec((1, LANES), chan)],
        out_specs=[pl.BlockSpec((lc, LANES), lambda b, cb, ci: (b * nc + ci, cb))] * 2,
        out_shape=[jax.ShapeDtypeStruct((T, LRU_W), F32)] * 2,
        scratch_shapes=[pltpu.VMEM((SUBLANES, LANES), F32), pltpu.VMEM((1, LANES), F32)],
        compiler_params=_params(("arbitrary", "arbitrary", "arbitrary"), 32),
    )(proj_lru, proj_lru, conv_w, conv_b, wa_bd, wx_bd, b_a, b_x, lam)


def _tri(prefix):
    r = lax.broadcasted_iota(jnp.int32, (K_BLK, K_BLK), 0)
    c = lax.broadcasted_iota(jnp.int32, (K_BLK, K_BLK), 1)
    return ((r <= c) if prefix else (r >= c)).astype(BF16)


def _attn_consts(qi):
    r = lax.broadcasted_iota(jnp.int32, (Q_BLK, K_BLK), 0)
    c = lax.broadcasted_iota(jnp.int32, (Q_BLK, K_BLK), 1)
    causal = c + ((qi // Q_PER_K) * K_BLK - qi * Q_BLK) < r
    lane = lax.broadcasted_iota(jnp.int32, (1, LANES), 1)
    return causal, lane, (lane < HEAD_D, lane >= HEAD_D)


def _log1m_beta(z, mask):
    lg = -(jnp.maximum(z, 0.0) + jnp.log(1.0 + jnp.exp(-jnp.abs(z))))
    return lg if mask is None else jnp.where(mask, lg, 0.0)


def _key_rows(j):
    return pl.ds(pl.multiple_of(j * K_BLK, K_BLK), K_BLK)


def _attn_fwd(qkv, tri_suffix, ride, ride_sliced, B, S):
    T = B * S
    nq = S // Q_BLK
    nhp = SB_W // LANES
    scale = HEAD_D ** -0.5
    assert S // K_BLK <= LANES
    nr = len(ride)
    ride_shape, ride_specs, ride_sems = _exchange_shapes(ride, ride_sliced)

    def body(q_ref, k_ref, v_ref, tri_ref, *rest):
        o_ref, run_ref = rest[nr:nr + 2]
        start_ride, wait_ride = _exchange_copies(rest[:nr], rest[nr + 2:2 * nr + 2], ride_sliced, *rest[2 * nr + 2:])
        qi = pl.program_id(2)
        step_no = (pl.program_id(0) * nhp + pl.program_id(1)) * nq + qi
        pl.when(step_no == 0)(start_ride)
        jd = qi // Q_PER_K
        causal, lane, halves = _attn_consts(qi)
        q = q_ref[...]
        qh = [jnp.where(hm, q, jnp.zeros_like(q)) * jnp.asarray(scale, BF16) for hm in halves]

        def group(blocks, carry):
            runs, tables, acc = carry
            runs, tables = list(runs), list(tables)
            ks = [k_ref[_key_rows(jl), :] for jl, _, _ in blocks]
            vs = [v_ref[_key_rows(jl), :] for jl, _, _ in blocks]
            chains = [(b, h) for b in range(len(blocks)) for h in range(2)]
            z = {c: _dot_nt(qh[c[1]], ks[c[0]]) for c in chains}
            lg = {c: _log1m_beta(z[c], blocks[c[0]][2]) for c in chains}
            suf = {c: _dot(lg[c].astype(BF16), tri_ref[...]) for c in chains}
            att = {}
            for b, h in chains:
                _, jlane, mask = blocks[b]
                a = jnp.exp(z[b, h] + suf[b, h] + runs[h])
                att[b, h] = (a if mask is None else jnp.where(mask, a, 0.0)).astype(BF16)
                tables[h] = jnp.where(lane == jlane, runs[h], tables[h])
                runs[h] = runs[h] + suf[b, h][:, 0:1]
            for b, h in chains:
                acc = acc + _dot(att[b, h], jnp.where(halves[h], vs[b], jnp.zeros_like(vs[b])))
            return tuple(runs), tuple(tables), acc

        col0 = jnp.zeros((Q_BLK, 1), F32)
        zero = jnp.zeros((Q_BLK, LANES), F32)
        start = ((col0, col0), (zero, zero), zero)
        two = jd % 2
        carry = lax.cond(two == 1, lambda: group([(jd, jd, causal), (jd - 1, jd - 1, None)], start),
                         lambda: group([(jd, jd, causal)], start))
        top = jd - 1 - two

        def run(ja, n):
            return [(ja - i, ja - i, None) for i in range(n)]

        odd_pair = ((top + 1) // 2) % 2
        carry = lax.cond(odd_pair == 1, lambda cr: group(run(top, 2), cr), lambda cr: cr, carry)
        top4 = top - 2 * odd_pair
        _, tables, acc = lax.fori_loop(0, (top4 + 1) // 4, lambda it, cr: group(run(top4 - 4 * it, 4), cr), carry)
        o_ref[...] = acc
        run_ref[:, 0:LANES] = tables[0]
        run_ref[:, LANES:2 * LANES] = tables[1]
        pl.when(step_no == B * nhp * nq - 1)(wait_ride)

    return pl.pallas_call(
        body, name="attn_fwd", grid=(B, nhp, nq),
        in_specs=[pl.BlockSpec((Q_BLK, LANES), lambda b, hp, qi: (b * nq + qi, hp)),
                  pl.BlockSpec((S, LANES), lambda b, hp, qi: (b, nhp + hp)),
                  pl.BlockSpec((S, LANES), lambda b, hp, qi: (b, 2 * nhp + hp)),
                  pl.BlockSpec((K_BLK, K_BLK), lambda b, hp, qi: (0, 0))] + ride_specs,
        out_specs=[pl.BlockSpec((Q_BLK, LANES), lambda b, hp, qi: (b * nq + qi, hp)),
                   pl.BlockSpec((Q_BLK, 2 * LANES), lambda b, hp, qi: (b * nq + qi, hp))] + ride_specs,
        out_shape=[jax.ShapeDtypeStruct((T, SB_W), F32), jax.ShapeDtypeStruct((T, 2 * SB_W), F32)] + ride_shape,
        scratch_shapes=ride_sems,
        compiler_params=_params(("arbitrary", "arbitrary", "arbitrary"), 48),
    )(qkv, qkv, qkv, tri_suffix, *ride)


def _fwd_mix(y_lru, y_sb, ga, gb, w_out16, x2, tm):
    T = x2.shape[0]

    def body(yl_ref, ys_ref, ga_ref, gb_ref, w_ref, x_ref, h1_ref, mix_ref):
        na, _, _ = _rms_fwd(yl_ref[...], ga_ref[...])
        nb, _, _ = _rms_fwd(ys_ref[...], gb_ref[...])
        na = na.astype(BF16)
        nb = nb.astype(BF16)
        mix_ref[:, 0:LRU_W] = na
        mix_ref[:, LRU_W:D_MODEL] = nb
        h1_ref[...] = x_ref[...] + _dot(na, w_ref[0:LRU_W, :]) + _dot(nb, w_ref[LRU_W:D_MODEL, :])

    row = lambda i: (i, 0)
    fix = lambda i: (0, 0)
    return pl.pallas_call(
        body, name="fwd_mix", grid=(T // tm,),
        in_specs=[pl.BlockSpec((tm, LRU_W), row), pl.BlockSpec((tm, SB_W), row),
                  pl.BlockSpec((1, LRU_W), fix), pl.BlockSpec((1, SB_W), fix),
                  pl.BlockSpec((D_MODEL, D_MODEL), fix), pl.BlockSpec((tm, D_MODEL), row)],
        out_specs=[pl.BlockSpec((tm, D_MODEL), row), pl.BlockSpec((tm, D_MODEL), row)],
        out_shape=[jax.ShapeDtypeStruct((T, D_MODEL), F32), jax.ShapeDtypeStruct((T, D_MODEL), BF16)],
        compiler_params=_params(("arbitrary",), 48),
    )(y_lru, y_sb, ga, gb, w_out16, x2)


def _fwd_mlp(h1, g2, w_up16, w_down16, gf, tgt, tm, per_step):
    T = h1.shape[0]
    nf = D_FF // (FF_CHUNK * per_step)

    def body(h1_ref, g2_ref, wu_ref, wd_ref, gf_ref, t_ref, up_ref, dh2_ref, dgf_ref, loss_ref, hn_s, acc):
        i, j = pl.program_id(0), pl.program_id(1)

        @pl.when(j == 0)
        def _():
            h1v = h1_ref[...]
            hn, _, _ = _rms_fwd(h1v, g2_ref[...])
            hn_s[...] = hn.astype(BF16)
            acc[...] = h1v

        down = None
        for c in range(per_step):
            cols = slice(c * FF_CHUNK, (c + 1) * FF_CHUNK)
            up = jnp.maximum(_dot(hn_s[...], wu_ref[c]), 0.0)
            up_ref[:, cols] = up.astype(BF16)
            part = _dot((up * up).astype(BF16), wd_ref[cols, :])
            down = part if down is None else down + part
        acc[...] += down

        @pl.when((i == 0) & (j == 0))
        def _():
            dgf_ref[...] = jnp.zeros_like(dgf_ref)
            loss_ref[...] = jnp.zeros_like(loss_ref)

        @pl.when(j == nf - 1)
        def _():
            gfv = gf_ref[...]
            y, xhat, rstd = _rms_fwd(acc[...], gfv)
            err = y - t_ref[...]
            loss_ref[...] += jnp.sum(0.5 * jnp.sum(err * err, axis=-1, keepdims=True) * (1.0 / D_MODEL))
            dy = err * (1.0 / D_MODEL)
            dgf_ref[...] += jnp.sum(dy * xhat, axis=0, keepdims=True)
            dh2_ref[...] = _rms_bwd(dy, xhat, rstd, gfv)

    row = lambda i, j: (i, 0)
    fix = lambda i, j: (0, 0)
    return pl.pallas_call(
        body, name="fwd_mlp", grid=(T // tm, nf),
        in_specs=[pl.BlockSpec((tm, D_MODEL), row), pl.BlockSpec((1, D_MODEL), fix),
                  pl.BlockSpec((per_step, D_MODEL, FF_CHUNK), lambda i, j: (j, 0, 0)),
                  pl.BlockSpec((per_step * FF_CHUNK, D_MODEL), lambda i, j: (j, 0)),
                  pl.BlockSpec((1, D_MODEL), fix), pl.BlockSpec((tm, D_MODEL), row)],
        out_specs=[pl.BlockSpec((tm, per_step * FF_CHUNK), lambda i, j: (i, j)), pl.BlockSpec((tm, D_MODEL), row),
                   pl.BlockSpec((1, D_MODEL), fix), pl.BlockSpec((1, LANES), fix)],
        out_shape=[jax.ShapeDtypeStruct((T, D_FF), BF16), jax.ShapeDtypeStruct((T, D_MODEL), F32),
                   jax.ShapeDtypeStruct((1, D_MODEL), F32), jax.ShapeDtypeStruct((1, LANES), F32)],
        scratch_shapes=[pltpu.VMEM((tm, D_MODEL), BF16), pltpu.VMEM((tm, D_MODEL), F32)],
        compiler_params=_params(("arbitrary", "arbitrary"), 56),
    )(h1, g2, w_up16, w_down16, gf, tgt)


def _bwd_mlp(dh2, up16, h1, g2, w_up16, w_down16, tm, per_step):
    T = h1.shape[0]
    nf = D_FF // (FF_CHUNK * per_step)

    def body(dh2_ref, up_ref, h1_ref, g2_ref, wu_ref, wd_ref, dup_ref, dh1_ref, hn_ref, dh2b_ref, dg2_ref, acc):
        i, j = pl.program_id(0), pl.program_id(1)

        @pl.when(j == 0)
        def _():
            hn, _, _ = _rms_fwd(h1_ref[...], g2_ref[...])
            hn_ref[...] = hn.astype(BF16)
            dh2b_ref[...] = dh2_ref[...].astype(BF16)
            acc[...] = jnp.zeros_like(acc)

        dhn = None
        for c in range(per_step):
            cols = slice(c * FF_CHUNK, (c + 1) * FF_CHUNK)
            u = up_ref[:, cols].astype(F32)
            dup = (2.0 * u * _dot_nt(dh2b_ref[...], wd_ref[cols, :])).astype(BF16)
            dup_ref[:, cols] = dup
            part = _dot_nt(dup, wu_ref[c])
            dhn = part if dhn is None else dhn + part
        acc[...] += dhn

        @pl.when((i == 0) & (j == 0))
        def _():
            dg2_ref[...] = jnp.zeros_like(dg2_ref)

        @pl.when(j == nf - 1)
        def _():
            g2v = g2_ref[...]
            _, xhat, rstd = _rms_fwd(h1_ref[...], g2v)
            dhn = acc[...]
            dg2_ref[...] += jnp.sum(dhn * xhat, axis=0, keepdims=True)
            dh1_ref[...] = dh2_ref[...] + _rms_bwd(dhn, xhat, rstd, g2v)

    row = lambda i, j: (i, 0)
    fix = lambda i, j: (0, 0)
    return pl.pallas_call(
        body, name="bwd_mlp", grid=(T // tm, nf),
        in_specs=[pl.BlockSpec((tm, D_MODEL), row), pl.BlockSpec((tm, per_step * FF_CHUNK), lambda i, j: (i, j)),
                  pl.BlockSpec((tm, D_MODEL), row), pl.BlockSpec((1, D_MODEL), fix),
                  pl.BlockSpec((per_step, D_MODEL, FF_CHUNK), lambda i, j: (j, 0, 0)),
                  pl.BlockSpec((per_step * FF_CHUNK, D_MODEL), lambda i, j: (j, 0))],
        out_specs=[pl.BlockSpec((tm, per_step * FF_CHUNK), lambda i, j: (i, j)), pl.BlockSpec((tm, D_MODEL), row),
                   pl.BlockSpec((tm, D_MODEL), row), pl.BlockSpec((tm, D_MODEL), row),
                   pl.BlockSpec((1, D_MODEL), fix)],
        out_shape=[jax.ShapeDtypeStruct((T, D_FF), BF16), jax.ShapeDtypeStruct((T, D_MODEL), F32),
                   jax.ShapeDtypeStruct((T, D_MODEL), BF16), jax.ShapeDtypeStruct((T, D_MODEL), BF16),
                   jax.ShapeDtypeStruct((1, D_MODEL), F32)],
        scratch_shapes=[pltpu.VMEM((tm, D_MODEL), F32)],
        compiler_params=_params(("arbitrary", "arbitrary"), 48),
    )(dh2, up16, h1, g2, w_up16, w_down16)


def _matmul_tn(name, a, b, bm, bn, tk, out_shape, out_block, out_index, a_prep=None, b_prep=None, out_dtype=F32,
               ride=None, ride_sliced=None, split=None):
    T, M = a.shape
    N = b.shape[1]
    grid = (M // bm, N // bn, T // tk)
    ride, ride_sliced = list(ride or []), list(ride_sliced or [])
    nr = len(ride)
    ride_shape, ride_specs, ride_sems = _exchange_shapes(ride, ride_sliced)

    def body(a_ref, b_ref, *rest):
        o_ref, acc = rest[nr], rest[2 * nr + 1]
        k = pl.program_id(2)
        step_no = (pl.program_id(0) * grid[1] + pl.program_id(1)) * grid[2] + k
        if nr:
            start_ride, wait_ride = _exchange_copies(rest[:nr], rest[nr + 1:2 * nr + 1], ride_sliced,
                                                     *rest[2 * nr + 2:])
            pl.when(step_no == 0)(start_ride)
        av = a_ref[...] if a_prep is None else a_prep(a_ref[...])
        bv = b_ref[...] if b_prep is None else b_prep(b_ref[...])
        p = _dot_tn(av, bv)

        @pl.when(k == 0)
        def _():
            acc[...] = p

        @pl.when(k > 0)
        def _():
            acc[...] += p

        @pl.when(k == grid[2] - 1)
        def _():
            if split is None:
                o_ref[...] = acc[...].astype(out_dtype)
            else:
                axis, n = split
                w = (bm, bn)[axis] // n
                for c in range(n):
                    slab = acc[c * w:(c + 1) * w, :] if axis == 0 else acc[:, c * w:(c + 1) * w]
                    o_ref[c] = slab.astype(out_dtype)

        if nr:
            pl.when(step_no == grid[0] * grid[1] * grid[2] - 1)(wait_ride)

    return pl.pallas_call(
        body, name=name, grid=grid,
        in_specs=[pl.BlockSpec((tk, bm), lambda m, n, k: (k, m)), pl.BlockSpec((tk, bn), lambda m, n, k: (k, n))]
        + ride_specs,
        out_specs=[pl.BlockSpec(out_block, out_index)] + ride_specs,
        out_shape=[jax.ShapeDtypeStruct(out_shape, out_dtype)] + ride_shape,
        scratch_shapes=[pltpu.VMEM((bm, bn), F32)] + (ride_sems if nr else []),
        compiler_params=_params(("arbitrary", "arbitrary", "arbitrary"), 48),
    )(a, b, *ride)


def _bwd_mix(dh1, w_out16, y_lru, y_sb, ga, gb, tm):
    T = dh1.shape[0]

    def body(d_ref, w_ref, yl_ref, ys_ref, ga_ref, gb_ref, dyl_ref, dys_ref, dga_ref, dgb_ref):
        @pl.when(pl.program_id(0) == 0)
        def _():
            dga_ref[...] = jnp.zeros_like(dga_ref)
            dgb_ref[...] = jnp.zeros_like(dgb_ref)

        d16 = d_ref[...].astype(BF16)
        for y_ref, g_ref, lo, dy_ref, dg_ref in ((yl_ref, ga_ref, 0, dyl_ref, dga_ref),
                                                 (ys_ref, gb_ref, LRU_W, dys_ref, dgb_ref)):
            gv = g_ref[...]
            dn = _dot_nt(d16, w_ref[lo:lo + LRU_W, :])
            _, xhat, rstd = _rms_fwd(y_ref[...], gv)
            dg_ref[...] += jnp.sum(dn * xhat, axis=0, keepdims=True)
            dy_ref[...] = _rms_bwd(dn, xhat, rstd, gv).astype(dy_ref.dtype)

    row = lambda i: (i, 0)
    fix = lambda i: (0, 0)
    return pl.pallas_call(
        body, name="bwd_mix", grid=(T // tm,),
        in_specs=[pl.BlockSpec((tm, D_MODEL), row), pl.BlockSpec((D_MODEL, D_MODEL), fix),
                  pl.BlockSpec((tm, LRU_W), row), pl.BlockSpec((tm, SB_W), row),
                  pl.BlockSpec((1, LRU_W), fix), pl.BlockSpec((1, SB_W), fix)],
        out_specs=[pl.BlockSpec((tm, LRU_W), row), pl.BlockSpec((tm, SB_W), row),
                   pl.BlockSpec((1, LRU_W), fix), pl.BlockSpec((1, SB_W), fix)],
        out_shape=[jax.ShapeDtypeStruct((T, LRU_W), F32), jax.ShapeDtypeStruct((T, SB_W), BF16),
                   jax.ShapeDtypeStruct((1, LRU_W), F32), jax.ShapeDtypeStruct((1, SB_W), F32)],
        compiler_params=_params(("arbitrary",), 48),
    )(dh1, w_out16, y_lru, y_sb, ga, gb)


def _attn_bwd(qkv, run_tab, dy_sb, tri_suffix, tri_prefix, ride, ride_sliced, B, S):
    T = B * S
    nq = S // Q_BLK
    nkb = S // K_BLK
    nhp = SB_W // LANES
    scale = HEAD_D ** -0.5

    nr = len(ride)
    ride_shape, ride_specs, ride_sems = _exchange_shapes(ride, ride_sliced)

    def body(q_ref, k_ref, v_ref, run_ref, do_ref, ts_ref, tp_ref, *rest):
        dq_ref, dk_ref, dv_ref = rest[nr:nr + 3]
        dkt_ref, dvt_ref = rest[2 * nr + 3:2 * nr + 5]
        start_ride, wait_ride = _exchange_copies(rest[:nr], rest[nr + 3:2 * nr + 3], ride_sliced, *rest[2 * nr + 5:])
        qi = pl.program_id(2)
        step_no = (pl.program_id(0) * nhp + pl.program_id(1)) * nq + qi
        pl.when(step_no == 0)(start_ride)
        jd = qi // Q_PER_K

        @pl.when(qi == 0)
        def _():
            dkt_ref[...] = jnp.zeros_like(dkt_ref)
            dvt_ref[...] = jnp.zeros_like(dvt_ref)

        causal, lane, halves = _attn_consts(qi)
        q = q_ref[...]
        do = do_ref[...]
        qh = [jnp.where(hm, q, jnp.zeros_like(q)) * jnp.asarray(scale, BF16) for hm in halves]
        doh = [jnp.where(hm, do, jnp.zeros_like(do)) for hm in halves]
        qt = jnp.concatenate([h.astype(F32).T.astype(BF16) for h in qh], axis=1)
        dot_ = jnp.concatenate([h.astype(F32).T.astype(BF16) for h in doh], axis=1)
        tables = [run_ref[:, 0:LANES], run_ref[:, LANES:2 * LANES]]

        def group(blocks, carry):
            prefixes, dq = carry
            prefixes = list(prefixes)
            ks = [k_ref[_key_rows(jl), :] for jl, _, _ in blocks]
            vs = [v_ref[_key_rows(jl), :] for jl, _, _ in blocks]
            chains = [(b, h) for b in range(len(blocks)) for h in range(2)]
            z = {c: _dot_nt(qh[c[1]], ks[c[0]]) for c in chains}
            da = {c: _dot_nt(doh[c[1]], vs[c[0]]) for c in chains}
            lg = {c: _log1m_beta(z[c], blocks[c[0]][2]) for c in chains}
            suf = {c: _dot(lg[c].astype(BF16), ts_ref[...]) for c in chains}
            att, g = {}, {}
            for b, h in chains:
                _, jlane, mask = blocks[b]
                run = jnp.sum(jnp.where(lane == jlane, tables[h], 0.0), axis=1, keepdims=True)
                a = jnp.exp(z[b, h] + suf[b, h] + run)
                a = a if mask is None else jnp.where(mask, a, 0.0)
                g[b, h] = a * da[b, h]
                att[b, h] = a.astype(BF16)
            gpre = {c: _dot(g[c].astype(BF16), tp_ref[...]) for c in chains}
            dz = {}
            for b, h in chains:
                mask = blocks[b][2]
                d = g[b, h] - jnp.exp(z[b, h] + lg[b, h]) * (prefixes[h] + gpre[b, h])
                dz[b, h] = (d if mask is None else jnp.where(mask, d, 0.0)).astype(BF16)
                prefixes[h] = prefixes[h] + gpre[b, h][:, K_BLK - 1:K_BLK]
            for b, h in chains:
                dq = dq + _dot(dz[b, h], jnp.where(halves[h], ks[b], jnp.zeros_like(ks[b])))
            for b, (jl, _, _) in enumerate(blocks):
                dkt_ref[jl] += _dot(qt, jnp.concatenate([dz[b, 0], dz[b, 1]], axis=0))
                dvt_ref[jl] += _dot(dot_, jnp.concatenate([att[b, 0], att[b, 1]], axis=0))
            return tuple(prefixes), dq

        def run(ja, n):
            return [(ja + i, ja + i, None) for i in range(n)]

        col0 = jnp.zeros((Q_BLK, 1), F32)
        fours = (jd // 2) // 2
        carry = lax.fori_loop(0, fours, lambda it, cr: group(run(4 * it, 4), cr),
                              ((col0, col0), jnp.zeros((Q_BLK, LANES), F32)))
        carry = lax.cond((jd // 2) % 2 == 1, lambda cr: group(run(4 * fours, 2), cr), lambda cr: cr, carry)
        carry = lax.cond(jd % 2 == 1, lambda cr: group([(jd - 1, jd - 1, None), (jd, jd, causal)], cr),
                         lambda cr: group([(jd, jd, causal)], cr), carry)
        dq_ref[...] = (carry[1] * scale).astype(BF16)

        @pl.when(qi == nq - 1)
        def _():
            for j in range(nkb):
                dk_ref[j * K_BLK:(j + 1) * K_BLK, :] = dkt_ref[j].T.astype(BF16)
                dv_ref[j * K_BLK:(j + 1) * K_BLK, :] = dvt_ref[j].T.astype(BF16)

        pl.when(step_no == B * nhp * nq - 1)(wait_ride)

    qblk = pl.BlockSpec((Q_BLK, LANES), lambda b, hp, qi: (b * nq + qi, hp))
    tri = pl.BlockSpec((K_BLK, K_BLK), lambda b, hp, qi: (0, 0))
    seq = pl.BlockSpec((S, LANES), lambda b, hp, qi: (b, hp))
    return pl.pallas_call(
        body, name="attn_bwd", grid=(B, nhp, nq),
        in_specs=[qblk, pl.BlockSpec((S, LANES), lambda b, hp, qi: (b, nhp + hp)),
                  pl.BlockSpec((S, LANES), lambda b, hp, qi: (b, 2 * nhp + hp)),
                  pl.BlockSpec((Q_BLK, 2 * LANES), lambda b, hp, qi: (b * nq + qi, hp)), qblk, tri, tri] + ride_specs,
        out_specs=[qblk, seq, seq] + ride_specs,
        out_shape=[jax.ShapeDtypeStruct((T, SB_W), BF16)] * 3 + ride_shape,
        scratch_shapes=[pltpu.VMEM((nkb, LANES, K_BLK), F32)] * 2 + ride_sems,
        compiler_params=_params(("arbitrary", "arbitrary", "arbitrary"), 48),
    )(qkv, qkv, qkv, run_tab, dy_sb, tri_suffix, tri_prefix, *ride)


def _lru_bwd(proj_lru, h, dy_lru, conv_w, conv_b, wa_bd, wx_bd, b_a, b_x, lam, B, S, lc):
    T = B * S
    nc = S // lc
    ncb = LRU_W // LANES
    hpc = lc // SUBLANES

    def body(x_ref, xh_ref, g_ref, h_ref, hh_ref, dy_ref, cw_ref, cb_ref, wa_ref, wx_ref, ba_ref, bx_ref, lam_ref,
             dx_ref, dg_ref, dcw_ref, dcb_ref, dwa_ref, dwx_ref, dba_ref, dbx_ref, dlam_ref,
             lam_s, dc_s, a_first, lam_first, dc_head):
        b, ci = pl.program_id(1), pl.program_id(2)
        first_chunk = ci == nc - 1

        @pl.when(ci == 0)
        def _():
            a_first[...] = jnp.zeros_like(a_first)
            lam_first[...] = jnp.zeros_like(lam_first)
            dc_head[...] = jnp.zeros_like(dc_head)

        @pl.when((b == 0) & (ci == 0))
        def _():
            for ref in (dcw_ref, dcb_ref, dwa_ref, dwx_ref, dba_ref, dbx_ref, dlam_ref):
                ref[...] = jnp.zeros_like(ref)

        x = x_ref[...]
        taps = _conv_taps(x, jnp.where(first_chunk, 0.0, xh_ref[...]), lc)
        c = cb_ref[...] + sum(cw_ref[pl.ds(CONV_K - 1 - k, 1), :] * taps[k] for k in range(CONV_K))
        r, i, sp, dsp_dlam, a, s = _lru_gates(c, wa_ref, wx_ref, ba_ref, bx_ref, lam_ref)
        hv = h_ref[...]
        he = jnp.concatenate([jnp.where(first_chunk, 0.0, hh_ref[...]), hv], axis=0)
        h_prev = pltpu.roll(he, 1, 0)[SUBLANES:SUBLANES + lc]
        dy = dy_ref[...]
        gelu, dgelu = _gelu_parts(g_ref[...])
        dg_ref[...] = (dy * hv * dgelu).astype(BF16)

        row = lax.broadcasted_iota(jnp.int32, (lc, LANES), 0)
        a_next = jnp.where(row < lc - 1, pltpu.roll(a, lc - 1, 0), a_first[...])
        lam_loc, a_run = _scan(a_next, dy * gelu, reverse=True)
        lam_s[...] = lam_loc + a_run * lam_first[...]
        lam_t = lam_s[...]
        lam_first[...] = lam_s[pl.ds(0, 1), :]
        lam_s[...] = a
        a_first[...] = lam_s[pl.ds(0, 1), :]

        ic = i * c
        dlog_a = lam_t * h_prev * a - (lam_t * ic) * (a * a) / s
        dpre_r = (dlog_a * ((-LRU_C) * sp)) * r * (1.0 - r)
        dpre_i = (lam_t * s * c) * i * (1.0 - i)
        dlam_ref[...] += jnp.sum(dlog_a * r, axis=0, keepdims=True) * ((-LRU_C) * dsp_dlam)
        dr16 = dpre_r.astype(BF16)
        di16 = dpre_i.astype(BF16)
        c16 = c.astype(BF16)
        dwa_ref[0] += _dot_tn(c16, dr16)
        dwx_ref[0] += _dot_tn(c16, di16)
        dba_ref[...] += jnp.sum(dpre_r, axis=0, keepdims=True)
        dbx_ref[...] += jnp.sum(dpre_i, axis=0, keepdims=True)
        dc = lam_t * s * i + _dot_nt(dr16, wa_ref[0]) + _dot_nt(di16, wx_ref[0])
        dcb_ref[...] += jnp.sum(dc, axis=0, keepdims=True)
        for k in range(CONV_K):
            dcw_ref[pl.ds(CONV_K - 1 - k, 1), :] += jnp.sum(dc * taps[k], axis=0, keepdims=True)
        dce = jnp.concatenate([dc, dc_head[...]], axis=0)
        dx = cw_ref[pl.ds(CONV_K - 1, 1), :] * dc
        for k in range(1, CONV_K):
            dx = dx + cw_ref[pl.ds(CONV_K - 1 - k, 1), :] * pltpu.roll(dce, lc + SUBLANES - k, 0)[0:lc]
        dx_ref[...] = dx.astype(BF16)
        dc_s[...] = dc
        dc_head[...] = dc_s[pl.ds(0, SUBLANES), :]

    def chunk(col):
        return pl.BlockSpec((lc, LANES), lambda cb, b, ci: (b * nc + nc - 1 - ci, col(cb)))

    def halo(col):
        return pl.BlockSpec((SUBLANES, LANES),
                            lambda cb, b, ci: (jnp.maximum((b * nc + nc - 1 - ci) * hpc - 1, 0), col(cb)))

    chan = lambda cb, b, ci: (0, cb)
    blk = lambda cb, b, ci: (cb, 0, 0)
    vec = pl.BlockSpec((1, LANES), chan)
    mat = pl.BlockSpec((1, LANES, LANES), blk)
    return pl.pallas_call(
        body, name="lru_bwd", grid=(ncb, B, nc),
        in_specs=[chunk(lambda cb: cb), halo(lambda cb: cb), chunk(lambda cb: ncb + cb),
                  chunk(lambda cb: cb), halo(lambda cb: cb), chunk(lambda cb: cb),
                  pl.BlockSpec((CONV_K, LANES), chan), vec, mat, mat, vec, vec, vec],
        out_specs=[chunk(lambda cb: cb), chunk(lambda cb: cb), pl.BlockSpec((CONV_K, LANES), chan), vec,
                   mat, mat, vec, vec, vec],
        out_shape=[jax.ShapeDtypeStruct((T, LRU_W), BF16), jax.ShapeDtypeStruct((T, LRU_W), BF16),
                   jax.ShapeDtypeStruct((CONV_K, LRU_W), F32), jax.ShapeDtypeStruct((1, LRU_W), F32),
                   jax.ShapeDtypeStruct((ncb, LANES, LANES), F32), jax.ShapeDtypeStruct((ncb, LANES, LANES), F32),
                   jax.ShapeDtypeStruct((1, LRU_W), F32), jax.ShapeDtypeStruct((1, LRU_W), F32),
                   jax.ShapeDtypeStruct((1, LRU_W), F32)],
        scratch_shapes=[pltpu.VMEM((lc, LANES), F32), pltpu.VMEM((lc, LANES), F32), pltpu.VMEM((1, LANES), F32),
                        pltpu.VMEM((1, LANES), F32), pltpu.VMEM((SUBLANES, LANES), F32)],
        compiler_params=_params(("arbitrary", "arbitrary", "arbitrary"), 32),
    )(proj_lru, proj_lru, proj_lru, h, h, dy_lru, conv_w, conv_b, wa_bd, wx_bd, b_a, b_x, lam)


def _bwd_in(pieces, w_in16t, x2, g1, dh1, tm):
    T = x2.shape[0]
    npc = len(pieces)

    def body(*refs):
        p_refs = refs[:npc]
        w_ref, x_ref, g_ref, d_ref, dx_ref, dproj_ref, xn_ref, dg1_ref = refs[npc:]

        @pl.when(pl.program_id(0) == 0)
        def _():
            dg1_ref[...] = jnp.zeros_like(dg1_ref)

        dxn = jnp.zeros((tm, D_MODEL), F32)
        for n, p_ref in enumerate(p_refs):
            cols = slice(n * LRU_W, (n + 1) * LRU_W)
            p16 = p_ref[...]
            dproj_ref[:, cols] = p16
            dxn = dxn + _dot(p16, w_ref[cols, :])
        gv = g_ref[...]
        xn, xhat, rstd = _rms_fwd(x_ref[...], gv)
        xn_ref[...] = xn.astype(BF16)
        dg1_ref[...] += jnp.sum(dxn * xhat, axis=0, keepdims=True)
        dx_ref[...] = d_ref[...] + _rms_bwd(dxn, xhat, rstd, gv)

    row = lambda i: (i, 0)
    fix = lambda i: (0, 0)
    return pl.pallas_call(
        body, name="bwd_in", grid=(T // tm,),
        in_specs=[pl.BlockSpec((tm, LRU_W), row)] * npc + [
            pl.BlockSpec((IN_COLS, D_MODEL), fix), pl.BlockSpec((tm, D_MODEL), row),
            pl.BlockSpec((1, D_MODEL), fix), pl.BlockSpec((tm, D_MODEL), row)],
        out_specs=[pl.BlockSpec((tm, D_MODEL), row), pl.BlockSpec((tm, IN_COLS), row),
                   pl.BlockSpec((tm, D_MODEL), row), pl.BlockSpec((1, D_MODEL), fix)],
        out_shape=[jax.ShapeDtypeStruct((T, D_MODEL), F32), jax.ShapeDtypeStruct((T, IN_COLS), BF16),
                   jax.ShapeDtypeStruct((T, D_MODEL), BF16), jax.ShapeDtypeStruct((1, D_MODEL), F32)],
        compiler_params=_params(("arbitrary",), 56),
    )(*pieces, w_in16t, x2, g1, dh1)


def _adam_shards(name, shards, steps, ride, ride_sliced):
    ns = len(shards)
    nr = len(ride)
    ride_shape, ride_specs, ride_sems = _exchange_shapes(ride, ride_sliced)

    def body(*refs):
        ins, rest = refs[:4 * ns], refs[4 * ns:]
        outs = rest[nr:nr + 4 * ns]
        start_ride, wait_ride = _exchange_copies(rest[:nr], rest[nr + 4 * ns:2 * nr + 4 * ns], ride_sliced,
                                                 *rest[2 * nr + 4 * ns:])
        pl.when(pl.program_id(0) == 0)(start_ride)
        for i in range(ns):
            p_ref, w_ref, m_ref, v_ref = ins[4 * i:4 * i + 4]
            g = p_ref[0]
            for p in range(1, N_DEV):
                g = g + p_ref[p]
            outs[4 * i][...] = g
            outs[4 * i + 1][...], outs[4 * i + 2][...], outs[4 * i + 3][...] = _adamw(w_ref[...], g, m_ref[...], v_ref[...])
        pl.when(pl.program_id(0) == steps - 1)(wait_ride)

    in_specs, out_specs, out_shape, args = [], [], [], []
    for parts, w, m, v in shards:
        R, C = w.shape
        blk = pl.BlockSpec((R // steps, C), lambda i: (i, 0))
        in_specs += [pl.BlockSpec((N_DEV, R // steps, C), lambda i: (0, i, 0)), blk, blk, blk]
        out_specs += [blk] * 4
        out_shape += [jax.ShapeDtypeStruct((R, C), F32)] * 4
        args += [parts, w, m, v]
    res = pl.pallas_call(
        body, name=name, grid=(steps,), in_specs=in_specs + ride_specs, out_specs=out_specs + ride_specs,
        out_shape=out_shape + ride_shape, scratch_shapes=ride_sems,
        compiler_params=_params(("arbitrary",), 48),
    )(*args, *ride)
    return [list(res[4 * i:4 * i + 4]) for i in range(ns)], list(res[4 * ns:])


def _adam_params(name, items):
    def body(*refs):
        ins, outs = refs[:4 * len(items)], refs[4 * len(items):]
        for i in range(len(items)):
            p_ref, w_ref, m_ref, v_ref = ins[4 * i:4 * i + 4]
            g = p_ref[0]
            for p in range(1, N_DEV):
                g = g + p_ref[p]
            outs[4 * i][...] = g
            outs[4 * i + 1][...], outs[4 * i + 2][...], outs[4 * i + 3][...] = _adamw(w_ref[...], g, m_ref[...], v_ref[...])

    args = [a for item in items for a in item]
    outs = [w for _, w, _, _ in items for _ in range(4)]
    res = pl.pallas_call(
        body, name=name, grid=(1,), in_specs=[_whole(a.shape) for a in args], out_specs=[_whole(w.shape) for w in outs],
        out_shape=[jax.ShapeDtypeStruct(w.shape, F32) for w in outs], compiler_params=_params(("arbitrary",), 32),
    )(*args)
    return [list(res[4 * i:4 * i + 4]) for i in range(len(items))]


def _adam_given(name, g, w, m, v, tr):
    R, C = w.shape

    def body(g_ref, w_ref, m_ref, v_ref, d_ref, m2_ref, v2_ref):
        d_ref[...], m2_ref[...], v2_ref[...] = _adamw(w_ref[...], g_ref[...], m_ref[...], v_ref[...])

    blk = pl.BlockSpec((tr, C), lambda i: (i, 0))
    return [g] + list(pl.pallas_call(
        body, name=name, grid=(R // tr,), in_specs=[blk] * 4, out_specs=[blk] * 3,
        out_shape=[jax.ShapeDtypeStruct((R, C), F32)] * 3, compiler_params=_params(("arbitrary",), 32),
    )(g, w, m, v))


def _sum_parts(name, parts):
    def body(p_ref, g_ref):
        g = p_ref[0].astype(F32)
        for p in range(1, N_DEV):
            g = g + p_ref[p].astype(F32)
        g_ref[...] = g

    return pl.pallas_call(
        body, name=name, grid=(1,), in_specs=[_whole(parts.shape)], out_specs=_whole(parts.shape[1:]),
        out_shape=jax.ShapeDtypeStruct(parts.shape[1:], F32), compiler_params=_params(("arbitrary",), 32),
    )(parts)


def _block_diag_pairs(w):
    w = w.reshape(LRU_BLOCKS // 2, 2, HEAD_D, HEAD_D)
    out = jnp.zeros((LRU_BLOCKS // 2, LANES, LANES), w.dtype)
    out = out.at[:, :HEAD_D, :HEAD_D].set(w[:, 0])
    return out.at[:, HEAD_D:, HEAD_D:].set(w[:, 1])


def _diag_blocks(w):
    return jnp.stack([w[:, :HEAD_D, :HEAD_D], w[:, HEAD_D:, HEAD_D:]], axis=1).reshape(LRU_BLOCKS, HEAD_D, HEAD_D)


def kernel(x, norm1_g, w_in, conv_w, conv_b, lru_w_a, lru_b_a, lru_w_x, lru_b_x, lru_lambda, lru_out_g, sb_out_g, w_out, norm2_g, w_up, w_down, final_g, loss_target, m_norm1_g, m_w_in, m_conv_w, m_conv_b, m_lru_w_a, m_lru_b_a, m_lru_w_x, m_lru_b_x, m_lru_lambda, m_lru_out_g, m_sb_out_g, m_w_out, m_norm2_g, m_w_up, m_w_down, m_final_g, v_norm1_g, v_w_in, v_conv_w, v_conv_b, v_lru_w_a, v_lru_b_a, v_lru_w_x, v_lru_b_x, v_lru_lambda, v_lru_out_g, v_sb_out_g, v_w_out, v_norm2_g, v_w_up, v_w_down, v_final_g):
    B, S, _ = x.shape
    T = B * S
    tm = min(512, T)
    tk = min(GRAD_TK, T)
    lc = min(1024, S)
    x2 = x.reshape(T, D_MODEL)
    tgt = loss_target.reshape(T, D_MODEL)
    cw_cols = CONV_K * LRU_W // N_DEV // CONV_K

    shards16 = _cast_shards([w_in[0].T, w_out[0], w_up[0], w_down[0]])
    cw_pad = jnp.zeros((SUBLANES, LANES), F32).at[:CONV_K, :cw_cols].set(conv_w[0])
    g_in, g_cw = _gather_two_level("gather_w_in", [shards16[0], cw_pad])
    w_in16t = g_in.reshape(IN_COLS, D_MODEL)
    conv_w_full = g_cw[:, :CONV_K, :cw_cols].transpose(1, 0, 2).reshape(CONV_K, LRU_W)
    wa_bd = _block_diag_pairs(lru_w_a[0]).astype(BF16)
    wx_bd = _block_diag_pairs(lru_w_x[0]).astype(BF16)
    b_a = lru_b_a.reshape(1, LRU_W)
    b_x = lru_b_x.reshape(1, LRU_W)
    gf = final_g.reshape(1, D_MODEL)

    proj_lru, qkv = _fwd_in(x2, norm1_g, w_in16t, min(1024, T))
    y_lru, h = _lru_fwd(proj_lru, conv_w_full, conv_b, wa_bd, wx_bd, b_a, b_x, lru_lambda, B, S, lc)
    tri_suffix, tri_prefix = _tri(False), _tri(True)
    y_sb, run_tab, g_out, g_up, g_down = _attn_fwd(qkv, tri_suffix, list(shards16[1:]), [False] * 3, B, S)
    w_out16 = g_out.reshape(D_MODEL, D_MODEL)
    w_down16 = g_down.reshape(D_FF, D_MODEL)
    h1, mix16 = _fwd_mix(y_lru, y_sb, lru_out_g, sb_out_g, w_out16, x2, min(1024, T))
    up16, dh2, d_final_g, loss_part = _fwd_mlp(h1, norm2_g, g_up, w_down16, gf, tgt, min(1024, T), 2)

    dup16, dh1, hn16, dh2b, d_norm2_g = _bwd_mlp(dh2, up16, h1, norm2_g, g_up, w_down16, tm, 4)
    sq = lambda u: (u.astype(F32) * u.astype(F32)).astype(BF16)
    wide = 4
    gw_up, = _matmul_tn("grad_w_up", hn16, dup16, D_MODEL, wide * FF_CHUNK, tk // 2, (N_DEV, D_MODEL, FF_CHUNK),
                        (wide, D_MODEL, FF_CHUNK), lambda m, n, k: (n, 0, 0), split=(1, wide))
    gw_down, = _matmul_tn("grad_w_down", up16, dh2b, wide * FF_CHUNK, D_MODEL, tk // 2, (N_DEV, FF_CHUNK, D_MODEL),
                          (wide, FF_CHUNK, D_MODEL), lambda m, n, k: (m, 0, 0), a_prep=sq, split=(0, wide))
    dy_lru, dy_sb, d_lru_out_g, d_sb_out_g = _bwd_mix(dh1, w_out16, y_lru, y_sb, lru_out_g, sb_out_g, min(1024, T))
    gw_out, = _matmul_tn("grad_w_out", mix16, dh1, D_MODEL, FF_CHUNK, tk, (D_MODEL, D_MODEL),
                         (D_MODEL, FF_CHUNK), lambda m, n, k: (0, n), b_prep=lambda u: u.astype(BF16))
    parts_out = gw_out.reshape(N_DEV, D_MODEL // N_DEV, D_MODEL)
    dq, dk, dv, r_out, r_up, r_down = _attn_bwd(qkv, run_tab, dy_sb, tri_suffix, tri_prefix,
                                                [parts_out, gw_up, gw_down], [True] * 3, B, S)
    (dx_lru, dg_lru, d_conv_w, d_conv_b, d_wa, d_wx, d_b_a, d_b_x, d_lambda) = _lru_bwd(
        proj_lru, h, dy_lru, conv_w_full, conv_b, wa_bd, wx_bd, b_a, b_x, lru_lambda, B, S, lc)
    dx, dproj16, xn16, d_norm1_g = _bwd_in([dx_lru, dg_lru, dq, dk, dv], w_in16t, x2, norm1_g, dh1, tm)

    small = {"norm1_g": (d_norm1_g, norm1_g, m_norm1_g, v_norm1_g), "conv_b": (d_conv_b, conv_b, m_conv_b, v_conv_b),
             "lru_w_a": (_diag_blocks(d_wa), lru_w_a, m_lru_w_a, v_lru_w_a),
             "lru_b_a": (d_b_a, lru_b_a, m_lru_b_a, v_lru_b_a),
             "lru_w_x": (_diag_blocks(d_wx), lru_w_x, m_lru_w_x, v_lru_w_x),
             "lru_b_x": (d_b_x, lru_b_x, m_lru_b_x, v_lru_b_x),
             "lru_lambda": (d_lambda, lru_lambda, m_lru_lambda, v_lru_lambda),
             "lru_out_g": (d_lru_out_g, lru_out_g, m_lru_out_g, v_lru_out_g),
             "sb_out_g": (d_sb_out_g, sb_out_g, m_sb_out_g, v_sb_out_g),
             "norm2_g": (d_norm2_g, norm2_g, m_norm2_g, v_norm2_g),
             "final_g": (d_final_g, final_g, m_final_g, v_final_g)}
    names = list(small)

    def held(n, a):
        return a.reshape((1, D_MODEL) if n == "final_g" else small[n][1].shape)

    parts_cw = d_conv_w.reshape(CONV_K, N_DEV, cw_cols).transpose(1, 0, 2)[:, None]
    gw_in_t, *got = _matmul_tn(
        "grad_w_in", dproj16, xn16, IN_COLS // 2, D_MODEL, tk // 2, (IN_COLS, D_MODEL), (IN_COLS // 2, D_MODEL),
        lambda m, n, k: (m, 0), out_dtype=BF16,
        ride=[parts_cw] + [held(n, small[n][0]) for n in names] + [loss_part],
        ride_sliced=[True] + [False] * (len(names) + 1))

    big, (r_in,) = _adam_shards(
        "adam_out_up_down", [(r_out, w_out[0], m_w_out[0], v_w_out[0]), (r_up, w_up[0], m_w_up[0], v_w_up[0]),
                             (r_down, w_down[0], m_w_down[0], v_w_down[0])],
        4, [gw_in_t.reshape(N_DEV, IN_COLS // N_DEV, D_MODEL)], [True])
    g_in_t = _sum_parts("sum_w_in", r_in)
    out = dict(zip(("w_out", "w_up", "w_down"), big))
    out["w_in"] = _adam_given("adam_w_in", g_in_t.T, w_in[0], m_w_in[0], v_w_in[0], 256)
    out = {n: [a[None] for a in res] for n, res in out.items()}
    items = [(got[0], conv_w, m_conv_w, v_conv_w)]
    items += [(parts, *[held(n, a) for a in small[n][1:]]) for n, parts in zip(names, got[1:-1])]
    for n, res in zip(["conv_w"] + names, _adam_params("adam_small", items)):
        out[n] = [a.reshape((conv_w if n == "conv_w" else small[n][1]).shape) for a in res]
    loss = _sum_parts("sum_loss", got[-1])[0, 0]
    weights = ["norm1_g", "w_in", "conv_w", "conv_b", "lru_w_a", "lru_b_a", "lru_w_x", "lru_b_x", "lru_lambda",
               "lru_out_g", "sb_out_g", "w_out", "norm2_g", "w_up", "w_down", "final_g"]
    return (loss, dx.reshape(B, S, D_MODEL), *[out[n][0] for n in weights], *[out[n][1] for n in weights],
            *[out[n][2] for n in weights], *[out[n][3] for n in weights])
```

```python
import jax
import jax.numpy as jnp
from jax import lax
from jax.experimental import pallas as pl
from jax.experimental.pallas import tpu as pltpu

F32 = jnp.float32
BF16 = jnp.bfloat16

D_MODEL = 1024
LRU_W = 512
SB_W = 512
HEAD_D = 64
D_FF = 4096
IN_COLS = 2 * LRU_W + 3 * SB_W
CONV_K = 4
LRU_BLOCKS = 8
LRU_C = 8.0
EPS = 1e-6
N_DEV = 8
LANES = 128
SUBLANES = 8
FF_CHUNK = 512
GRAD_TK = 2048
Q_BLK = 256
K_BLK = 256
Q_PER_K = K_BLK // Q_BLK

ADAM_LR = 0.001
ADAM_B1 = 0.9
ADAM_B2 = 0.999
ADAM_EPS = 1e-08
ADAM_WD = 0.01
ADAM_STEP = 10


def _params(sem=None, vmem_mb=None):
    kw = {}
    if sem is not None:
        kw["dimension_semantics"] = sem
    if vmem_mb is not None:
        kw["vmem_limit_bytes"] = vmem_mb << 20
    return pltpu.CompilerParams(**kw)


def _dot(a, b):
    return jnp.dot(a, b, preferred_element_type=F32)


def _dot_nt(a, b):
    return lax.dot_general(a, b, (((1,), (1,)), ((), ())), preferred_element_type=F32)


def _dot_tn(a, b):
    return lax.dot_general(a, b, (((0,), (0,)), ((), ())), preferred_element_type=F32)


def _rms_fwd(x, g):
    rstd = lax.rsqrt(jnp.mean(x * x, axis=-1, keepdims=True) + EPS)
    xhat = x * rstd
    return xhat * g, xhat, rstd


def _rms_bwd(dy, xhat, rstd, g):
    dxhat = dy * g
    return rstd * (dxhat - xhat * jnp.mean(dxhat * xhat, axis=-1, keepdims=True))


def _sigmoid(x):
    return 1.0 / (1.0 + jnp.exp(-x))


def _log1p_pos(e):
    series = e * (1.0 - e * (0.5 - e * (1.0 / 3.0 - e * 0.25)))
    return jnp.where(e < 1e-2, series, jnp.log(1.0 + e))


def _neg_expm1(x):
    series = -x * (1.0 + x * (0.5 + x * (1.0 / 6.0 + x * (1.0 / 24.0))))
    return jnp.where(x > -1e-2, series, 1.0 - jnp.exp(x))


def _gelu_parts(g):
    k0 = 0.7978845608028654
    k1 = 0.044715
    t = jnp.tanh(k0 * (g + k1 * g * g * g))
    val = 0.5 * g * (1.0 + t)
    grad = 0.5 * (1.0 + t) + 0.5 * g * (1.0 - t * t) * k0 * (1.0 + 3.0 * k1 * g * g)
    return val, grad


def _scan(a, b, reverse):
    n = a.shape[0]
    row = lax.broadcasted_iota(jnp.int32, a.shape, 0)
    s = 1
    while s < n:
        if reverse:
            keep = row < n - s
            shift = n - s
        else:
            keep = row >= s
            shift = s
        bs = jnp.where(keep, pltpu.roll(b, shift, 0), 0.0)
        a_s = jnp.where(keep, pltpu.roll(a, shift, 0), 1.0)
        b = a * bs + b
        a = a * a_s
        s *= 2
    return b, a


def _adamw(w, g, m, v):
    m = ADAM_B1 * m + (1.0 - ADAM_B1) * g
    v = ADAM_B2 * v + (1.0 - ADAM_B2) * (g * g)
    m_hat = m / (1.0 - ADAM_B1 ** ADAM_STEP)
    v_hat = v / (1.0 - ADAM_B2 ** ADAM_STEP)
    delta = -ADAM_LR * (m_hat / (jnp.sqrt(v_hat) + ADAM_EPS) + ADAM_WD * w)
    return delta, m, v


def _my_index():
    return 4 * lax.axis_index("x") + 2 * lax.axis_index("y") + lax.axis_index("c")


def _peer(k):
    x, y, c = lax.axis_index("x"), lax.axis_index("y"), lax.axis_index("c")
    px = 1 - x if (k >> 2) & 1 else x
    py = 1 - y if (k >> 1) & 1 else y
    pc = 1 - c if k & 1 else c
    return (px, py, pc), 4 * px + 2 * py + pc


def _exchange_shapes(srcs, sliced):
    n = len(srcs)
    out_shape = [jax.ShapeDtypeStruct(s.shape if sl else (N_DEV,) + s.shape, s.dtype) for s, sl in zip(srcs, sliced)]
    specs = [pl.BlockSpec(memory_space=pl.ANY)] * n
    sems = [pltpu.SemaphoreType.DMA((n, N_DEV - 1)), pltpu.SemaphoreType.DMA((n, N_DEV - 1)),
            pltpu.SemaphoreType.DMA((n,))]
    return out_shape, specs, sems


def _exchange_copies(ins, outs, sliced, send_sems, recv_sems, local_sems):
    n = len(ins)

    def part(a, p):
        return ins[a].at[p] if sliced[a] else ins[a]

    def copies(receiving):
        me = _my_index()
        local = [pltpu.make_async_copy(part(a, me), outs[a].at[me], local_sems.at[a]) for a in range(n)]
        remote = []
        for k in range(1, N_DEV):
            dev, idx = _peer(k)
            for a in range(n):
                remote.append(pltpu.make_async_remote_copy(
                    src_ref=part(a, idx), dst_ref=outs[a].at[idx if receiving else me],
                    send_sem=send_sems.at[a, k - 1], recv_sem=recv_sems.at[a, k - 1],
                    device_id=dev, device_id_type=pl.DeviceIdType.MESH))
        return local, remote

    def start():
        local, remote = copies(receiving=False)
        for cp in local + remote:
            cp.start()

    def wait():
        local, remote = copies(receiving=True)
        for cp in remote + local:
            cp.wait()

    return start, wait


def _gather_two_level(name, srcs):
    n = len(srcs)
    out_shape = [jax.ShapeDtypeStruct((N_DEV,) + s.shape, s.dtype) for s in srcs]

    def body(*refs):
        ins, outs = refs[:n], refs[n:2 * n]
        send_sems, recv_sems, local_sems = refs[2 * n:]
        x, y, c = lax.axis_index("x"), lax.axis_index("y"), lax.axis_index("c")
        here, sibling = (x, y, c), (x, y, 1 - c)
        chips = [(1 - x, y), (x, 1 - y), (1 - x, 1 - y)]

        def slot(px, py, pc):
            return 4 * px + 2 * py + pc

        def copy(a, k, block, to, src=None):
            return pltpu.make_async_remote_copy(
                src_ref=ins[a] if src is None else src, dst_ref=outs[a].at[slot(*block)],
                send_sem=send_sems.at[a, k], recv_sem=recv_sems.at[a, k], device_id=to,
                device_id_type=pl.DeviceIdType.MESH)

        local = [pltpu.make_async_copy(ins[a], outs[a].at[slot(*here)], local_sems.at[a]) for a in range(n)]
        first = [copy(a, 0, here, sibling) for a in range(n)]
        first += [copy(a, 1 + j, here, (*chip, c)) for j, chip in enumerate(chips) for a in range(n)]
        for cp in local + first:
            cp.start()
        passed = []
        for j, chip in enumerate(chips):
            for a in range(n):
                copy(a, 1 + j, (*chip, c), here).wait_recv()
                passed.append(copy(a, 4 + j, (*chip, c), sibling, src=outs[a].at[slot(*chip, c)]))
                passed[-1].start()
        for a in range(n):
            copy(a, 0, sibling, here).wait_recv()
        for j, chip in enumerate(chips):
            for a in range(n):
                copy(a, 4 + j, (*chip, 1 - c), here).wait_recv()
        for cp in first + passed:
            cp.wait_send()
        for cp in local:
            cp.wait()

    spec = [pl.BlockSpec(memory_space=pl.ANY)] * n
    return pl.pallas_call(
        body, name=name, out_shape=out_shape, in_specs=spec, out_specs=spec,
        scratch_shapes=[pltpu.SemaphoreType.DMA((n, N_DEV - 1)), pltpu.SemaphoreType.DMA((n, N_DEV - 1)),
                        pltpu.SemaphoreType.DMA((n,))],
    )(*srcs)


def _whole(shape):
    return pl.BlockSpec(shape, lambda *_: (0,) * len(shape))


def _cast_shards(ws, steps=4):
    def body(*refs):
        for i in range(len(ws)):
            refs[len(ws) + i][...] = refs[i][...].astype(BF16)

    specs = [pl.BlockSpec((w.shape[0] // steps, w.shape[1]), lambda i: (i, 0)) for w in ws]
    return pl.pallas_call(
        body, name="cast_shards", grid=(steps,), in_specs=specs, out_specs=specs,
        out_shape=[jax.ShapeDtypeStruct(w.shape, BF16) for w in ws],
        compiler_params=_params(("arbitrary",), 32),
    )(*ws)


def _fwd_in(x2, g1, w_in16t, tm):
    T = x2.shape[0]

    def body(x_ref, g_ref, w_ref, lru_ref, qkv_ref):
        xn, _, _ = _rms_fwd(x_ref[...], g_ref[...])
        xn = xn.astype(BF16)
        lru_ref[...] = _dot_nt(xn, w_ref[0:2 * LRU_W, :])
        qkv_ref[...] = _dot_nt(xn, w_ref[2 * LRU_W:IN_COLS, :]).astype(BF16)

    return pl.pallas_call(
        body, name="fwd_in", grid=(T // tm,),
        in_specs=[pl.BlockSpec((tm, D_MODEL), lambda i: (i, 0)),
                  pl.BlockSpec((1, D_MODEL), lambda i: (0, 0)),
                  pl.BlockSpec((IN_COLS, D_MODEL), lambda i: (0, 0))],
        out_specs=[pl.BlockSpec((tm, 2 * LRU_W), lambda i: (i, 0)),
                   pl.BlockSpec((tm, 3 * SB_W), lambda i: (i, 0))],
        out_shape=[jax.ShapeDtypeStruct((T, 2 * LRU_W), F32), jax.ShapeDtypeStruct((T, 3 * SB_W), BF16)],
        compiler_params=_params(("arbitrary",), 48),
    )(x2, g1, w_in16t)


def _lru_gates(c, wa_ref, wx_ref, ba_ref, bx_ref, lam_ref):
    c16 = c.astype(BF16)
    r = _sigmoid(_dot(c16, wa_ref[0]) + ba_ref[...])
    i = _sigmoid(_dot(c16, wx_ref[0]) + bx_ref[...])
    lam = lam_ref[...]
    e = jnp.exp(-jnp.abs(lam))
    sp = jnp.maximum(-lam, 0.0) + _log1p_pos(e)
    dsp_dlam = -jnp.where(lam >= 0.0, e, 1.0) / (1.0 + e)
    log_a = (-LRU_C) * r * sp
    a = jnp.exp(log_a)
    s = jnp.sqrt(_neg_expm1(2.0 * log_a))
    return r, i, sp, dsp_dlam, a, s


def _conv_taps(x, halo, lc):
    xe = jnp.concatenate([halo, x], axis=0)
    return [x] + [pltpu.roll(xe, k, 0)[SUBLANES:SUBLANES + lc] for k in range(1, CONV_K)]


def _lru_fwd(proj_lru, conv_w, conv_b, wa_bd, wx_bd, b_a, b_x, lam, B, S, lc):
    T = B * S
    nc = S // lc
    ncb = LRU_W // LANES

    def body(x_ref, g_ref, cw_ref, cb_ref, wa_ref, wx_ref, ba_ref, bx_ref, lam_ref, y_ref, h_ref, tail, carry):
        ci = pl.program_id(2)

        @pl.when(ci == 0)
        def _():
            tail[...] = jnp.zeros_like(tail)
            carry[...] = jnp.zeros_like(carry)

        x = x_ref[...]
        taps = _conv_taps(x, tail[...], lc)
        c = cb_ref[...] + sum(cw_ref[pl.ds(CONV_K - 1 - k, 1), :] * taps[k] for k in range(CONV_K))
        tail[...] = x_ref[pl.ds(lc - SUBLANES, SUBLANES), :]
        r, i, sp, _, a, s = _lru_gates(c, wa_ref, wx_ref, ba_ref, bx_ref, lam_ref)
        h_loc, a_run = _scan(a, s * (i * c), reverse=False)
        h_ref[...] = h_loc + a_run * carry[...]
        carry[...] = h_ref[pl.ds(lc - 1, 1), :]
        gelu, _ = _gelu_parts(g_ref[...])
        y_ref[...] = h_ref[...] * gelu

    chan = lambda b, cb, ci: (0, cb)
    return pl.pallas_call(
        body, name="lru_fwd", grid=(B, ncb, nc),
        in_specs=[pl.BlockSpec((lc, LANES), lambda b, cb, ci: (b * nc + ci, cb)),
                  pl.BlockSpec((lc, LANES), lambda b, cb, ci: (b * nc + ci, ncb + cb)),
                  pl.BlockSpec((CONV_K, LANES), chan), pl.BlockSpec((1, LANES), chan),
                  pl.BlockSpec((1, LANES, LANES), lambda b, cb, ci: (cb, 0, 0)),
                  pl.BlockSpec((1, LANES, LANES), lambda b, cb, ci: (cb, 0, 0)),
                  pl.BlockSpec((1, LANES), chan), pl.BlockSpec((1, LANES), chan), pl.BlockSpec((1, LANES), chan)],
        out_specs=[pl.BlockSpec((lc, LANES), lambda b, cb, ci: (b * nc + ci, cb))] * 2,
        out_shape=[jax.ShapeDtypeStruct((T, LRU_W), F32)] * 2,
        scratch_shapes=[pltpu.VMEM((SUBLANES, LANES), F32), pltpu.VMEM((1, LANES), F32)],
        compiler_params=_params(("arbitrary", "arbitrary", "arbitrary"), 32),
    )(proj_lru, proj_lru, conv_w, conv_b, wa_bd, wx_bd, b_a, b_x, lam)


def _tri(prefix):
    r = lax.broadcasted_iota(jnp.int32, (K_BLK, K_BLK), 0)
    c = lax.broadcasted_iota(jnp.int32, (K_BLK, K_BLK), 1)
    return ((r <= c) if prefix else (r >= c)).astype(BF16)


def _attn_consts(qi):
    r = lax.broadcasted_iota(jnp.int32, (Q_BLK, K_BLK), 0)
    c = lax.broadcasted_iota(jnp.int32, (Q_BLK, K_BLK), 1)
    causal = c + ((qi // Q_PER_K) * K_BLK - qi * Q_BLK) < r
    lane = lax.broadcasted_iota(jnp.int32, (1, LANES), 1)
    return causal, lane, (lane < HEAD_D, lane >= HEAD_D)


def _log1m_beta(z, mask):
    lg = -(jnp.maximum(z, 0.0) + jnp.log(1.0 + jnp.exp(-jnp.abs(z))))
    return lg if mask is None else jnp.where(mask, lg, 0.0)


def _key_rows(j):
    return pl.ds(pl.multiple_of(j * K_BLK, K_BLK), K_BLK)


def _attn_fwd(qkv, tri_suffix, ride, ride_sliced, B, S):
    T = B * S
    nq = S // Q_BLK
    nhp = SB_W // LANES
    scale = HEAD_D ** -0.5
    assert S // K_BLK <= LANES
    nr = len(ride)
    ride_shape, ride_specs, ride_sems = _exchange_shapes(ride, ride_sliced)

    def body(q_ref, k_ref, v_ref, tri_ref, *rest):
        o_ref, run_ref = rest[nr:nr + 2]
        start_ride, wait_ride = _exchange_copies(rest[:nr], rest[nr + 2:2 * nr + 2], ride_sliced, *rest[2 * nr + 2:])
        qi = pl.program_id(2)
        step_no = (pl.program_id(0) * nhp + pl.program_id(1)) * nq + qi
        pl.when(step_no == 0)(start_ride)
        jd = qi // Q_PER_K
        causal, lane, halves = _attn_consts(qi)
        q = q_ref[...]
        qh = [jnp.where(hm, q, jnp.zeros_like(q)) * jnp.asarray(scale, BF16) for hm in halves]

        def group(blocks, carry):
            runs, tables, acc = carry
            runs, tables = list(runs), list(tables)
            ks = [k_ref[_key_rows(jl), :] for jl, _, _ in blocks]
            vs = [v_ref[_key_rows(jl), :] for jl, _, _ in blocks]
            chains = [(b, h) for b in range(len(blocks)) for h in range(2)]
            z = {c: _dot_nt(qh[c[1]], ks[c[0]]) for c in chains}
            lg = {c: _log1m_beta(z[c], blocks[c[0]][2]) for c in chains}
            suf = {c: _dot(lg[c].astype(BF16), tri_ref[...]) for c in chains}
            att = {}
            for b, h in chains:
                _, jlane, mask = blocks[b]
                a = jnp.exp(z[b, h] + suf[b, h] + runs[h])
                att[b, h] = (a if mask is None else jnp.where(mask, a, 0.0)).astype(BF16)
                tables[h] = jnp.where(lane == jlane, runs[h], tables[h])
                runs[h] = runs[h] + suf[b, h][:, 0:1]
            for b, h in chains:
                acc = acc + _dot(att[b, h], jnp.where(halves[h], vs[b], jnp.zeros_like(vs[b])))
            return tuple(runs), tuple(tables), acc

        col0 = jnp.zeros((Q_BLK, 1), F32)
        zero = jnp.zeros((Q_BLK, LANES), F32)
        start = ((col0, col0), (zero, zero), zero)
        two = jd % 2
        carry = lax.cond(two == 1, lambda: group([(jd, jd, causal), (jd - 1, jd - 1, None)], start),
                         lambda: group([(jd, jd, causal)], start))
        top = jd - 1 - two

        def run(ja, n):
            return [(ja - i, ja - i, None) for i in range(n)]

        left = top + 1
        pair, four = (left % 6 == 2).astype(jnp.int32), (left % 6 == 4).astype(jnp.int32)
        carry = lax.cond(pair == 1, lambda cr: group(run(top, 2), cr), lambda cr: cr, carry)
        carry = lax.cond(four == 1, lambda cr: group(run(top, 4), cr), lambda cr: cr, carry)
        top6 = top - 2 * pair - 4 * four
        _, tables, acc = lax.fori_loop(0, left // 6, lambda it, cr: group(run(top6 - 6 * it, 6), cr), carry)
        o_ref[...] = acc
        run_ref[:, 0:LANES] = tables[0]
        run_ref[:, LANES:2 * LANES] = tables[1]
        pl.when(step_no == B * nhp * nq - 1)(wait_ride)

    return pl.pallas_call(
        body, name="attn_fwd", grid=(B, nhp, nq),
        in_specs=[pl.BlockSpec((Q_BLK, LANES), lambda b, hp, qi: (b * nq + qi, hp)),
                  pl.BlockSpec((S, LANES), lambda b, hp, qi: (b, nhp + hp)),
                  pl.BlockSpec((S, LANES), lambda b, hp, qi: (b, 2 * nhp + hp)),
                  pl.BlockSpec((K_BLK, K_BLK), lambda b, hp, qi: (0, 0))] + ride_specs,
        out_specs=[pl.BlockSpec((Q_BLK, LANES), lambda b, hp, qi: (b * nq + qi, hp)),
                   pl.BlockSpec((Q_BLK, 2 * LANES), lambda b, hp, qi: (b * nq + qi, hp))] + ride_specs,
        out_shape=[jax.ShapeDtypeStruct((T, SB_W), F32), jax.ShapeDtypeStruct((T, 2 * SB_W), F32)] + ride_shape,
        scratch_shapes=ride_sems,
        compiler_params=_params(("arbitrary", "arbitrary", "arbitrary"), 56),
    )(qkv, qkv, qkv, tri_suffix, *ride)


def _fwd_mix(y_lru, y_sb, ga, gb, w_out16, x2, tm):
    T = x2.shape[0]

    def body(yl_ref, ys_ref, ga_ref, gb_ref, w_ref, x_ref, h1_ref, mix_ref):
        na, _, _ = _rms_fwd(yl_ref[...], ga_ref[...])
        nb, _, _ = _rms_fwd(ys_ref[...], gb_ref[...])
        na = na.astype(BF16)
        nb = nb.astype(BF16)
        mix_ref[:, 0:LRU_W] = na
        mix_ref[:, LRU_W:D_MODEL] = nb
        h1_ref[...] = x_ref[...] + _dot(na, w_ref[0:LRU_W, :]) + _dot(nb, w_ref[LRU_W:D_MODEL, :])

    row = lambda i: (i, 0)
    fix = lambda i: (0, 0)
    return pl.pallas_call(
        body, name="fwd_mix", grid=(T // tm,),
        in_specs=[pl.BlockSpec((tm, LRU_W), row), pl.BlockSpec((tm, SB_W), row),
                  pl.BlockSpec((1, LRU_W), fix), pl.BlockSpec((1, SB_W), fix),
                  pl.BlockSpec((D_MODEL, D_MODEL), fix), pl.BlockSpec((tm, D_MODEL), row)],
        out_specs=[pl.BlockSpec((tm, D_MODEL), row), pl.BlockSpec((tm, D_MODEL), row)],
        out_shape=[jax.ShapeDtypeStruct((T, D_MODEL), F32), jax.ShapeDtypeStruct((T, D_MODEL), BF16)],
        compiler_params=_params(("arbitrary",), 48),
    )(y_lru, y_sb, ga, gb, w_out16, x2)


def _fwd_mlp(h1, g2, w_up16, w_down16, gf, tgt, tm, per_step):
    T = h1.shape[0]
    nf = D_FF // (FF_CHUNK * per_step)

    def body(h1_ref, g2_ref, wu_ref, wd_ref, gf_ref, t_ref, up_ref, dh2_ref, dgf_ref, loss_ref, hn_s, acc):
        i, j = pl.program_id(0), pl.program_id(1)

        @pl.when(j == 0)
        def _():
            h1v = h1_ref[...]
            hn, _, _ = _rms_fwd(h1v, g2_ref[...])
            hn_s[...] = hn.astype(BF16)
            acc[...] = h1v

        down = None
        for c in range(per_step):
            cols = slice(c * FF_CHUNK, (c + 1) * FF_CHUNK)
            up = jnp.maximum(_dot(hn_s[...], wu_ref[c]), 0.0)
            up_ref[:, cols] = up.astype(BF16)
            part = _dot((up * up).astype(BF16), wd_ref[cols, :])
            down = part if down is None else down + part
        acc[...] += down

        @pl.when((i == 0) & (j == 0))
        def _():
            dgf_ref[...] = jnp.zeros_like(dgf_ref)
            loss_ref[...] = jnp.zeros_like(loss_ref)

        @pl.when(j == nf - 1)
        def _():
            gfv = gf_ref[...]
            y, xhat, rstd = _rms_fwd(acc[...], gfv)
            err = y - t_ref[...]
            loss_ref[...] += jnp.sum(0.5 * jnp.sum(err * err, axis=-1, keepdims=True) * (1.0 / D_MODEL))
            dy = err * (1.0 / D_MODEL)
            dgf_ref[...] += jnp.sum(dy * xhat, axis=0, keepdims=True)
            dh2_ref[...] = _rms_bwd(dy, xhat, rstd, gfv)

    row = lambda i, j: (i, 0)
    fix = lambda i, j: (0, 0)
    return pl.pallas_call(
        body, name="fwd_mlp", grid=(T // tm, nf),
        in_specs=[pl.BlockSpec((tm, D_MODEL), row), pl.BlockSpec((1, D_MODEL), fix),
                  pl.BlockSpec((per_step, D_MODEL, FF_CHUNK), lambda i, j: (j, 0, 0)),
                  pl.BlockSpec((per_step * FF_CHUNK, D_MODEL), lambda i, j: (j, 0)),
                  pl.BlockSpec((1, D_MODEL), fix), pl.BlockSpec((tm, D_MODEL), row)],
        out_specs=[pl.BlockSpec((tm, per_step * FF_CHUNK), lambda i, j: (i, j)), pl.BlockSpec((tm, D_MODEL), row),
                   pl.BlockSpec((1, D_MODEL), fix), pl.BlockSpec((1, LANES), fix)],
        out_shape=[jax.ShapeDtypeStruct((T, D_FF), BF16), jax.ShapeDtypeStruct((T, D_MODEL), F32),
                   jax.ShapeDtypeStruct((1, D_MODEL), F32), jax.ShapeDtypeStruct((1, LANES), F32)],
        scratch_shapes=[pltpu.VMEM((tm, D_MODEL), BF16), pltpu.VMEM((tm, D_MODEL), F32)],
        compiler_params=_params(("arbitrary", "arbitrary"), 56),
    )(h1, g2, w_up16, w_down16, gf, tgt)


def _bwd_mlp(dh2, up16, h1, g2, w_up16, w_down16, tm, per_step):
    T = h1.shape[0]
    nf = D_FF // (FF_CHUNK * per_step)

    def body(dh2_ref, up_ref, h1_ref, g2_ref, wu_ref, wd_ref, dup_ref, dh1_ref, hn_ref, dh2b_ref, dg2_ref, acc):
        i, j = pl.program_id(0), pl.program_id(1)

        @pl.when(j == 0)
        def _():
            hn, _, _ = _rms_fwd(h1_ref[...], g2_ref[...])
            hn_ref[...] = hn.astype(BF16)
            dh2b_ref[...] = dh2_ref[...].astype(BF16)
            acc[...] = jnp.zeros_like(acc)

        dhn = None
        for c in range(per_step):
            cols = slice(c * FF_CHUNK, (c + 1) * FF_CHUNK)
            u = up_ref[:, cols].astype(F32)
            dup = (2.0 * u * _dot_nt(dh2b_ref[...], wd_ref[cols, :])).astype(BF16)
            dup_ref[:, cols] = dup
            part = _dot_nt(dup, wu_ref[c])
            dhn = part if dhn is None else dhn + part
        acc[...] += dhn

        @pl.when((i == 0) & (j == 0))
        def _():
            dg2_ref[...] = jnp.zeros_like(dg2_ref)

        @pl.when(j == nf - 1)
        def _():
            g2v = g2_ref[...]
            _, xhat, rstd = _rms_fwd(h1_ref[...], g2v)
            dhn = acc[...]
            dg2_ref[...] += jnp.sum(dhn * xhat, axis=0, keepdims=True)
            dh1_ref[...] = dh2_ref[...] + _rms_bwd(dhn, xhat, rstd, g2v)

    row = lambda i, j: (i, 0)
    fix = lambda i, j: (0, 0)
    return pl.pallas_call(
        body, name="bwd_mlp", grid=(T // tm, nf),
        in_specs=[pl.BlockSpec((tm, D_MODEL), row), pl.BlockSpec((tm, per_step * FF_CHUNK), lambda i, j: (i, j)),
                  pl.BlockSpec((tm, D_MODEL), row), pl.BlockSpec((1, D_MODEL), fix),
                  pl.BlockSpec((per_step, D_MODEL, FF_CHUNK), lambda i, j: (j, 0, 0)),
                  pl.BlockSpec((per_step * FF_CHUNK, D_MODEL), lambda i, j: (j, 0))],
        out_specs=[pl.BlockSpec((tm, per_step * FF_CHUNK), lambda i, j: (i, j)), pl.BlockSpec((tm, D_MODEL), row),
                   pl.BlockSpec((tm, D_MODEL), row), pl.BlockSpec((tm, D_MODEL), row),
                   pl.BlockSpec((1, D_MODEL), fix)],
        out_shape=[jax.ShapeDtypeStruct((T, D_FF), BF16), jax.ShapeDtypeStruct((T, D_MODEL), F32),
                   jax.ShapeDtypeStruct((T, D_MODEL), BF16), jax.ShapeDtypeStruct((T, D_MODEL), BF16),
                   jax.ShapeDtypeStruct((1, D_MODEL), F32)],
        scratch_shapes=[pltpu.VMEM((tm, D_MODEL), F32)],
        compiler_params=_params(("arbitrary", "arbitrary"), 48),
    )(dh2, up16, h1, g2, w_up16, w_down16)


def _matmul_tn(name, a, b, bm, bn, tk, out_shape, out_block, out_index, a_prep=None, b_prep=None, out_dtype=F32,
               ride=None, ride_sliced=None, split=None):
    T, M = a.shape
    N = b.shape[1]
    grid = (M // bm, N // bn, T // tk)
    ride, ride_sliced = list(ride or []), list(ride_sliced or [])
    nr = len(ride)
    ride_shape, ride_specs, ride_sems = _exchange_shapes(ride, ride_sliced)

    def body(a_ref, b_ref, *rest):
        o_ref, acc = rest[nr], rest[2 * nr + 1]
        k = pl.program_id(2)
        step_no = (pl.program_id(0) * grid[1] + pl.program_id(1)) * grid[2] + k
        if nr:
            start_ride, wait_ride = _exchange_copies(rest[:nr], rest[nr + 1:2 * nr + 1], ride_sliced,
                                                     *rest[2 * nr + 2:])
            pl.when(step_no == 0)(start_ride)
        av = a_ref[...] if a_prep is None else a_prep(a_ref[...])
        bv = b_ref[...] if b_prep is None else b_prep(b_ref[...])
        p = _dot_tn(av, bv)

        @pl.when(k == 0)
        def _():
            acc[...] = p

        @pl.when(k > 0)
        def _():
            acc[...] += p

        @pl.when(k == grid[2] - 1)
        def _():
            if split is None:
                o_ref[...] = acc[...].astype(out_dtype)
            else:
                axis, n = split
                w = (bm, bn)[axis] // n
                for c in range(n):
                    slab = acc[c * w:(c + 1) * w, :] if axis == 0 else acc[:, c * w:(c + 1) * w]
                    o_ref[c] = slab.astype(out_dtype)

        if nr:
            pl.when(step_no == grid[0] * grid[1] * grid[2] - 1)(wait_ride)

    return pl.pallas_call(
        body, name=name, grid=grid,
        in_specs=[pl.BlockSpec((tk, bm), lambda m, n, k: (k, m)), pl.BlockSpec((tk, bn), lambda m, n, k: (k, n))]
        + ride_specs,
        out_specs=[pl.BlockSpec(out_block, out_index)] + ride_specs,
        out_shape=[jax.ShapeDtypeStruct(out_shape, out_dtype)] + ride_shape,
        scratch_shapes=[pltpu.VMEM((bm, bn), F32)] + (ride_sems if nr else []),
        compiler_params=_params(("arbitrary", "arbitrary", "arbitrary"), 48),
    )(a, b, *ride)


def _bwd_mix(dh1, w_out16, y_lru, y_sb, ga, gb, tm):
    T = dh1.shape[0]

    def body(d_ref, w_ref, yl_ref, ys_ref, ga_ref, gb_ref, dyl_ref, dys_ref, dga_ref, dgb_ref):
        @pl.when(pl.program_id(0) == 0)
        def _():
            dga_ref[...] = jnp.zeros_like(dga_ref)
            dgb_ref[...] = jnp.zeros_like(dgb_ref)

        d16 = d_ref[...].astype(BF16)
        for y_ref, g_ref, lo, dy_ref, dg_ref in ((yl_ref, ga_ref, 0, dyl_ref, dga_ref),
                                                 (ys_ref, gb_ref, LRU_W, dys_ref, dgb_ref)):
            gv = g_ref[...]
            dn = _dot_nt(d16, w_ref[lo:lo + LRU_W, :])
            _, xhat, rstd = _rms_fwd(y_ref[...], gv)
            dg_ref[...] += jnp.sum(dn * xhat, axis=0, keepdims=True)
            dy_ref[...] = _rms_bwd(dn, xhat, rstd, gv).astype(dy_ref.dtype)

    row = lambda i: (i, 0)
    fix = lambda i: (0, 0)
    return pl.pallas_call(
        body, name="bwd_mix", grid=(T // tm,),
        in_specs=[pl.BlockSpec((tm, D_MODEL), row), pl.BlockSpec((D_MODEL, D_MODEL), fix),
                  pl.BlockSpec((tm, LRU_W), row), pl.BlockSpec((tm, SB_W), row),
                  pl.BlockSpec((1, LRU_W), fix), pl.BlockSpec((1, SB_W), fix)],
        out_specs=[pl.BlockSpec((tm, LRU_W), row), pl.BlockSpec((tm, SB_W), row),
                   pl.BlockSpec((1, LRU_W), fix), pl.BlockSpec((1, SB_W), fix)],
        out_shape=[jax.ShapeDtypeStruct((T, LRU_W), F32), jax.ShapeDtypeStruct((T, SB_W), BF16),
                   jax.ShapeDtypeStruct((1, LRU_W), F32), jax.ShapeDtypeStruct((1, SB_W), F32)],
        compiler_params=_params(("arbitrary",), 48),
    )(dh1, w_out16, y_lru, y_sb, ga, gb)


def _attn_bwd(qkv, run_tab, dy_sb, tri_suffix, tri_prefix, ride, ride_sliced, B, S):
    T = B * S
    nq = S // Q_BLK
    nkb = S // K_BLK
    nhp = SB_W // LANES
    scale = HEAD_D ** -0.5

    nr = len(ride)
    ride_shape, ride_specs, ride_sems = _exchange_shapes(ride, ride_sliced)

    def body(q_ref, k_ref, v_ref, run_ref, do_ref, ts_ref, tp_ref, *rest):
        dq_ref, dk_ref, dv_ref = rest[nr:nr + 3]
        dkt_ref, dvt_ref = rest[2 * nr + 3:2 * nr + 5]
        start_ride, wait_ride = _exchange_copies(rest[:nr], rest[nr + 3:2 * nr + 3], ride_sliced, *rest[2 * nr + 5:])
        qi = pl.program_id(2)
        step_no = (pl.program_id(0) * nhp + pl.program_id(1)) * nq + qi
        pl.when(step_no == 0)(start_ride)
        jd = qi // Q_PER_K

        @pl.when(qi == 0)
        def _():
            dkt_ref[...] = jnp.zeros_like(dkt_ref)
            dvt_ref[...] = jnp.zeros_like(dvt_ref)

        causal, lane, halves = _attn_consts(qi)
        q = q_ref[...]
        do = do_ref[...]
        qh = [jnp.where(hm, q, jnp.zeros_like(q)) * jnp.asarray(scale, BF16) for hm in halves]
        doh = [jnp.where(hm, do, jnp.zeros_like(do)) for hm in halves]
        qt = jnp.concatenate([h.astype(F32).T.astype(BF16) for h in qh], axis=1)
        dot_ = jnp.concatenate([h.astype(F32).T.astype(BF16) for h in doh], axis=1)
        tables = [run_ref[:, 0:LANES], run_ref[:, LANES:2 * LANES]]

        def group(blocks, carry):
            prefixes, dq = carry
            prefixes = list(prefixes)
            ks = [k_ref[_key_rows(jl), :] for jl, _, _ in blocks]
            vs = [v_ref[_key_rows(jl), :] for jl, _, _ in blocks]
            chains = [(b, h) for b in range(len(blocks)) for h in range(2)]
            z = {c: _dot_nt(qh[c[1]], ks[c[0]]) for c in chains}
            da = {c: _dot_nt(doh[c[1]], vs[c[0]]) for c in chains}
            lg = {c: _log1m_beta(z[c], blocks[c[0]][2]) for c in chains}
            suf = {c: _dot(lg[c].astype(BF16), ts_ref[...]) for c in chains}
            att, g = {}, {}
            for b, h in chains:
                _, jlane, mask = blocks[b]
                run = jnp.sum(jnp.where(lane == jlane, tables[h], 0.0), axis=1, keepdims=True)
                a = jnp.exp(z[b, h] + suf[b, h] + run)
                a = a if mask is None else jnp.where(mask, a, 0.0)
                g[b, h] = a * da[b, h]
                att[b, h] = a.astype(BF16)
            gpre = {c: _dot(g[c].astype(BF16), tp_ref[...]) for c in chains}
            dz = {}
            for b, h in chains:
                mask = blocks[b][2]
                d = g[b, h] - jnp.exp(z[b, h] + lg[b, h]) * (prefixes[h] + gpre[b, h])
                dz[b, h] = (d if mask is None else jnp.where(mask, d, 0.0)).astype(BF16)
                prefixes[h] = prefixes[h] + gpre[b, h][:, K_BLK - 1:K_BLK]
            for b, h in chains:
                dq = dq + _dot(dz[b, h], jnp.where(halves[h], ks[b], jnp.zeros_like(ks[b])))
            for b, (jl, _, _) in enumerate(blocks):
                dkt_ref[jl] += _dot(qt, jnp.concatenate([dz[b, 0], dz[b, 1]], axis=0))
                dvt_ref[jl] += _dot(dot_, jnp.concatenate([att[b, 0], att[b, 1]], axis=0))
            return tuple(prefixes), dq

        def run(ja, n):
            return [(ja + i, ja + i, None) for i in range(n)]

        col0 = jnp.zeros((Q_BLK, 1), F32)
        left = 2 * (jd // 2)
        sixes = left // 6
        carry = lax.fori_loop(0, sixes, lambda it, cr: group(run(6 * it, 6), cr),
                              ((col0, col0), jnp.zeros((Q_BLK, LANES), F32)))
        carry = lax.cond(left % 6 == 4, lambda cr: group(run(6 * sixes, 4), cr), lambda cr: cr, carry)
        carry = lax.cond(left % 6 == 2, lambda cr: group(run(6 * sixes, 2), cr), lambda cr: cr, carry)
        carry = lax.cond(jd % 2 == 1, lambda cr: group([(jd - 1, jd - 1, None), (jd, jd, causal)], cr),
                         lambda cr: group([(jd, jd, causal)], cr), carry)
        dq_ref[...] = (carry[1] * scale).astype(BF16)

        @pl.when(qi == nq - 1)
        def _():
            for j in range(nkb):
                dk_ref[j * K_BLK:(j + 1) * K_BLK, :] = dkt_ref[j].T.astype(BF16)
                dv_ref[j * K_BLK:(j + 1) * K_BLK, :] = dvt_ref[j].T.astype(BF16)

        pl.when(step_no == B * nhp * nq - 1)(wait_ride)

    qblk = pl.BlockSpec((Q_BLK, LANES), lambda b, hp, qi: (b * nq + qi, hp))
    tri = pl.BlockSpec((K_BLK, K_BLK), lambda b, hp, qi: (0, 0))
    seq = pl.BlockSpec((S, LANES), lambda b, hp, qi: (b, hp))
    return pl.pallas_call(
        body, name="attn_bwd", grid=(B, nhp, nq),
        in_specs=[qblk, pl.BlockSpec((S, LANES), lambda b, hp, qi: (b, nhp + hp)),
                  pl.BlockSpec((S, LANES), lambda b, hp, qi: (b, 2 * nhp + hp)),
                  pl.BlockSpec((Q_BLK, 2 * LANES), lambda b, hp, qi: (b * nq + qi, hp)), qblk, tri, tri] + ride_specs,
        out_specs=[qblk, seq, seq] + ride_specs,
        out_shape=[jax.ShapeDtypeStruct((T, SB_W), BF16)] * 3 + ride_shape,
        scratch_shapes=[pltpu.VMEM((nkb, LANES, K_BLK), F32)] * 2 + ride_sems,
        compiler_params=_params(("arbitrary", "arbitrary", "arbitrary"), 56),
    )(qkv, qkv, qkv, run_tab, dy_sb, tri_suffix, tri_prefix, *ride)


def _lru_bwd(proj_lru, h, dy_lru, conv_w, conv_b, wa_bd, wx_bd, b_a, b_x, lam, B, S, lc):
    T = B * S
    nc = S // lc
    ncb = LRU_W // LANES
    hpc = lc // SUBLANES

    def body(x_ref, xh_ref, g_ref, h_ref, hh_ref, dy_ref, cw_ref, cb_ref, wa_ref, wx_ref, ba_ref, bx_ref, lam_ref,
             dx_ref, dg_ref, dcw_ref, dcb_ref, dwa_ref, dwx_ref, dba_ref, dbx_ref, dlam_ref,
             lam_s, dc_s, a_first, lam_first, dc_head):
        b, ci = pl.program_id(1), pl.program_id(2)
        first_chunk = ci == nc - 1

        @pl.when(ci == 0)
        def _():
            a_first[...] = jnp.zeros_like(a_first)
            lam_first[...] = jnp.zeros_like(lam_first)
            dc_head[...] = jnp.zeros_like(dc_head)

        @pl.when((b == 0) & (ci == 0))
        def _():
            for ref in (dcw_ref, dcb_ref, dwa_ref, dwx_ref, dba_ref, dbx_ref, dlam_ref):
                ref[...] = jnp.zeros_like(ref)

        x = x_ref[...]
        taps = _conv_taps(x, jnp.where(first_chunk, 0.0, xh_ref[...]), lc)
        c = cb_ref[...] + sum(cw_ref[pl.ds(CONV_K - 1 - k, 1), :] * taps[k] for k in range(CONV_K))
        r, i, sp, dsp_dlam, a, s = _lru_gates(c, wa_ref, wx_ref, ba_ref, bx_ref, lam_ref)
        hv = h_ref[...]
        he = jnp.concatenate([jnp.where(first_chunk, 0.0, hh_ref[...]), hv], axis=0)
        h_prev = pltpu.roll(he, 1, 0)[SUBLANES:SUBLANES + lc]
        dy = dy_ref[...]
        gelu, dgelu = _gelu_parts(g_ref[...])
        dg_ref[...] = (dy * hv * dgelu).astype(BF16)

        row = lax.broadcasted_iota(jnp.int32, (lc, LANES), 0)
        a_next = jnp.where(row < lc - 1, pltpu.roll(a, lc - 1, 0), a_first[...])
        lam_loc, a_run = _scan(a_next, dy * gelu, reverse=True)
        lam_s[...] = lam_loc + a_run * lam_first[...]
        lam_t = lam_s[...]
        lam_first[...] = lam_s[pl.ds(0, 1), :]
        lam_s[...] = a
        a_first[...] = lam_s[pl.ds(0, 1), :]

        ic = i * c
        dlog_a = lam_t * h_prev * a - (lam_t * ic) * (a * a) / s
        dpre_r = (dlog_a * ((-LRU_C) * sp)) * r * (1.0 - r)
        dpre_i = (lam_t * s * c) * i * (1.0 - i)
        dlam_ref[...] += jnp.sum(dlog_a * r, axis=0, keepdims=True) * ((-LRU_C) * dsp_dlam)
        dr16 = dpre_r.astype(BF16)
        di16 = dpre_i.astype(BF16)
        c16 = c.astype(BF16)
        dwa_ref[0] += _dot_tn(c16, dr16)
        dwx_ref[0] += _dot_tn(c16, di16)
        dba_ref[...] += jnp.sum(dpre_r, axis=0, keepdims=True)
        dbx_ref[...] += jnp.sum(dpre_i, axis=0, keepdims=True)
        dc = lam_t * s * i + _dot_nt(dr16, wa_ref[0]) + _dot_nt(di16, wx_ref[0])
        dcb_ref[...] += jnp.sum(dc, axis=0, keepdims=True)
        for k in range(CONV_K):
            dcw_ref[pl.ds(CONV_K - 1 - k, 1), :] += jnp.sum(dc * taps[k], axis=0, keepdims=True)
        dce = jnp.concatenate([dc, dc_head[...]], axis=0)
        dx = cw_ref[pl.ds(CONV_K - 1, 1), :] * dc
        for k in range(1, CONV_K):
            dx = dx + cw_ref[pl.ds(CONV_K - 1 - k, 1), :] * pltpu.roll(dce, lc + SUBLANES - k, 0)[0:lc]
        dx_ref[...] = dx.astype(BF16)
        dc_s[...] = dc
        dc_head[...] = dc_s[pl.ds(0, SUBLANES), :]

    def chunk(col):
        return pl.BlockSpec((lc, LANES), lambda cb, b, ci: (b * nc + nc - 1 - ci, col(cb)))

    def halo(col):
        return pl.BlockSpec((SUBLANES, LANES),
                            lambda cb, b, ci: (jnp.maximum((b * nc + nc - 1 - ci) * hpc - 1, 0), col(cb)))

    chan = lambda cb, b, ci: (0, cb)
    blk = lambda cb, b, ci: (cb, 0, 0)
    vec = pl.BlockSpec((1, LANES), chan)
    mat = pl.BlockSpec((1, LANES, LANES), blk)
    return pl.pallas_call(
        body, name="lru_bwd", grid=(ncb, B, nc),
        in_specs=[chunk(lambda cb: cb), halo(lambda cb: cb), chunk(lambda cb: ncb + cb),
                  chunk(lambda cb: cb), halo(lambda cb: cb), chunk(lambda cb: cb),
                  pl.BlockSpec((CONV_K, LANES), chan), vec, mat, mat, vec, vec, vec],
        out_specs=[chunk(lambda cb: cb), chunk(lambda cb: cb), pl.BlockSpec((CONV_K, LANES), chan), vec,
                   mat, mat, vec, vec, vec],
        out_shape=[jax.ShapeDtypeStruct((T, LRU_W), BF16), jax.ShapeDtypeStruct((T, LRU_W), BF16),
                   jax.ShapeDtypeStruct((CONV_K, LRU_W), F32), jax.ShapeDtypeStruct((1, LRU_W), F32),
                   jax.ShapeDtypeStruct((ncb, LANES, LANES), F32), jax.ShapeDtypeStruct((ncb, LANES, LANES), F32),
                   jax.ShapeDtypeStruct((1, LRU_W), F32), jax.ShapeDtypeStruct((1, LRU_W), F32),
                   jax.ShapeDtypeStruct((1, LRU_W), F32)],
        scratch_shapes=[pltpu.VMEM((lc, LANES), F32), pltpu.VMEM((lc, LANES), F32), pltpu.VMEM((1, LANES), F32),
                        pltpu.VMEM((1, LANES), F32), pltpu.VMEM((SUBLANES, LANES), F32)],
        compiler_params=_params(("arbitrary", "arbitrary", "arbitrary"), 32),
    )(proj_lru, proj_lru, proj_lru, h, h, dy_lru, conv_w, conv_b, wa_bd, wx_bd, b_a, b_x, lam)


def _bwd_in(pieces, w_in16t, x2, g1, dh1, tm):
    T = x2.shape[0]
    npc = len(pieces)

    def body(*refs):
        p_refs = refs[:npc]
        w_ref, x_ref, g_ref, d_ref, dx_ref, dproj_ref, xn_ref, dg1_ref = refs[npc:]

        @pl.when(pl.program_id(0) == 0)
        def _():
            dg1_ref[...] = jnp.zeros_like(dg1_ref)

        dxn = jnp.zeros((tm, D_MODEL), F32)
        for n, p_ref in enumerate(p_refs):
            cols = slice(n * LRU_W, (n + 1) * LRU_W)
            p16 = p_ref[...]
            dproj_ref[:, cols] = p16
            dxn = dxn + _dot(p16, w_ref[cols, :])
        gv = g_ref[...]
        xn, xhat, rstd = _rms_fwd(x_ref[...], gv)
        xn_ref[...] = xn.astype(BF16)
        dg1_ref[...] += jnp.sum(dxn * xhat, axis=0, keepdims=True)
        dx_ref[...] = d_ref[...] + _rms_bwd(dxn, xhat, rstd, gv)

    row = lambda i: (i, 0)
    fix = lambda i: (0, 0)
    return pl.pallas_call(
        body, name="bwd_in", grid=(T // tm,),
        in_specs=[pl.BlockSpec((tm, LRU_W), row)] * npc + [
            pl.BlockSpec((IN_COLS, D_MODEL), fix), pl.BlockSpec((tm, D_MODEL), row),
            pl.BlockSpec((1, D_MODEL), fix), pl.BlockSpec((tm, D_MODEL), row)],
        out_specs=[pl.BlockSpec((tm, D_MODEL), row), pl.BlockSpec((tm, IN_COLS), row),
                   pl.BlockSpec((tm, D_MODEL), row), pl.BlockSpec((1, D_MODEL), fix)],
        out_shape=[jax.ShapeDtypeStruct((T, D_MODEL), F32), jax.ShapeDtypeStruct((T, IN_COLS), BF16),
                   jax.ShapeDtypeStruct((T, D_MODEL), BF16), jax.ShapeDtypeStruct((1, D_MODEL), F32)],
        compiler_params=_params(("arbitrary",), 56),
    )(*pieces, w_in16t, x2, g1, dh1)


def _adam_shards(name, shards, steps, ride, ride_sliced):
    ns = len(shards)
    nr = len(ride)
    ride_shape, ride_specs, ride_sems = _exchange_shapes(ride, ride_sliced)

    def body(*refs):
        ins, rest = refs[:4 * ns], refs[4 * ns:]
        outs = rest[nr:nr + 4 * ns]
        start_ride, wait_ride = _exchange_copies(rest[:nr], rest[nr + 4 * ns:2 * nr + 4 * ns], ride_sliced,
                                                 *rest[2 * nr + 4 * ns:])
        pl.when(pl.program_id(0) == 0)(start_ride)
        for i in range(ns):
            p_ref, w_ref, m_ref, v_ref = ins[4 * i:4 * i + 4]
            g = p_ref[0]
            for p in range(1, N_DEV):
                g = g + p_ref[p]
            outs[4 * i][...] = g
            outs[4 * i + 1][...], outs[4 * i + 2][...], outs[4 * i + 3][...] = _adamw(w_ref[...], g, m_ref[...], v_ref[...])
        pl.when(pl.program_id(0) == steps - 1)(wait_ride)

    in_specs, out_specs, out_shape, args = [], [], [], []
    for parts, w, m, v in shards:
        R, C = w.shape
        blk = pl.BlockSpec((R // steps, C), lambda i: (i, 0))
        in_specs += [pl.BlockSpec((N_DEV, R // steps, C), lambda i: (0, i, 0)), blk, blk, blk]
        out_specs += [blk] * 4
        out_shape += [jax.ShapeDtypeStruct((R, C), F32)] * 4
        args += [parts, w, m, v]
    res = pl.pallas_call(
        body, name=name, grid=(steps,), in_specs=in_specs + ride_specs, out_specs=out_specs + ride_specs,
        out_shape=out_shape + ride_shape, scratch_shapes=ride_sems,
        compiler_params=_params(("arbitrary",), 48),
    )(*args, *ride)
    return [list(res[4 * i:4 * i + 4]) for i in range(ns)], list(res[4 * ns:])


def _adam_params(name, items):
    def body(*refs):
        ins, outs = refs[:4 * len(items)], refs[4 * len(items):]
        for i in range(len(items)):
            p_ref, w_ref, m_ref, v_ref = ins[4 * i:4 * i + 4]
            g = p_ref[0]
            for p in range(1, N_DEV):
                g = g + p_ref[p]
            outs[4 * i][...] = g
            outs[4 * i + 1][...], outs[4 * i + 2][...], outs[4 * i + 3][...] = _adamw(w_ref[...], g, m_ref[...], v_ref[...])

    args = [a for item in items for a in item]
    outs = [w for _, w, _, _ in items for _ in range(4)]
    res = pl.pallas_call(
        body, name=name, grid=(1,), in_specs=[_whole(a.shape) for a in args], out_specs=[_whole(w.shape) for w in outs],
        out_shape=[jax.ShapeDtypeStruct(w.shape, F32) for w in outs], compiler_params=_params(("arbitrary",), 32),
    )(*args)
    return [list(res[4 * i:4 * i + 4]) for i in range(len(items))]


def _adam_given(name, g, w, m, v, tr):
    R, C = w.shape

    def body(g_ref, w_ref, m_ref, v_ref, d_ref, m2_ref, v2_ref):
        d_ref[...], m2_ref[...], v2_ref[...] = _adamw(w_ref[...], g_ref[...], m_ref[...], v_ref[...])

    blk = pl.BlockSpec((tr, C), lambda i: (i, 0))
    return [g] + list(pl.pallas_call(
        body, name=name, grid=(R // tr,), in_specs=[blk] * 4, out_specs=[blk] * 3,
        out_shape=[jax.ShapeDtypeStruct((R, C), F32)] * 3, compiler_params=_params(("arbitrary",), 32),
    )(g, w, m, v))


def _sum_parts(name, parts):
    def body(p_ref, g_ref):
        g = p_ref[0].astype(F32)
        for p in range(1, N_DEV):
            g = g + p_ref[p].astype(F32)
        g_ref[...] = g

    return pl.pallas_call(
        body, name=name, grid=(1,), in_specs=[_whole(parts.shape)], out_specs=_whole(parts.shape[1:]),
        out_shape=jax.ShapeDtypeStruct(parts.shape[1:], F32), compiler_params=_params(("arbitrary",), 32),
    )(parts)


def _block_diag_pairs(w):
    w = w.reshape(LRU_BLOCKS // 2, 2, HEAD_D, HEAD_D)
    out = jnp.zeros((LRU_BLOCKS // 2, LANES, LANES), w.dtype)
    out = out.at[:, :HEAD_D, :HEAD_D].set(w[:, 0])
    return out.at[:, HEAD_D:, HEAD_D:].set(w[:, 1])


def _diag_blocks(w):
    return jnp.stack([w[:, :HEAD_D, :HEAD_D], w[:, HEAD_D:, HEAD_D:]], axis=1).reshape(LRU_BLOCKS, HEAD_D, HEAD_D)


def kernel(x, norm1_g, w_in, conv_w, conv_b, lru_w_a, lru_b_a, lru_w_x, lru_b_x, lru_lambda, lru_out_g, sb_out_g, w_out, norm2_g, w_up, w_down, final_g, loss_target, m_norm1_g, m_w_in, m_conv_w, m_conv_b, m_lru_w_a, m_lru_b_a, m_lru_w_x, m_lru_b_x, m_lru_lambda, m_lru_out_g, m_sb_out_g, m_w_out, m_norm2_g, m_w_up, m_w_down, m_final_g, v_norm1_g, v_w_in, v_conv_w, v_conv_b, v_lru_w_a, v_lru_b_a, v_lru_w_x, v_lru_b_x, v_lru_lambda, v_lru_out_g, v_sb_out_g, v_w_out, v_norm2_g, v_w_up, v_w_down, v_final_g):
    B, S, _ = x.shape
    T = B * S
    tm = min(512, T)
    tk = min(GRAD_TK, T)
    lc = min(1024, S)
    x2 = x.reshape(T, D_MODEL)
    tgt = loss_target.reshape(T, D_MODEL)
    cw_cols = LRU_W // N_DEV

    shards16 = _cast_shards([w_in[0].T, w_out[0], w_up[0], w_down[0]])
    cw_pad = jnp.zeros((SUBLANES, LANES), F32).at[:CONV_K, :cw_cols].set(conv_w[0])
    g_in, g_cw = _gather_two_level("gather_w_in", [shards16[0], cw_pad])
    w_in16t = g_in.reshape(IN_COLS, D_MODEL)
    conv_w_full = g_cw[:, :CONV_K, :cw_cols].transpose(1, 0, 2).reshape(CONV_K, LRU_W)
    wa_bd = _block_diag_pairs(lru_w_a[0]).astype(BF16)
    wx_bd = _block_diag_pairs(lru_w_x[0]).astype(BF16)
    b_a = lru_b_a.reshape(1, LRU_W)
    b_x = lru_b_x.reshape(1, LRU_W)
    gf = final_g.reshape(1, D_MODEL)

    proj_lru, qkv = _fwd_in(x2, norm1_g, w_in16t, min(1024, T))
    y_lru, h = _lru_fwd(proj_lru, conv_w_full, conv_b, wa_bd, wx_bd, b_a, b_x, lru_lambda, B, S, lc)
    tri_suffix, tri_prefix = _tri(False), _tri(True)
    y_sb, run_tab, g_out, g_up, g_down = _attn_fwd(qkv, tri_suffix, list(shards16[1:]), [False] * 3, B, S)
    w_out16 = g_out.reshape(D_MODEL, D_MODEL)
    w_down16 = g_down.reshape(D_FF, D_MODEL)
    h1, mix16 = _fwd_mix(y_lru, y_sb, lru_out_g, sb_out_g, w_out16, x2, min(1024, T))
    up16, dh2, d_final_g, loss_part = _fwd_mlp(h1, norm2_g, g_up, w_down16, gf, tgt, min(1024, T), 2)

    dup16, dh1, hn16, dh2b, d_norm2_g = _bwd_mlp(dh2, up16, h1, norm2_g, g_up, w_down16, tm, 4)
    sq = lambda u: (u.astype(F32) * u.astype(F32)).astype(BF16)
    wide = 4
    gw_up, = _matmul_tn("grad_w_up", hn16, dup16, D_MODEL, wide * FF_CHUNK, tk // 2, (N_DEV, D_MODEL, FF_CHUNK),
                        (wide, D_MODEL, FF_CHUNK), lambda m, n, k: (n, 0, 0), split=(1, wide))
    gw_down, = _matmul_tn("grad_w_down", up16, dh2b, wide * FF_CHUNK, D_MODEL, tk // 2, (N_DEV, FF_CHUNK, D_MODEL),
                          (wide, FF_CHUNK, D_MODEL), lambda m, n, k: (m, 0, 0), a_prep=sq, split=(0, wide))
    dy_lru, dy_sb, d_lru_out_g, d_sb_out_g = _bwd_mix(dh1, w_out16, y_lru, y_sb, lru_out_g, sb_out_g, min(1024, T))
    gw_out, = _matmul_tn("grad_w_out", mix16, dh1, D_MODEL, FF_CHUNK, tk, (D_MODEL, D_MODEL),
                         (D_MODEL, FF_CHUNK), lambda m, n, k: (0, n), b_prep=lambda u: u.astype(BF16))
    parts_out = gw_out.reshape(N_DEV, D_MODEL // N_DEV, D_MODEL)
    dq, dk, dv, r_out, r_up, r_down = _attn_bwd(qkv, run_tab, dy_sb, tri_suffix, tri_prefix,
                                                [parts_out, gw_up, gw_down], [True] * 3, B, S)
    (dx_lru, dg_lru, d_conv_w, d_conv_b, d_wa, d_wx, d_b_a, d_b_x, d_lambda) = _lru_bwd(
        proj_lru, h, dy_lru, conv_w_full, conv_b, wa_bd, wx_bd, b_a, b_x, lru_lambda, B, S, lc)
    dx, dproj16, xn16, d_norm1_g = _bwd_in([dx_lru, dg_lru, dq, dk, dv], w_in16t, x2, norm1_g, dh1, tm)

    small = {"norm1_g": (d_norm1_g, norm1_g, m_norm1_g, v_norm1_g), "conv_b": (d_conv_b, conv_b, m_conv_b, v_conv_b),
             "lru_w_a": (_diag_blocks(d_wa), lru_w_a, m_lru_w_a, v_lru_w_a),
             "lru_b_a": (d_b_a, lru_b_a, m_lru_b_a, v_lru_b_a),
             "lru_w_x": (_diag_blocks(d_wx), lru_w_x, m_lru_w_x, v_lru_w_x),
             "lru_b_x": (d_b_x, lru_b_x, m_lru_b_x, v_lru_b_x),
             "lru_lambda": (d_lambda, lru_lambda, m_lru_lambda, v_lru_lambda),
             "lru_out_g": (d_lru_out_g, lru_out_g, m_lru_out_g, v_lru_out_g),
             "sb_out_g": (d_sb_out_g, sb_out_g, m_sb_out_g, v_sb_out_g),
             "norm2_g": (d_norm2_g, norm2_g, m_norm2_g, v_norm2_g),
             "final_g": (d_final_g, final_g, m_final_g, v_final_g)}
    names = list(small)

    def held(n, a):
        return a.reshape((1, D_MODEL) if n == "final_g" else small[n][1].shape)

    parts_cw = d_conv_w.reshape(CONV_K, N_DEV, cw_cols).transpose(1, 0, 2)[:, None]
    gw_in_t, *got = _matmul_tn(
        "grad_w_in", dproj16, xn16, IN_COLS // 2, D_MODEL, tk // 2, (IN_COLS, D_MODEL), (IN_COLS // 2, D_MODEL),
        lambda m, n, k: (m, 0), out_dtype=BF16,
        ride=[parts_cw] + [held(n, small[n][0]) for n in names] + [loss_part],
        ride_sliced=[True] + [False] * (len(names) + 1))

    big, (r_in,) = _adam_shards(
        "adam_out_up_down", [(r_out, w_out[0], m_w_out[0], v_w_out[0]), (r_up, w_up[0], m_w_up[0], v_w_up[0]),
                             (r_down, w_down[0], m_w_down[0], v_w_down[0])],
        4, [gw_in_t.reshape(N_DEV, IN_COLS // N_DEV, D_MODEL)], [True])
    g_in_t = _sum_parts("sum_w_in", r_in)
    out = dict(zip(("w_out", "w_up", "w_down"), big))
    out["w_in"] = _adam_given("adam_w_in", g_in_t.T, w_in[0], m_w_in[0], v_w_in[0], 256)
    out = {n: [a[None] for a in res] for n, res in out.items()}
    items = [(got[0], conv_w, m_conv_w, v_conv_w)]
    items += [(parts, *[held(n, a) for a in small[n][1:]]) for n, parts in zip(names, got[1:-1])]
    for n, res in zip(["conv_w"] + names, _adam_params("adam_small", items)):
        out[n] = [a.reshape((conv_w if n == "conv_w" else small[n][1]).shape) for a in res]
    loss = _sum_parts("sum_loss", got[-1])[0, 0]
    weights = ["norm1_g", "w_in", "conv_w", "conv_b", "lru_w_a", "lru_b_a", "lru_w_x", "lru_b_x", "lru_lambda",
               "lru_out_g", "sb_out_g", "w_out", "norm2_g", "w_up", "w_down", "final_g"]
    return (loss, dx.reshape(B, S, D_MODEL), *[out[n][0] for n in weights], *[out[n][1] for n in weights],
            *[out[n][2] for n in weights], *[out[n][3] for n in weights])
```
